```python
import math
import jax, jax.numpy as jnp
from jax import lax
import numpy as np

D_MODEL = 2048
BATCH = 2
SEQ = 4096
DEPTH = 1

HEAD_DIM = 128
N_Q_HEADS = 8
N_KV_HEADS = 2
GQA_GROUP = N_Q_HEADS // N_KV_HEADS
ATTN_WIDTH = N_Q_HEADS * HEAD_DIM
KV_WIDTH = N_KV_HEADS * HEAD_DIM
WINDOW = 128
BLOCK = 128
N_BUCKETS = 32
MAX_DISTANCE = 128
HY_WIDTH = D_MODEL - ATTN_WIDTH
HY_ORDER = 2
SHORT_CONV = 3
POS_BANDS = 16
POS_EMB = 1 + 2 * POS_BANDS
FILTER_HIDDEN = 64
FAST_DECAY_PCT = 0.3
SLOW_DECAY_PCT = 1.5
DECAY_TARGET = 1e-2
MIX_WIDTH = ATTN_WIDTH + HY_WIDTH
IN_COLS = ATTN_WIDTH + 2 * KV_WIDTH + (HY_ORDER + 1) * HY_WIDTH
NORM_GROUP = 128
N_KEYS = 128
N_EXPERTS = N_KEYS * N_KEYS
PEER_HEADS = 8
PEER_QDIM = 256
PEER_QHALF = PEER_QDIM // 2
PEER_TOPK = 16
PEER_CHUNK = 128
NEG = -1e30

kernel_name = "hymba_swa_hyena_peer_deepnorm"


def layer_norm(x, g, b, eps=1e-5):
    xf = x.astype(jnp.float32)
    mu = jnp.mean(xf, -1, keepdims=True)
    var = jnp.mean(jnp.square(xf - mu), -1, keepdims=True)
    y = (xf - mu) * lax.rsqrt(var + eps)
    return (y * g.astype(jnp.float32) + b.astype(jnp.float32)).astype(x.dtype)


def group_rms_norm(x, g, eps=1e-6):
    B, S, W = x.shape
    xf = x.astype(jnp.float32).reshape(B, S, W // NORM_GROUP, NORM_GROUP)
    y = xf * lax.rsqrt(jnp.mean(jnp.square(xf), -1, keepdims=True) + eps)
    return (y.reshape(B, S, W) * g.astype(jnp.float32)).astype(x.dtype)


def t5_bucket(rel):
    nb = N_BUCKETS // 2
    ret = (rel > 0).astype(jnp.int32) * nb
    n = jnp.abs(rel)
    max_exact = nb // 2
    nf = jnp.maximum(n, 1).astype(jnp.float32)
    large = max_exact + (jnp.log(nf / max_exact) / math.log(MAX_DISTANCE / max_exact)
                         * (nb - max_exact)).astype(jnp.int32)
    large = jnp.minimum(large, nb - 1)
    return ret + jnp.where(n < max_exact, n, large)


def windowed_gqa(q, k, v, rel_bias, sink):
    B, S = q.shape[0], q.shape[1]
    nb = S // BLOCK
    qb = q.reshape(B, nb, BLOCK, N_KV_HEADS, GQA_GROUP, HEAD_DIM)
    pad = ((0, 0), (BLOCK, BLOCK), (0, 0), (0, 0))
    kp = jnp.pad(k, pad).reshape(B, nb + 2, BLOCK, N_KV_HEADS, HEAD_DIM)
    vp = jnp.pad(v, pad).reshape(B, nb + 2, BLOCK, N_KV_HEADS, HEAD_DIM)
    kb = jnp.concatenate([kp[:, :-2], kp[:, 1:-1], kp[:, 2:]], axis=2)
    vb = jnp.concatenate([vp[:, :-2], vp[:, 1:-1], vp[:, 2:]], axis=2)
    s = jnp.einsum('bnqgrd,bnkgd->bngrqk', qb, kb).astype(jnp.float32) / math.sqrt(HEAD_DIM)
    qi = jnp.arange(BLOCK, dtype=jnp.int32)
    kj = jnp.arange(3 * BLOCK, dtype=jnp.int32)
    rel = kj[None, :] - BLOCK - qi[:, None]
    bias = rel_bias.astype(jnp.float32)[t5_bucket(rel)]
    bias = jnp.transpose(bias, (2, 0, 1)).reshape(N_KV_HEADS, GQA_GROUP, BLOCK, 3 * BLOCK)
    kabs = jnp.arange(nb, dtype=jnp.int32)[:, None, None] * BLOCK + kj[None, None, :] - BLOCK
    valid = (jnp.abs(rel) <= WINDOW)[None] & (kabs >= 0) & (kabs < S)
    s = jnp.where(valid[None, :, None, None], s + bias, NEG)
    sk = sink.astype(jnp.float32).reshape(N_KV_HEADS, GQA_GROUP)[None, None, :, :, None, None]
    m = jnp.maximum(jnp.max(s, -1, keepdims=True), sk)
    p = jnp.exp(s - m)
    p = p / (jnp.sum(p, -1, keepdims=True) + jnp.exp(sk - m))
    o = jnp.einsum('bngrqk,bnkgd->bnqgrd', p.astype(v.dtype), vb)
    return o.reshape(B, S, ATTN_WIDTH)


def short_conv(u, w, b):
    up = jnp.pad(u, ((0, 0), (1, 1), (0, 0)))
    return up[:, :-2] * w[0] + up[:, 1:-1] * w[1] + up[:, 2:] * w[2] + b


def implicit_filters_fft(L, w1, b1, fr1, w2, b2, fr2, w3):
    f32 = jnp.float32
    t = jnp.arange(L, dtype=f32)
    tn = t / max(L - 1, 1)
    w = 2.0 * math.pi * t / L
    bands = jnp.linspace(1e-4, POS_BANDS - 1, POS_BANDS, dtype=f32)
    z = jnp.concatenate([tn[:, None], jnp.cos(w[:, None] * bands), -jnp.sin(w[:, None] * bands)], -1)
    hid = jnp.sin(fr1.astype(f32) * (z @ w1.astype(f32) + b1.astype(f32)))
    hid = jnp.sin(fr2.astype(f32) * (hid @ w2.astype(f32) + b2.astype(f32)))
    h = (hid @ w3.astype(f32)).reshape(L, HY_ORDER, 2, HY_WIDTH)
    max_decay = math.log(DECAY_TARGET) / FAST_DECAY_PCT
    min_decay = math.log(DECAY_TARGET) / SLOW_DECAY_PCT
    deltas = jnp.linspace(min_decay, max_decay, HY_WIDTH, dtype=f32)
    decay = jnp.exp(-tn[:, None] * jnp.abs(deltas)[None])
    h = h * decay[:, None, None, :]
    h = h / (jnp.sum(jnp.abs(h), axis=(0, 2), keepdims=True) + 1e-6)
    hf, hb = h[:, :, 0], h[:, :, 1]
    kfull = jnp.concatenate([hf, jnp.zeros((1, HY_ORDER, HY_WIDTH), f32), hb[1:][::-1]], axis=0)
    return jnp.fft.rfft(kfull, axis=0)


def fft_long_conv(u, kf, d):
    L = u.shape[1]
    uf = jnp.fft.rfft(u, n=2 * L, axis=1)
    y = jnp.fft.irfft(uf * kf[None], n=2 * L, axis=1)[:, :L]
    return y + u * d.astype(jnp.float32)


def hyena(hy, conv_w, conv_b, kf, hy_bias):
    u = short_conv(hy, conv_w, conv_b).astype(jnp.float32)
    v, x1, x2 = jnp.split(u, HY_ORDER + 1, axis=-1)
    gates = (x1, x2)
    z = v
    for o in range(HY_ORDER):
        z = gates[o] * fft_long_conv(z, kf[:, o], hy_bias[o])
    return z.astype(hy.dtype)


def peer(h, wq, subkeys, U, V):
    B, S, D = h.shape
    T = B * S
    hf = h.reshape(T, D)
    q = (hf @ wq).reshape(T, PEER_HEADS, PEER_QDIM).astype(jnp.float32)
    q1, q2 = q[..., :PEER_QHALF], q[..., PEER_QHALF:]
    s1 = jnp.einsum('thd,kd->thk', q1, subkeys[0].astype(jnp.float32))
    s2 = jnp.einsum('thd,kd->thk', q2, subkeys[1].astype(jnp.float32))
    v1, i1 = lax.top_k(s1, PEER_TOPK)
    v2, i2 = lax.top_k(s2, PEER_TOPK)
    cand = (v1[..., :, None] + v2[..., None, :]).reshape(T, PEER_HEADS, PEER_TOPK * PEER_TOPK)
    sc, ci = lax.top_k(cand, PEER_TOPK)
    e = (jnp.take_along_axis(i1, ci // PEER_TOPK, -1) * N_KEYS
         + jnp.take_along_axis(i2, ci % PEER_TOPK, -1))
    g = jax.nn.softmax(sc, axis=-1)
    nc = T // PEER_CHUNK
    E = PEER_HEADS * PEER_TOPK
    eidx = e.reshape(nc, PEER_CHUNK, E)
    gg = g.reshape(nc, PEER_CHUNK, E)
    hc = hf.reshape(nc, PEER_CHUNK, D)

    def expert_block(args):
        h_, e_, g_ = args
        a = jnp.einsum('tkd,td->tk', U[e_], h_).astype(jnp.float32)
        w = (g_ * jax.nn.gelu(a, approximate=False)).astype(V.dtype)
        return jnp.einsum('tk,tkd->td', w, V[e_])

    out = lax.map(expert_block, (hc, eidx, gg))
    return out.reshape(B, S, D)


def setup_inputs(seed: int = 0) -> dict:
    key = jax.random.key(seed)
    ks = iter(jax.random.split(key, 40))
    beta = (8.0 * DEPTH) ** (-0.25)

    def nrm(shape, scale):
        return jax.random.normal(next(ks), shape, jnp.float32) * scale

    L = DEPTH
    return {
        "x": nrm((BATCH, SEQ, D_MODEL), 1.0),
        "ln0_g": 1.0 + nrm((D_MODEL,), 0.02),
        "ln0_b": nrm((D_MODEL,), 0.02),
        "rel_bias": nrm((N_BUCKETS, N_Q_HEADS), 0.5),
        "w_in": nrm((L, D_MODEL, IN_COLS), D_MODEL ** -0.5),
        "sink": nrm((L, N_Q_HEADS), 0.5),
        "conv_w": nrm((L, SHORT_CONV, (HY_ORDER + 1) * HY_WIDTH), 0.6),
        "conv_b": nrm((L, (HY_ORDER + 1) * HY_WIDTH), 0.02),
        "f_w1": nrm((L, POS_EMB, FILTER_HIDDEN), POS_EMB ** -0.5),
        "f_b1": nrm((L, FILTER_HIDDEN), 0.1),
        "f_freq1": 1.0 + nrm((L, FILTER_HIDDEN), 0.1),
        "f_w2": nrm((L, FILTER_HIDDEN, FILTER_HIDDEN), FILTER_HIDDEN ** -0.5),
        "f_b2": nrm((L, FILTER_HIDDEN), 0.1),
        "f_freq2": 1.0 + nrm((L, FILTER_HIDDEN), 0.1),
        "f_w3": nrm((L, FILTER_HIDDEN, HY_ORDER * 2 * HY_WIDTH), FILTER_HIDDEN ** -0.5),
        "hy_bias": nrm((L, HY_ORDER, HY_WIDTH), 0.1),
        "mix_norm_g": 1.0 + nrm((L, MIX_WIDTH), 0.02),
        "w_out": nrm((L, MIX_WIDTH, D_MODEL), beta * MIX_WIDTH ** -0.5),
        "ln1_g": 1.0 + nrm((L, D_MODEL), 0.02),
        "ln1_b": nrm((L, D_MODEL), 0.02),
        "peer_wq": nrm((L, D_MODEL, PEER_HEADS * PEER_QDIM), D_MODEL ** -0.5),
        "peer_subkeys": nrm((L, 2, N_KEYS, PEER_QHALF), PEER_QHALF ** -0.5),
        "peer_u": nrm((L, N_EXPERTS, D_MODEL), D_MODEL ** -0.5),
        "peer_v": nrm((L, N_EXPERTS, D_MODEL), beta),
        "ln2_g": 1.0 + nrm((L, D_MODEL), 0.02),
        "ln2_b": nrm((L, D_MODEL), 0.02),
    }


def reference(x, ln0_g, ln0_b, rel_bias, w_in, sink, conv_w, conv_b, f_w1, f_b1, f_freq1,
              f_w2, f_b2, f_freq2, f_w3, hy_bias, mix_norm_g, w_out, ln1_g, ln1_b,
              peer_wq, peer_subkeys, peer_u, peer_v, ln2_g, ln2_b):
    alpha = (2.0 * DEPTH) ** 0.25
    B, S, _ = x.shape
    h = layer_norm(x, ln0_g, ln0_b)
    q_end = ATTN_WIDTH
    k_end = q_end + KV_WIDTH
    v_end = k_end + KV_WIDTH
    for l in range(DEPTH):
        proj = h @ w_in[l]
        q = proj[..., :q_end].reshape(B, S, N_Q_HEADS, HEAD_DIM)
        k = proj[..., q_end:k_end].reshape(B, S, N_KV_HEADS, HEAD_DIM)
        v = proj[..., k_end:v_end].reshape(B, S, N_KV_HEADS, HEAD_DIM)
        hy = proj[..., v_end:]
        attn = windowed_gqa(q, k, v, rel_bias, sink[l])
        kf = implicit_filters_fft(S, f_w1[l], f_b1[l], f_freq1[l], f_w2[l], f_b2[l],
                                  f_freq2[l], f_w3[l])
        hyo = hyena(hy, conv_w[l], conv_b[l], kf, hy_bias[l])
        mix = group_rms_norm(jnp.concatenate([attn, hyo], -1), mix_norm_g[l])
        mix = mix @ w_out[l]
        h = layer_norm(alpha * h + mix, ln1_g[l], ln1_b[l])
        ffn = peer(h, peer_wq[l], peer_subkeys[l], peer_u[l], peer_v[l])
        h = layer_norm(alpha * h + ffn, ln2_g[l], ln2_b[l])
    return h
```

```python
import functools
import math

import numpy as np
import jax
import jax.numpy as jnp
from jax import lax
from jax.experimental import pallas as pl
from jax.experimental.pallas import tpu as pltpu

F32 = jnp.float32
BF16 = jnp.bfloat16

D_MODEL = 2048
HEAD_DIM = 128
N_Q_HEADS = 8
N_KV_HEADS = 2
GQA_GROUP = N_Q_HEADS // N_KV_HEADS
ATTN_WIDTH = N_Q_HEADS * HEAD_DIM
KV_WIDTH = N_KV_HEADS * HEAD_DIM
WINDOW = 128
BLOCK = 128
N_BUCKETS = 32
MAX_DISTANCE = 128
HY_WIDTH = D_MODEL - ATTN_WIDTH
HY_ORDER = 2
POS_BANDS = 16
POS_EMB = 1 + 2 * POS_BANDS
FILTER_HIDDEN = 64
FAST_DECAY_PCT = 0.3
SLOW_DECAY_PCT = 1.5
DECAY_TARGET = 1e-2
NORM_GROUP = 128
N_KEYS = 128
PEER_HEADS = 8
PEER_QDIM = 256
PEER_QHALF = PEER_QDIM // 2
PEER_TOPK = 16
NEG = -1e30
DEPTH = 1

LANES = 128
VMEM_LIMIT = 56 * 1024 * 1024

FFT_N1 = 64
FFT_N2 = 128
FFT_K1 = FFT_N1 // 2 + 1
FFT_K1_PAD = 40
FFT_SLOT = 2 * FFT_N2
FFT_ROWS = FFT_K1_PAD * FFT_SLOT


def _cparams(sem, vmem=VMEM_LIMIT):
    return pltpu.CompilerParams(dimension_semantics=sem, vmem_limit_bytes=vmem)


def _ln0_kernel(x_ref, g_ref, b_ref, h_ref, hb_ref):
    x = x_ref[...]
    mu = jnp.mean(x, -1, keepdims=True)
    xc = x - mu
    var = jnp.mean(xc * xc, -1, keepdims=True)
    y = xc * lax.rsqrt(var + 1e-5) * g_ref[...] + b_ref[...]
    h_ref[...] = y
    hb_ref[...] = y.astype(BF16)


def _ln0(x2d, g, b, tm=512):
    T, D = x2d.shape
    return pl.pallas_call(
        _ln0_kernel,
        grid=(T // tm,),
        in_specs=[pl.BlockSpec((tm, D), lambda i: (i, 0)),
                  pl.BlockSpec((1, D), lambda i: (0, 0)),
                  pl.BlockSpec((1, D), lambda i: (0, 0))],
        out_specs=[pl.BlockSpec((tm, D), lambda i: (i, 0)),
                   pl.BlockSpec((tm, D), lambda i: (i, 0))],
        out_shape=[jax.ShapeDtypeStruct((T, D), F32), jax.ShapeDtypeStruct((T, D), BF16)],
        compiler_params=_cparams(("parallel",)),
        name="ln0",
    )(x2d, g.reshape(1, D), b.reshape(1, D))


def _mm_kernel(a_ref, b_ref, o_ref):
    o_ref[...] = jnp.dot(a_ref[...], b_ref[...], preferred_element_type=F32)


def _matmul(a, b, tm, tn):
    M, K = a.shape
    N = b.shape[1]
    return pl.pallas_call(
        _mm_kernel,
        grid=(N // tn, M // tm),
        in_specs=[pl.BlockSpec((tm, K), lambda j, i: (i, 0)),
                  pl.BlockSpec((K, tn), lambda j, i: (0, j))],
        out_specs=pl.BlockSpec((tm, tn), lambda j, i: (i, j)),
        out_shape=jax.ShapeDtypeStruct((M, N), F32),
        compiler_params=_cparams(("parallel", "parallel")),
        name="in_proj",
    )(a, b)


def _t5_bucket(rel):
    nb = N_BUCKETS // 2
    ret = (rel > 0).astype(jnp.int32) * nb
    n = jnp.abs(rel)
    max_exact = nb // 2
    nf = jnp.maximum(n, 1).astype(F32)
    large = max_exact + (jnp.log(nf / max_exact) / math.log(MAX_DISTANCE / max_exact)
                         * (nb - max_exact)).astype(jnp.int32)
    large = jnp.minimum(large, nb - 1)
    return ret + jnp.where(n < max_exact, n, large)


def _attn_kernel(sink_ref, q_ref, kp_ref, kc_ref, kn_ref, vp_ref, vc_ref, vn_ref, bias_ref, o_ref,
                 *, nb, seq):
    n = pl.program_id(0) % nb
    qi = lax.broadcasted_iota(jnp.int32, (BLOCK, 3 * BLOCK), 0)
    kj = lax.broadcasted_iota(jnp.int32, (BLOCK, 3 * BLOCK), 1)
    rel = kj - BLOCK - qi
    kabs = n * BLOCK + kj - BLOCK
    valid = (jnp.abs(rel) <= WINDOW) & (kabs >= 0) & (kabs < seq)
    k = jnp.concatenate([kp_ref[...], kc_ref[...], kn_ref[...]], axis=0).astype(BF16)
    v = jnp.concatenate([vp_ref[...], vc_ref[...], vn_ref[...]], axis=0).astype(BF16)
    scale = 1.0 / math.sqrt(HEAD_DIM)
    for g in range(N_KV_HEADS):
        kg = k[:, g * HEAD_DIM:(g + 1) * HEAD_DIM]
        vg = v[:, g * HEAD_DIM:(g + 1) * HEAD_DIM]
        for r in range(GQA_GROUP):
            h = g * GQA_GROUP + r
            qh = q_ref[:, h * HEAD_DIM:(h + 1) * HEAD_DIM].astype(BF16)
            s = lax.dot_general(qh, kg, (((1,), (1,)), ((), ())), preferred_element_type=F32) * scale
            s = jnp.where(valid, s + bias_ref[h], NEG)
            sk = sink_ref[h]
            m = jnp.maximum(jnp.max(s, -1, keepdims=True), sk)
            p = jnp.exp(s - m)
            denom = jnp.sum(p, -1, keepdims=True) + jnp.exp(sk - m)
            p = p / denom
            o_ref[:, h * HEAD_DIM:(h + 1) * HEAD_DIM] = jnp.dot(p.astype(BF16), vg,
                                                                 preferred_element_type=F32)


def _attention(proj, bias, sink, batch, seq):
    T = proj.shape[0]
    nb = seq // BLOCK
    kcol = ATTN_WIDTH // KV_WIDTH
    vcol = kcol + 1

    def prev(i):
        return i - jnp.where(i % nb == 0, 0, 1)

    def nxt(i):
        return i + jnp.where(i % nb == nb - 1, 0, 1)

    kern = functools.partial(_attn_kernel, nb=nb, seq=seq)
    return pl.pallas_call(
        kern,
        grid=(T // BLOCK,),
        in_specs=[pl.BlockSpec(memory_space=pltpu.SMEM),
                  pl.BlockSpec((BLOCK, ATTN_WIDTH), lambda i: (i, 0)),
                  pl.BlockSpec((BLOCK, KV_WIDTH), lambda i: (prev(i), kcol)),
                  pl.BlockSpec((BLOCK, KV_WIDTH), lambda i: (i, kcol)),
                  pl.BlockSpec((BLOCK, KV_WIDTH), lambda i: (nxt(i), kcol)),
                  pl.BlockSpec((BLOCK, KV_WIDTH), lambda i: (prev(i), vcol)),
                  pl.BlockSpec((BLOCK, KV_WIDTH), lambda i: (i, vcol)),
                  pl.BlockSpec((BLOCK, KV_WIDTH), lambda i: (nxt(i), vcol)),
                  pl.BlockSpec((N_Q_HEADS, BLOCK, 3 * BLOCK), lambda i: (0, 0, 0))],
        out_specs=pl.BlockSpec((BLOCK, ATTN_WIDTH), lambda i: (i, 0)),
        out_shape=jax.ShapeDtypeStruct((T, ATTN_WIDTH), F32),
        compiler_params=_cparams(("parallel",)),
        name="window_attn",
    )(sink, proj, proj, proj, proj, proj, proj, proj, bias)


def _filter_kernel(z_ref, w1_ref, b1_ref, fr1_ref, w2_ref, b2_ref, fr2_ref, w3_ref, dl_ref, o_ref,
                   *, tt, seq):
    hi = lax.Precision.HIGHEST
    a = jnp.dot(z_ref[...], w1_ref[...], preferred_element_type=F32, precision=hi) + b1_ref[...]
    hid = jnp.sin(fr1_ref[...] * a)
    a = jnp.dot(hid, w2_ref[...], preferred_element_type=F32, precision=hi) + b2_ref[...]
    hid = jnp.sin(fr2_ref[...] * a)
    h = jnp.dot(hid, w3_ref[...], preferred_element_type=F32, precision=hi)
    rows = lax.broadcasted_iota(jnp.int32, h.shape, 0) + pl.program_id(0) * tt
    tn = rows.astype(F32) / float(max(seq - 1, 1))
    o_ref[...] = h * jnp.exp(-tn * dl_ref[...])


def _filters(seq, w1, b1, fr1, w2, b2, fr2, w3, tt=512):
    t = jnp.arange(seq, dtype=F32)
    tn = t / max(seq - 1, 1)
    w = 2.0 * math.pi * t / seq
    bands = jnp.linspace(1e-4, POS_BANDS - 1, POS_BANDS, dtype=F32)
    z = jnp.concatenate([tn[:, None], jnp.cos(w[:, None] * bands), -jnp.sin(w[:, None] * bands)], -1)
    zp = jnp.pad(z, ((0, 0), (0, FILTER_HIDDEN - POS_EMB)))
    w1p = jnp.pad(w1, ((0, FILTER_HIDDEN - POS_EMB), (0, 0)))
    max_decay = math.log(DECAY_TARGET) / FAST_DECAY_PCT
    min_decay = math.log(DECAY_TARGET) / SLOW_DECAY_PCT
    deltas = jnp.abs(jnp.linspace(min_decay, max_decay, HY_WIDTH, dtype=F32))
    ncol = HY_ORDER * 2 * HY_WIDTH
    dl = jnp.tile(deltas, HY_ORDER * 2).reshape(1, ncol)
    H = FILTER_HIDDEN
    kern = functools.partial(_filter_kernel, tt=tt, seq=seq)
    full = lambda r, c: pl.BlockSpec((r, c), lambda i: (0, 0))
    return pl.pallas_call(
        kern,
        grid=(seq // tt,),
        in_specs=[pl.BlockSpec((tt, H), lambda i: (i, 0)),
                  full(H, H), full(1, H), full(1, H), full(H, H), full(1, H), full(1, H),
                  full(H, ncol), full(1, ncol)],
        out_specs=pl.BlockSpec((tt, ncol), lambda i: (i, 0)),
        out_shape=jax.ShapeDtypeStruct((seq, ncol), F32),
        compiler_params=_cparams(("parallel",)),
        name="hyena_filter_mlp",
    )(zp, w1p, b1.reshape(1, H), fr1.reshape(1, H), w2, b2.reshape(1, H), fr2.reshape(1, H), w3, dl)


def _sconv_kernel(x_ref, w_ref, b_ref, o_ref):
    x = x_ref[0]
    L = x.shape[0]
    rows = lax.broadcasted_iota(jnp.int32, x.shape, 0)
    xm = jnp.where(rows == 0, 0.0, pltpu.roll(x, 1, 0))
    xp = jnp.where(rows == L - 1, 0.0, pltpu.roll(x, L - 1, 0))
    w = w_ref[...]
    o_ref[0] = xm * w[0:1] + x * w[1:2] + xp * w[2:3] + b_ref[...]


def _short_conv(proj3, conv_w, conv_b, col0, ct=256):
    B, L, _ = proj3.shape
    C = conv_w.shape[1]
    off = col0 // ct
    return pl.pallas_call(
        _sconv_kernel,
        grid=(B, C // ct),
        in_specs=[pl.BlockSpec((1, L, ct), lambda b, c: (b, 0, c + off)),
                  pl.BlockSpec((3, ct), lambda b, c: (0, c)),
                  pl.BlockSpec((1, ct), lambda b, c: (0, c))],
        out_specs=pl.BlockSpec((1, L, ct), lambda b, c: (b, 0, c)),
        out_shape=jax.ShapeDtypeStruct((B, L, C), F32),
        compiler_params=_cparams(("parallel", "parallel")),
        name="hyena_short_conv",
    )(proj3, conv_w, conv_b.reshape(1, C))


def _dft_constants():
    n1h = FFT_N1 // 2
    k1 = np.arange(FFT_K1_PAD)[:, None].astype(np.float64)
    n1 = np.arange(n1h)[None, :].astype(np.float64)
    ang = 2.0 * np.pi * k1 * n1 / FFT_N1
    live = (np.arange(FFT_K1_PAD) < FFT_K1)[:, None]
    f1 = np.concatenate([np.where(live, np.cos(ang), 0.0), np.where(live, -np.sin(ang), 0.0)], 0)
    kk1 = np.arange(FFT_K1)[:, None, None].astype(np.float64)
    k2 = np.arange(FFT_N2)[None, :, None].astype(np.float64)
    n2 = np.arange(FFT_N2)[None, None, :].astype(np.float64)
    phi = 2.0 * np.pi * (n2 * k2 / FFT_N2 + n2 * kk1 / (FFT_N1 * FFT_N2))
    c, s = np.cos(phi), np.sin(phi)
    g = np.concatenate([np.concatenate([c, s], 2), np.concatenate([-s, c], 2)], 1)
    ct_, st_ = np.transpose(c, (0, 2, 1)), np.transpose(s, (0, 2, 1))
    ginv = np.concatenate([np.concatenate([ct_, -st_], 2), np.concatenate([st_, ct_], 2)], 1)
    wk = np.where((np.arange(FFT_K1_PAD) == 0) | (np.arange(FFT_K1_PAD) == FFT_N1 // 2), 1.0, 2.0)
    wk = np.where(np.arange(FFT_K1_PAD) < FFT_K1, wk, 0.0)[None, :] / (FFT_N1 * FFT_N2)
    angi = 2.0 * np.pi * np.arange(n1h)[:, None] * np.arange(FFT_K1_PAD)[None, :] / FFT_N1
    finv_r = wk * np.cos(angi)
    finv_i = -wk * np.sin(angi)
    as_bf = lambda a: jnp.asarray(a.astype(np.float32)).astype(BF16)
    return as_bf(f1), as_bf(g), as_bf(finv_r), as_bf(finv_i), as_bf(ginv)


def _fft_stage1(src_ref, w_ref, f1_ref):
    n1h = FFT_N1 // 2

    def body(n2, carry):
        xs = src_ref[pl.ds(n2, n1h, stride=FFT_N2), :].astype(BF16)
        r = jnp.dot(f1_ref[...], xs, preferred_element_type=F32)
        w_ref[pl.ds(n2, FFT_K1_PAD, stride=FFT_SLOT), :] = r[0:FFT_K1_PAD]
        w_ref[pl.ds(FFT_N2 + n2, FFT_K1_PAD, stride=FFT_SLOT), :] = r[FFT_K1_PAD:2 * FFT_K1_PAD]
        return carry

    lax.fori_loop(0, FFT_N2, body, 0)


def _fft_stage2(w_ref, g_ref):
    def body(k1, carry):
        base = pl.multiple_of(k1 * FFT_SLOT, FFT_SLOT)
        a = w_ref[pl.ds(base, FFT_SLOT), :].astype(BF16)
        w_ref[pl.ds(base, FFT_SLOT), :] = jnp.dot(g_ref[k1], a, preferred_element_type=F32)
        return carry

    lax.fori_loop(0, FFT_K1, body, 0)


def _kf_kernel(hf_ref, hb_ref, f1_ref, g_ref, kf_ref, w_ref, tmp_ref):
    hf = hf_ref[...]
    hb = hb_ref[...]
    l1 = jnp.sum(jnp.abs(hf), 0, keepdims=True) + jnp.sum(jnp.abs(hb), 0, keepdims=True)
    scale = 1.0 / (l1 + 1e-6)
    rows = lax.broadcasted_iota(jnp.int32, hb.shape, 0)
    tmp_ref[...] = jnp.where(rows == 0, 0.0, hb)
    out = kf_ref.at[0]
    _fft_stage1(hf_ref, out, f1_ref)
    _fft_stage2(out, g_ref)
    _fft_stage1(tmp_ref, w_ref, f1_ref)
    _fft_stage2(w_ref, g_ref)

    def body(k1, carry):
        base = pl.multiple_of(k1 * FFT_SLOT, FFT_SLOT)
        out[pl.ds(base, FFT_N2), :] = (out[pl.ds(base, FFT_N2), :] + w_ref[pl.ds(base, FFT_N2), :]) * scale
        bi = base + FFT_N2
        out[pl.ds(bi, FFT_N2), :] = (out[pl.ds(bi, FFT_N2), :] - w_ref[pl.ds(bi, FFT_N2), :]) * scale
        return carry

    lax.fori_loop(0, FFT_K1_PAD, body, 0)


def _filter_spectra(hfilt, consts):
    L = hfilt.shape[0]
    C = HY_WIDTH
    nct = C // LANES
    f1, g, _, _, _ = consts
    return pl.pallas_call(
        _kf_kernel,
        grid=(HY_ORDER, nct),
        in_specs=[pl.BlockSpec((L, LANES), lambda o, c: (0, o * 2 * nct + c)),
                  pl.BlockSpec((L, LANES), lambda o, c: (0, o * 2 * nct + nct + c)),
                  pl.BlockSpec(f1.shape, lambda o, c: (0, 0)),
                  pl.BlockSpec(g.shape, lambda o, c: (0, 0, 0))],
        out_specs=pl.BlockSpec((1, FFT_ROWS, LANES), lambda o, c: (o, 0, c)),
        out_shape=jax.ShapeDtypeStruct((HY_ORDER, FFT_ROWS, C), F32),
        scratch_shapes=[pltpu.VMEM((FFT_ROWS, LANES), F32), pltpu.VMEM((L, LANES), F32)],
        compiler_params=_cparams(("parallel", "parallel")),
        name="hyena_filter_fft",
    )(hfilt, hfilt, f1, g)


def _conv_kernel(z_ref, gate_ref, kf_ref, d_ref, f1_ref, g_ref, fir_ref, fii_ref, ginv_ref, o_ref, w_ref):
    zsrc = z_ref.at[0]
    kf = kf_ref.at[0]
    out = o_ref.at[0]
    _fft_stage1(zsrc, w_ref, f1_ref)

    def body(k1, carry):
        base = pl.multiple_of(k1 * FFT_SLOT, FFT_SLOT)
        a = w_ref[pl.ds(base, FFT_SLOT), :].astype(BF16)
        s = jnp.dot(g_ref[k1], a, preferred_element_type=F32)
        sr, si = s[0:FFT_N2], s[FFT_N2:FFT_SLOT]
        kr = kf[pl.ds(base, FFT_N2), :]
        ki = kf[pl.ds(base + FFT_N2, FFT_N2), :]
        y = jnp.concatenate([sr * kr - si * ki, sr * ki + si * kr], axis=0).astype(BF16)
        w_ref[pl.ds(base, FFT_SLOT), :] = jnp.dot(ginv_ref[k1], y, preferred_element_type=F32)
        return carry

    lax.fori_loop(0, FFT_K1, body, 0)

    n1h = FFT_N1 // 2

    def body2(n2, carry):
        zr = w_ref[pl.ds(n2, FFT_K1_PAD, stride=FFT_SLOT), :].astype(BF16)
        zi = w_ref[pl.ds(FFT_N2 + n2, FFT_K1_PAD, stride=FFT_SLOT), :].astype(BF16)
        x = (jnp.dot(fir_ref[...], zr, preferred_element_type=F32)
             + jnp.dot(fii_ref[...], zi, preferred_element_type=F32))
        out[pl.ds(n2, n1h, stride=FFT_N2), :] = x
        return carry

    lax.fori_loop(0, FFT_N2, body2, 0)
    z = z_ref[0]
    o_ref[0] = gate_ref[0] * (o_ref[0] + z * d_ref[...])


def _long_conv(z_arr, z_off, gate_arr, gate_off, kf_all, order, d, consts):
    B, L, _ = z_arr.shape
    C = HY_WIDTH
    nct = C // LANES
    f1, g, fir, fii, ginv = consts
    cst2 = lambda a: pl.BlockSpec(a.shape, lambda c, b: (0, 0))
    cst3 = lambda a: pl.BlockSpec(a.shape, lambda c, b: (0, 0, 0))
    return pl.pallas_call(
        _conv_kernel,
        grid=(nct, B),
        in_specs=[pl.BlockSpec((1, L, LANES), lambda c, b: (b, 0, c + z_off)),
                  pl.BlockSpec((1, L, LANES), lambda c, b: (b, 0, c + gate_off)),
                  pl.BlockSpec((1, FFT_ROWS, LANES), lambda c, b: (order, 0, c)),
                  pl.BlockSpec((1, LANES), lambda c, b: (0, c)),
                  cst2(f1), cst3(g), cst2(fir), cst2(fii), cst3(ginv)],
        out_specs=pl.BlockSpec((1, L, LANES), lambda c, b: (b, 0, c)),
        out_shape=jax.ShapeDtypeStruct((B, L, C), F32),
        scratch_shapes=[pltpu.VMEM((FFT_ROWS, LANES), F32)],
        compiler_params=_cparams(("parallel", "parallel")),
        name=f"hyena_long_conv{order}",
    )(z_arr, gate_arr, kf_all, d.reshape(1, C), f1, g, fir, fii, ginv)


def _mix_kernel(attn_ref, hyo_ref, h0_ref, mg_ref, w_ref, g_ref, b_ref, h1_ref, *, alpha):
    def gnorm(x, goff):
        parts = []
        for gi in range(x.shape[1] // NORM_GROUP):
            xg = x[:, gi * NORM_GROUP:(gi + 1) * NORM_GROUP]
            ms = jnp.mean(xg * xg, -1, keepdims=True)
            gg = mg_ref[:, goff + gi * NORM_GROUP: goff + (gi + 1) * NORM_GROUP]
            parts.append((xg * lax.rsqrt(ms + 1e-6) * gg).astype(BF16))
        return jnp.concatenate(parts, axis=1)

    a = gnorm(attn_ref[...], 0)
    y = gnorm(hyo_ref[...], ATTN_WIDTH)
    mix = (jnp.dot(a, w_ref[0:ATTN_WIDTH, :], preferred_element_type=F32)
           + jnp.dot(y, w_ref[ATTN_WIDTH:, :], preferred_element_type=F32))
    x = alpha * h0_ref[...] + mix
    mu = jnp.mean(x, -1, keepdims=True)
    xc = x - mu
    var = jnp.mean(xc * xc, -1, keepdims=True)
    h1_ref[...] = xc * lax.rsqrt(var + 1e-5) * g_ref[...] + b_ref[...]


def _mix_out(attn, hyo, h0, mix_g, w_out_bf, ln_g, ln_b, alpha, tm=256):
    T, D = h0.shape
    kern = functools.partial(_mix_kernel, alpha=alpha)
    return pl.pallas_call(
        kern,
        grid=(T // tm,),
        in_specs=[pl.BlockSpec((tm, ATTN_WIDTH), lambda i: (i, 0)),
                  pl.BlockSpec((tm, HY_WIDTH), lambda i: (i, 0)),
                  pl.BlockSpec((tm, D), lambda i: (i, 0)),
                  pl.BlockSpec((1, D), lambda i: (0, 0)),
                  pl.BlockSpec((D, D), lambda i: (0, 0)),
                  pl.BlockSpec((1, D), lambda i: (0, 0)),
                  pl.BlockSpec((1, D), lambda i: (0, 0))],
        out_specs=pl.BlockSpec((tm, D), lambda i: (i, 0)),
        out_shape=jax.ShapeDtypeStruct((T, D), F32),
        compiler_params=_cparams(("parallel",)),
        name="mix_out_ln1",
    )(attn, hyo, h0, mix_g.reshape(1, D), w_out_bf, ln_g.reshape(1, D), ln_b.reshape(1, D))


def _top_values(s, k):
    R = s.shape[0]
    rows = lax.broadcasted_iota(jnp.int32, s.shape, 0)
    vals = []
    for _ in range(k):
        m = jnp.max(s, axis=0, keepdims=True)
        first = jnp.min(jnp.where(s == m, rows, R), axis=0, keepdims=True)
        s = jnp.where(rows == first, -jnp.inf, s)
        vals.append(m)
    return vals


def _peer_score_kernel(wq_ref, h_ref, sk1_ref, sk2_ref, s1_ref, s2_ref, e1_ref, e2_ref, th_ref, q_ref):
    q_ref[...] = jnp.dot(wq_ref[...], h_ref[...], preferred_element_type=F32)

    def head(h, carry):
        base = pl.multiple_of(h * PEER_QDIM, PEER_QDIM)
        q1 = q_ref[pl.ds(base, PEER_QHALF), :].astype(BF16)
        q2 = q_ref[pl.ds(base + PEER_QHALF, PEER_QHALF), :].astype(BF16)
        s1 = jnp.dot(sk1_ref[...], q1, preferred_element_type=F32)
        s2 = jnp.dot(sk2_ref[...], q2, preferred_element_type=F32)
        v1 = _top_values(s1, PEER_TOPK)
        v2 = _top_values(s2, PEER_TOPK)
        v2all = jnp.concatenate(v2, axis=0)
        cand = jnp.concatenate([v1[a] + v2all for a in range(PEER_TOPK)], axis=0)
        sc = _top_values(cand, PEER_TOPK)
        zsum = sc[0] * 0.0
        for j in range(PEER_TOPK):
            zsum = zsum + jnp.exp(sc[j] - sc[0])
        s1_ref[h] = s1
        s2_ref[h] = s2
        e1_ref[h] = jnp.exp(s1 - v1[0]) / zsum
        e2_ref[h] = jnp.exp(s2 - v2[0])
        th_ref[h] = sc[PEER_TOPK - 1]
        return carry

    lax.fori_loop(0, PEER_HEADS, head, 0)


def _peer_scores(wqT_bf, h1T_bf, sk1_bf, sk2_bf, tl=256):
    D, T = h1T_bf.shape
    Q = wqT_bf.shape[0]
    big = jax.ShapeDtypeStruct((PEER_HEADS, N_KEYS, T), F32)
    bspec = pl.BlockSpec((PEER_HEADS, N_KEYS, tl), lambda i: (0, 0, i))
    return pl.pallas_call(
        _peer_score_kernel,
        grid=(T // tl,),
        in_specs=[pl.BlockSpec((Q, D), lambda i: (0, 0)),
                  pl.BlockSpec((D, tl), lambda i: (0, i)),
                  pl.BlockSpec((N_KEYS, PEER_QHALF), lambda i: (0, 0)),
                  pl.BlockSpec((N_KEYS, PEER_QHALF), lambda i: (0, 0))],
        out_specs=[bspec, bspec, bspec, bspec, pl.BlockSpec((PEER_HEADS, 1, tl), lambda i: (0, 0, i))],
        out_shape=[big, big, big, big, jax.ShapeDtypeStruct((PEER_HEADS, 1, T), F32)],
        scratch_shapes=[pltpu.VMEM((Q, tl), F32)],
        compiler_params=_cparams(("parallel",)),
        name="peer_scores_topk",
    )(wqT_bf, h1T_bf, sk1_bf, sk2_bf)


def _gelu_exact(a):
    return 0.5 * a * (1.0 + lax.erf(a * (1.0 / math.sqrt(2.0))))


def _peer_dense_kernel(hb_ref, u_ref, vt_ref, s1_ref, s2_ref, e1_ref, e2_ref, th_ref,
                       o_ref, acc_ref, w_ref, *, te, tm):
    j = pl.program_id(1)

    @pl.when(j == 0)
    def _():
        acc_ref[...] = jnp.zeros_like(acc_ref)

    a_t = jnp.dot(u_ref[...], hb_ref[...], preferred_element_type=F32)
    for r in range(te // N_KEYS):
        i1 = j * (te // N_KEYS) + r
        for c in range(tm // LANES):
            cs = slice(c * LANES, (c + 1) * LANES)
            gsum = jnp.zeros((N_KEYS, LANES), F32)
            s1rows = s1_ref[i1, :, cs]
            e1rows = e1_ref[i1, :, cs]
            for h in range(PEER_HEADS):
                s1row = s1rows[h:h + 1]
                e1row = e1rows[h:h + 1]
                th = th_ref[h:h + 1, cs]
                ssum = s2_ref[h, :, cs] + s1row
                gsum = gsum + jnp.where(ssum >= th, e2_ref[h, :, cs] * e1row, 0.0)
            a = a_t[r * N_KEYS:(r + 1) * N_KEYS, cs]
            w_ref[r * N_KEYS:(r + 1) * N_KEYS, cs] = (gsum * _gelu_exact(a)).astype(BF16)
    acc_ref[...] += jnp.dot(vt_ref[...], w_ref[...], preferred_element_type=F32)

    @pl.when(j == pl.num_programs(1) - 1)
    def _():
        o_ref[...] = acc_ref[...]


def _peer_dense(h1T_bf, u_bf, vT_bf, s1, s2, e1, e2, th, tm=512, te=512):
    D, T = h1T_bf.shape
    E = u_bf.shape[0]
    kern = functools.partial(_peer_dense_kernel, te=te, tm=tm)
    sspec = pl.BlockSpec((PEER_HEADS, N_KEYS, tm), lambda i, j: (0, 0, i))
    xspec = pl.BlockSpec((N_KEYS, PEER_HEADS, tm), lambda i, j: (0, 0, i))
    s1 = jnp.transpose(s1, (1, 0, 2))
    e1 = jnp.transpose(e1, (1, 0, 2))
    return pl.pallas_call(
        kern,
        grid=(T // tm, E // te),
        in_specs=[pl.BlockSpec((D, tm), lambda i, j: (0, i)),
                  pl.BlockSpec((te, D), lambda i, j: (j, 0)),
                  pl.BlockSpec((D, te), lambda i, j: (0, j)),
                  xspec, sspec, xspec, sspec,
                  pl.BlockSpec((PEER_HEADS, tm), lambda i, j: (0, i))],
        out_specs=pl.BlockSpec((D, tm), lambda i, j: (0, i)),
        out_shape=jax.ShapeDtypeStruct((D, T), F32),
        scratch_shapes=[pltpu.VMEM((D, tm), F32), pltpu.VMEM((te, tm), BF16)],
        compiler_params=_cparams(("parallel", "arbitrary")),
        name="peer_dense",
    )(h1T_bf, u_bf, vT_bf, s1, s2, e1, e2, th)


def _res_ln_kernel(h_ref, f_ref, g_ref, b_ref, o_ref, *, alpha):
    x = alpha * h_ref[...] + f_ref[...]
    mu = jnp.mean(x, -1, keepdims=True)
    xc = x - mu
    var = jnp.mean(xc * xc, -1, keepdims=True)
    o_ref[...] = xc * lax.rsqrt(var + 1e-5) * g_ref[...] + b_ref[...]


def _res_ln(h, f, g, b, alpha, tm=512):
    T, D = h.shape
    row = pl.BlockSpec((tm, D), lambda i: (i, 0))
    vec = pl.BlockSpec((1, D), lambda i: (0, 0))
    return pl.pallas_call(
        functools.partial(_res_ln_kernel, alpha=alpha),
        grid=(T // tm,),
        in_specs=[row, row, vec, vec],
        out_specs=row,
        out_shape=jax.ShapeDtypeStruct((T, D), F32),
        compiler_params=_cparams(("parallel",)),
        name="res_ln2",
    )(h, f, g.reshape(1, D), b.reshape(1, D))


def kernel(x, ln0_g, ln0_b, rel_bias, w_in, sink, conv_w, conv_b, f_w1, f_b1, f_freq1, f_w2, f_b2,
           f_freq2, f_w3, hy_bias, mix_norm_g, w_out, ln1_g, ln1_b, peer_wq, peer_subkeys, peer_u,
           peer_v, ln2_g, ln2_b):
    B, S, D = x.shape
    T = B * S
    alpha = (2.0 * DEPTH) ** 0.25
    consts = _dft_constants()

    qi = jnp.arange(BLOCK, dtype=jnp.int32)
    kj = jnp.arange(3 * BLOCK, dtype=jnp.int32)
    rel = kj[None, :] - BLOCK - qi[:, None]
    bias = jnp.transpose(rel_bias.astype(F32)[_t5_bucket(rel)], (2, 0, 1))

    h, h_bf = _ln0(x.reshape(T, D), ln0_g, ln0_b)
    for l in range(DEPTH):
        proj = _matmul(h_bf, w_in[l].astype(BF16), tm=512, tn=1536)
        attn = _attention(proj, bias, sink[l], B, S)

        hfilt = _filters(S, f_w1[l], f_b1[l], f_freq1[l], f_w2[l], f_b2[l], f_freq2[l], f_w3[l])
        kf = _filter_spectra(hfilt, consts)
        u = _short_conv(proj.reshape(B, S, -1), conv_w[l], conv_b[l], ATTN_WIDTH + 2 * KV_WIDTH)
        nct = HY_WIDTH // LANES
        z1 = _long_conv(u, 0, u, nct, kf, 0, hy_bias[l, 0], consts)
        hyo = _long_conv(z1, 0, u, 2 * nct, kf, 1, hy_bias[l, 1], consts)

        h1 = _mix_out(attn, hyo.reshape(T, HY_WIDTH), h, mix_norm_g[l], w_out[l].astype(BF16),
                      ln1_g[l], ln1_b[l], alpha)

        h1T = h1.T
        h1T_bf = h1T.astype(BF16)
        s1, s2, e1, e2, th = _peer_scores(peer_wq[l].T.astype(BF16), h1T_bf,
                                          peer_subkeys[l, 0].astype(BF16), peer_subkeys[l, 1].astype(BF16))
        ffnT = _peer_dense(h1T_bf, peer_u[l].astype(BF16), peer_v[l].T.astype(BF16),
                           s1, s2, e1, e2, th.reshape(PEER_HEADS, T))
        h = _res_ln(h1, ffnT.T, ln2_g[l], ln2_b[l], alpha)
        if l + 1 < DEPTH:
            h_bf = h.astype(BF16)
    return h.reshape(B, S, D)
```

```python
import functools
import math

import numpy as np
import jax
import jax.numpy as jnp
from jax import lax
from jax.experimental import pallas as pl
from jax.experimental.pallas import tpu as pltpu

F32 = jnp.float32
BF16 = jnp.bfloat16

D_MODEL = 2048
HEAD_DIM = 128
N_Q_HEADS = 8
N_KV_HEADS = 2
GQA_GROUP = N_Q_HEADS // N_KV_HEADS
ATTN_WIDTH = N_Q_HEADS * HEAD_DIM
KV_WIDTH = N_KV_HEADS * HEAD_DIM
WINDOW = 128
BLOCK = 128
N_BUCKETS = 32
MAX_DISTANCE = 128
HY_WIDTH = D_MODEL - ATTN_WIDTH
HY_ORDER = 2
POS_BANDS = 16
POS_EMB = 1 + 2 * POS_BANDS
FILTER_HIDDEN = 64
FAST_DECAY_PCT = 0.3
SLOW_DECAY_PCT = 1.5
DECAY_TARGET = 1e-2
NORM_GROUP = 128
N_KEYS = 128
PEER_HEADS = 8
PEER_QDIM = 256
PEER_QHALF = PEER_QDIM // 2
PEER_TOPK = 16
NEG = -1e30
DEPTH = 1

LANES = 128
VMEM_LIMIT = 56 * 1024 * 1024

FFT_N1 = 64
FFT_N2 = 128
FFT_K1 = FFT_N1 // 2 + 1
FFT_K1_PAD = 40
FFT_SLOT = 2 * FFT_N2
KF_ROWS = FFT_K1 * FFT_SLOT
SUBLANES = 8
W_PITCH = FFT_SLOT + SUBLANES
X_PITCH = FFT_N2 + SUBLANES
W_ROWS = FFT_K1_PAD * W_PITCH
X_ROWS = (FFT_N1 // 2) * X_PITCH


def _cparams(sem, vmem=VMEM_LIMIT):
    return pltpu.CompilerParams(dimension_semantics=sem, vmem_limit_bytes=vmem)


def _ln0_kernel(x_ref, g_ref, b_ref, h_ref, hb_ref):
    x = x_ref[...]
    mu = jnp.mean(x, -1, keepdims=True)
    xc = x - mu
    var = jnp.mean(xc * xc, -1, keepdims=True)
    y = xc * lax.rsqrt(var + 1e-5) * g_ref[...] + b_ref[...]
    h_ref[...] = y
    hb_ref[...] = y.astype(BF16)


def _ln0(x2d, g, b, tm=512):
    T, D = x2d.shape
    return pl.pallas_call(
        _ln0_kernel,
        grid=(T // tm,),
        in_specs=[pl.BlockSpec((tm, D), lambda i: (i, 0)),
                  pl.BlockSpec((1, D), lambda i: (0, 0)),
                  pl.BlockSpec((1, D), lambda i: (0, 0))],
        out_specs=[pl.BlockSpec((tm, D), lambda i: (i, 0)),
                   pl.BlockSpec((tm, D), lambda i: (i, 0))],
        out_shape=[jax.ShapeDtypeStruct((T, D), F32), jax.ShapeDtypeStruct((T, D), BF16)],
        compiler_params=_cparams(("parallel",)),
        name="ln0",
    )(x2d, g.reshape(1, D), b.reshape(1, D))


def _mm_kernel(a_ref, b_ref, o_ref):
    o_ref[...] = jnp.dot(a_ref[...], b_ref[...], preferred_element_type=F32)


def _matmul(a, b, tm, tn):
    M, K = a.shape
    N = b.shape[1]
    return pl.pallas_call(
        _mm_kernel,
        grid=(N // tn, M // tm),
        in_specs=[pl.BlockSpec((tm, K), lambda j, i: (i, 0)),
                  pl.BlockSpec((K, tn), lambda j, i: (0, j))],
        out_specs=pl.BlockSpec((tm, tn), lambda j, i: (i, j)),
        out_shape=jax.ShapeDtypeStruct((M, N), F32),
        compiler_params=_cparams(("parallel", "parallel")),
        name="in_proj",
    )(a, b)


def _t5_bucket(rel):
    nb = N_BUCKETS // 2
    ret = (rel > 0).astype(jnp.int32) * nb
    n = jnp.abs(rel)
    max_exact = nb // 2
    nf = jnp.maximum(n, 1).astype(F32)
    large = max_exact + (jnp.log(nf / max_exact) / math.log(MAX_DISTANCE / max_exact)
                         * (nb - max_exact)).astype(jnp.int32)
    large = jnp.minimum(large, nb - 1)
    return ret + jnp.where(n < max_exact, n, large)


def _attn_kernel(sink_ref, q_ref, kp_ref, kc_ref, kn_ref, vp_ref, vc_ref, vn_ref, bias_ref, o_ref,
                 *, nb, seq):
    n = pl.program_id(0) % nb
    qi = lax.broadcasted_iota(jnp.int32, (BLOCK, 3 * BLOCK), 0)
    kj = lax.broadcasted_iota(jnp.int32, (BLOCK, 3 * BLOCK), 1)
    rel = kj - BLOCK - qi
    kabs = n * BLOCK + kj - BLOCK
    valid = (jnp.abs(rel) <= WINDOW) & (kabs >= 0) & (kabs < seq)
    k = jnp.concatenate([kp_ref[...], kc_ref[...], kn_ref[...]], axis=0).astype(BF16)
    v = jnp.concatenate([vp_ref[...], vc_ref[...], vn_ref[...]], axis=0).astype(BF16)
    scale = 1.0 / math.sqrt(HEAD_DIM)
    for g in range(N_KV_HEADS):
        kg = k[:, g * HEAD_DIM:(g + 1) * HEAD_DIM]
        vg = v[:, g * HEAD_DIM:(g + 1) * HEAD_DIM]
        for r in range(GQA_GROUP):
            h = g * GQA_GROUP + r
            qh = q_ref[:, h * HEAD_DIM:(h + 1) * HEAD_DIM].astype(BF16)
            s = lax.dot_general(qh, kg, (((1,), (1,)), ((), ())), preferred_element_type=F32) * scale
            s = jnp.where(valid, s + bias_ref[h], NEG)
            sk = sink_ref[h]
            m = jnp.maximum(jnp.max(s, -1, keepdims=True), sk)
            p = jnp.exp(s - m)
            denom = jnp.sum(p, -1, keepdims=True) + jnp.exp(sk - m)
            p = p / denom
            o_ref[:, h * HEAD_DIM:(h + 1) * HEAD_DIM] = jnp.dot(p.astype(BF16), vg,
                                                                 preferred_element_type=F32)


def _attention(proj, bias, sink, batch, seq):
    T = proj.shape[0]
    nb = seq // BLOCK
    kcol = ATTN_WIDTH // KV_WIDTH
    vcol = kcol + 1

    def prev(i):
        return i - jnp.where(i % nb == 0, 0, 1)

    def nxt(i):
        return i + jnp.where(i % nb == nb - 1, 0, 1)

    kern = functools.partial(_attn_kernel, nb=nb, seq=seq)
    return pl.pallas_call(
        kern,
        grid=(T // BLOCK,),
        in_specs=[pl.BlockSpec(memory_space=pltpu.SMEM),
                  pl.BlockSpec((BLOCK, ATTN_WIDTH), lambda i: (i, 0)),
                  pl.BlockSpec((BLOCK, KV_WIDTH), lambda i: (prev(i), kcol)),
                  pl.BlockSpec((BLOCK, KV_WIDTH), lambda i: (i, kcol)),
                  pl.BlockSpec((BLOCK, KV_WIDTH), lambda i: (nxt(i), kcol)),
                  pl.BlockSpec((BLOCK, KV_WIDTH), lambda i: (prev(i), vcol)),
                  pl.BlockSpec((BLOCK, KV_WIDTH), lambda i: (i, vcol)),
                  pl.BlockSpec((BLOCK, KV_WIDTH), lambda i: (nxt(i), vcol)),
                  pl.BlockSpec((N_Q_HEADS, BLOCK, 3 * BLOCK), lambda i: (0, 0, 0))],
        out_specs=pl.BlockSpec((BLOCK, ATTN_WIDTH), lambda i: (i, 0)),
        out_shape=jax.ShapeDtypeStruct((T, ATTN_WIDTH), F32),
        compiler_params=_cparams(("parallel",)),
        name="window_attn",
    )(sink, proj, proj, proj, proj, proj, proj, proj, bias)


def _filter_kernel(z_ref, w1_ref, b1_ref, fr1_ref, w2_ref, b2_ref, fr2_ref, w3_ref, dl_ref, o_ref,
                   *, tt, seq):
    hi = lax.Precision.HIGHEST
    a = jnp.dot(z_ref[...], w1_ref[...], preferred_element_type=F32, precision=hi) + b1_ref[...]
    hid = jnp.sin(fr1_ref[...] * a)
    a = jnp.dot(hid, w2_ref[...], preferred_element_type=F32, precision=hi) + b2_ref[...]
    hid = jnp.sin(fr2_ref[...] * a)
    h = jnp.dot(hid, w3_ref[...], preferred_element_type=F32, precision=hi)
    rows = lax.broadcasted_iota(jnp.int32, h.shape, 0) + pl.program_id(0) * tt
    tn = rows.astype(F32) / float(max(seq - 1, 1))
    o_ref[...] = h * jnp.exp(-tn * dl_ref[...])


def _filters(seq, w1, b1, fr1, w2, b2, fr2, w3, tt=512):
    t = jnp.arange(seq, dtype=F32)
    tn = t / max(seq - 1, 1)
    w = 2.0 * math.pi * t / seq
    bands = jnp.linspace(1e-4, POS_BANDS - 1, POS_BANDS, dtype=F32)
    z = jnp.concatenate([tn[:, None], jnp.cos(w[:, None] * bands), -jnp.sin(w[:, None] * bands)], -1)
    zp = jnp.pad(z, ((0, 0), (0, FILTER_HIDDEN - POS_EMB)))
    w1p = jnp.pad(w1, ((0, FILTER_HIDDEN - POS_EMB), (0, 0)))
    max_decay = math.log(DECAY_TARGET) / FAST_DECAY_PCT
    min_decay = math.log(DECAY_TARGET) / SLOW_DECAY_PCT
    deltas = jnp.abs(jnp.linspace(min_decay, max_decay, HY_WIDTH, dtype=F32))
    ncol = HY_ORDER * 2 * HY_WIDTH
    dl = jnp.tile(deltas, HY_ORDER * 2).reshape(1, ncol)
    H = FILTER_HIDDEN
    kern = functools.partial(_filter_kernel, tt=tt, seq=seq)
    full = lambda r, c: pl.BlockSpec((r, c), lambda i: (0, 0))
    return pl.pallas_call(
        kern,
        grid=(seq // tt,),
        in_specs=[pl.BlockSpec((tt, H), lambda i: (i, 0)),
                  full(H, H), full(1, H), full(1, H), full(H, H), full(1, H), full(1, H),
                  full(H, ncol), full(1, ncol)],
        out_specs=pl.BlockSpec((tt, ncol), lambda i: (i, 0)),
        out_shape=jax.ShapeDtypeStruct((seq, ncol), F32),
        compiler_params=_cparams(("parallel",)),
        name="hyena_filter_mlp",
    )(zp, w1p, b1.reshape(1, H), fr1.reshape(1, H), w2, b2.reshape(1, H), fr2.reshape(1, H), w3, dl)


def _sconv_kernel(x_ref, w_ref, b_ref, o_ref):
    x = x_ref[0]
    L = x.shape[0]
    rows = lax.broadcasted_iota(jnp.int32, x.shape, 0)
    xm = jnp.where(rows == 0, 0.0, pltpu.roll(x, 1, 0))
    xp = jnp.where(rows == L - 1, 0.0, pltpu.roll(x, L - 1, 0))
    w = w_ref[...]
    o_ref[0] = xm * w[0:1] + x * w[1:2] + xp * w[2:3] + b_ref[...]


def _short_conv(proj3, conv_w, conv_b, col0, ct=256):
    B, L, _ = proj3.shape
    C = conv_w.shape[1]
    off = col0 // ct
    return pl.pallas_call(
        _sconv_kernel,
        grid=(B, C // ct),
        in_specs=[pl.BlockSpec((1, L, ct), lambda b, c: (b, 0, c + off)),
                  pl.BlockSpec((3, ct), lambda b, c: (0, c)),
                  pl.BlockSpec((1, ct), lambda b, c: (0, c))],
        out_specs=pl.BlockSpec((1, L, ct), lambda b, c: (b, 0, c)),
        out_shape=jax.ShapeDtypeStruct((B, L, C), F32),
        compiler_params=_cparams(("parallel", "parallel")),
        name="hyena_short_conv",
    )(proj3, conv_w, conv_b.reshape(1, C))


def _dft_constants():
    n1h = FFT_N1 // 2
    k1 = np.arange(FFT_K1_PAD)[:, None].astype(np.float64)
    n1 = np.arange(n1h)[None, :].astype(np.float64)
    ang = 2.0 * np.pi * k1 * n1 / FFT_N1
    live = (np.arange(FFT_K1_PAD) < FFT_K1)[:, None]
    f1 = np.concatenate([np.where(live, np.cos(ang), 0.0), np.where(live, -np.sin(ang), 0.0)], 0)
    kk1 = np.arange(FFT_K1)[:, None, None].astype(np.float64)
    k2 = np.arange(FFT_N2)[None, :, None].astype(np.float64)
    n2 = np.arange(FFT_N2)[None, None, :].astype(np.float64)
    phi = 2.0 * np.pi * (n2 * k2 / FFT_N2 + n2 * kk1 / (FFT_N1 * FFT_N2))
    c, s = np.cos(phi), np.sin(phi)
    g = np.concatenate([np.concatenate([c, s], 2), np.concatenate([-s, c], 2)], 1)
    ct_, st_ = np.transpose(c, (0, 2, 1)), np.transpose(s, (0, 2, 1))
    ginv = np.concatenate([np.concatenate([ct_, -st_], 2), np.concatenate([st_, ct_], 2)], 1)
    wk = np.where((np.arange(FFT_K1_PAD) == 0) | (np.arange(FFT_K1_PAD) == FFT_N1 // 2), 1.0, 2.0)
    wk = np.where(np.arange(FFT_K1_PAD) < FFT_K1, wk, 0.0)[None, :] / (FFT_N1 * FFT_N2)
    angi = 2.0 * np.pi * np.arange(n1h)[:, None] * np.arange(FFT_K1_PAD)[None, :] / FFT_N1
    finv = np.concatenate([wk * np.cos(angi), -wk * np.sin(angi)], 1)
    as_bf = lambda a: jnp.asarray(a.astype(np.float32)).astype(BF16)
    return as_bf(f1), as_bf(g), as_bf(finv), as_bf(ginv)


def _pad_rows_in(src_ref, xp_ref, zero_first_row=False):
    for n1 in range(FFT_N1 // 2):
        blk = src_ref[pl.ds(n1 * FFT_N2, FFT_N2), :]
        if zero_first_row and n1 == 0:
            rows = lax.broadcasted_iota(jnp.int32, blk.shape, 0)
            blk = jnp.where(rows == 0, 0.0, blk)
        xp_ref[pl.ds(n1 * X_PITCH, FFT_N2), :] = blk


def _fft_stage1(xp_ref, w_ref, f1_ref):
    n1h = FFT_N1 // 2
    kp = FFT_K1_PAD

    def body(i, carry):
        n2 = 2 * i
        xa = xp_ref[pl.ds(n2, n1h, stride=X_PITCH), :]
        xb = xp_ref[pl.ds(n2 + 1, n1h, stride=X_PITCH), :]
        xs = jnp.concatenate([xa, xb], axis=1).astype(BF16)
        r = jnp.dot(f1_ref[...], xs, preferred_element_type=F32)
        w_ref[pl.ds(n2, kp, stride=W_PITCH), :] = r[0:kp, 0:LANES]
        w_ref[pl.ds(n2 + 1, kp, stride=W_PITCH), :] = r[0:kp, LANES:2 * LANES]
        w_ref[pl.ds(FFT_N2 + n2, kp, stride=W_PITCH), :] = r[kp:2 * kp, 0:LANES]
        w_ref[pl.ds(FFT_N2 + n2 + 1, kp, stride=W_PITCH), :] = r[kp:2 * kp, LANES:2 * LANES]
        return carry

    lax.fori_loop(0, FFT_N2 // 2, body, 0, unroll=4)


def _kf_kernel(hf_ref, hb_ref, f1_ref, g_ref, kf_ref, wf_ref, wb_ref, xp_ref):
    l1 = (jnp.sum(jnp.abs(hf_ref[...]), 0, keepdims=True)
          + jnp.sum(jnp.abs(hb_ref[...]), 0, keepdims=True))
    scale = 1.0 / (l1 + 1e-6)
    _pad_rows_in(hf_ref, xp_ref)
    _fft_stage1(xp_ref, wf_ref, f1_ref)
    _pad_rows_in(hb_ref, xp_ref, zero_first_row=True)
    _fft_stage1(xp_ref, wb_ref, f1_ref)
    out = kf_ref.at[0]

    def body(k1, carry):
        src = pl.multiple_of(k1 * W_PITCH, 8)
        dst = pl.multiple_of(k1 * FFT_SLOT, FFT_SLOT)
        gk = g_ref[k1]
        sf = jnp.dot(gk, wf_ref[pl.ds(src, FFT_SLOT), :].astype(BF16), preferred_element_type=F32)
        sb = jnp.dot(gk, wb_ref[pl.ds(src, FFT_SLOT), :].astype(BF16), preferred_element_type=F32)
        out[pl.ds(dst, FFT_N2), :] = (sf[0:FFT_N2] + sb[0:FFT_N2]) * scale
        out[pl.ds(dst + FFT_N2, FFT_N2), :] = (sf[FFT_N2:FFT_SLOT] - sb[FFT_N2:FFT_SLOT]) * scale
        return carry

    lax.fori_loop(0, FFT_K1, body, 0, unroll=3)


def _filter_spectra(hfilt, consts):
    L = hfilt.shape[0]
    C = HY_WIDTH
    nct = C // LANES
    f1, g, _, _ = consts
    return pl.pallas_call(
        _kf_kernel,
        grid=(HY_ORDER, nct),
        in_specs=[pl.BlockSpec((L, LANES), lambda o, c: (0, o * 2 * nct + c)),
                  pl.BlockSpec((L, LANES), lambda o, c: (0, o * 2 * nct + nct + c)),
                  pl.BlockSpec(f1.shape, lambda o, c: (0, 0)),
                  pl.BlockSpec(g.shape, lambda o, c: (0, 0, 0))],
        out_specs=pl.BlockSpec((1, KF_ROWS, LANES), lambda o, c: (o, 0, c)),
        out_shape=jax.ShapeDtypeStruct((HY_ORDER, KF_ROWS, C), F32),
        scratch_shapes=[pltpu.VMEM((W_ROWS, LANES), F32), pltpu.VMEM((W_ROWS, LANES), F32),
                        pltpu.VMEM((X_ROWS, LANES), F32)],
        compiler_params=_cparams(("parallel", "parallel")),
        name="hyena_filter_fft",
    )(hfilt, hfilt, f1, g)


def _conv_kernel(z_ref, gate_ref, kf_ref, d_ref, f1_ref, g_ref, finv_ref, ginv_ref, o_ref, w_ref, xp_ref):
    zsrc = z_ref.at[0]
    kf = kf_ref.at[0]
    _pad_rows_in(zsrc, xp_ref)
    _fft_stage1(xp_ref, w_ref, f1_ref)

    def body(k1, carry):
        base = pl.multiple_of(k1 * W_PITCH, 8)
        kbase = pl.multiple_of(k1 * FFT_SLOT, FFT_SLOT)
        a = w_ref[pl.ds(base, FFT_SLOT), :].astype(BF16)
        s = jnp.dot(g_ref[k1], a, preferred_element_type=F32)
        sr, si = s[0:FFT_N2], s[FFT_N2:FFT_SLOT]
        kr = kf[pl.ds(kbase, FFT_N2), :]
        ki = kf[pl.ds(kbase + FFT_N2, FFT_N2), :]
        y = jnp.concatenate([sr * kr - si * ki, sr * ki + si * kr], axis=0).astype(BF16)
        w_ref[pl.ds(base, FFT_SLOT), :] = jnp.dot(ginv_ref[k1], y, preferred_element_type=F32)
        return carry

    lax.fori_loop(0, FFT_K1, body, 0, unroll=3)

    n1h = FFT_N1 // 2
    kp = FFT_K1_PAD

    def body2(i, carry):
        n2 = 2 * i
        zr = jnp.concatenate([w_ref[pl.ds(n2, kp, stride=W_PITCH), :],
                              w_ref[pl.ds(n2 + 1, kp, stride=W_PITCH), :]], axis=1)
        zi = jnp.concatenate([w_ref[pl.ds(FFT_N2 + n2, kp, stride=W_PITCH), :],
                              w_ref[pl.ds(FFT_N2 + n2 + 1, kp, stride=W_PITCH), :]], axis=1)
        zz = jnp.concatenate([zr, zi], axis=0).astype(BF16)
        x = jnp.dot(finv_ref[...], zz, preferred_element_type=F32)
        xp_ref[pl.ds(n2, n1h, stride=X_PITCH), :] = x[:, 0:LANES]
        xp_ref[pl.ds(n2 + 1, n1h, stride=X_PITCH), :] = x[:, LANES:2 * LANES]
        return carry

    lax.fori_loop(0, FFT_N2 // 2, body2, 0, unroll=4)
    d = d_ref[...]
    for n1 in range(n1h):
        rs = pl.ds(n1 * FFT_N2, FFT_N2)
        y = xp_ref[pl.ds(n1 * X_PITCH, FFT_N2), :]
        o_ref[0, rs, :] = gate_ref[0, rs, :] * (y + z_ref[0, rs, :] * d)


def _long_conv(z_arr, z_off, gate_arr, gate_off, kf_all, order, d, consts):
    B, L, _ = z_arr.shape
    C = HY_WIDTH
    nct = C // LANES
    f1, g, finv, ginv = consts
    cst2 = lambda a: pl.BlockSpec(a.shape, lambda c, b: (0, 0))
    cst3 = lambda a: pl.BlockSpec(a.shape, lambda c, b: (0, 0, 0))
    return pl.pallas_call(
        _conv_kernel,
        grid=(nct, B),
        in_specs=[pl.BlockSpec((1, L, LANES), lambda c, b: (b, 0, c + z_off)),
                  pl.BlockSpec((1, L, LANES), lambda c, b: (b, 0, c + gate_off)),
                  pl.BlockSpec((1, KF_ROWS, LANES), lambda c, b: (order, 0, c)),
                  pl.BlockSpec((1, LANES), lambda c, b: (0, c)),
                  cst2(f1), cst3(g), cst2(finv), cst3(ginv)],
        out_specs=pl.BlockSpec((1, L, LANES), lambda c, b: (b, 0, c)),
        out_shape=jax.ShapeDtypeStruct((B, L, C), F32),
        scratch_shapes=[pltpu.VMEM((W_ROWS, LANES), F32), pltpu.VMEM((X_ROWS, LANES), F32)],
        compiler_params=_cparams(("parallel", "parallel")),
        name=f"hyena_long_conv{order}",
    )(z_arr, gate_arr, kf_all, d.reshape(1, C), f1, g, finv, ginv)


def _mix_kernel(attn_ref, hyo_ref, h0_ref, mg_ref, w_ref, g_ref, b_ref, h1_ref, *, alpha):
    def gnorm(x, goff):
        parts = []
        for gi in range(x.shape[1] // NORM_GROUP):
            xg = x[:, gi * NORM_GROUP:(gi + 1) * NORM_GROUP]
            ms = jnp.mean(xg * xg, -1, keepdims=True)
            gg = mg_ref[:, goff + gi * NORM_GROUP: goff + (gi + 1) * NORM_GROUP]
            parts.append((xg * lax.rsqrt(ms + 1e-6) * gg).astype(BF16))
        return jnp.concatenate(parts, axis=1)

    a = gnorm(attn_ref[...], 0)
    y = gnorm(hyo_ref[...], ATTN_WIDTH)
    mix = (jnp.dot(a, w_ref[0:ATTN_WIDTH, :], preferred_element_type=F32)
           + jnp.dot(y, w_ref[ATTN_WIDTH:, :], preferred_element_type=F32))
    x = alpha * h0_ref[...] + mix
    mu = jnp.mean(x, -1, keepdims=True)
    xc = x - mu
    var = jnp.mean(xc * xc, -1, keepdims=True)
    h1_ref[...] = xc * lax.rsqrt(var + 1e-5) * g_ref[...] + b_ref[...]


def _mix_out(attn, hyo, h0, mix_g, w_out_bf, ln_g, ln_b, alpha, tm=256):
    T, D = h0.shape
    kern = functools.partial(_mix_kernel, alpha=alpha)
    return pl.pallas_call(
        kern,
        grid=(T // tm,),
        in_specs=[pl.BlockSpec((tm, ATTN_WIDTH), lambda i: (i, 0)),
                  pl.BlockSpec((tm, HY_WIDTH), lambda i: (i, 0)),
                  pl.BlockSpec((tm, D), lambda i: (i, 0)),
                  pl.BlockSpec((1, D), lambda i: (0, 0)),
                  pl.BlockSpec((D, D), lambda i: (0, 0)),
                  pl.BlockSpec((1, D), lambda i: (0, 0)),
                  pl.BlockSpec((1, D), lambda i: (0, 0))],
        out_specs=pl.BlockSpec((tm, D), lambda i: (i, 0)),
        out_shape=jax.ShapeDtypeStruct((T, D), F32),
        compiler_params=_cparams(("parallel",)),
        name="mix_out_ln1",
    )(attn, hyo, h0, mix_g.reshape(1, D), w_out_bf, ln_g.reshape(1, D), ln_b.reshape(1, D))


def _top_values(s, k):
    R = s.shape[0]
    rows = lax.broadcasted_iota(jnp.int32, s.shape, 0)
    vals = []
    for _ in range(k):
        m = jnp.max(s, axis=0, keepdims=True)
        first = jnp.min(jnp.where(s == m, rows, R), axis=0, keepdims=True)
        s = jnp.where(rows == first, -jnp.inf, s)
        vals.append(m)
    return vals


def _peer_score_kernel(wq_ref, h_ref, sk1_ref, sk2_ref, s1_ref, s2_ref, e1_ref, e2_ref, th_ref, q_ref):
    q_ref[...] = jnp.dot(wq_ref[...], h_ref[...], preferred_element_type=F32)

    def head(h, carry):
        base = pl.multiple_of(h * PEER_QDIM, PEER_QDIM)
        q1 = q_ref[pl.ds(base, PEER_QHALF), :].astype(BF16)
        q2 = q_ref[pl.ds(base + PEER_QHALF, PEER_QHALF), :].astype(BF16)
        s1 = jnp.dot(sk1_ref[...], q1, preferred_element_type=F32)
        s2 = jnp.dot(sk2_ref[...], q2, preferred_element_type=F32)
        v1 = _top_values(s1, PEER_TOPK)
        v2 = _top_values(s2, PEER_TOPK)
        v2all = jnp.concatenate(v2, axis=0)
        cand = jnp.concatenate([v1[a] + v2all for a in range(PEER_TOPK)], axis=0)
        sc = _top_values(cand, PEER_TOPK)
        zsum = sc[0] * 0.0
        for j in range(PEER_TOPK):
            zsum = zsum + jnp.exp(sc[j] - sc[0])
        s1_ref[h] = s1
        s2_ref[h] = s2
        e1_ref[h] = jnp.exp(s1 - v1[0]) / zsum
        e2_ref[h] = jnp.exp(s2 - v2[0])
        th_ref[h] = sc[PEER_TOPK - 1]
        return carry

    lax.fori_loop(0, PEER_HEADS, head, 0)


def _peer_scores(wqT_bf, h1T_bf, sk1_bf, sk2_bf, tl=256):
    D, T = h1T_bf.shape
    Q = wqT_bf.shape[0]
    big = jax.ShapeDtypeStruct((PEER_HEADS, N_KEYS, T), F32)
    bspec = pl.BlockSpec((PEER_HEADS, N_KEYS, tl), lambda i: (0, 0, i))
    return pl.pallas_call(
        _peer_score_kernel,
        grid=(T // tl,),
        in_specs=[pl.BlockSpec((Q, D), lambda i: (0, 0)),
                  pl.BlockSpec((D, tl), lambda i: (0, i)),
                  pl.BlockSpec((N_KEYS, PEER_QHALF), lambda i: (0, 0)),
                  pl.BlockSpec((N_KEYS, PEER_QHALF), lambda i: (0, 0))],
        out_specs=[bspec, bspec, bspec, bspec, pl.BlockSpec((PEER_HEADS, 1, tl), lambda i: (0, 0, i))],
        out_shape=[big, big, big, big, jax.ShapeDtypeStruct((PEER_HEADS, 1, T), F32)],
        scratch_shapes=[pltpu.VMEM((Q, tl), F32)],
        compiler_params=_cparams(("parallel",)),
        name="peer_scores_topk",
    )(wqT_bf, h1T_bf, sk1_bf, sk2_bf)


def _gelu_exact(a):
    return 0.5 * a * (1.0 + lax.erf(a * (1.0 / math.sqrt(2.0))))


def _peer_dense_kernel(hb_ref, u_ref, vt_ref, s1_ref, s2_ref, e1_ref, e2_ref, th_ref,
                       o_ref, acc_ref, w_ref, *, te, tm):
    j = pl.program_id(1)

    @pl.when(j == 0)
    def _():
        acc_ref[...] = jnp.zeros_like(acc_ref)

    a_t = jnp.dot(u_ref[...], hb_ref[...], preferred_element_type=F32)
    for r in range(te // N_KEYS):
        i1 = j * (te // N_KEYS) + r
        for c in range(tm // LANES):
            cs = slice(c * LANES, (c + 1) * LANES)
            gsum = jnp.zeros((N_KEYS, LANES), F32)
            s1rows = s1_ref[i1, :, cs]
            e1rows = e1_ref[i1, :, cs]
            for h in range(PEER_HEADS):
                s1row = s1rows[h:h + 1]
                e1row = e1rows[h:h + 1]
                th = th_ref[h:h + 1, cs]
                ssum = s2_ref[h, :, cs] + s1row
                gsum = gsum + jnp.where(ssum >= th, e2_ref[h, :, cs] * e1row, 0.0)
            a = a_t[r * N_KEYS:(r + 1) * N_KEYS, cs]
            w_ref[r * N_KEYS:(r + 1) * N_KEYS, cs] = (gsum * _gelu_exact(a)).astype(BF16)
    acc_ref[...] += jnp.dot(vt_ref[...], w_ref[...], preferred_element_type=F32)

    @pl.when(j == pl.num_programs(1) - 1)
    def _():
        o_ref[...] = acc_ref[...]


def _peer_dense(h1T_bf, u_bf, vT_bf, s1, s2, e1, e2, th, tm=512, te=512):
    D, T = h1T_bf.shape
    E = u_bf.shape[0]
    kern = functools.partial(_peer_dense_kernel, te=te, tm=tm)
    sspec = pl.BlockSpec((PEER_HEADS, N_KEYS, tm), lambda i, j: (0, 0, i))
    xspec = pl.BlockSpec((N_KEYS, PEER_HEADS, tm), lambda i, j: (0, 0, i))
    s1 = jnp.transpose(s1, (1, 0, 2))
    e1 = jnp.transpose(e1, (1, 0, 2))
    return pl.pallas_call(
        kern,
        grid=(T // tm, E // te),
        in_specs=[pl.BlockSpec((D, tm), lambda i, j: (0, i)),
                  pl.BlockSpec((te, D), lambda i, j: (j, 0)),
                  pl.BlockSpec((D, te), lambda i, j: (0, j)),
                  xspec, sspec, xspec, sspec,
                  pl.BlockSpec((PEER_HEADS, tm), lambda i, j: (0, i))],
        out_specs=pl.BlockSpec((D, tm), lambda i, j: (0, i)),
        out_shape=jax.ShapeDtypeStruct((D, T), F32),
        scratch_shapes=[pltpu.VMEM((D, tm), F32), pltpu.VMEM((te, tm), BF16)],
        compiler_params=_cparams(("parallel", "arbitrary")),
        name="peer_dense",
    )(h1T_bf, u_bf, vT_bf, s1, s2, e1, e2, th)


def _res_ln_kernel(h_ref, f_ref, g_ref, b_ref, o_ref, *, alpha):
    x = alpha * h_ref[...] + f_ref[...]
    mu = jnp.mean(x, -1, keepdims=True)
    xc = x - mu
    var = jnp.mean(xc * xc, -1, keepdims=True)
    o_ref[...] = xc * lax.rsqrt(var + 1e-5) * g_ref[...] + b_ref[...]


def _res_ln(h, f, g, b, alpha, tm=512):
    T, D = h.shape
    row = pl.BlockSpec((tm, D), lambda i: (i, 0))
    vec = pl.BlockSpec((1, D), lambda i: (0, 0))
    return pl.pallas_call(
        functools.partial(_res_ln_kernel, alpha=alpha),
        grid=(T // tm,),
        in_specs=[row, row, vec, vec],
        out_specs=row,
        out_shape=jax.ShapeDtypeStruct((T, D), F32),
        compiler_params=_cparams(("parallel",)),
        name="res_ln2",
    )(h, f, g.reshape(1, D), b.reshape(1, D))


def kernel(x, ln0_g, ln0_b, rel_bias, w_in, sink, conv_w, conv_b, f_w1, f_b1, f_freq1, f_w2, f_b2,
           f_freq2, f_w3, hy_bias, mix_norm_g, w_out, ln1_g, ln1_b, peer_wq, peer_subkeys, peer_u,
           peer_v, ln2_g, ln2_b):
    B, S, D = x.shape
    T = B * S
    alpha = (2.0 * DEPTH) ** 0.25
    consts = _dft_constants()

    qi = jnp.arange(BLOCK, dtype=jnp.int32)
    kj = jnp.arange(3 * BLOCK, dtype=jnp.int32)
    rel = kj[None, :] - BLOCK - qi[:, None]
    bias = jnp.transpose(rel_bias.astype(F32)[_t5_bucket(rel)], (2, 0, 1))

    h, h_bf = _ln0(x.reshape(T, D), ln0_g, ln0_b)
    for l in range(DEPTH):
        proj = _matmul(h_bf, w_in[l].astype(BF16), tm=512, tn=1536)
        attn = _attention(proj, bias, sink[l], B, S)

        hfilt = _filters(S, f_w1[l], f_b1[l], f_freq1[l], f_w2[l], f_b2[l], f_freq2[l], f_w3[l])
        kf = _filter_spectra(hfilt, consts)
        u = _short_conv(proj.reshape(B, S, -1), conv_w[l], conv_b[l], ATTN_WIDTH + 2 * KV_WIDTH)
        nct = HY_WIDTH // LANES
        z1 = _long_conv(u, 0, u, nct, kf, 0, hy_bias[l, 0], consts)
        hyo = _long_conv(z1, 0, u, 2 * nct, kf, 1, hy_bias[l, 1], consts)

        h1 = _mix_out(attn, hyo.reshape(T, HY_WIDTH), h, mix_norm_g[l], w_out[l].astype(BF16),
                      ln1_g[l], ln1_b[l], alpha)

        h1T = h1.T
        h1T_bf = h1T.astype(BF16)
        s1, s2, e1, e2, th = _peer_scores(peer_wq[l].T.astype(BF16), h1T_bf,
                                          peer_subkeys[l, 0].astype(BF16), peer_subkeys[l, 1].astype(BF16))
        ffnT = _peer_dense(h1T_bf, peer_u[l].astype(BF16), peer_v[l].T.astype(BF16),
                           s1, s2, e1, e2, th.reshape(PEER_HEADS, T))
        h = _res_ln(h1, ffnT.T, ln2_g[l], ln2_b[l], alpha)
        if l + 1 < DEPTH:
            h_bf = h.astype(BF16)
    return h.reshape(B, S, D)
```

```python
import functools
import math

import numpy as np
import jax
import jax.numpy as jnp
from jax import lax
from jax.experimental import pallas as pl
from jax.experimental.pallas import tpu as pltpu

F32 = jnp.float32
BF16 = jnp.bfloat16

D_MODEL = 2048
HEAD_DIM = 128
N_Q_HEADS = 8
N_KV_HEADS = 2
GQA_GROUP = N_Q_HEADS // N_KV_HEADS
ATTN_WIDTH = N_Q_HEADS * HEAD_DIM
KV_WIDTH = N_KV_HEADS * HEAD_DIM
WINDOW = 128
BLOCK = 128
N_BUCKETS = 32
MAX_DISTANCE = 128
HY_WIDTH = D_MODEL - ATTN_WIDTH
HY_ORDER = 2
POS_BANDS = 16
POS_EMB = 1 + 2 * POS_BANDS
FILTER_HIDDEN = 64
FAST_DECAY_PCT = 0.3
SLOW_DECAY_PCT = 1.5
DECAY_TARGET = 1e-2
NORM_GROUP = 128
N_KEYS = 128
PEER_HEADS = 8
PEER_QDIM = 256
PEER_QHALF = PEER_QDIM // 2
PEER_TOPK = 16
NEG = -1e30
DEPTH = 1

LANES = 128
VMEM_LIMIT = 56 * 1024 * 1024

FFT_N1 = 64
FFT_N2 = 128
FFT_K1 = FFT_N1 // 2 + 1
FFT_K1_PAD = 40
FFT_SLOT = 2 * FFT_N2
KF_ROWS = FFT_K1 * FFT_SLOT
SUBLANES = 8
W_PITCH = FFT_SLOT + SUBLANES
X_PITCH = FFT_N2 + SUBLANES
W_ROWS = FFT_K1_PAD * W_PITCH
X_ROWS = (FFT_N1 // 2) * X_PITCH


def _cparams(sem, vmem=VMEM_LIMIT):
    return pltpu.CompilerParams(dimension_semantics=sem, vmem_limit_bytes=vmem)


def _ln0_kernel(x_ref, g_ref, b_ref, h_ref, hb_ref):
    x = x_ref[...]
    mu = jnp.mean(x, -1, keepdims=True)
    xc = x - mu
    var = jnp.mean(xc * xc, -1, keepdims=True)
    y = xc * lax.rsqrt(var + 1e-5) * g_ref[...] + b_ref[...]
    h_ref[...] = y
    hb_ref[...] = y.astype(BF16)


def _ln0(x2d, g, b, tm=512):
    T, D = x2d.shape
    return pl.pallas_call(
        _ln0_kernel,
        grid=(T // tm,),
        in_specs=[pl.BlockSpec((tm, D), lambda i: (i, 0)),
                  pl.BlockSpec((1, D), lambda i: (0, 0)),
                  pl.BlockSpec((1, D), lambda i: (0, 0))],
        out_specs=[pl.BlockSpec((tm, D), lambda i: (i, 0)),
                   pl.BlockSpec((tm, D), lambda i: (i, 0))],
        out_shape=[jax.ShapeDtypeStruct((T, D), F32), jax.ShapeDtypeStruct((T, D), BF16)],
        compiler_params=_cparams(("parallel",)),
        name="ln0",
    )(x2d, g.reshape(1, D), b.reshape(1, D))


def _mm_kernel(a_ref, b_ref, o_ref):
    o_ref[...] = jnp.dot(a_ref[...], b_ref[...], preferred_element_type=F32)


def _matmul(a, b, tm, tn):
    M, K = a.shape
    N = b.shape[1]
    return pl.pallas_call(
        _mm_kernel,
        grid=(N // tn, M // tm),
        in_specs=[pl.BlockSpec((tm, K), lambda j, i: (i, 0)),
                  pl.BlockSpec((K, tn), lambda j, i: (0, j))],
        out_specs=pl.BlockSpec((tm, tn), lambda j, i: (i, j)),
        out_shape=jax.ShapeDtypeStruct((M, N), F32),
        compiler_params=_cparams(("parallel", "parallel")),
        name="in_proj",
    )(a, b)


def _t5_bucket(rel):
    nb = N_BUCKETS // 2
    ret = (rel > 0).astype(jnp.int32) * nb
    n = jnp.abs(rel)
    max_exact = nb // 2
    nf = jnp.maximum(n, 1).astype(F32)
    large = max_exact + (jnp.log(nf / max_exact) / math.log(MAX_DISTANCE / max_exact)
                         * (nb - max_exact)).astype(jnp.int32)
    large = jnp.minimum(large, nb - 1)
    return ret + jnp.where(n < max_exact, n, large)


def _attn_kernel(sink_ref, q_ref, kp_ref, kc_ref, kn_ref, vp_ref, vc_ref, vn_ref, bias_ref, o_ref,
                 *, nb, seq):
    n = pl.program_id(0) % nb
    qi = lax.broadcasted_iota(jnp.int32, (BLOCK, 3 * BLOCK), 0)
    kj = lax.broadcasted_iota(jnp.int32, (BLOCK, 3 * BLOCK), 1)
    rel = kj - BLOCK - qi
    kabs = n * BLOCK + kj - BLOCK
    valid = (jnp.abs(rel) <= WINDOW) & (kabs >= 0) & (kabs < seq)
    k = jnp.concatenate([kp_ref[...], kc_ref[...], kn_ref[...]], axis=0).astype(BF16)
    v = jnp.concatenate([vp_ref[...], vc_ref[...], vn_ref[...]], axis=0).astype(BF16)
    scale = 1.0 / math.sqrt(HEAD_DIM)
    for g in range(N_KV_HEADS):
        kg = k[:, g * HEAD_DIM:(g + 1) * HEAD_DIM]
        vg = v[:, g * HEAD_DIM:(g + 1) * HEAD_DIM]
        for r in range(GQA_GROUP):
            h = g * GQA_GROUP + r
            qh = q_ref[:, h * HEAD_DIM:(h + 1) * HEAD_DIM].astype(BF16)
            s = lax.dot_general(qh, kg, (((1,), (1,)), ((), ())), preferred_element_type=F32) * scale
            s = jnp.where(valid, s + bias_ref[h], NEG)
            sk = sink_ref[h]
            m = jnp.maximum(jnp.max(s, -1, keepdims=True), sk)
            p = jnp.exp(s - m)
            denom = jnp.sum(p, -1, keepdims=True) + jnp.exp(sk - m)
            p = p / denom
            o_ref[:, h * HEAD_DIM:(h + 1) * HEAD_DIM] = jnp.dot(p.astype(BF16), vg,
                                                                 preferred_element_type=F32)


def _attention(proj, bias, sink, batch, seq):
    T = proj.shape[0]
    nb = seq // BLOCK
    kcol = ATTN_WIDTH // KV_WIDTH
    vcol = kcol + 1

    def prev(i):
        return i - jnp.where(i % nb == 0, 0, 1)

    def nxt(i):
        return i + jnp.where(i % nb == nb - 1, 0, 1)

    kern = functools.partial(_attn_kernel, nb=nb, seq=seq)
    return pl.pallas_call(
        kern,
        grid=(T // BLOCK,),
        in_specs=[pl.BlockSpec(memory_space=pltpu.SMEM),
                  pl.BlockSpec((BLOCK, ATTN_WIDTH), lambda i: (i, 0)),
                  pl.BlockSpec((BLOCK, KV_WIDTH), lambda i: (prev(i), kcol)),
                  pl.BlockSpec((BLOCK, KV_WIDTH), lambda i: (i, kcol)),
                  pl.BlockSpec((BLOCK, KV_WIDTH), lambda i: (nxt(i), kcol)),
                  pl.BlockSpec((BLOCK, KV_WIDTH), lambda i: (prev(i), vcol)),
                  pl.BlockSpec((BLOCK, KV_WIDTH), lambda i: (i, vcol)),
                  pl.BlockSpec((BLOCK, KV_WIDTH), lambda i: (nxt(i), vcol)),
                  pl.BlockSpec((N_Q_HEADS, BLOCK, 3 * BLOCK), lambda i: (0, 0, 0))],
        out_specs=pl.BlockSpec((BLOCK, ATTN_WIDTH), lambda i: (i, 0)),
        out_shape=jax.ShapeDtypeStruct((T, ATTN_WIDTH), F32),
        compiler_params=_cparams(("parallel",)),
        name="window_attn",
    )(sink, proj, proj, proj, proj, proj, proj, proj, bias)


def _filter_kernel(z_ref, w1_ref, b1_ref, fr1_ref, w2_ref, b2_ref, fr2_ref, w3_ref, dl_ref, o_ref,
                   *, tt, seq):
    hi = lax.Precision.HIGHEST
    a = jnp.dot(z_ref[...], w1_ref[...], preferred_element_type=F32, precision=hi) + b1_ref[...]
    hid = jnp.sin(fr1_ref[...] * a)
    a = jnp.dot(hid, w2_ref[...], preferred_element_type=F32, precision=hi) + b2_ref[...]
    hid = jnp.sin(fr2_ref[...] * a)
    h = jnp.dot(hid, w3_ref[...], preferred_element_type=F32, precision=hi)
    rows = lax.broadcasted_iota(jnp.int32, h.shape, 0) + pl.program_id(0) * tt
    tn = rows.astype(F32) / float(max(seq - 1, 1))
    o_ref[...] = h * jnp.exp(-tn * dl_ref[...])


def _filters(seq, w1, b1, fr1, w2, b2, fr2, w3, tt=512):
    t = jnp.arange(seq, dtype=F32)
    tn = t / max(seq - 1, 1)
    w = 2.0 * math.pi * t / seq
    bands = jnp.linspace(1e-4, POS_BANDS - 1, POS_BANDS, dtype=F32)
    z = jnp.concatenate([tn[:, None], jnp.cos(w[:, None] * bands), -jnp.sin(w[:, None] * bands)], -1)
    zp = jnp.pad(z, ((0, 0), (0, FILTER_HIDDEN - POS_EMB)))
    w1p = jnp.pad(w1, ((0, FILTER_HIDDEN - POS_EMB), (0, 0)))
    max_decay = math.log(DECAY_TARGET) / FAST_DECAY_PCT
    min_decay = math.log(DECAY_TARGET) / SLOW_DECAY_PCT
    deltas = jnp.abs(jnp.linspace(min_decay, max_decay, HY_WIDTH, dtype=F32))
    ncol = HY_ORDER * 2 * HY_WIDTH
    dl = jnp.tile(deltas, HY_ORDER * 2).reshape(1, ncol)
    H = FILTER_HIDDEN
    kern = functools.partial(_filter_kernel, tt=tt, seq=seq)
    full = lambda r, c: pl.BlockSpec((r, c), lambda i: (0, 0))
    return pl.pallas_call(
        kern,
        grid=(seq // tt,),
        in_specs=[pl.BlockSpec((tt, H), lambda i: (i, 0)),
                  full(H, H), full(1, H), full(1, H), full(H, H), full(1, H), full(1, H),
                  full(H, ncol), full(1, ncol)],
        out_specs=pl.BlockSpec((tt, ncol), lambda i: (i, 0)),
        out_shape=jax.ShapeDtypeStruct((seq, ncol), F32),
        compiler_params=_cparams(("parallel",)),
        name="hyena_filter_mlp",
    )(zp, w1p, b1.reshape(1, H), fr1.reshape(1, H), w2, b2.reshape(1, H), fr2.reshape(1, H), w3, dl)


def _sconv_kernel(x_ref, w_ref, b_ref, o_ref):
    x = x_ref[0]
    L = x.shape[0]
    rows = lax.broadcasted_iota(jnp.int32, x.shape, 0)
    xm = jnp.where(rows == 0, 0.0, pltpu.roll(x, 1, 0))
    xp = jnp.where(rows == L - 1, 0.0, pltpu.roll(x, L - 1, 0))
    w = w_ref[...]
    o_ref[0] = xm * w[0:1] + x * w[1:2] + xp * w[2:3] + b_ref[...]


def _short_conv(proj3, conv_w, conv_b, col0, ct=256):
    B, L, _ = proj3.shape
    C = conv_w.shape[1]
    off = col0 // ct
    return pl.pallas_call(
        _sconv_kernel,
        grid=(B, C // ct),
        in_specs=[pl.BlockSpec((1, L, ct), lambda b, c: (b, 0, c + off)),
                  pl.BlockSpec((3, ct), lambda b, c: (0, c)),
                  pl.BlockSpec((1, ct), lambda b, c: (0, c))],
        out_specs=pl.BlockSpec((1, L, ct), lambda b, c: (b, 0, c)),
        out_shape=jax.ShapeDtypeStruct((B, L, C), F32),
        compiler_params=_cparams(("parallel", "parallel")),
        name="hyena_short_conv",
    )(proj3, conv_w, conv_b.reshape(1, C))


def _dft_constants():
    n1h = FFT_N1 // 2
    k1 = np.arange(FFT_K1_PAD)[:, None].astype(np.float64)
    n1 = np.arange(n1h)[None, :].astype(np.float64)
    ang = 2.0 * np.pi * k1 * n1 / FFT_N1
    live = (np.arange(FFT_K1_PAD) < FFT_K1)[:, None]
    f1 = np.concatenate([np.where(live, np.cos(ang), 0.0), np.where(live, -np.sin(ang), 0.0)], 0)
    kk1 = np.arange(FFT_K1)[:, None, None].astype(np.float64)
    k2 = np.arange(FFT_N2)[None, :, None].astype(np.float64)
    n2 = np.arange(FFT_N2)[None, None, :].astype(np.float64)
    phi = 2.0 * np.pi * (n2 * k2 / FFT_N2 + n2 * kk1 / (FFT_N1 * FFT_N2))
    c, s = np.cos(phi), np.sin(phi)
    g = np.concatenate([np.concatenate([c, s], 2), np.concatenate([-s, c], 2)], 1)
    ct_, st_ = np.transpose(c, (0, 2, 1)), np.transpose(s, (0, 2, 1))
    ginv = np.concatenate([np.concatenate([ct_, -st_], 2), np.concatenate([st_, ct_], 2)], 1)
    wk = np.where((np.arange(FFT_K1_PAD) == 0) | (np.arange(FFT_K1_PAD) == FFT_N1 // 2), 1.0, 2.0)
    wk = np.where(np.arange(FFT_K1_PAD) < FFT_K1, wk, 0.0)[None, :] / (FFT_N1 * FFT_N2)
    angi = 2.0 * np.pi * np.arange(n1h)[:, None] * np.arange(FFT_K1_PAD)[None, :] / FFT_N1
    finv = np.concatenate([wk * np.cos(angi), -wk * np.sin(angi)], 1)
    as_bf = lambda a: jnp.asarray(a.astype(np.float32)).astype(BF16)
    return as_bf(f1), as_bf(g), as_bf(finv), as_bf(ginv)


def _pad_rows_in(src_ref, xp_ref, zero_first_row=False):
    for n1 in range(FFT_N1 // 2):
        blk = src_ref[pl.ds(n1 * FFT_N2, FFT_N2), :]
        if zero_first_row and n1 == 0:
            rows = lax.broadcasted_iota(jnp.int32, blk.shape, 0)
            blk = jnp.where(rows == 0, 0.0, blk)
        xp_ref[pl.ds(n1 * X_PITCH, FFT_N2), :] = blk


def _fft_stage1(xp_ref, w_ref, f1_ref):
    n1h = FFT_N1 // 2
    kp = FFT_K1_PAD

    def body(i, carry):
        n2 = 2 * i
        xa = xp_ref[pl.ds(n2, n1h, stride=X_PITCH), :]
        xb = xp_ref[pl.ds(n2 + 1, n1h, stride=X_PITCH), :]
        xs = jnp.concatenate([xa, xb], axis=1).astype(BF16)
        r = jnp.dot(f1_ref[...], xs, preferred_element_type=F32)
        w_ref[pl.ds(n2, kp, stride=W_PITCH), :] = r[0:kp, 0:LANES]
        w_ref[pl.ds(n2 + 1, kp, stride=W_PITCH), :] = r[0:kp, LANES:2 * LANES]
        w_ref[pl.ds(FFT_N2 + n2, kp, stride=W_PITCH), :] = r[kp:2 * kp, 0:LANES]
        w_ref[pl.ds(FFT_N2 + n2 + 1, kp, stride=W_PITCH), :] = r[kp:2 * kp, LANES:2 * LANES]
        return carry

    lax.fori_loop(0, FFT_N2 // 2, body, 0, unroll=4)


def _kf_kernel(hf_ref, hb_ref, f1_ref, g_ref, kf_ref, wf_ref, wb_ref, xp_ref):
    l1 = (jnp.sum(jnp.abs(hf_ref[...]), 0, keepdims=True)
          + jnp.sum(jnp.abs(hb_ref[...]), 0, keepdims=True))
    scale = 1.0 / (l1 + 1e-6)
    _pad_rows_in(hf_ref, xp_ref)
    _fft_stage1(xp_ref, wf_ref, f1_ref)
    _pad_rows_in(hb_ref, xp_ref, zero_first_row=True)
    _fft_stage1(xp_ref, wb_ref, f1_ref)
    out = kf_ref.at[0]

    def body(k1, carry):
        src = pl.multiple_of(k1 * W_PITCH, 8)
        dst = pl.multiple_of(k1 * FFT_SLOT, FFT_SLOT)
        gk = g_ref[k1]
        sf = jnp.dot(gk, wf_ref[pl.ds(src, FFT_SLOT), :].astype(BF16), preferred_element_type=F32)
        sb = jnp.dot(gk, wb_ref[pl.ds(src, FFT_SLOT), :].astype(BF16), preferred_element_type=F32)
        out[pl.ds(dst, FFT_N2), :] = (sf[0:FFT_N2] + sb[0:FFT_N2]) * scale
        out[pl.ds(dst + FFT_N2, FFT_N2), :] = (sf[FFT_N2:FFT_SLOT] - sb[FFT_N2:FFT_SLOT]) * scale
        return carry

    lax.fori_loop(0, FFT_K1, body, 0, unroll=3)


def _filter_spectra(hfilt, consts):
    L = hfilt.shape[0]
    C = HY_WIDTH
    nct = C // LANES
    f1, g, _, _ = consts
    return pl.pallas_call(
        _kf_kernel,
        grid=(HY_ORDER, nct),
        in_specs=[pl.BlockSpec((L, LANES), lambda o, c: (0, o * 2 * nct + c)),
                  pl.BlockSpec((L, LANES), lambda o, c: (0, o * 2 * nct + nct + c)),
                  pl.BlockSpec(f1.shape, lambda o, c: (0, 0)),
                  pl.BlockSpec(g.shape, lambda o, c: (0, 0, 0))],
        out_specs=pl.BlockSpec((1, KF_ROWS, LANES), lambda o, c: (o, 0, c)),
        out_shape=jax.ShapeDtypeStruct((HY_ORDER, KF_ROWS, C), F32),
        scratch_shapes=[pltpu.VMEM((W_ROWS, LANES), F32), pltpu.VMEM((W_ROWS, LANES), F32),
                        pltpu.VMEM((X_ROWS, LANES), F32)],
        compiler_params=_cparams(("parallel", "parallel")),
        name="hyena_filter_fft",
    )(hfilt, hfilt, f1, g)


def _conv_kernel(z_ref, gate_ref, kf_ref, d_ref, f1_ref, g_ref, finv_ref, ginv_ref, o_ref, w_ref, xp_ref):
    zsrc = z_ref.at[0]
    kf = kf_ref.at[0]
    _pad_rows_in(zsrc, xp_ref)
    _fft_stage1(xp_ref, w_ref, f1_ref)

    def body(k1, carry):
        base = pl.multiple_of(k1 * W_PITCH, 8)
        kbase = pl.multiple_of(k1 * FFT_SLOT, FFT_SLOT)
        a = w_ref[pl.ds(base, FFT_SLOT), :].astype(BF16)
        s = jnp.dot(g_ref[k1], a, preferred_element_type=F32)
        sr, si = s[0:FFT_N2], s[FFT_N2:FFT_SLOT]
        kr = kf[pl.ds(kbase, FFT_N2), :]
        ki = kf[pl.ds(kbase + FFT_N2, FFT_N2), :]
        y = jnp.concatenate([sr * kr - si * ki, sr * ki + si * kr], axis=0).astype(BF16)
        w_ref[pl.ds(base, FFT_SLOT), :] = jnp.dot(ginv_ref[k1], y, preferred_element_type=F32)
        return carry

    lax.fori_loop(0, FFT_K1, body, 0, unroll=3)

    n1h = FFT_N1 // 2
    kp = FFT_K1_PAD

    def body2(i, carry):
        n2 = 2 * i
        zr = jnp.concatenate([w_ref[pl.ds(n2, kp, stride=W_PITCH), :],
                              w_ref[pl.ds(n2 + 1, kp, stride=W_PITCH), :]], axis=1)
        zi = jnp.concatenate([w_ref[pl.ds(FFT_N2 + n2, kp, stride=W_PITCH), :],
                              w_ref[pl.ds(FFT_N2 + n2 + 1, kp, stride=W_PITCH), :]], axis=1)
        zz = jnp.concatenate([zr, zi], axis=0).astype(BF16)
        x = jnp.dot(finv_ref[...], zz, preferred_element_type=F32)
        xp_ref[pl.ds(n2, n1h, stride=X_PITCH), :] = x[:, 0:LANES]
        xp_ref[pl.ds(n2 + 1, n1h, stride=X_PITCH), :] = x[:, LANES:2 * LANES]
        return carry

    lax.fori_loop(0, FFT_N2 // 2, body2, 0, unroll=4)
    d = d_ref[...]
    for n1 in range(n1h):
        rs = pl.ds(n1 * FFT_N2, FFT_N2)
        y = xp_ref[pl.ds(n1 * X_PITCH, FFT_N2), :]
        o_ref[0, rs, :] = gate_ref[0, rs, :] * (y + z_ref[0, rs, :] * d)


def _long_conv(z_arr, z_off, gate_arr, gate_off, kf_all, order, d, consts):
    B, L, _ = z_arr.shape
    C = HY_WIDTH
    nct = C // LANES
    f1, g, finv, ginv = consts
    cst2 = lambda a: pl.BlockSpec(a.shape, lambda c, b: (0, 0))
    cst3 = lambda a: pl.BlockSpec(a.shape, lambda c, b: (0, 0, 0))
    return pl.pallas_call(
        _conv_kernel,
        grid=(nct, B),
        in_specs=[pl.BlockSpec((1, L, LANES), lambda c, b: (b, 0, c + z_off)),
                  pl.BlockSpec((1, L, LANES), lambda c, b: (b, 0, c + gate_off)),
                  pl.BlockSpec((1, KF_ROWS, LANES), lambda c, b: (order, 0, c)),
                  pl.BlockSpec((1, LANES), lambda c, b: (0, c)),
                  cst2(f1), cst3(g), cst2(finv), cst3(ginv)],
        out_specs=pl.BlockSpec((1, L, LANES), lambda c, b: (b, 0, c)),
        out_shape=jax.ShapeDtypeStruct((B, L, C), F32),
        scratch_shapes=[pltpu.VMEM((W_ROWS, LANES), F32), pltpu.VMEM((X_ROWS, LANES), F32)],
        compiler_params=_cparams(("parallel", "parallel")),
        name=f"hyena_long_conv{order}",
    )(z_arr, gate_arr, kf_all, d.reshape(1, C), f1, g, finv, ginv)


def _mix_kernel(attn_ref, hyo_ref, h0_ref, mg_ref, w_ref, g_ref, b_ref, h1_ref, *, alpha):
    def gnorm(x, goff):
        parts = []
        for gi in range(x.shape[1] // NORM_GROUP):
            xg = x[:, gi * NORM_GROUP:(gi + 1) * NORM_GROUP]
            ms = jnp.mean(xg * xg, -1, keepdims=True)
            gg = mg_ref[:, goff + gi * NORM_GROUP: goff + (gi + 1) * NORM_GROUP]
            parts.append((xg * lax.rsqrt(ms + 1e-6) * gg).astype(BF16))
        return jnp.concatenate(parts, axis=1)

    a = gnorm(attn_ref[...], 0)
    y = gnorm(hyo_ref[...], ATTN_WIDTH)
    mix = (jnp.dot(a, w_ref[0:ATTN_WIDTH, :], preferred_element_type=F32)
           + jnp.dot(y, w_ref[ATTN_WIDTH:, :], preferred_element_type=F32))
    x = alpha * h0_ref[...] + mix
    mu = jnp.mean(x, -1, keepdims=True)
    xc = x - mu
    var = jnp.mean(xc * xc, -1, keepdims=True)
    h1_ref[...] = xc * lax.rsqrt(var + 1e-5) * g_ref[...] + b_ref[...]


def _mix_out(attn, hyo, h0, mix_g, w_out_bf, ln_g, ln_b, alpha, tm=256):
    T, D = h0.shape
    kern = functools.partial(_mix_kernel, alpha=alpha)
    return pl.pallas_call(
        kern,
        grid=(T // tm,),
        in_specs=[pl.BlockSpec((tm, ATTN_WIDTH), lambda i: (i, 0)),
                  pl.BlockSpec((tm, HY_WIDTH), lambda i: (i, 0)),
                  pl.BlockSpec((tm, D), lambda i: (i, 0)),
                  pl.BlockSpec((1, D), lambda i: (0, 0)),
                  pl.BlockSpec((D, D), lambda i: (0, 0)),
                  pl.BlockSpec((1, D), lambda i: (0, 0)),
                  pl.BlockSpec((1, D), lambda i: (0, 0))],
        out_specs=pl.BlockSpec((tm, D), lambda i: (i, 0)),
        out_shape=jax.ShapeDtypeStruct((T, D), F32),
        compiler_params=_cparams(("parallel",)),
        name="mix_out_ln1",
    )(attn, hyo, h0, mix_g.reshape(1, D), w_out_bf, ln_g.reshape(1, D), ln_b.reshape(1, D))


def _top_values(s, k):
    R = s.shape[0]
    rows = lax.broadcasted_iota(jnp.int32, s.shape, 0)
    vals = []
    for _ in range(k):
        m = jnp.max(s, axis=0, keepdims=True)
        first = jnp.min(jnp.where(s == m, rows, R), axis=0, keepdims=True)
        s = jnp.where(rows == first, -jnp.inf, s)
        vals.append(m)
    return vals


def _peer_score_kernel(wq_ref, h_ref, sk1_ref, sk2_ref, s1_ref, s2_ref, e1_ref, e2_ref, th_ref, q_ref):
    q_ref[...] = jnp.dot(wq_ref[...], h_ref[...], preferred_element_type=F32)

    def head(h, carry):
        base = pl.multiple_of(h * PEER_QDIM, PEER_QDIM)
        q1 = q_ref[pl.ds(base, PEER_QHALF), :].astype(BF16)
        q2 = q_ref[pl.ds(base + PEER_QHALF, PEER_QHALF), :].astype(BF16)
        s1 = jnp.dot(sk1_ref[...], q1, preferred_element_type=F32)
        s2 = jnp.dot(sk2_ref[...], q2, preferred_element_type=F32)
        v1 = _top_values(s1, PEER_TOPK)
        v2 = _top_values(s2, PEER_TOPK)
        v2all = jnp.concatenate(v2, axis=0)
        cand = jnp.concatenate([v1[a] + v2all for a in range(PEER_TOPK)], axis=0)
        sc = _top_values(cand, PEER_TOPK)
        zsum = sc[0] * 0.0
        for j in range(PEER_TOPK):
            zsum = zsum + jnp.exp(sc[j] - sc[0])
        th_ref[h] = sc[PEER_TOPK - 1]
        s1_ref[h] = s1
        s2_ref[h] = s2
        e1_ref[h] = jnp.exp(s1 - v1[0]) / zsum
        e2_ref[h] = jnp.exp(s2 - v2[0])
        return carry

    lax.fori_loop(0, PEER_HEADS, head, 0)


def _peer_scores(wqT_bf, h1T_bf, sk1_bf, sk2_bf, tl=256):
    D, T = h1T_bf.shape
    Q = wqT_bf.shape[0]
    big = jax.ShapeDtypeStruct((PEER_HEADS, N_KEYS, T), F32)
    bspec = pl.BlockSpec((PEER_HEADS, N_KEYS, tl), lambda i: (0, 0, i))
    return pl.pallas_call(
        _peer_score_kernel,
        grid=(T // tl,),
        in_specs=[pl.BlockSpec((Q, D), lambda i: (0, 0)),
                  pl.BlockSpec((D, tl), lambda i: (0, i)),
                  pl.BlockSpec((N_KEYS, PEER_QHALF), lambda i: (0, 0)),
                  pl.BlockSpec((N_KEYS, PEER_QHALF), lambda i: (0, 0))],
        out_specs=[bspec, bspec, bspec, bspec, pl.BlockSpec((PEER_HEADS, 1, tl), lambda i: (0, 0, i))],
        out_shape=[big, big, big, big, jax.ShapeDtypeStruct((PEER_HEADS, 1, T), F32)],
        scratch_shapes=[pltpu.VMEM((Q, tl), F32)],
        compiler_params=_cparams(("parallel",)),
        name="peer_scores_topk",
    )(wqT_bf, h1T_bf, sk1_bf, sk2_bf)


W_ROWS_PER_GROUP = 32


def _gelu_exact(a):
    return 0.5 * a * (1.0 + lax.erf(a * (1.0 / math.sqrt(2.0))))


def _peer_dense_kernel(hb_ref, u_ref, vt_ref, s1_ref, s2_ref, e1_ref, e2_ref, th_ref,
                       o_ref, acc_ref, a0_ref, a1_ref, w0_ref, w1_ref, *, te, tm, n_j, n_tiles):
    s = pl.program_id(0)
    sv = jnp.clip(s - 2, 0, n_tiles - 1)
    jv = sv % n_j
    jw = jnp.clip(s - 1, 0, n_tiles - 1) % n_j

    @pl.when(s == 0)
    def _():
        for ref in (a0_ref, a1_ref, w0_ref, w1_ref):
            ref[...] = jnp.zeros_like(ref)

    @pl.when(jv == 0)
    def _():
        acc_ref[...] = jnp.zeros_like(acc_ref)

    wide = 2 * LANES
    n_r, n_c = te // N_KEYS, tm // LANES
    d_model = acc_ref.shape[0]

    def body(a_cur, a_prv, w_cur, w_prv):
        def stage_a(q):
            ms = slice((q // 2) * wide, (q // 2 + 1) * wide)
            ls = slice((q % 2) * wide, (q % 2 + 1) * wide)
            a_cur[ms, ls] = jnp.dot(u_ref[ms, :], hb_ref[:, ls], preferred_element_type=F32)

        def stage_v(p):
            fs = slice((p // 2) * (d_model // 8), (p // 2 + 1) * (d_model // 8))
            ls = slice((p % 2) * wide, (p % 2 + 1) * wide)
            acc_ref[fs, ls] += jnp.dot(vt_ref[fs, :], w_cur[:, ls], preferred_element_type=F32)

        n_g = N_KEYS // W_ROWS_PER_GROUP

        def stage_w(idx):
            p, g = idx // n_g, idx % n_g
            r, c = p // n_c, p % n_c
            i1 = jw * n_r + r
            cs = slice(c * LANES, (c + 1) * LANES)
            s1rows = s1_ref[i1, :, cs]
            e1rows = e1_ref[i1, :, cs]
            throws = th_ref[:, cs]
            ks = slice(g * W_ROWS_PER_GROUP, (g + 1) * W_ROWS_PER_GROUP)
            gsum = jnp.zeros((W_ROWS_PER_GROUP, LANES), F32)
            for h in range(PEER_HEADS):
                sel = (s2_ref[h, ks, cs] + s1rows[h:h + 1]) >= throws[h:h + 1]
                gsum = gsum + jnp.where(sel, e2_ref[h, ks, cs] * e1rows[h:h + 1], 0.0)
            ws = slice(r * N_KEYS + g * W_ROWS_PER_GROUP, r * N_KEYS + (g + 1) * W_ROWS_PER_GROUP)
            w_prv[ws, cs] = (gsum * _gelu_exact(a_prv[ws, cs])).astype(BF16)

        for q in range(4):
            stage_a(q)
            for t in range(4):
                stage_v(4 * q + t)
                for gg in range(4):
                    stage_w(16 * q + 4 * t + gg)

    @pl.when(s % 2 == 0)
    def _():
        body(a0_ref, a1_ref, w0_ref, w1_ref)

    @pl.when(s % 2 == 1)
    def _():
        body(a1_ref, a0_ref, w1_ref, w0_ref)

    @pl.when(jv == n_j - 1)
    def _():
        o_ref[...] = acc_ref[...]


def _peer_dense(h1T_bf, u_bf, vT_bf, s1, s2, e1, e2, th, tm=512, te=512):
    D, T = h1T_bf.shape
    E = u_bf.shape[0]
    assert te == 4 * N_KEYS and tm == 4 * LANES, "the stage interleave is written for 512 x 512 tiles"
    n_i, n_j = T // tm, E // te
    n_tiles = n_i * n_j
    kern = functools.partial(_peer_dense_kernel, te=te, tm=tm, n_j=n_j, n_tiles=n_tiles)
    ta = lambda s: jnp.minimum(s, n_tiles - 1)
    tw = lambda s: jnp.clip(s - 1, 0, n_tiles - 1)
    tv = lambda s: jnp.clip(s - 2, 0, n_tiles - 1)
    sspec = pl.BlockSpec((PEER_HEADS, N_KEYS, tm), lambda s: (0, 0, tw(s) // n_j))
    xspec = pl.BlockSpec((N_KEYS, PEER_HEADS, tm), lambda s: (0, 0, tw(s) // n_j))
    s1 = jnp.transpose(s1, (1, 0, 2))
    e1 = jnp.transpose(e1, (1, 0, 2))
    return pl.pallas_call(
        kern,
        grid=(n_tiles + 2,),
        in_specs=[pl.BlockSpec((D, tm), lambda s: (0, ta(s) // n_j)),
                  pl.BlockSpec((te, D), lambda s: (ta(s) % n_j, 0)),
                  pl.BlockSpec((D, te), lambda s: (0, tv(s) % n_j)),
                  xspec, sspec, xspec, sspec,
                  pl.BlockSpec((PEER_HEADS, tm), lambda s: (0, tw(s) // n_j))],
        out_specs=pl.BlockSpec((D, tm), lambda s: (0, tv(s) // n_j)),
        out_shape=jax.ShapeDtypeStruct((D, T), F32),
        scratch_shapes=[pltpu.VMEM((D, tm), F32),
                        pltpu.VMEM((te, tm), F32), pltpu.VMEM((te, tm), F32),
                        pltpu.VMEM((te, tm), BF16), pltpu.VMEM((te, tm), BF16)],
        compiler_params=_cparams(("arbitrary",)),
        name="peer_dense",
    )(h1T_bf, u_bf, vT_bf, s1, s2, e1, e2, th)


def _res_ln_kernel(h_ref, f_ref, g_ref, b_ref, o_ref, *, alpha):
    x = alpha * h_ref[...] + f_ref[...]
    mu = jnp.mean(x, -1, keepdims=True)
    xc = x - mu
    var = jnp.mean(xc * xc, -1, keepdims=True)
    o_ref[...] = xc * lax.rsqrt(var + 1e-5) * g_ref[...] + b_ref[...]


def _res_ln(h, f, g, b, alpha, tm=512):
    T, D = h.shape
    row = pl.BlockSpec((tm, D), lambda i: (i, 0))
    vec = pl.BlockSpec((1, D), lambda i: (0, 0))
    return pl.pallas_call(
        functools.partial(_res_ln_kernel, alpha=alpha),
        grid=(T // tm,),
        in_specs=[row, row, vec, vec],
        out_specs=row,
        out_shape=jax.ShapeDtypeStruct((T, D), F32),
        compiler_params=_cparams(("parallel",)),
        name="res_ln2",
    )(h, f, g.reshape(1, D), b.reshape(1, D))


def kernel(x, ln0_g, ln0_b, rel_bias, w_in, sink, conv_w, conv_b, f_w1, f_b1, f_freq1, f_w2, f_b2,
           f_freq2, f_w3, hy_bias, mix_norm_g, w_out, ln1_g, ln1_b, peer_wq, peer_subkeys, peer_u,
           peer_v, ln2_g, ln2_b):
    B, S, D = x.shape
    T = B * S
    alpha = (2.0 * DEPTH) ** 0.25
    consts = _dft_constants()

    qi = jnp.arange(BLOCK, dtype=jnp.int32)
    kj = jnp.arange(3 * BLOCK, dtype=jnp.int32)
    rel = kj[None, :] - BLOCK - qi[:, None]
    bias = jnp.transpose(rel_bias.astype(F32)[_t5_bucket(rel)], (2, 0, 1))

    h, h_bf = _ln0(x.reshape(T, D), ln0_g, ln0_b)
    for l in range(DEPTH):
        proj = _matmul(h_bf, w_in[l].astype(BF16), tm=512, tn=1536)
        attn = _attention(proj, bias, sink[l], B, S)

        hfilt = _filters(S, f_w1[l], f_b1[l], f_freq1[l], f_w2[l], f_b2[l], f_freq2[l], f_w3[l])
        kf = _filter_spectra(hfilt, consts)
        u = _short_conv(proj.reshape(B, S, -1), conv_w[l], conv_b[l], ATTN_WIDTH + 2 * KV_WIDTH)
        nct = HY_WIDTH // LANES
        z1 = _long_conv(u, 0, u, nct, kf, 0, hy_bias[l, 0], consts)
        hyo = _long_conv(z1, 0, u, 2 * nct, kf, 1, hy_bias[l, 1], consts)

        h1 = _mix_out(attn, hyo.reshape(T, HY_WIDTH), h, mix_norm_g[l], w_out[l].astype(BF16),
                      ln1_g[l], ln1_b[l], alpha)

        h1T = h1.T
        h1T_bf = h1T.astype(BF16)
        s1, s2, e1, e2, th = _peer_scores(peer_wq[l].T.astype(BF16), h1T_bf,
                                          peer_subkeys[l, 0].astype(BF16), peer_subkeys[l, 1].astype(BF16))
        ffnT = _peer_dense(h1T_bf, peer_u[l].astype(BF16), peer_v[l].T.astype(BF16),
                           s1, s2, e1, e2, th.reshape(PEER_HEADS, T))
        h = _res_ln(h1, ffnT.T, ln2_g[l], ln2_b[l], alpha)
        if l + 1 < DEPTH:
            h_bf = h.astype(BF16)
    return h.reshape(B, S, D)
```

```python
import functools
import math

import numpy as np
import jax
import jax.numpy as jnp
from jax import lax
from jax.experimental import pallas as pl
from jax.experimental.pallas import tpu as pltpu

F32 = jnp.float32
BF16 = jnp.bfloat16

D_MODEL = 2048
HEAD_DIM = 128
N_Q_HEADS = 8
N_KV_HEADS = 2
GQA_GROUP = N_Q_HEADS // N_KV_HEADS
ATTN_WIDTH = N_Q_HEADS * HEAD_DIM
KV_WIDTH = N_KV_HEADS * HEAD_DIM
WINDOW = 128
BLOCK = 128
N_BUCKETS = 32
MAX_DISTANCE = 128
HY_WIDTH = D_MODEL - ATTN_WIDTH
HY_ORDER = 2
POS_BANDS = 16
POS_EMB = 1 + 2 * POS_BANDS
FILTER_HIDDEN = 64
FAST_DECAY_PCT = 0.3
SLOW_DECAY_PCT = 1.5
DECAY_TARGET = 1e-2
NORM_GROUP = 128
N_KEYS = 128
PEER_HEADS = 8
PEER_QDIM = 256
PEER_QHALF = PEER_QDIM // 2
PEER_TOPK = 16
NEG = -1e30
DEPTH = 1

LANES = 128
VMEM_LIMIT = 56 * 1024 * 1024

FFT_N1 = 64
FFT_N2 = 128
FFT_K1 = FFT_N1 // 2 + 1
FFT_K1_PAD = 40
FFT_SLOT = 2 * FFT_N2
KF_ROWS = FFT_K1 * FFT_SLOT
SUBLANES = 8
W_PITCH = FFT_SLOT + SUBLANES
X_PITCH = FFT_N2 + SUBLANES
W_ROWS = FFT_K1_PAD * W_PITCH
X_ROWS = (FFT_N1 // 2) * X_PITCH


def _cparams(sem, vmem=VMEM_LIMIT):
    return pltpu.CompilerParams(dimension_semantics=sem, vmem_limit_bytes=vmem)


def _ln0_kernel(x_ref, g_ref, b_ref, h_ref, hb_ref):
    x = x_ref[...]
    mu = jnp.mean(x, -1, keepdims=True)
    xc = x - mu
    var = jnp.mean(xc * xc, -1, keepdims=True)
    y = xc * lax.rsqrt(var + 1e-5) * g_ref[...] + b_ref[...]
    h_ref[...] = y
    hb_ref[...] = y.astype(BF16)


def _ln0(x2d, g, b, tm=512):
    T, D = x2d.shape
    return pl.pallas_call(
        _ln0_kernel,
        grid=(T // tm,),
        in_specs=[pl.BlockSpec((tm, D), lambda i: (i, 0)),
                  pl.BlockSpec((1, D), lambda i: (0, 0)),
                  pl.BlockSpec((1, D), lambda i: (0, 0))],
        out_specs=[pl.BlockSpec((tm, D), lambda i: (i, 0)),
                   pl.BlockSpec((tm, D), lambda i: (i, 0))],
        out_shape=[jax.ShapeDtypeStruct((T, D), F32), jax.ShapeDtypeStruct((T, D), BF16)],
        compiler_params=_cparams(("parallel",)),
        name="ln0",
    )(x2d, g.reshape(1, D), b.reshape(1, D))


def _mm_kernel(a_ref, b_ref, o_ref):
    o_ref[...] = jnp.dot(a_ref[...], b_ref[...], preferred_element_type=F32)


def _matmul(a, b, tm, tn):
    M, K = a.shape
    N = b.shape[1]
    return pl.pallas_call(
        _mm_kernel,
        grid=(N // tn, M // tm),
        in_specs=[pl.BlockSpec((tm, K), lambda j, i: (i, 0)),
                  pl.BlockSpec((K, tn), lambda j, i: (0, j))],
        out_specs=pl.BlockSpec((tm, tn), lambda j, i: (i, j)),
        out_shape=jax.ShapeDtypeStruct((M, N), F32),
        compiler_params=_cparams(("parallel", "parallel")),
        name="in_proj",
    )(a, b)


def _t5_bucket(rel):
    nb = N_BUCKETS // 2
    ret = (rel > 0).astype(jnp.int32) * nb
    n = jnp.abs(rel)
    max_exact = nb // 2
    nf = jnp.maximum(n, 1).astype(F32)
    large = max_exact + (jnp.log(nf / max_exact) / math.log(MAX_DISTANCE / max_exact)
                         * (nb - max_exact)).astype(jnp.int32)
    large = jnp.minimum(large, nb - 1)
    return ret + jnp.where(n < max_exact, n, large)


def _attn_kernel(sink_ref, q_ref, kp_ref, kc_ref, kn_ref, vp_ref, vc_ref, vn_ref, bias_ref, o_ref,
                 *, nb, seq):
    n = pl.program_id(0) % nb
    qi = lax.broadcasted_iota(jnp.int32, (BLOCK, 3 * BLOCK), 0)
    kj = lax.broadcasted_iota(jnp.int32, (BLOCK, 3 * BLOCK), 1)
    rel = kj - BLOCK - qi
    kabs = n * BLOCK + kj - BLOCK
    valid = (jnp.abs(rel) <= WINDOW) & (kabs >= 0) & (kabs < seq)
    k = jnp.concatenate([kp_ref[...], kc_ref[...], kn_ref[...]], axis=0).astype(BF16)
    v = jnp.concatenate([vp_ref[...], vc_ref[...], vn_ref[...]], axis=0).astype(BF16)
    scale = 1.0 / math.sqrt(HEAD_DIM)
    for g in range(N_KV_HEADS):
        kg = k[:, g * HEAD_DIM:(g + 1) * HEAD_DIM]
        vg = v[:, g * HEAD_DIM:(g + 1) * HEAD_DIM]
        for r in range(GQA_GROUP):
            h = g * GQA_GROUP + r
            qh = q_ref[:, h * HEAD_DIM:(h + 1) * HEAD_DIM].astype(BF16)
            s = lax.dot_general(qh, kg, (((1,), (1,)), ((), ())), preferred_element_type=F32) * scale
            s = jnp.where(valid, s + bias_ref[h], NEG)
            sk = sink_ref[h]
            m = jnp.maximum(jnp.max(s, -1, keepdims=True), sk)
            p = jnp.exp(s - m)
            denom = jnp.sum(p, -1, keepdims=True) + jnp.exp(sk - m)
            p = p / denom
            o_ref[:, h * HEAD_DIM:(h + 1) * HEAD_DIM] = jnp.dot(p.astype(BF16), vg,
                                                                 preferred_element_type=F32)


def _attention(proj, bias, sink, batch, seq):
    T = proj.shape[0]
    nb = seq // BLOCK
    kcol = ATTN_WIDTH // KV_WIDTH
    vcol = kcol + 1

    def prev(i):
        return i - jnp.where(i % nb == 0, 0, 1)

    def nxt(i):
        return i + jnp.where(i % nb == nb - 1, 0, 1)

    kern = functools.partial(_attn_kernel, nb=nb, seq=seq)
    return pl.pallas_call(
        kern,
        grid=(T // BLOCK,),
        in_specs=[pl.BlockSpec(memory_space=pltpu.SMEM),
                  pl.BlockSpec((BLOCK, ATTN_WIDTH), lambda i: (i, 0)),
                  pl.BlockSpec((BLOCK, KV_WIDTH), lambda i: (prev(i), kcol)),
                  pl.BlockSpec((BLOCK, KV_WIDTH), lambda i: (i, kcol)),
                  pl.BlockSpec((BLOCK, KV_WIDTH), lambda i: (nxt(i), kcol)),
                  pl.BlockSpec((BLOCK, KV_WIDTH), lambda i: (prev(i), vcol)),
                  pl.BlockSpec((BLOCK, KV_WIDTH), lambda i: (i, vcol)),
                  pl.BlockSpec((BLOCK, KV_WIDTH), lambda i: (nxt(i), vcol)),
                  pl.BlockSpec((N_Q_HEADS, BLOCK, 3 * BLOCK), lambda i: (0, 0, 0))],
        out_specs=pl.BlockSpec((BLOCK, ATTN_WIDTH), lambda i: (i, 0)),
        out_shape=jax.ShapeDtypeStruct((T, ATTN_WIDTH), F32),
        compiler_params=_cparams(("parallel",)),
        name="window_attn",
    )(sink, proj, proj, proj, proj, proj, proj, proj, bias)


def _filter_kernel(z_ref, w1_ref, b1_ref, fr1_ref, w2_ref, b2_ref, fr2_ref, w3_ref, dl_ref, o_ref,
                   *, tt, seq):
    hi = lax.Precision.HIGHEST
    a = jnp.dot(z_ref[...], w1_ref[...], preferred_element_type=F32, precision=hi) + b1_ref[...]
    hid = jnp.sin(fr1_ref[...] * a)
    a = jnp.dot(hid, w2_ref[...], preferred_element_type=F32, precision=hi) + b2_ref[...]
    hid = jnp.sin(fr2_ref[...] * a)
    h = jnp.dot(hid, w3_ref[...], preferred_element_type=F32, precision=hi)
    rows = lax.broadcasted_iota(jnp.int32, h.shape, 0) + pl.program_id(0) * tt
    tn = rows.astype(F32) / float(max(seq - 1, 1))
    o_ref[...] = h * jnp.exp(-tn * dl_ref[...])


def _filters(seq, w1, b1, fr1, w2, b2, fr2, w3, tt=512):
    t = jnp.arange(seq, dtype=F32)
    tn = t / max(seq - 1, 1)
    w = 2.0 * math.pi * t / seq
    bands = jnp.linspace(1e-4, POS_BANDS - 1, POS_BANDS, dtype=F32)
    z = jnp.concatenate([tn[:, None], jnp.cos(w[:, None] * bands), -jnp.sin(w[:, None] * bands)], -1)
    zp = jnp.pad(z, ((0, 0), (0, FILTER_HIDDEN - POS_EMB)))
    w1p = jnp.pad(w1, ((0, FILTER_HIDDEN - POS_EMB), (0, 0)))
    max_decay = math.log(DECAY_TARGET) / FAST_DECAY_PCT
    min_decay = math.log(DECAY_TARGET) / SLOW_DECAY_PCT
    deltas = jnp.abs(jnp.linspace(min_decay, max_decay, HY_WIDTH, dtype=F32))
    ncol = HY_ORDER * 2 * HY_WIDTH
    dl = jnp.tile(deltas, HY_ORDER * 2).reshape(1, ncol)
    H = FILTER_HIDDEN
    kern = functools.partial(_filter_kernel, tt=tt, seq=seq)
    full = lambda r, c: pl.BlockSpec((r, c), lambda i: (0, 0))
    return pl.pallas_call(
        kern,
        grid=(seq // tt,),
        in_specs=[pl.BlockSpec((tt, H), lambda i: (i, 0)),
                  full(H, H), full(1, H), full(1, H), full(H, H), full(1, H), full(1, H),
                  full(H, ncol), full(1, ncol)],
        out_specs=pl.BlockSpec((tt, ncol), lambda i: (i, 0)),
        out_shape=jax.ShapeDtypeStruct((seq, ncol), F32),
        compiler_params=_cparams(("parallel",)),
        name="hyena_filter_mlp",
    )(zp, w1p, b1.reshape(1, H), fr1.reshape(1, H), w2, b2.reshape(1, H), fr2.reshape(1, H), w3, dl)


def _sconv_kernel(x_ref, w_ref, b_ref, o_ref):
    x = x_ref[0]
    L = x.shape[0]
    rows = lax.broadcasted_iota(jnp.int32, x.shape, 0)
    xm = jnp.where(rows == 0, 0.0, pltpu.roll(x, 1, 0))
    xp = jnp.where(rows == L - 1, 0.0, pltpu.roll(x, L - 1, 0))
    w = w_ref[...]
    o_ref[0] = xm * w[0:1] + x * w[1:2] + xp * w[2:3] + b_ref[...]


def _short_conv(proj3, conv_w, conv_b, col0, ct=256):
    B, L, _ = proj3.shape
    C = conv_w.shape[1]
    off = col0 // ct
    return pl.pallas_call(
        _sconv_kernel,
        grid=(B, C // ct),
        in_specs=[pl.BlockSpec((1, L, ct), lambda b, c: (b, 0, c + off)),
                  pl.BlockSpec((3, ct), lambda b, c: (0, c)),
                  pl.BlockSpec((1, ct), lambda b, c: (0, c))],
        out_specs=pl.BlockSpec((1, L, ct), lambda b, c: (b, 0, c)),
        out_shape=jax.ShapeDtypeStruct((B, L, C), F32),
        compiler_params=_cparams(("parallel", "parallel")),
        name="hyena_short_conv",
    )(proj3, conv_w, conv_b.reshape(1, C))


def _dft_constants():
    n1h = FFT_N1 // 2
    k1 = np.arange(FFT_K1_PAD)[:, None].astype(np.float64)
    n1 = np.arange(n1h)[None, :].astype(np.float64)
    ang = 2.0 * np.pi * k1 * n1 / FFT_N1
    live = (np.arange(FFT_K1_PAD) < FFT_K1)[:, None]
    f1 = np.concatenate([np.where(live, np.cos(ang), 0.0), np.where(live, -np.sin(ang), 0.0)], 0)
    kk1 = np.arange(FFT_K1)[:, None, None].astype(np.float64)
    k2 = np.arange(FFT_N2)[None, :, None].astype(np.float64)
    n2 = np.arange(FFT_N2)[None, None, :].astype(np.float64)
    phi = 2.0 * np.pi * (n2 * k2 / FFT_N2 + n2 * kk1 / (FFT_N1 * FFT_N2))
    c, s = np.cos(phi), np.sin(phi)
    g = np.concatenate([np.concatenate([c, s], 2), np.concatenate([-s, c], 2)], 1)
    ct_, st_ = np.transpose(c, (0, 2, 1)), np.transpose(s, (0, 2, 1))
    ginv = np.concatenate([np.concatenate([ct_, -st_], 2), np.concatenate([st_, ct_], 2)], 1)
    wk = np.where((np.arange(FFT_K1_PAD) == 0) | (np.arange(FFT_K1_PAD) == FFT_N1 // 2), 1.0, 2.0)
    wk = np.where(np.arange(FFT_K1_PAD) < FFT_K1, wk, 0.0)[None, :] / (FFT_N1 * FFT_N2)
    angi = 2.0 * np.pi * np.arange(n1h)[:, None] * np.arange(FFT_K1_PAD)[None, :] / FFT_N1
    finv = np.concatenate([wk * np.cos(angi), -wk * np.sin(angi)], 1)
    as_bf = lambda a: jnp.asarray(a.astype(np.float32)).astype(BF16)
    return as_bf(f1), as_bf(g), as_bf(finv), as_bf(ginv)


def _pad_rows_in(src_ref, xp_ref, zero_first_row=False):
    for n1 in range(FFT_N1 // 2):
        blk = src_ref[pl.ds(n1 * FFT_N2, FFT_N2), :]
        if zero_first_row and n1 == 0:
            rows = lax.broadcasted_iota(jnp.int32, blk.shape, 0)
            blk = jnp.where(rows == 0, 0.0, blk)
        xp_ref[pl.ds(n1 * X_PITCH, FFT_N2), :] = blk


def _fft_stage1(xp_ref, w_ref, f1_ref):
    n1h = FFT_N1 // 2
    kp = FFT_K1_PAD

    def body(i, carry):
        n2 = 2 * i
        xa = xp_ref[pl.ds(n2, n1h, stride=X_PITCH), :]
        xb = xp_ref[pl.ds(n2 + 1, n1h, stride=X_PITCH), :]
        xs = jnp.concatenate([xa, xb], axis=1).astype(BF16)
        r = jnp.dot(f1_ref[...], xs, preferred_element_type=F32)
        w_ref[pl.ds(n2, kp, stride=W_PITCH), :] = r[0:kp, 0:LANES]
        w_ref[pl.ds(n2 + 1, kp, stride=W_PITCH), :] = r[0:kp, LANES:2 * LANES]
        w_ref[pl.ds(FFT_N2 + n2, kp, stride=W_PITCH), :] = r[kp:2 * kp, 0:LANES]
        w_ref[pl.ds(FFT_N2 + n2 + 1, kp, stride=W_PITCH), :] = r[kp:2 * kp, LANES:2 * LANES]
        return carry

    lax.fori_loop(0, FFT_N2 // 2, body, 0, unroll=4)


def _kf_kernel(hf_ref, hb_ref, f1_ref, g_ref, kf_ref, wf_ref, wb_ref, xp_ref):
    l1 = (jnp.sum(jnp.abs(hf_ref[...]), 0, keepdims=True)
          + jnp.sum(jnp.abs(hb_ref[...]), 0, keepdims=True))
    scale = 1.0 / (l1 + 1e-6)
    _pad_rows_in(hf_ref, xp_ref)
    _fft_stage1(xp_ref, wf_ref, f1_ref)
    _pad_rows_in(hb_ref, xp_ref, zero_first_row=True)
    _fft_stage1(xp_ref, wb_ref, f1_ref)
    out = kf_ref.at[0]

    def body(k1, carry):
        src = pl.multiple_of(k1 * W_PITCH, 8)
        dst = pl.multiple_of(k1 * FFT_SLOT, FFT_SLOT)
        gk = g_ref[k1]
        sf = jnp.dot(gk, wf_ref[pl.ds(src, FFT_SLOT), :].astype(BF16), preferred_element_type=F32)
        sb = jnp.dot(gk, wb_ref[pl.ds(src, FFT_SLOT), :].astype(BF16), preferred_element_type=F32)
        out[pl.ds(dst, FFT_N2), :] = (sf[0:FFT_N2] + sb[0:FFT_N2]) * scale
        out[pl.ds(dst + FFT_N2, FFT_N2), :] = (sf[FFT_N2:FFT_SLOT] - sb[FFT_N2:FFT_SLOT]) * scale
        return carry

    lax.fori_loop(0, FFT_K1, body, 0, unroll=3)


def _filter_spectra(hfilt, consts):
    L = hfilt.shape[0]
    C = HY_WIDTH
    nct = C // LANES
    f1, g, _, _ = consts
    return pl.pallas_call(
        _kf_kernel,
        grid=(HY_ORDER, nct),
        in_specs=[pl.BlockSpec((L, LANES), lambda o, c: (0, o * 2 * nct + c)),
                  pl.BlockSpec((L, LANES), lambda o, c: (0, o * 2 * nct + nct + c)),
                  pl.BlockSpec(f1.shape, lambda o, c: (0, 0)),
                  pl.BlockSpec(g.shape, lambda o, c: (0, 0, 0))],
        out_specs=pl.BlockSpec((1, KF_ROWS, LANES), lambda o, c: (o, 0, c)),
        out_shape=jax.ShapeDtypeStruct((HY_ORDER, KF_ROWS, C), F32),
        scratch_shapes=[pltpu.VMEM((W_ROWS, LANES), F32), pltpu.VMEM((W_ROWS, LANES), F32),
                        pltpu.VMEM((X_ROWS, LANES), F32)],
        compiler_params=_cparams(("parallel", "parallel")),
        name="hyena_filter_fft",
    )(hfilt, hfilt, f1, g)


def _conv_kernel(z_ref, gate_ref, kf_ref, d_ref, f1_ref, g_ref, finv_ref, ginv_ref, o_ref, w_ref, xp_ref):
    zsrc = z_ref.at[0]
    kf = kf_ref.at[0]
    _pad_rows_in(zsrc, xp_ref)
    _fft_stage1(xp_ref, w_ref, f1_ref)

    def body(k1, carry):
        base = pl.multiple_of(k1 * W_PITCH, 8)
        kbase = pl.multiple_of(k1 * FFT_SLOT, FFT_SLOT)
        a = w_ref[pl.ds(base, FFT_SLOT), :].astype(BF16)
        s = jnp.dot(g_ref[k1], a, preferred_element_type=F32)
        sr, si = s[0:FFT_N2], s[FFT_N2:FFT_SLOT]
        kr = kf[pl.ds(kbase, FFT_N2), :]
        ki = kf[pl.ds(kbase + FFT_N2, FFT_N2), :]
        y = jnp.concatenate([sr * kr - si * ki, sr * ki + si * kr], axis=0).astype(BF16)
        w_ref[pl.ds(base, FFT_SLOT), :] = jnp.dot(ginv_ref[k1], y, preferred_element_type=F32)
        return carry

    lax.fori_loop(0, FFT_K1, body, 0, unroll=3)

    n1h = FFT_N1 // 2
    kp = FFT_K1_PAD

    def body2(i, carry):
        n2 = 2 * i
        zr = jnp.concatenate([w_ref[pl.ds(n2, kp, stride=W_PITCH), :],
                              w_ref[pl.ds(n2 + 1, kp, stride=W_PITCH), :]], axis=1)
        zi = jnp.concatenate([w_ref[pl.ds(FFT_N2 + n2, kp, stride=W_PITCH), :],
                              w_ref[pl.ds(FFT_N2 + n2 + 1, kp, stride=W_PITCH), :]], axis=1)
        zz = jnp.concatenate([zr, zi], axis=0).astype(BF16)
        x = jnp.dot(finv_ref[...], zz, preferred_element_type=F32)
        xp_ref[pl.ds(n2, n1h, stride=X_PITCH), :] = x[:, 0:LANES]
        xp_ref[pl.ds(n2 + 1, n1h, stride=X_PITCH), :] = x[:, LANES:2 * LANES]
        return carry

    lax.fori_loop(0, FFT_N2 // 2, body2, 0, unroll=4)
    d = d_ref[...]
    for n1 in range(n1h):
        rs = pl.ds(n1 * FFT_N2, FFT_N2)
        y = xp_ref[pl.ds(n1 * X_PITCH, FFT_N2), :]
        o_ref[0, rs, :] = gate_ref[0, rs, :] * (y + z_ref[0, rs, :] * d)


def _long_conv(z_arr, z_off, gate_arr, gate_off, kf_all, order, d, consts):
    B, L, _ = z_arr.shape
    C = HY_WIDTH
    nct = C // LANES
    f1, g, finv, ginv = consts
    cst2 = lambda a: pl.BlockSpec(a.shape, lambda c, b: (0, 0))
    cst3 = lambda a: pl.BlockSpec(a.shape, lambda c, b: (0, 0, 0))
    return pl.pallas_call(
        _conv_kernel,
        grid=(nct, B),
        in_specs=[pl.BlockSpec((1, L, LANES), lambda c, b: (b, 0, c + z_off)),
                  pl.BlockSpec((1, L, LANES), lambda c, b: (b, 0, c + gate_off)),
                  pl.BlockSpec((1, KF_ROWS, LANES), lambda c, b: (order, 0, c)),
                  pl.BlockSpec((1, LANES), lambda c, b: (0, c)),
                  cst2(f1), cst3(g), cst2(finv), cst3(ginv)],
        out_specs=pl.BlockSpec((1, L, LANES), lambda c, b: (b, 0, c)),
        out_shape=jax.ShapeDtypeStruct((B, L, C), F32),
        scratch_shapes=[pltpu.VMEM((W_ROWS, LANES), F32), pltpu.VMEM((X_ROWS, LANES), F32)],
        compiler_params=_cparams(("parallel", "parallel")),
        name=f"hyena_long_conv{order}",
    )(z_arr, gate_arr, kf_all, d.reshape(1, C), f1, g, finv, ginv)


def _mix_kernel(attn_ref, hyo_ref, h0_ref, mg_ref, w_ref, g_ref, b_ref, h1_ref, *, alpha):
    def gnorm(x, goff):
        parts = []
        for gi in range(x.shape[1] // NORM_GROUP):
            xg = x[:, gi * NORM_GROUP:(gi + 1) * NORM_GROUP]
            ms = jnp.mean(xg * xg, -1, keepdims=True)
            gg = mg_ref[:, goff + gi * NORM_GROUP: goff + (gi + 1) * NORM_GROUP]
            parts.append((xg * lax.rsqrt(ms + 1e-6) * gg).astype(BF16))
        return jnp.concatenate(parts, axis=1)

    a = gnorm(attn_ref[...], 0)
    y = gnorm(hyo_ref[...], ATTN_WIDTH)
    mix = (jnp.dot(a, w_ref[0:ATTN_WIDTH, :], preferred_element_type=F32)
           + jnp.dot(y, w_ref[ATTN_WIDTH:, :], preferred_element_type=F32))
    x = alpha * h0_ref[...] + mix
    mu = jnp.mean(x, -1, keepdims=True)
    xc = x - mu
    var = jnp.mean(xc * xc, -1, keepdims=True)
    h1_ref[...] = xc * lax.rsqrt(var + 1e-5) * g_ref[...] + b_ref[...]


def _mix_out(attn, hyo, h0, mix_g, w_out_bf, ln_g, ln_b, alpha, tm=256):
    T, D = h0.shape
    kern = functools.partial(_mix_kernel, alpha=alpha)
    return pl.pallas_call(
        kern,
        grid=(T // tm,),
        in_specs=[pl.BlockSpec((tm, ATTN_WIDTH), lambda i: (i, 0)),
                  pl.BlockSpec((tm, HY_WIDTH), lambda i: (i, 0)),
                  pl.BlockSpec((tm, D), lambda i: (i, 0)),
                  pl.BlockSpec((1, D), lambda i: (0, 0)),
                  pl.BlockSpec((D, D), lambda i: (0, 0)),
                  pl.BlockSpec((1, D), lambda i: (0, 0)),
                  pl.BlockSpec((1, D), lambda i: (0, 0))],
        out_specs=pl.BlockSpec((tm, D), lambda i: (i, 0)),
        out_shape=jax.ShapeDtypeStruct((T, D), F32),
        compiler_params=_cparams(("parallel",)),
        name="mix_out_ln1",
    )(attn, hyo, h0, mix_g.reshape(1, D), w_out_bf, ln_g.reshape(1, D), ln_b.reshape(1, D))


def _top_values(s, k):
    R = s.shape[0]
    rows = lax.broadcasted_iota(jnp.int32, s.shape, 0)
    vals = []
    for _ in range(k):
        m = jnp.max(s, axis=0, keepdims=True)
        first = jnp.min(jnp.where(s == m, rows, R), axis=0, keepdims=True)
        s = jnp.where(rows == first, -jnp.inf, s)
        vals.append(m)
    return vals


def _peer_score_kernel(wq_ref, h_ref, sk1_ref, sk2_ref, s1_ref, s2_ref, e1_ref, e2_ref, th_ref, q_ref):
    q_ref[...] = jnp.dot(wq_ref[...], h_ref[...], preferred_element_type=F32)

    def head(h, carry):
        base = pl.multiple_of(h * PEER_QDIM, PEER_QDIM)
        q1 = q_ref[pl.ds(base, PEER_QHALF), :].astype(BF16)
        q2 = q_ref[pl.ds(base + PEER_QHALF, PEER_QHALF), :].astype(BF16)
        s1 = jnp.dot(sk1_ref[...], q1, preferred_element_type=F32)
        s2 = jnp.dot(sk2_ref[...], q2, preferred_element_type=F32)
        v1 = _top_values(s1, PEER_TOPK)
        v2 = _top_values(s2, PEER_TOPK)
        v2all = jnp.concatenate(v2, axis=0)
        cand = jnp.concatenate([v1[a] + v2all for a in range(PEER_TOPK)], axis=0)
        sc = _top_values(cand, PEER_TOPK)
        zsum = sc[0] * 0.0
        for j in range(PEER_TOPK):
            zsum = zsum + jnp.exp(sc[j] - sc[0])
        th_ref[h] = sc[PEER_TOPK - 1]
        s1_ref[h] = s1
        s2_ref[h] = s2
        e1_ref[h] = jnp.exp(s1 - v1[0]) / zsum
        e2_ref[h] = jnp.exp(s2 - v2[0])
        return carry

    lax.fori_loop(0, PEER_HEADS, head, 0)


def _peer_scores(wqT_bf, h1T_bf, sk1_bf, sk2_bf, tl=256):
    D, T = h1T_bf.shape
    Q = wqT_bf.shape[0]
    big = jax.ShapeDtypeStruct((PEER_HEADS, N_KEYS, T), F32)
    bspec = pl.BlockSpec((PEER_HEADS, N_KEYS, tl), lambda i: (0, 0, i))
    return pl.pallas_call(
        _peer_score_kernel,
        grid=(T // tl,),
        in_specs=[pl.BlockSpec((Q, D), lambda i: (0, 0)),
                  pl.BlockSpec((D, tl), lambda i: (0, i)),
                  pl.BlockSpec((N_KEYS, PEER_QHALF), lambda i: (0, 0)),
                  pl.BlockSpec((N_KEYS, PEER_QHALF), lambda i: (0, 0))],
        out_specs=[bspec, bspec, bspec, bspec, pl.BlockSpec((PEER_HEADS, 1, tl), lambda i: (0, 0, i))],
        out_shape=[big, big, big, big, jax.ShapeDtypeStruct((PEER_HEADS, 1, T), F32)],
        scratch_shapes=[pltpu.VMEM((Q, tl), F32)],
        compiler_params=_cparams(("parallel",)),
        name="peer_scores_topk",
    )(wqT_bf, h1T_bf, sk1_bf, sk2_bf)


W_ROWS_PER_GROUP = 32


def _gelu_exact(a):
    return 0.5 * a * (1.0 + lax.erf(a * (1.0 / math.sqrt(2.0))))


def _peer_dense_kernel(hb_ref, u_ref, vt_ref, s1_ref, s2_ref, e1_ref, e2_ref, th_ref,
                       o_ref, acc_ref, a0_ref, a1_ref, w0_ref, w1_ref, *, te, tm, n_j, n_tiles):
    s = pl.program_id(0)
    sv = jnp.clip(s - 2, 0, n_tiles - 1)
    jv = sv % n_j
    jw = jnp.clip(s - 1, 0, n_tiles - 1) % n_j

    @pl.when(s == 0)
    def _():
        for ref in (a0_ref, a1_ref, w0_ref, w1_ref):
            ref[...] = jnp.zeros_like(ref)

    @pl.when(jv == 0)
    def _():
        acc_ref[...] = jnp.zeros_like(acc_ref)

    wide = 2 * LANES
    n_r, n_c = te // N_KEYS, tm // LANES
    d_model = acc_ref.shape[0]

    def body(a_cur, a_prv, w_cur, w_prv):
        def stage_a(q):
            ms = slice((q // 2) * wide, (q // 2 + 1) * wide)
            ls = slice((q % 2) * wide, (q % 2 + 1) * wide)
            a_cur[ms, ls] = jnp.dot(u_ref[ms, :], hb_ref[:, ls], preferred_element_type=F32)

        def stage_v(p):
            fs = slice((p // 2) * (d_model // 8), (p // 2 + 1) * (d_model // 8))
            ls = slice((p % 2) * wide, (p % 2 + 1) * wide)
            acc_ref[fs, ls] += jnp.dot(vt_ref[0, fs, :], w_cur[:, ls], preferred_element_type=F32)

        n_g = N_KEYS // W_ROWS_PER_GROUP

        def stage_w(idx):
            p, g = idx // n_g, idx % n_g
            r, c = p // n_c, p % n_c
            i1 = jw * n_r + r
            cs = slice(c * LANES, (c + 1) * LANES)
            s1rows = s1_ref[i1, :, cs]
            e1rows = e1_ref[i1, :, cs]
            throws = th_ref[:, cs]
            ks = slice(g * W_ROWS_PER_GROUP, (g + 1) * W_ROWS_PER_GROUP)
            gsum = jnp.zeros((W_ROWS_PER_GROUP, LANES), F32)
            for h in range(PEER_HEADS):
                sel = (s2_ref[h, ks, cs] + s1rows[h:h + 1]) >= throws[h:h + 1]
                gsum = gsum + jnp.where(sel, e2_ref[h, ks, cs] * e1rows[h:h + 1], 0.0)
            ws = slice(r * N_KEYS + g * W_ROWS_PER_GROUP, r * N_KEYS + (g + 1) * W_ROWS_PER_GROUP)
            w_prv[ws, cs] = (gsum * _gelu_exact(a_prv[ws, cs])).astype(BF16)

        for q in range(4):
            stage_a(q)
            for t in range(4):
                stage_v(4 * q + t)
                for gg in range(4):
                    stage_w(16 * q + 4 * t + gg)

    @pl.when(s % 2 == 0)
    def _():
        body(a0_ref, a1_ref, w0_ref, w1_ref)

    @pl.when(s % 2 == 1)
    def _():
        body(a1_ref, a0_ref, w1_ref, w0_ref)

    @pl.when(jv == n_j - 1)
    def _():
        o_ref[...] = acc_ref[...]


def _peer_dense(h1T_bf, u_bf, v_bf, s1, s2, e1, e2, th, tm=512, te=512):
    D, T = h1T_bf.shape
    E = u_bf.shape[0]
    assert te == 4 * N_KEYS and tm == 4 * LANES, "the stage interleave is written for 512 x 512 tiles"
    n_i, n_j = T // tm, E // te
    vT_bf = jnp.transpose(v_bf.reshape(n_j, te, D), (0, 2, 1))
    n_tiles = n_i * n_j
    kern = functools.partial(_peer_dense_kernel, te=te, tm=tm, n_j=n_j, n_tiles=n_tiles)
    ta = lambda s: jnp.minimum(s, n_tiles - 1)
    tw = lambda s: jnp.clip(s - 1, 0, n_tiles - 1)
    tv = lambda s: jnp.clip(s - 2, 0, n_tiles - 1)
    sspec = pl.BlockSpec((PEER_HEADS, N_KEYS, tm), lambda s: (0, 0, tw(s) // n_j))
    xspec = pl.BlockSpec((N_KEYS, PEER_HEADS, tm), lambda s: (0, 0, tw(s) // n_j))
    s1 = jnp.transpose(s1, (1, 0, 2))
    e1 = jnp.transpose(e1, (1, 0, 2))
    return pl.pallas_call(
        kern,
        grid=(n_tiles + 2,),
        in_specs=[pl.BlockSpec((D, tm), lambda s: (0, ta(s) // n_j)),
                  pl.BlockSpec((te, D), lambda s: (ta(s) % n_j, 0)),
                  pl.BlockSpec((1, D, te), lambda s: (tv(s) % n_j, 0, 0)),
                  xspec, sspec, xspec, sspec,
                  pl.BlockSpec((PEER_HEADS, tm), lambda s: (0, tw(s) // n_j))],
        out_specs=pl.BlockSpec((D, tm), lambda s: (0, tv(s) // n_j)),
        out_shape=jax.ShapeDtypeStruct((D, T), F32),
        scratch_shapes=[pltpu.VMEM((D, tm), F32),
                        pltpu.VMEM((te, tm), F32), pltpu.VMEM((te, tm), F32),
                        pltpu.VMEM((te, tm), BF16), pltpu.VMEM((te, tm), BF16)],
        compiler_params=_cparams(("arbitrary",)),
        name="peer_dense",
    )(h1T_bf, u_bf, vT_bf, s1, s2, e1, e2, th)


def _res_ln_kernel(h_ref, f_ref, g_ref, b_ref, o_ref, *, alpha):
    x = alpha * h_ref[...] + f_ref[...]
    mu = jnp.mean(x, -1, keepdims=True)
    xc = x - mu
    var = jnp.mean(xc * xc, -1, keepdims=True)
    o_ref[...] = xc * lax.rsqrt(var + 1e-5) * g_ref[...] + b_ref[...]


def _res_ln(h, f, g, b, alpha, tm=512):
    T, D = h.shape
    row = pl.BlockSpec((tm, D), lambda i: (i, 0))
    vec = pl.BlockSpec((1, D), lambda i: (0, 0))
    return pl.pallas_call(
        functools.partial(_res_ln_kernel, alpha=alpha),
        grid=(T // tm,),
        in_specs=[row, row, vec, vec],
        out_specs=row,
        out_shape=jax.ShapeDtypeStruct((T, D), F32),
        compiler_params=_cparams(("parallel",)),
        name="res_ln2",
    )(h, f, g.reshape(1, D), b.reshape(1, D))


def kernel(x, ln0_g, ln0_b, rel_bias, w_in, sink, conv_w, conv_b, f_w1, f_b1, f_freq1, f_w2, f_b2,
           f_freq2, f_w3, hy_bias, mix_norm_g, w_out, ln1_g, ln1_b, peer_wq, peer_subkeys, peer_u,
           peer_v, ln2_g, ln2_b):
    B, S, D = x.shape
    T = B * S
    alpha = (2.0 * DEPTH) ** 0.25
    consts = _dft_constants()

    qi = jnp.arange(BLOCK, dtype=jnp.int32)
    kj = jnp.arange(3 * BLOCK, dtype=jnp.int32)
    rel = kj[None, :] - BLOCK - qi[:, None]
    onehot = (_t5_bucket(rel)[..., None] == jnp.arange(N_BUCKETS, dtype=jnp.int32)).astype(F32)
    bias = jnp.einsum("qkb,bh->hqk", onehot, rel_bias.astype(F32), precision=lax.Precision.HIGHEST)

    h, h_bf = _ln0(x.reshape(T, D), ln0_g, ln0_b)
    for l in range(DEPTH):
        proj = _matmul(h_bf, w_in[l].astype(BF16), tm=512, tn=1536)
        attn = _attention(proj, bias, sink[l], B, S)

        hfilt = _filters(S, f_w1[l], f_b1[l], f_freq1[l], f_w2[l], f_b2[l], f_freq2[l], f_w3[l])
        kf = _filter_spectra(hfilt, consts)
        u = _short_conv(proj.reshape(B, S, -1), conv_w[l], conv_b[l], ATTN_WIDTH + 2 * KV_WIDTH)
        nct = HY_WIDTH // LANES
        z1 = _long_conv(u, 0, u, nct, kf, 0, hy_bias[l, 0], consts)
        hyo = _long_conv(z1, 0, u, 2 * nct, kf, 1, hy_bias[l, 1], consts)

        h1 = _mix_out(attn, hyo.reshape(T, HY_WIDTH), h, mix_norm_g[l], w_out[l].astype(BF16),
                      ln1_g[l], ln1_b[l], alpha)

        h1T = h1.T
        h1T_bf = h1T.astype(BF16)
        s1, s2, e1, e2, th = _peer_scores(peer_wq[l].T.astype(BF16), h1T_bf,
                                          peer_subkeys[l, 0].astype(BF16), peer_subkeys[l, 1].astype(BF16))
        ffnT = _peer_dense(h1T_bf, peer_u[l].astype(BF16), peer_v[l].astype(BF16),
                           s1, s2, e1, e2, th.reshape(PEER_HEADS, T))
        h = _res_ln(h1, ffnT.T, ln2_g[l], ln2_b[l], alpha)
        if l + 1 < DEPTH:
            h_bf = h.astype(BF16)
    return h.reshape(B, S, D)
```

```python
import functools
import math

import numpy as np
import jax
import jax.numpy as jnp
from jax import lax
from jax.experimental import pallas as pl
from jax.experimental.pallas import tpu as pltpu

F32 = jnp.float32
BF16 = jnp.bfloat16

D_MODEL = 2048
HEAD_DIM = 128
N_Q_HEADS = 8
N_KV_HEADS = 2
GQA_GROUP = N_Q_HEADS // N_KV_HEADS
ATTN_WIDTH = N_Q_HEADS * HEAD_DIM
KV_WIDTH = N_KV_HEADS * HEAD_DIM
WINDOW = 128
BLOCK = 128
N_BUCKETS = 32
MAX_DISTANCE = 128
HY_WIDTH = D_MODEL - ATTN_WIDTH
HY_ORDER = 2
POS_BANDS = 16
POS_EMB = 1 + 2 * POS_BANDS
FILTER_HIDDEN = 64
FAST_DECAY_PCT = 0.3
SLOW_DECAY_PCT = 1.5
DECAY_TARGET = 1e-2
NORM_GROUP = 128
N_KEYS = 128
PEER_HEADS = 8
PEER_QDIM = 256
PEER_QHALF = PEER_QDIM // 2
PEER_TOPK = 16
NEG = -1e30
DEPTH = 1

LANES = 128
VMEM_LIMIT = 56 * 1024 * 1024

FFT_N1 = 64
FFT_N2 = 128
FFT_K1 = FFT_N1 // 2 + 1
FFT_K1_PAD = 40
FFT_SLOT = 2 * FFT_N2
KF_ROWS = FFT_K1 * FFT_SLOT
SUBLANES = 8
W_PITCH = FFT_SLOT + SUBLANES
X_PITCH = FFT_N2 + SUBLANES
W_ROWS = FFT_K1_PAD * W_PITCH
X_ROWS = (FFT_N1 // 2) * X_PITCH


def _cparams(sem, vmem=VMEM_LIMIT):
    return pltpu.CompilerParams(dimension_semantics=sem, vmem_limit_bytes=vmem)


def _ln0_kernel(x_ref, g_ref, b_ref, h_ref, hb_ref):
    x = x_ref[...]
    mu = jnp.mean(x, -1, keepdims=True)
    xc = x - mu
    var = jnp.mean(xc * xc, -1, keepdims=True)
    y = xc * lax.rsqrt(var + 1e-5) * g_ref[...] + b_ref[...]
    h_ref[...] = y
    hb_ref[...] = y.astype(BF16)


def _ln0(x2d, g, b, tm=512):
    T, D = x2d.shape
    return pl.pallas_call(
        _ln0_kernel,
        grid=(T // tm,),
        in_specs=[pl.BlockSpec((tm, D), lambda i: (i, 0)),
                  pl.BlockSpec((1, D), lambda i: (0, 0)),
                  pl.BlockSpec((1, D), lambda i: (0, 0))],
        out_specs=[pl.BlockSpec((tm, D), lambda i: (i, 0)),
                   pl.BlockSpec((tm, D), lambda i: (i, 0))],
        out_shape=[jax.ShapeDtypeStruct((T, D), F32), jax.ShapeDtypeStruct((T, D), BF16)],
        compiler_params=_cparams(("parallel",)),
        name="ln0",
    )(x2d, g.reshape(1, D), b.reshape(1, D))


def _mm_kernel(a_ref, b_ref, o_ref):
    o_ref[...] = jnp.dot(a_ref[...], b_ref[...], preferred_element_type=F32)


def _matmul(a, b, tm, tn):
    M, K = a.shape
    N = b.shape[1]
    return pl.pallas_call(
        _mm_kernel,
        grid=(N // tn, M // tm),
        in_specs=[pl.BlockSpec((tm, K), lambda j, i: (i, 0)),
                  pl.BlockSpec((K, tn), lambda j, i: (0, j))],
        out_specs=pl.BlockSpec((tm, tn), lambda j, i: (i, j)),
        out_shape=jax.ShapeDtypeStruct((M, N), F32),
        compiler_params=_cparams(("parallel", "parallel")),
        name="in_proj",
    )(a, b)


def _t5_bucket(rel):
    nb = N_BUCKETS // 2
    ret = (rel > 0).astype(jnp.int32) * nb
    n = jnp.abs(rel)
    max_exact = nb // 2
    nf = jnp.maximum(n, 1).astype(F32)
    large = max_exact + (jnp.log(nf / max_exact) / math.log(MAX_DISTANCE / max_exact)
                         * (nb - max_exact)).astype(jnp.int32)
    large = jnp.minimum(large, nb - 1)
    return ret + jnp.where(n < max_exact, n, large)


def _attn_kernel(sink_ref, q_ref, kp_ref, kc_ref, kn_ref, vp_ref, vc_ref, vn_ref, bias_ref, o_ref,
                 *, nb, seq):
    n = pl.program_id(0) % nb
    qi = lax.broadcasted_iota(jnp.int32, (BLOCK, 3 * BLOCK), 0)
    kj = lax.broadcasted_iota(jnp.int32, (BLOCK, 3 * BLOCK), 1)
    rel = kj - BLOCK - qi
    kabs = n * BLOCK + kj - BLOCK
    valid = (jnp.abs(rel) <= WINDOW) & (kabs >= 0) & (kabs < seq)
    k = jnp.concatenate([kp_ref[...], kc_ref[...], kn_ref[...]], axis=0).astype(BF16)
    v = jnp.concatenate([vp_ref[...], vc_ref[...], vn_ref[...]], axis=0).astype(BF16)
    scale = 1.0 / math.sqrt(HEAD_DIM)
    for g in range(N_KV_HEADS):
        kg = k[:, g * HEAD_DIM:(g + 1) * HEAD_DIM]
        vg = v[:, g * HEAD_DIM:(g + 1) * HEAD_DIM]
        for r in range(GQA_GROUP):
            h = g * GQA_GROUP + r
            qh = q_ref[:, h * HEAD_DIM:(h + 1) * HEAD_DIM].astype(BF16)
            s = lax.dot_general(qh, kg, (((1,), (1,)), ((), ())), preferred_element_type=F32) * scale
            s = jnp.where(valid, s + bias_ref[h], NEG)
            sk = sink_ref[h]
            m = jnp.maximum(jnp.max(s, -1, keepdims=True), sk)
            p = jnp.exp(s - m)
            denom = jnp.sum(p, -1, keepdims=True) + jnp.exp(sk - m)
            p = p / denom
            o_ref[:, h * HEAD_DIM:(h + 1) * HEAD_DIM] = jnp.dot(p.astype(BF16), vg,
                                                                 preferred_element_type=F32)


def _attention(proj, bias, sink, batch, seq):
    T = proj.shape[0]
    nb = seq // BLOCK
    kcol = ATTN_WIDTH // KV_WIDTH
    vcol = kcol + 1

    def prev(i):
        return i - jnp.where(i % nb == 0, 0, 1)

    def nxt(i):
        return i + jnp.where(i % nb == nb - 1, 0, 1)

    kern = functools.partial(_attn_kernel, nb=nb, seq=seq)
    return pl.pallas_call(
        kern,
        grid=(T // BLOCK,),
        in_specs=[pl.BlockSpec(memory_space=pltpu.SMEM),
                  pl.BlockSpec((BLOCK, ATTN_WIDTH), lambda i: (i, 0)),
                  pl.BlockSpec((BLOCK, KV_WIDTH), lambda i: (prev(i), kcol)),
                  pl.BlockSpec((BLOCK, KV_WIDTH), lambda i: (i, kcol)),
                  pl.BlockSpec((BLOCK, KV_WIDTH), lambda i: (nxt(i), kcol)),
                  pl.BlockSpec((BLOCK, KV_WIDTH), lambda i: (prev(i), vcol)),
                  pl.BlockSpec((BLOCK, KV_WIDTH), lambda i: (i, vcol)),
                  pl.BlockSpec((BLOCK, KV_WIDTH), lambda i: (nxt(i), vcol)),
                  pl.BlockSpec((N_Q_HEADS, BLOCK, 3 * BLOCK), lambda i: (0, 0, 0))],
        out_specs=pl.BlockSpec((BLOCK, ATTN_WIDTH), lambda i: (i, 0)),
        out_shape=jax.ShapeDtypeStruct((T, ATTN_WIDTH), F32),
        compiler_params=_cparams(("parallel",)),
        name="window_attn",
    )(sink, proj, proj, proj, proj, proj, proj, proj, bias)


def _filter_kernel(z_ref, w1_ref, b1_ref, fr1_ref, w2_ref, b2_ref, fr2_ref, w3_ref, dl_ref, o_ref,
                   *, tt, seq):
    hi = lax.Precision.HIGHEST
    a = jnp.dot(z_ref[...], w1_ref[...], preferred_element_type=F32, precision=hi) + b1_ref[...]
    hid = jnp.sin(fr1_ref[...] * a)
    a = jnp.dot(hid, w2_ref[...], preferred_element_type=F32, precision=hi) + b2_ref[...]
    hid = jnp.sin(fr2_ref[...] * a)
    h = jnp.dot(hid, w3_ref[...], preferred_element_type=F32, precision=hi)
    rows = lax.broadcasted_iota(jnp.int32, h.shape, 0) + pl.program_id(0) * tt
    tn = rows.astype(F32) / float(max(seq - 1, 1))
    o_ref[...] = h * jnp.exp(-tn * dl_ref[...])


def _filters(seq, w1, b1, fr1, w2, b2, fr2, w3, tt=512):
    t = jnp.arange(seq, dtype=F32)
    tn = t / max(seq - 1, 1)
    w = 2.0 * math.pi * t / seq
    bands = jnp.linspace(1e-4, POS_BANDS - 1, POS_BANDS, dtype=F32)
    z = jnp.concatenate([tn[:, None], jnp.cos(w[:, None] * bands), -jnp.sin(w[:, None] * bands)], -1)
    zp = jnp.pad(z, ((0, 0), (0, FILTER_HIDDEN - POS_EMB)))
    w1p = jnp.pad(w1, ((0, FILTER_HIDDEN - POS_EMB), (0, 0)))
    max_decay = math.log(DECAY_TARGET) / FAST_DECAY_PCT
    min_decay = math.log(DECAY_TARGET) / SLOW_DECAY_PCT
    deltas = jnp.abs(jnp.linspace(min_decay, max_decay, HY_WIDTH, dtype=F32))
    ncol = HY_ORDER * 2 * HY_WIDTH
    dl = jnp.tile(deltas, HY_ORDER * 2).reshape(1, ncol)
    H = FILTER_HIDDEN
    kern = functools.partial(_filter_kernel, tt=tt, seq=seq)
    full = lambda r, c: pl.BlockSpec((r, c), lambda i: (0, 0))
    return pl.pallas_call(
        kern,
        grid=(seq // tt,),
        in_specs=[pl.BlockSpec((tt, H), lambda i: (i, 0)),
                  full(H, H), full(1, H), full(1, H), full(H, H), full(1, H), full(1, H),
                  full(H, ncol), full(1, ncol)],
        out_specs=pl.BlockSpec((tt, ncol), lambda i: (i, 0)),
        out_shape=jax.ShapeDtypeStruct((seq, ncol), F32),
        compiler_params=_cparams(("parallel",)),
        name="hyena_filter_mlp",
    )(zp, w1p, b1.reshape(1, H), fr1.reshape(1, H), w2, b2.reshape(1, H), fr2.reshape(1, H), w3, dl)


def _sconv_kernel(x_ref, w_ref, b_ref, o_ref):
    x = x_ref[0]
    L = x.shape[0]
    rows = lax.broadcasted_iota(jnp.int32, x.shape, 0)
    xm = jnp.where(rows == 0, 0.0, pltpu.roll(x, 1, 0))
    xp = jnp.where(rows == L - 1, 0.0, pltpu.roll(x, L - 1, 0))
    w = w_ref[...]
    o_ref[0] = xm * w[0:1] + x * w[1:2] + xp * w[2:3] + b_ref[...]


def _short_conv(proj3, conv_w, conv_b, col0, ct=256):
    B, L, _ = proj3.shape
    C = conv_w.shape[1]
    off = col0 // ct
    return pl.pallas_call(
        _sconv_kernel,
        grid=(B, C // ct),
        in_specs=[pl.BlockSpec((1, L, ct), lambda b, c: (b, 0, c + off)),
                  pl.BlockSpec((3, ct), lambda b, c: (0, c)),
                  pl.BlockSpec((1, ct), lambda b, c: (0, c))],
        out_specs=pl.BlockSpec((1, L, ct), lambda b, c: (b, 0, c)),
        out_shape=jax.ShapeDtypeStruct((B, L, C), F32),
        compiler_params=_cparams(("parallel", "parallel")),
        name="hyena_short_conv",
    )(proj3, conv_w, conv_b.reshape(1, C))


def _dft_constants():
    n1h = FFT_N1 // 2
    k1 = np.arange(FFT_K1_PAD)[:, None].astype(np.float64)
    n1 = np.arange(n1h)[None, :].astype(np.float64)
    ang = 2.0 * np.pi * k1 * n1 / FFT_N1
    live = (np.arange(FFT_K1_PAD) < FFT_K1)[:, None]
    f1 = np.concatenate([np.where(live, np.cos(ang), 0.0), np.where(live, -np.sin(ang), 0.0)], 0)
    kk1 = np.arange(FFT_K1)[:, None, None].astype(np.float64)
    k2 = np.arange(FFT_N2)[None, :, None].astype(np.float64)
    n2 = np.arange(FFT_N2)[None, None, :].astype(np.float64)
    phi = 2.0 * np.pi * (n2 * k2 / FFT_N2 + n2 * kk1 / (FFT_N1 * FFT_N2))
    c, s = np.cos(phi), np.sin(phi)
    g = np.concatenate([np.concatenate([c, s], 2), np.concatenate([-s, c], 2)], 1)
    ct_, st_ = np.transpose(c, (0, 2, 1)), np.transpose(s, (0, 2, 1))
    ginv = np.concatenate([np.concatenate([ct_, -st_], 2), np.concatenate([st_, ct_], 2)], 1)
    wk = np.where((np.arange(FFT_K1_PAD) == 0) | (np.arange(FFT_K1_PAD) == FFT_N1 // 2), 1.0, 2.0)
    wk = np.where(np.arange(FFT_K1_PAD) < FFT_K1, wk, 0.0)[None, :] / (FFT_N1 * FFT_N2)
    angi = 2.0 * np.pi * np.arange(n1h)[:, None] * np.arange(FFT_K1_PAD)[None, :] / FFT_N1
    finv = np.concatenate([wk * np.cos(angi), -wk * np.sin(angi)], 1)
    as_bf = lambda a: jnp.asarray(a.astype(np.float32)).astype(BF16)
    return as_bf(f1), as_bf(g), as_bf(finv), as_bf(ginv)


def _pad_rows_in(src_ref, xp_ref, zero_first_row=False):
    for n1 in range(FFT_N1 // 2):
        blk = src_ref[pl.ds(n1 * FFT_N2, FFT_N2), :]
        if zero_first_row and n1 == 0:
            rows = lax.broadcasted_iota(jnp.int32, blk.shape, 0)
            blk = jnp.where(rows == 0, 0.0, blk)
        xp_ref[pl.ds(n1 * X_PITCH, FFT_N2), :] = blk


def _fft_stage1(xp_ref, w_ref, f1_ref):
    n1h = FFT_N1 // 2
    kp = FFT_K1_PAD

    def body(i, carry):
        n2 = 2 * i
        xa = xp_ref[pl.ds(n2, n1h, stride=X_PITCH), :]
        xb = xp_ref[pl.ds(n2 + 1, n1h, stride=X_PITCH), :]
        xs = jnp.concatenate([xa, xb], axis=1).astype(BF16)
        r = jnp.dot(f1_ref[...], xs, preferred_element_type=F32)
        w_ref[pl.ds(n2, kp, stride=W_PITCH), :] = r[0:kp, 0:LANES]
        w_ref[pl.ds(n2 + 1, kp, stride=W_PITCH), :] = r[0:kp, LANES:2 * LANES]
        w_ref[pl.ds(FFT_N2 + n2, kp, stride=W_PITCH), :] = r[kp:2 * kp, 0:LANES]
        w_ref[pl.ds(FFT_N2 + n2 + 1, kp, stride=W_PITCH), :] = r[kp:2 * kp, LANES:2 * LANES]
        return carry

    lax.fori_loop(0, FFT_N2 // 2, body, 0, unroll=4)


def _kf_kernel(hf_ref, hb_ref, f1_ref, g_ref, kf_ref, wf_ref, wb_ref, xp_ref):
    l1 = (jnp.sum(jnp.abs(hf_ref[...]), 0, keepdims=True)
          + jnp.sum(jnp.abs(hb_ref[...]), 0, keepdims=True))
    scale = 1.0 / (l1 + 1e-6)
    _pad_rows_in(hf_ref, xp_ref)
    _fft_stage1(xp_ref, wf_ref, f1_ref)
    _pad_rows_in(hb_ref, xp_ref, zero_first_row=True)
    _fft_stage1(xp_ref, wb_ref, f1_ref)
    out = kf_ref.at[0]

    def body(k1, carry):
        src = pl.multiple_of(k1 * W_PITCH, 8)
        dst = pl.multiple_of(k1 * FFT_SLOT, FFT_SLOT)
        gk = g_ref[k1]
        sf = jnp.dot(gk, wf_ref[pl.ds(src, FFT_SLOT), :].astype(BF16), preferred_element_type=F32)
        sb = jnp.dot(gk, wb_ref[pl.ds(src, FFT_SLOT), :].astype(BF16), preferred_element_type=F32)
        out[pl.ds(dst, FFT_N2), :] = (sf[0:FFT_N2] + sb[0:FFT_N2]) * scale
        out[pl.ds(dst + FFT_N2, FFT_N2), :] = (sf[FFT_N2:FFT_SLOT] - sb[FFT_N2:FFT_SLOT]) * scale
        return carry

    lax.fori_loop(0, FFT_K1, body, 0, unroll=3)


def _filter_spectra(hfilt, consts):
    L = hfilt.shape[0]
    C = HY_WIDTH
    nct = C // LANES
    f1, g, _, _ = consts
    return pl.pallas_call(
        _kf_kernel,
        grid=(HY_ORDER, nct),
        in_specs=[pl.BlockSpec((L, LANES), lambda o, c: (0, o * 2 * nct + c)),
                  pl.BlockSpec((L, LANES), lambda o, c: (0, o * 2 * nct + nct + c)),
                  pl.BlockSpec(f1.shape, lambda o, c: (0, 0)),
                  pl.BlockSpec(g.shape, lambda o, c: (0, 0, 0))],
        out_specs=pl.BlockSpec((1, KF_ROWS, LANES), lambda o, c: (o, 0, c)),
        out_shape=jax.ShapeDtypeStruct((HY_ORDER, KF_ROWS, C), F32),
        scratch_shapes=[pltpu.VMEM((W_ROWS, LANES), F32), pltpu.VMEM((W_ROWS, LANES), F32),
                        pltpu.VMEM((X_ROWS, LANES), F32)],
        compiler_params=_cparams(("parallel", "parallel")),
        name="hyena_filter_fft",
    )(hfilt, hfilt, f1, g)


def _conv_kernel(z_ref, gate_ref, kf_ref, d_ref, f1_ref, g_ref, finv_ref, ginv_ref, o_ref, w_ref, xp_ref):
    zsrc = z_ref.at[0]
    kf = kf_ref.at[0]
    _pad_rows_in(zsrc, xp_ref)
    _fft_stage1(xp_ref, w_ref, f1_ref)

    def body(k1, carry):
        base = pl.multiple_of(k1 * W_PITCH, 8)
        kbase = pl.multiple_of(k1 * FFT_SLOT, FFT_SLOT)
        a = w_ref[pl.ds(base, FFT_SLOT), :].astype(BF16)
        s = jnp.dot(g_ref[k1], a, preferred_element_type=F32)
        sr, si = s[0:FFT_N2], s[FFT_N2:FFT_SLOT]
        kr = kf[pl.ds(kbase, FFT_N2), :]
        ki = kf[pl.ds(kbase + FFT_N2, FFT_N2), :]
        y = jnp.concatenate([sr * kr - si * ki, sr * ki + si * kr], axis=0).astype(BF16)
        w_ref[pl.ds(base, FFT_SLOT), :] = jnp.dot(ginv_ref[k1], y, preferred_element_type=F32)
        return carry

    lax.fori_loop(0, FFT_K1, body, 0, unroll=3)

    n1h = FFT_N1 // 2
    kp = FFT_K1_PAD

    def body2(i, carry):
        n2 = 2 * i
        zr = jnp.concatenate([w_ref[pl.ds(n2, kp, stride=W_PITCH), :],
                              w_ref[pl.ds(n2 + 1, kp, stride=W_PITCH), :]], axis=1)
        zi = jnp.concatenate([w_ref[pl.ds(FFT_N2 + n2, kp, stride=W_PITCH), :],
                              w_ref[pl.ds(FFT_N2 + n2 + 1, kp, stride=W_PITCH), :]], axis=1)
        zz = jnp.concatenate([zr, zi], axis=0).astype(BF16)
        x = jnp.dot(finv_ref[...], zz, preferred_element_type=F32)
        xp_ref[pl.ds(n2, n1h, stride=X_PITCH), :] = x[:, 0:LANES]
        xp_ref[pl.ds(n2 + 1, n1h, stride=X_PITCH), :] = x[:, LANES:2 * LANES]
        return carry

    lax.fori_loop(0, FFT_N2 // 2, body2, 0, unroll=4)
    d = d_ref[...]
    for n1 in range(n1h):
        rs = pl.ds(n1 * FFT_N2, FFT_N2)
        y = xp_ref[pl.ds(n1 * X_PITCH, FFT_N2), :]
        o_ref[0, rs, :] = gate_ref[0, rs, :] * (y + z_ref[0, rs, :] * d)


def _long_conv(z_arr, z_off, gate_arr, gate_off, kf_all, order, d, consts):
    B, L, _ = z_arr.shape
    C = HY_WIDTH
    nct = C // LANES
    f1, g, finv, ginv = consts
    cst2 = lambda a: pl.BlockSpec(a.shape, lambda c, b: (0, 0))
    cst3 = lambda a: pl.BlockSpec(a.shape, lambda c, b: (0, 0, 0))
    return pl.pallas_call(
        _conv_kernel,
        grid=(nct, B),
        in_specs=[pl.BlockSpec((1, L, LANES), lambda c, b: (b, 0, c + z_off)),
                  pl.BlockSpec((1, L, LANES), lambda c, b: (b, 0, c + gate_off)),
                  pl.BlockSpec((1, KF_ROWS, LANES), lambda c, b: (order, 0, c)),
                  pl.BlockSpec((1, LANES), lambda c, b: (0, c)),
                  cst2(f1), cst3(g), cst2(finv), cst3(ginv)],
        out_specs=pl.BlockSpec((1, L, LANES), lambda c, b: (b, 0, c)),
        out_shape=jax.ShapeDtypeStruct((B, L, C), F32),
        scratch_shapes=[pltpu.VMEM((W_ROWS, LANES), F32), pltpu.VMEM((X_ROWS, LANES), F32)],
        compiler_params=_cparams(("parallel", "parallel")),
        name=f"hyena_long_conv{order}",
    )(z_arr, gate_arr, kf_all, d.reshape(1, C), f1, g, finv, ginv)


def _mix_kernel(attn_ref, hyo_ref, h0_ref, mg_ref, w_ref, g_ref, b_ref, h1_ref, *, alpha):
    def gnorm(x, goff):
        parts = []
        for gi in range(x.shape[1] // NORM_GROUP):
            xg = x[:, gi * NORM_GROUP:(gi + 1) * NORM_GROUP]
            ms = jnp.mean(xg * xg, -1, keepdims=True)
            gg = mg_ref[:, goff + gi * NORM_GROUP: goff + (gi + 1) * NORM_GROUP]
            parts.append((xg * lax.rsqrt(ms + 1e-6) * gg).astype(BF16))
        return jnp.concatenate(parts, axis=1)

    a = gnorm(attn_ref[...], 0)
    y = gnorm(hyo_ref[...], ATTN_WIDTH)
    mix = (jnp.dot(a, w_ref[0:ATTN_WIDTH, :], preferred_element_type=F32)
           + jnp.dot(y, w_ref[ATTN_WIDTH:, :], preferred_element_type=F32))
    x = alpha * h0_ref[...] + mix
    mu = jnp.mean(x, -1, keepdims=True)
    xc = x - mu
    var = jnp.mean(xc * xc, -1, keepdims=True)
    h1_ref[...] = xc * lax.rsqrt(var + 1e-5) * g_ref[...] + b_ref[...]


def _mix_out(attn, hyo, h0, mix_g, w_out_bf, ln_g, ln_b, alpha, tm=256):
    T, D = h0.shape
    kern = functools.partial(_mix_kernel, alpha=alpha)
    return pl.pallas_call(
        kern,
        grid=(T // tm,),
        in_specs=[pl.BlockSpec((tm, ATTN_WIDTH), lambda i: (i, 0)),
                  pl.BlockSpec((tm, HY_WIDTH), lambda i: (i, 0)),
                  pl.BlockSpec((tm, D), lambda i: (i, 0)),
                  pl.BlockSpec((1, D), lambda i: (0, 0)),
                  pl.BlockSpec((D, D), lambda i: (0, 0)),
                  pl.BlockSpec((1, D), lambda i: (0, 0)),
                  pl.BlockSpec((1, D), lambda i: (0, 0))],
        out_specs=pl.BlockSpec((tm, D), lambda i: (i, 0)),
        out_shape=jax.ShapeDtypeStruct((T, D), F32),
        compiler_params=_cparams(("parallel",)),
        name="mix_out_ln1",
    )(attn, hyo, h0, mix_g.reshape(1, D), w_out_bf, ln_g.reshape(1, D), ln_b.reshape(1, D))


def _top_values(s, k):
    R = s.shape[0]
    rows = lax.broadcasted_iota(jnp.int32, s.shape, 0)
    vals = []
    for _ in range(k):
        m = jnp.max(s, axis=0, keepdims=True)
        first = jnp.min(jnp.where(s == m, rows, R), axis=0, keepdims=True)
        s = jnp.where(rows == first, -jnp.inf, s)
        vals.append(m)
    return vals


def _peer_score_kernel(wq_ref, h_ref, sk1_ref, sk2_ref, cnt_ref, r2_ref, e1_ref, e2_ref, q_ref):
    q_ref[...] = jnp.dot(wq_ref[...], h_ref[...], preferred_element_type=F32)

    def head(h, carry):
        base = pl.multiple_of(h * PEER_QDIM, PEER_QDIM)
        q1 = q_ref[pl.ds(base, PEER_QHALF), :].astype(BF16)
        q2 = q_ref[pl.ds(base + PEER_QHALF, PEER_QHALF), :].astype(BF16)
        s1 = jnp.dot(sk1_ref[...], q1, preferred_element_type=F32)
        s2 = jnp.dot(sk2_ref[...], q2, preferred_element_type=F32)
        v1 = _top_values(s1, PEER_TOPK)
        v2 = _top_values(s2, PEER_TOPK)
        v2all = jnp.concatenate(v2, axis=0)
        cand = jnp.concatenate([v1[a] + v2all for a in range(PEER_TOPK)], axis=0)
        sc = _top_values(cand, PEER_TOPK)
        zsum = sc[0] * 0.0
        for j in range(PEER_TOPK):
            zsum = zsum + jnp.exp(sc[j] - sc[0])
        th = sc[PEER_TOPK - 1]
        cnt = jnp.zeros_like(s1)
        r2 = jnp.zeros_like(s2)
        for b in range(PEER_TOPK):
            cnt = cnt + jnp.where(s1 + v2[b] >= th, 1.0, 0.0)
            r2 = r2 + jnp.where(v2[b] > s2, 1.0, 0.0)
        cnt_ref[h] = cnt
        r2_ref[h] = pltpu.bitcast(r2.astype(BF16), jnp.uint32)
        e1_ref[h] = 0.5 * jnp.exp(s1 - v1[0]) / zsum
        e2_ref[h] = pltpu.bitcast(jnp.exp(s2 - v2[0]).astype(BF16), jnp.uint32)
        return carry

    lax.fori_loop(0, PEER_HEADS, head, 0)


def _peer_scores(wqT_bf, h1T_bf, sk1_bf, sk2_bf, tl=256):
    D, T = h1T_bf.shape
    Q = wqT_bf.shape[0]
    big = jax.ShapeDtypeStruct((PEER_HEADS, N_KEYS, T), F32)
    big16 = jax.ShapeDtypeStruct((PEER_HEADS, N_KEYS // 2, T), jnp.uint32)
    bspec = pl.BlockSpec((PEER_HEADS, N_KEYS, tl), lambda i: (0, 0, i))
    pspec = pl.BlockSpec((PEER_HEADS, N_KEYS // 2, tl), lambda i: (0, 0, i))
    return pl.pallas_call(
        _peer_score_kernel,
        grid=(T // tl,),
        in_specs=[pl.BlockSpec((Q, D), lambda i: (0, 0)),
                  pl.BlockSpec((D, tl), lambda i: (0, i)),
                  pl.BlockSpec((N_KEYS, PEER_QHALF), lambda i: (0, 0)),
                  pl.BlockSpec((N_KEYS, PEER_QHALF), lambda i: (0, 0))],
        out_specs=[bspec, pspec, bspec, pspec],
        out_shape=[big, big16, big, big16],
        scratch_shapes=[pltpu.VMEM((Q, tl), F32)],
        compiler_params=_cparams(("parallel",)),
        name="peer_scores_topk",
    )(wqT_bf, h1T_bf, sk1_bf, sk2_bf)


W_ROWS_PER_GROUP = 64


def _peer_dense_kernel(hb_ref, u_ref, vt_ref, cnt_ref, r2_ref, e1_ref, e2_ref,
                       o_ref, acc_ref, a0_ref, a1_ref, w0_ref, w1_ref, *, te, tm, n_j, n_tiles):
    s = pl.program_id(0)
    sv = jnp.clip(s - 2, 0, n_tiles - 1)
    jv = sv % n_j
    jw = jnp.clip(s - 1, 0, n_tiles - 1) % n_j

    @pl.when(s == 0)
    def _():
        for ref in (a0_ref, a1_ref, w0_ref, w1_ref):
            ref[...] = jnp.zeros_like(ref)

    @pl.when(jv == 0)
    def _():
        acc_ref[...] = jnp.zeros_like(acc_ref)

    wide = 2 * LANES
    n_r, n_c = te // N_KEYS, tm // LANES
    d_model = acc_ref.shape[0]

    def body(a_cur, a_prv, w_cur, w_prv):
        def stage_a(q):
            ms = slice((q // 2) * wide, (q // 2 + 1) * wide)
            ls = slice((q % 2) * wide, (q % 2 + 1) * wide)
            a_cur[ms, ls] = jnp.dot(u_ref[ms, :], hb_ref[:, ls], preferred_element_type=F32)

        def stage_v(p):
            fs = slice((p // 2) * (d_model // 8), (p // 2 + 1) * (d_model // 8))
            ls = slice((p % 2) * wide, (p % 2 + 1) * wide)
            acc_ref[fs, ls] += jnp.dot(vt_ref[0, fs, :], w_cur[:, ls], preferred_element_type=F32)

        n_g = N_KEYS // W_ROWS_PER_GROUP

        grp = (W_ROWS_PER_GROUP, LANES)

        def stage_w(idx):
            p, g = idx // n_g, idx % n_g
            r, c = p // n_c, p % n_c
            i1 = jw * n_r + r
            cs = slice(c * LANES, (c + 1) * LANES)
            cntrows = cnt_ref[i1, :, cs]
            e1rows = e1_ref[i1, :, cs]
            ks = slice(g * W_ROWS_PER_GROUP // 2, (g + 1) * W_ROWS_PER_GROUP // 2)
            gate = jnp.zeros(grp, BF16)
            half = (W_ROWS_PER_GROUP // 2, LANES)
            for h in range(PEER_HEADS):
                cnt_b = pltpu.bitcast(jnp.broadcast_to(cntrows[h:h + 1], half), BF16)
                e1_b = pltpu.bitcast(jnp.broadcast_to(e1rows[h:h + 1], half), BF16)
                sel = pltpu.bitcast(r2_ref[h, ks, cs], BF16) < cnt_b
                val = pltpu.bitcast(e2_ref[h, ks, cs], BF16) * e1_b
                gate = gate + jnp.where(sel, val, jnp.zeros(grp, BF16))
            ws = slice(r * N_KEYS + g * W_ROWS_PER_GROUP, r * N_KEYS + (g + 1) * W_ROWS_PER_GROUP)
            a = a_prv[ws, cs]
            act = a * (1.0 + lax.erf(a * (1.0 / math.sqrt(2.0))))
            w_prv[ws, cs] = gate * act.astype(BF16)

        for q in range(4):
            stage_a(q)
            for t in range(4):
                stage_v(4 * q + t)
                for gg in range(2):
                    stage_w(8 * q + 2 * t + gg)

    @pl.when(s % 2 == 0)
    def _():
        body(a0_ref, a1_ref, w0_ref, w1_ref)

    @pl.when(s % 2 == 1)
    def _():
        body(a1_ref, a0_ref, w1_ref, w0_ref)

    @pl.when(jv == n_j - 1)
    def _():
        o_ref[...] = acc_ref[...]


def _peer_dense(h1T_bf, u_bf, v_bf, cnt, r2, e1, e2, tm=512, te=512):
    D, T = h1T_bf.shape
    E = u_bf.shape[0]
    assert te == 4 * N_KEYS and tm == 4 * LANES, "the stage interleave is written for 512 x 512 tiles"
    n_i, n_j = T // tm, E // te
    vT_bf = jnp.transpose(v_bf.reshape(n_j, te, D), (0, 2, 1))
    n_tiles = n_i * n_j
    kern = functools.partial(_peer_dense_kernel, te=te, tm=tm, n_j=n_j, n_tiles=n_tiles)
    ta = lambda s: jnp.minimum(s, n_tiles - 1)
    tw = lambda s: jnp.clip(s - 1, 0, n_tiles - 1)
    tv = lambda s: jnp.clip(s - 2, 0, n_tiles - 1)
    sspec = pl.BlockSpec((PEER_HEADS, N_KEYS // 2, tm), lambda s: (0, 0, tw(s) // n_j))
    xspec = pl.BlockSpec((N_KEYS, PEER_HEADS, tm), lambda s: (0, 0, tw(s) // n_j))
    def row_major_pairs(x):
        u = lax.bitcast_convert_type(x.astype(BF16), jnp.uint16).astype(jnp.uint32)
        return jnp.transpose((u << 16) | u, (1, 0, 2))

    cnt = row_major_pairs(cnt)
    e1 = row_major_pairs(e1)
    return pl.pallas_call(
        kern,
        grid=(n_tiles + 2,),
        in_specs=[pl.BlockSpec((D, tm), lambda s: (0, ta(s) // n_j)),
                  pl.BlockSpec((te, D), lambda s: (ta(s) % n_j, 0)),
                  pl.BlockSpec((1, D, te), lambda s: (tv(s) % n_j, 0, 0)),
                  xspec, sspec, xspec, sspec],
        out_specs=pl.BlockSpec((D, tm), lambda s: (0, tv(s) // n_j)),
        out_shape=jax.ShapeDtypeStruct((D, T), F32),
        scratch_shapes=[pltpu.VMEM((D, tm), F32),
                        pltpu.VMEM((te, tm), F32), pltpu.VMEM((te, tm), F32),
                        pltpu.VMEM((te, tm), BF16), pltpu.VMEM((te, tm), BF16)],
        compiler_params=_cparams(("arbitrary",)),
        name="peer_dense",
    )(h1T_bf, u_bf, vT_bf, cnt, r2, e1, e2)


def _res_ln_kernel(h_ref, f_ref, g_ref, b_ref, o_ref, *, alpha):
    x = alpha * h_ref[...] + f_ref[...]
    mu = jnp.mean(x, -1, keepdims=True)
    xc = x - mu
    var = jnp.mean(xc * xc, -1, keepdims=True)
    o_ref[...] = xc * lax.rsqrt(var + 1e-5) * g_ref[...] + b_ref[...]


def _res_ln(h, f, g, b, alpha, tm=512):
    T, D = h.shape
    row = pl.BlockSpec((tm, D), lambda i: (i, 0))
    vec = pl.BlockSpec((1, D), lambda i: (0, 0))
    return pl.pallas_call(
        functools.partial(_res_ln_kernel, alpha=alpha),
        grid=(T // tm,),
        in_specs=[row, row, vec, vec],
        out_specs=row,
        out_shape=jax.ShapeDtypeStruct((T, D), F32),
        compiler_params=_cparams(("parallel",)),
        name="res_ln2",
    )(h, f, g.reshape(1, D), b.reshape(1, D))


def kernel(x, ln0_g, ln0_b, rel_bias, w_in, sink, conv_w, conv_b, f_w1, f_b1, f_freq1, f_w2, f_b2,
           f_freq2, f_w3, hy_bias, mix_norm_g, w_out, ln1_g, ln1_b, peer_wq, peer_subkeys, peer_u,
           peer_v, ln2_g, ln2_b):
    B, S, D = x.shape
    T = B * S
    alpha = (2.0 * DEPTH) ** 0.25
    consts = _dft_constants()

    qi = jnp.arange(BLOCK, dtype=jnp.int32)
    kj = jnp.arange(3 * BLOCK, dtype=jnp.int32)
    rel = kj[None, :] - BLOCK - qi[:, None]
    onehot = (_t5_bucket(rel)[..., None] == jnp.arange(N_BUCKETS, dtype=jnp.int32)).astype(F32)
    bias = jnp.einsum("qkb,bh->hqk", onehot, rel_bias.astype(F32), precision=lax.Precision.HIGHEST)

    h, h_bf = _ln0(x.reshape(T, D), ln0_g, ln0_b)
    for l in range(DEPTH):
        proj = _matmul(h_bf, w_in[l].astype(BF16), tm=512, tn=1536)
        attn = _attention(proj, bias, sink[l], B, S)

        hfilt = _filters(S, f_w1[l], f_b1[l], f_freq1[l], f_w2[l], f_b2[l], f_freq2[l], f_w3[l])
        kf = _filter_spectra(hfilt, consts)
        u = _short_conv(proj.reshape(B, S, -1), conv_w[l], conv_b[l], ATTN_WIDTH + 2 * KV_WIDTH)
        nct = HY_WIDTH // LANES
        z1 = _long_conv(u, 0, u, nct, kf, 0, hy_bias[l, 0], consts)
        hyo = _long_conv(z1, 0, u, 2 * nct, kf, 1, hy_bias[l, 1], consts)

        h1 = _mix_out(attn, hyo.reshape(T, HY_WIDTH), h, mix_norm_g[l], w_out[l].astype(BF16),
                      ln1_g[l], ln1_b[l], alpha)

        h1T = h1.T
        h1T_bf = h1T.astype(BF16)
        cnt, r2, e1, e2 = _peer_scores(peer_wq[l].T.astype(BF16), h1T_bf,
                                       peer_subkeys[l, 0].astype(BF16), peer_subkeys[l, 1].astype(BF16))
        ffnT = _peer_dense(h1T_bf, peer_u[l].astype(BF16), peer_v[l].astype(BF16), cnt, r2, e1, e2)
        h = _res_ln(h1, ffnT.T, ln2_g[l], ln2_b[l], alpha)
        if l + 1 < DEPTH:
            h_bf = h.astype(BF16)
    return h.reshape(B, S, D)
```

```python
import functools
import math

import numpy as np
import jax
import jax.numpy as jnp
from jax import lax
from jax.experimental import pallas as pl
from jax.experimental.pallas import tpu as pltpu

F32 = jnp.float32
BF16 = jnp.bfloat16

D_MODEL = 2048
HEAD_DIM = 128
N_Q_HEADS = 8
N_KV_HEADS = 2
GQA_GROUP = N_Q_HEADS // N_KV_HEADS
ATTN_WIDTH = N_Q_HEADS * HEAD_DIM
KV_WIDTH = N_KV_HEADS * HEAD_DIM
WINDOW = 128
BLOCK = 128
N_BUCKETS = 32
MAX_DISTANCE = 128
HY_WIDTH = D_MODEL - ATTN_WIDTH
HY_ORDER = 2
POS_BANDS = 16
POS_EMB = 1 + 2 * POS_BANDS
FILTER_HIDDEN = 64
FAST_DECAY_PCT = 0.3
SLOW_DECAY_PCT = 1.5
DECAY_TARGET = 1e-2
NORM_GROUP = 128
N_KEYS = 128
PEER_HEADS = 8
PEER_QDIM = 256
PEER_QHALF = PEER_QDIM // 2
PEER_TOPK = 16
NEG = -1e30
DEPTH = 1

LANES = 128
VMEM_LIMIT = 56 * 1024 * 1024

FFT_N1 = 64
FFT_N2 = 128
FFT_K1 = FFT_N1 // 2 + 1
FFT_K1_PAD = 40
FFT_SLOT = 2 * FFT_N2
KF_ROWS = FFT_K1 * FFT_SLOT
SUBLANES = 8
W_PITCH = FFT_SLOT + SUBLANES
X_PITCH = FFT_N2 + SUBLANES
W_ROWS = FFT_K1_PAD * W_PITCH
X_ROWS = (FFT_N1 // 2) * X_PITCH


def _cparams(sem, vmem=VMEM_LIMIT):
    return pltpu.CompilerParams(dimension_semantics=sem, vmem_limit_bytes=vmem)


def _ln0_kernel(x_ref, g_ref, b_ref, h_ref, hb_ref):
    x = x_ref[...]
    mu = jnp.mean(x, -1, keepdims=True)
    xc = x - mu
    var = jnp.mean(xc * xc, -1, keepdims=True)
    y = xc * lax.rsqrt(var + 1e-5) * g_ref[...] + b_ref[...]
    h_ref[...] = y
    hb_ref[...] = y.astype(BF16)


def _ln0(x2d, g, b, tm=512):
    T, D = x2d.shape
    return pl.pallas_call(
        _ln0_kernel,
        grid=(T // tm,),
        in_specs=[pl.BlockSpec((tm, D), lambda i: (i, 0)),
                  pl.BlockSpec((1, D), lambda i: (0, 0)),
                  pl.BlockSpec((1, D), lambda i: (0, 0))],
        out_specs=[pl.BlockSpec((tm, D), lambda i: (i, 0)),
                   pl.BlockSpec((tm, D), lambda i: (i, 0))],
        out_shape=[jax.ShapeDtypeStruct((T, D), F32), jax.ShapeDtypeStruct((T, D), BF16)],
        compiler_params=_cparams(("parallel",)),
        name="ln0",
    )(x2d, g.reshape(1, D), b.reshape(1, D))


def _mm_kernel(a_ref, b_ref, o_ref):
    o_ref[...] = jnp.dot(a_ref[...], b_ref[...], preferred_element_type=F32)


def _matmul(a, b, tm, tn):
    M, K = a.shape
    N = b.shape[1]
    return pl.pallas_call(
        _mm_kernel,
        grid=(N // tn, M // tm),
        in_specs=[pl.BlockSpec((tm, K), lambda j, i: (i, 0)),
                  pl.BlockSpec((K, tn), lambda j, i: (0, j))],
        out_specs=pl.BlockSpec((tm, tn), lambda j, i: (i, j)),
        out_shape=jax.ShapeDtypeStruct((M, N), F32),
        compiler_params=_cparams(("parallel", "parallel")),
        name="in_proj",
    )(a, b)


def _t5_bucket(rel):
    nb = N_BUCKETS // 2
    ret = (rel > 0).astype(jnp.int32) * nb
    n = jnp.abs(rel)
    max_exact = nb // 2
    nf = jnp.maximum(n, 1).astype(F32)
    large = max_exact + (jnp.log(nf / max_exact) / math.log(MAX_DISTANCE / max_exact)
                         * (nb - max_exact)).astype(jnp.int32)
    large = jnp.minimum(large, nb - 1)
    return ret + jnp.where(n < max_exact, n, large)


def _attn_kernel(sink_ref, q_ref, kp_ref, kc_ref, kn_ref, vp_ref, vc_ref, vn_ref, bias_ref, o_ref,
                 *, nb, seq):
    n = pl.program_id(0) % nb
    qi = lax.broadcasted_iota(jnp.int32, (BLOCK, 3 * BLOCK), 0)
    kj = lax.broadcasted_iota(jnp.int32, (BLOCK, 3 * BLOCK), 1)
    rel = kj - BLOCK - qi
    kabs = n * BLOCK + kj - BLOCK
    valid = (jnp.abs(rel) <= WINDOW) & (kabs >= 0) & (kabs < seq)
    k = jnp.concatenate([kp_ref[...], kc_ref[...], kn_ref[...]], axis=0).astype(BF16)
    v = jnp.concatenate([vp_ref[...], vc_ref[...], vn_ref[...]], axis=0).astype(BF16)
    scale = 1.0 / math.sqrt(HEAD_DIM)
    for g in range(N_KV_HEADS):
        kg = k[:, g * HEAD_DIM:(g + 1) * HEAD_DIM]
        vg = v[:, g * HEAD_DIM:(g + 1) * HEAD_DIM]
        for r in range(GQA_GROUP):
            h = g * GQA_GROUP + r
            qh = q_ref[:, h * HEAD_DIM:(h + 1) * HEAD_DIM].astype(BF16)
            s = lax.dot_general(qh, kg, (((1,), (1,)), ((), ())), preferred_element_type=F32) * scale
            s = jnp.where(valid, s + bias_ref[h], NEG)
            sk = sink_ref[h]
            m = jnp.maximum(jnp.max(s, -1, keepdims=True), sk)
            p = jnp.exp(s - m)
            denom = jnp.sum(p, -1, keepdims=True) + jnp.exp(sk - m)
            p = p / denom
            o_ref[:, h * HEAD_DIM:(h + 1) * HEAD_DIM] = jnp.dot(p.astype(BF16), vg,
                                                                 preferred_element_type=F32)


def _attention(proj, bias, sink, batch, seq):
    T = proj.shape[0]
    nb = seq // BLOCK
    kcol = ATTN_WIDTH // KV_WIDTH
    vcol = kcol + 1

    def prev(i):
        return i - jnp.where(i % nb == 0, 0, 1)

    def nxt(i):
        return i + jnp.where(i % nb == nb - 1, 0, 1)

    kern = functools.partial(_attn_kernel, nb=nb, seq=seq)
    return pl.pallas_call(
        kern,
        grid=(T // BLOCK,),
        in_specs=[pl.BlockSpec(memory_space=pltpu.SMEM),
                  pl.BlockSpec((BLOCK, ATTN_WIDTH), lambda i: (i, 0)),
                  pl.BlockSpec((BLOCK, KV_WIDTH), lambda i: (prev(i), kcol)),
                  pl.BlockSpec((BLOCK, KV_WIDTH), lambda i: (i, kcol)),
                  pl.BlockSpec((BLOCK, KV_WIDTH), lambda i: (nxt(i), kcol)),
                  pl.BlockSpec((BLOCK, KV_WIDTH), lambda i: (prev(i), vcol)),
                  pl.BlockSpec((BLOCK, KV_WIDTH), lambda i: (i, vcol)),
                  pl.BlockSpec((BLOCK, KV_WIDTH), lambda i: (nxt(i), vcol)),
                  pl.BlockSpec((N_Q_HEADS, BLOCK, 3 * BLOCK), lambda i: (0, 0, 0))],
        out_specs=pl.BlockSpec((BLOCK, ATTN_WIDTH), lambda i: (i, 0)),
        out_shape=jax.ShapeDtypeStruct((T, ATTN_WIDTH), F32),
        compiler_params=_cparams(("parallel",)),
        name="window_attn",
    )(sink, proj, proj, proj, proj, proj, proj, proj, bias)


def _filter_kernel(z_ref, w1_ref, b1_ref, fr1_ref, w2_ref, b2_ref, fr2_ref, w3_ref, dl_ref, o_ref,
                   *, tt, seq):
    hi = lax.Precision.HIGHEST
    a = jnp.dot(z_ref[...], w1_ref[...], preferred_element_type=F32, precision=hi) + b1_ref[...]
    hid = jnp.sin(fr1_ref[...] * a)
    a = jnp.dot(hid, w2_ref[...], preferred_element_type=F32, precision=hi) + b2_ref[...]
    hid = jnp.sin(fr2_ref[...] * a)
    h = jnp.dot(hid, w3_ref[...], preferred_element_type=F32, precision=hi)
    rows = lax.broadcasted_iota(jnp.int32, h.shape, 0) + pl.program_id(0) * tt
    tn = rows.astype(F32) / float(max(seq - 1, 1))
    o_ref[...] = h * jnp.exp(-tn * dl_ref[...])


def _filters(seq, w1, b1, fr1, w2, b2, fr2, w3, tt=512):
    t = jnp.arange(seq, dtype=F32)
    tn = t / max(seq - 1, 1)
    w = 2.0 * math.pi * t / seq
    bands = jnp.linspace(1e-4, POS_BANDS - 1, POS_BANDS, dtype=F32)
    z = jnp.concatenate([tn[:, None], jnp.cos(w[:, None] * bands), -jnp.sin(w[:, None] * bands)], -1)
    zp = jnp.pad(z, ((0, 0), (0, FILTER_HIDDEN - POS_EMB)))
    w1p = jnp.pad(w1, ((0, FILTER_HIDDEN - POS_EMB), (0, 0)))
    max_decay = math.log(DECAY_TARGET) / FAST_DECAY_PCT
    min_decay = math.log(DECAY_TARGET) / SLOW_DECAY_PCT
    deltas = jnp.abs(jnp.linspace(min_decay, max_decay, HY_WIDTH, dtype=F32))
    ncol = HY_ORDER * 2 * HY_WIDTH
    dl = jnp.tile(deltas, HY_ORDER * 2).reshape(1, ncol)
    H = FILTER_HIDDEN
    kern = functools.partial(_filter_kernel, tt=tt, seq=seq)
    full = lambda r, c: pl.BlockSpec((r, c), lambda i: (0, 0))
    return pl.pallas_call(
        kern,
        grid=(seq // tt,),
        in_specs=[pl.BlockSpec((tt, H), lambda i: (i, 0)),
                  full(H, H), full(1, H), full(1, H), full(H, H), full(1, H), full(1, H),
                  full(H, ncol), full(1, ncol)],
        out_specs=pl.BlockSpec((tt, ncol), lambda i: (i, 0)),
        out_shape=jax.ShapeDtypeStruct((seq, ncol), F32),
        compiler_params=_cparams(("parallel",)),
        name="hyena_filter_mlp",
    )(zp, w1p, b1.reshape(1, H), fr1.reshape(1, H), w2, b2.reshape(1, H), fr2.reshape(1, H), w3, dl)


def _sconv_kernel(x_ref, w_ref, b_ref, o_ref):
    x = x_ref[0]
    L = x.shape[0]
    rows = lax.broadcasted_iota(jnp.int32, x.shape, 0)
    xm = jnp.where(rows == 0, 0.0, pltpu.roll(x, 1, 0))
    xp = jnp.where(rows == L - 1, 0.0, pltpu.roll(x, L - 1, 0))
    w = w_ref[...]
    o_ref[0] = xm * w[0:1] + x * w[1:2] + xp * w[2:3] + b_ref[...]


def _short_conv(proj3, conv_w, conv_b, col0, ct=256):
    B, L, _ = proj3.shape
    C = conv_w.shape[1]
    off = col0 // ct
    return pl.pallas_call(
        _sconv_kernel,
        grid=(B, C // ct),
        in_specs=[pl.BlockSpec((1, L, ct), lambda b, c: (b, 0, c + off)),
                  pl.BlockSpec((3, ct), lambda b, c: (0, c)),
                  pl.BlockSpec((1, ct), lambda b, c: (0, c))],
        out_specs=pl.BlockSpec((1, L, ct), lambda b, c: (b, 0, c)),
        out_shape=jax.ShapeDtypeStruct((B, L, C), F32),
        compiler_params=_cparams(("parallel", "parallel")),
        name="hyena_short_conv",
    )(proj3, conv_w, conv_b.reshape(1, C))


def _dft_constants():
    n1h = FFT_N1 // 2
    k1 = np.arange(FFT_K1_PAD)[:, None].astype(np.float64)
    n1 = np.arange(n1h)[None, :].astype(np.float64)
    ang = 2.0 * np.pi * k1 * n1 / FFT_N1
    live = (np.arange(FFT_K1_PAD) < FFT_K1)[:, None]
    f1 = np.concatenate([np.where(live, np.cos(ang), 0.0), np.where(live, -np.sin(ang), 0.0)], 0)
    kk1 = np.arange(FFT_K1)[:, None, None].astype(np.float64)
    k2 = np.arange(FFT_N2)[None, :, None].astype(np.float64)
    n2 = np.arange(FFT_N2)[None, None, :].astype(np.float64)
    phi = 2.0 * np.pi * (n2 * k2 / FFT_N2 + n2 * kk1 / (FFT_N1 * FFT_N2))
    c, s = np.cos(phi), np.sin(phi)
    g = np.concatenate([np.concatenate([c, s], 2), np.concatenate([-s, c], 2)], 1)
    ct_, st_ = np.transpose(c, (0, 2, 1)), np.transpose(s, (0, 2, 1))
    ginv = np.concatenate([np.concatenate([ct_, -st_], 2), np.concatenate([st_, ct_], 2)], 1)
    wk = np.where((np.arange(FFT_K1_PAD) == 0) | (np.arange(FFT_K1_PAD) == FFT_N1 // 2), 1.0, 2.0)
    wk = np.where(np.arange(FFT_K1_PAD) < FFT_K1, wk, 0.0)[None, :] / (FFT_N1 * FFT_N2)
    angi = 2.0 * np.pi * np.arange(n1h)[:, None] * np.arange(FFT_K1_PAD)[None, :] / FFT_N1
    finv = np.concatenate([wk * np.cos(angi), -wk * np.sin(angi)], 1)
    as_bf = lambda a: jnp.asarray(a.astype(np.float32)).astype(BF16)
    return as_bf(f1), as_bf(g), as_bf(finv), as_bf(ginv)


def _pad_rows_in(src_ref, xp_ref, zero_first_row=False):
    for n1 in range(FFT_N1 // 2):
        blk = src_ref[pl.ds(n1 * FFT_N2, FFT_N2), :]
        if zero_first_row and n1 == 0:
            rows = lax.broadcasted_iota(jnp.int32, blk.shape, 0)
            blk = jnp.where(rows == 0, 0.0, blk)
        xp_ref[pl.ds(n1 * X_PITCH, FFT_N2), :] = blk


def _fft_stage1(xp_ref, w_ref, f1_ref):
    n1h = FFT_N1 // 2
    kp = FFT_K1_PAD

    def body(i, carry):
        n2 = 2 * i
        xa = xp_ref[pl.ds(n2, n1h, stride=X_PITCH), :]
        xb = xp_ref[pl.ds(n2 + 1, n1h, stride=X_PITCH), :]
        xs = jnp.concatenate([xa, xb], axis=1).astype(BF16)
        r = jnp.dot(f1_ref[...], xs, preferred_element_type=F32)
        w_ref[pl.ds(n2, kp, stride=W_PITCH), :] = r[0:kp, 0:LANES]
        w_ref[pl.ds(n2 + 1, kp, stride=W_PITCH), :] = r[0:kp, LANES:2 * LANES]
        w_ref[pl.ds(FFT_N2 + n2, kp, stride=W_PITCH), :] = r[kp:2 * kp, 0:LANES]
        w_ref[pl.ds(FFT_N2 + n2 + 1, kp, stride=W_PITCH), :] = r[kp:2 * kp, LANES:2 * LANES]
        return carry

    lax.fori_loop(0, FFT_N2 // 2, body, 0, unroll=4)


def _kf_kernel(hf_ref, hb_ref, f1_ref, g_ref, kf_ref, wf_ref, wb_ref, xp_ref):
    l1 = (jnp.sum(jnp.abs(hf_ref[...]), 0, keepdims=True)
          + jnp.sum(jnp.abs(hb_ref[...]), 0, keepdims=True))
    scale = 1.0 / (l1 + 1e-6)
    _pad_rows_in(hf_ref, xp_ref)
    _fft_stage1(xp_ref, wf_ref, f1_ref)
    _pad_rows_in(hb_ref, xp_ref, zero_first_row=True)
    _fft_stage1(xp_ref, wb_ref, f1_ref)
    out = kf_ref.at[0]

    def body(k1, carry):
        src = pl.multiple_of(k1 * W_PITCH, 8)
        dst = pl.multiple_of(k1 * FFT_SLOT, FFT_SLOT)
        gk = g_ref[k1]
        sf = jnp.dot(gk, wf_ref[pl.ds(src, FFT_SLOT), :].astype(BF16), preferred_element_type=F32)
        sb = jnp.dot(gk, wb_ref[pl.ds(src, FFT_SLOT), :].astype(BF16), preferred_element_type=F32)
        out[pl.ds(dst, FFT_N2), :] = (sf[0:FFT_N2] + sb[0:FFT_N2]) * scale
        out[pl.ds(dst + FFT_N2, FFT_N2), :] = (sf[FFT_N2:FFT_SLOT] - sb[FFT_N2:FFT_SLOT]) * scale
        return carry

    lax.fori_loop(0, FFT_K1, body, 0, unroll=3)


def _filter_spectra(hfilt, consts):
    L = hfilt.shape[0]
    C = HY_WIDTH
    nct = C // LANES
    f1, g, _, _ = consts
    return pl.pallas_call(
        _kf_kernel,
        grid=(HY_ORDER, nct),
        in_specs=[pl.BlockSpec((L, LANES), lambda o, c: (0, o * 2 * nct + c)),
                  pl.BlockSpec((L, LANES), lambda o, c: (0, o * 2 * nct + nct + c)),
                  pl.BlockSpec(f1.shape, lambda o, c: (0, 0)),
                  pl.BlockSpec(g.shape, lambda o, c: (0, 0, 0))],
        out_specs=pl.BlockSpec((1, KF_ROWS, LANES), lambda o, c: (o, 0, c)),
        out_shape=jax.ShapeDtypeStruct((HY_ORDER, KF_ROWS, C), F32),
        scratch_shapes=[pltpu.VMEM((W_ROWS, LANES), F32), pltpu.VMEM((W_ROWS, LANES), F32),
                        pltpu.VMEM((X_ROWS, LANES), F32)],
        compiler_params=_cparams(("parallel", "parallel")),
        name="hyena_filter_fft",
    )(hfilt, hfilt, f1, g)


def _conv_kernel(z_ref, gate_ref, kf_ref, d_ref, f1_ref, g_ref, finv_ref, ginv_ref, o_ref, w_ref, xp_ref):
    zsrc = z_ref.at[0]
    kf = kf_ref.at[0]
    _pad_rows_in(zsrc, xp_ref)
    _fft_stage1(xp_ref, w_ref, f1_ref)

    def body(k1, carry):
        base = pl.multiple_of(k1 * W_PITCH, 8)
        kbase = pl.multiple_of(k1 * FFT_SLOT, FFT_SLOT)
        a = w_ref[pl.ds(base, FFT_SLOT), :].astype(BF16)
        s = jnp.dot(g_ref[k1], a, preferred_element_type=F32)
        sr, si = s[0:FFT_N2], s[FFT_N2:FFT_SLOT]
        kr = kf[pl.ds(kbase, FFT_N2), :]
        ki = kf[pl.ds(kbase + FFT_N2, FFT_N2), :]
        y = jnp.concatenate([sr * kr - si * ki, sr * ki + si * kr], axis=0).astype(BF16)
        w_ref[pl.ds(base, FFT_SLOT), :] = jnp.dot(ginv_ref[k1], y, preferred_element_type=F32)
        return carry

    lax.fori_loop(0, FFT_K1, body, 0, unroll=3)

    n1h = FFT_N1 // 2
    kp = FFT_K1_PAD

    def body2(i, carry):
        n2 = 2 * i
        zr = jnp.concatenate([w_ref[pl.ds(n2, kp, stride=W_PITCH), :],
                              w_ref[pl.ds(n2 + 1, kp, stride=W_PITCH), :]], axis=1)
        zi = jnp.concatenate([w_ref[pl.ds(FFT_N2 + n2, kp, stride=W_PITCH), :],
                              w_ref[pl.ds(FFT_N2 + n2 + 1, kp, stride=W_PITCH), :]], axis=1)
        zz = jnp.concatenate([zr, zi], axis=0).astype(BF16)
        x = jnp.dot(finv_ref[...], zz, preferred_element_type=F32)
        xp_ref[pl.ds(n2, n1h, stride=X_PITCH), :] = x[:, 0:LANES]
        xp_ref[pl.ds(n2 + 1, n1h, stride=X_PITCH), :] = x[:, LANES:2 * LANES]
        return carry

    lax.fori_loop(0, FFT_N2 // 2, body2, 0, unroll=4)
    d = d_ref[...]
    for n1 in range(n1h):
        rs = pl.ds(n1 * FFT_N2, FFT_N2)
        y = xp_ref[pl.ds(n1 * X_PITCH, FFT_N2), :]
        o_ref[0, rs, :] = gate_ref[0, rs, :] * (y + z_ref[0, rs, :] * d)


def _long_conv(z_arr, z_off, gate_arr, gate_off, kf_all, order, d, consts):
    B, L, _ = z_arr.shape
    C = HY_WIDTH
    nct = C // LANES
    f1, g, finv, ginv = consts
    cst2 = lambda a: pl.BlockSpec(a.shape, lambda c, b: (0, 0))
    cst3 = lambda a: pl.BlockSpec(a.shape, lambda c, b: (0, 0, 0))
    return pl.pallas_call(
        _conv_kernel,
        grid=(nct, B),
        in_specs=[pl.BlockSpec((1, L, LANES), lambda c, b: (b, 0, c + z_off)),
                  pl.BlockSpec((1, L, LANES), lambda c, b: (b, 0, c + gate_off)),
                  pl.BlockSpec((1, KF_ROWS, LANES), lambda c, b: (order, 0, c)),
                  pl.BlockSpec((1, LANES), lambda c, b: (0, c)),
                  cst2(f1), cst3(g), cst2(finv), cst3(ginv)],
        out_specs=pl.BlockSpec((1, L, LANES), lambda c, b: (b, 0, c)),
        out_shape=jax.ShapeDtypeStruct((B, L, C), F32),
        scratch_shapes=[pltpu.VMEM((W_ROWS, LANES), F32), pltpu.VMEM((X_ROWS, LANES), F32)],
        compiler_params=_cparams(("parallel", "parallel")),
        name=f"hyena_long_conv{order}",
    )(z_arr, gate_arr, kf_all, d.reshape(1, C), f1, g, finv, ginv)


def _mix_kernel(attn_ref, hyo_ref, h0_ref, mg_ref, w_ref, g_ref, b_ref, h1_ref, *, alpha):
    def gnorm(x, goff):
        parts = []
        for gi in range(x.shape[1] // NORM_GROUP):
            xg = x[:, gi * NORM_GROUP:(gi + 1) * NORM_GROUP]
            ms = jnp.mean(xg * xg, -1, keepdims=True)
            gg = mg_ref[:, goff + gi * NORM_GROUP: goff + (gi + 1) * NORM_GROUP]
            parts.append((xg * lax.rsqrt(ms + 1e-6) * gg).astype(BF16))
        return jnp.concatenate(parts, axis=1)

    a = gnorm(attn_ref[...], 0)
    y = gnorm(hyo_ref[...], ATTN_WIDTH)
    mix = (jnp.dot(a, w_ref[0:ATTN_WIDTH, :], preferred_element_type=F32)
           + jnp.dot(y, w_ref[ATTN_WIDTH:, :], preferred_element_type=F32))
    x = alpha * h0_ref[...] + mix
    mu = jnp.mean(x, -1, keepdims=True)
    xc = x - mu
    var = jnp.mean(xc * xc, -1, keepdims=True)
    h1_ref[...] = xc * lax.rsqrt(var + 1e-5) * g_ref[...] + b_ref[...]


def _mix_out(attn, hyo, h0, mix_g, w_out_bf, ln_g, ln_b, alpha, tm=256):
    T, D = h0.shape
    kern = functools.partial(_mix_kernel, alpha=alpha)
    return pl.pallas_call(
        kern,
        grid=(T // tm,),
        in_specs=[pl.BlockSpec((tm, ATTN_WIDTH), lambda i: (i, 0)),
                  pl.BlockSpec((tm, HY_WIDTH), lambda i: (i, 0)),
                  pl.BlockSpec((tm, D), lambda i: (i, 0)),
                  pl.BlockSpec((1, D), lambda i: (0, 0)),
                  pl.BlockSpec((D, D), lambda i: (0, 0)),
                  pl.BlockSpec((1, D), lambda i: (0, 0)),
                  pl.BlockSpec((1, D), lambda i: (0, 0))],
        out_specs=pl.BlockSpec((tm, D), lambda i: (i, 0)),
        out_shape=jax.ShapeDtypeStruct((T, D), F32),
        compiler_params=_cparams(("parallel",)),
        name="mix_out_ln1",
    )(attn, hyo, h0, mix_g.reshape(1, D), w_out_bf, ln_g.reshape(1, D), ln_b.reshape(1, D))


def _oddeven_sort_pairs(n):
    pairs = []

    def merge(lo, m, r):
        step = r * 2
        if step < m:
            merge(lo, m, step)
            merge(lo + r, m, step)
            pairs.extend((i, i + r) for i in range(lo + r, lo + m - r, step))
        else:
            pairs.append((lo, lo + r))

    def sort(lo, m):
        if m > 1:
            sort(lo, m // 2)
            sort(lo + m // 2, m // 2)
            merge(lo, m, 1)

    sort(0, n)
    return pairs


def _bitonic_merge_pairs(n):
    pairs, d = [], n // 2
    while d >= 1:
        pairs.extend((i, i + d) for i in range(n) if (i & d) == 0)
        d //= 2
    return pairs


def _apply_network(xs, pairs):
    xs = list(xs)
    for i, j in pairs:
        xs[i], xs[j] = jnp.maximum(xs[i], xs[j]), jnp.minimum(xs[i], xs[j])
    return xs


def _top16_replicated(xs):
    xs = _apply_network(xs, _oddeven_sort_pairs(PEER_TOPK))
    merge = _bitonic_merge_pairs(PEER_TOPK)
    for shift in (4, 2, 1):
        other = [pltpu.roll(x, shift, 0) for x in xs]
        xs = [jnp.maximum(xs[i], other[PEER_TOPK - 1 - i]) for i in range(PEER_TOPK)]
        xs = _apply_network(xs, merge)
    return xs


def _pair_sum_candidates(v1, v2):
    sub = lax.broadcasted_iota(jnp.int32, v1[0].shape, 0)

    def by_sublane(vs):
        out = vs[SUBLANES - 1]
        for j in range(SUBLANES - 2, -1, -1):
            out = jnp.where(sub == j, vs[j], out)
        return out

    ninf = -jnp.inf
    v2lo, v2hi = by_sublane(v2[:SUBLANES]), by_sublane(v2[SUBLANES:])
    v1lo, v1hi = by_sublane(v1[:SUBLANES]), by_sublane(v1[SUBLANES:])
    cands = [v1[0] + v2lo, v1[0] + v2hi, v1[1] + v2lo]
    for a in (2, 3, 4):
        cands.append(jnp.where(sub < PEER_TOPK // (a + 1), v1[a] + v2lo, ninf))
    cands.append(jnp.where(sub >= 5, v1lo + v2[0], ninf))
    cands.append(v1hi + v2[0])
    cands.append(jnp.where(sub >= 5, v1lo + v2[1], ninf))
    pad = jnp.full(v1[0].shape, ninf, F32)
    return cands + [pad] * (PEER_TOPK - len(cands))


def _peer_score_kernel(wq_ref, h_ref, sk1_ref, sk2_ref, cnt_ref, r2_ref, e1_ref, e2_ref, q_ref):
    q_ref[...] = jnp.dot(wq_ref[...], h_ref[...], preferred_element_type=F32)

    def head(h, carry):
        base = pl.multiple_of(h * PEER_QDIM, PEER_QDIM)
        q1 = q_ref[pl.ds(base, PEER_QHALF), :].astype(BF16)
        q2 = q_ref[pl.ds(base + PEER_QHALF, PEER_QHALF), :].astype(BF16)
        s1 = jnp.dot(sk1_ref[...], q1, preferred_element_type=F32)
        s2 = jnp.dot(sk2_ref[...], q2, preferred_element_type=F32)
        slabs = lambda s: [s[SUBLANES * i:SUBLANES * (i + 1)] for i in range(N_KEYS // SUBLANES)]
        v1 = _top16_replicated(slabs(s1))
        v2 = _top16_replicated(slabs(s2))
        sc = _top16_replicated(_pair_sum_candidates(v1, v2))
        zsum = jnp.zeros_like(sc[0])
        for j in range(PEER_TOPK):
            zsum = zsum + jnp.exp(sc[j] - sc[0])
        row = lambda x: x[0:1]
        th = row(sc[PEER_TOPK - 1])
        cnt = jnp.zeros_like(s1)
        r2 = jnp.zeros_like(s2)
        for b in range(PEER_TOPK):
            cnt = cnt + jnp.where(s1 + row(v2[b]) >= th, 1.0, 0.0)
            r2 = r2 + jnp.where(row(v2[b]) > s2, 1.0, 0.0)
        cnt_ref[h] = cnt
        r2_ref[h] = pltpu.bitcast(r2.astype(BF16), jnp.uint32)
        e1_ref[h] = 0.5 * jnp.exp(s1 - row(v1[0])) / row(zsum)
        e2_ref[h] = pltpu.bitcast(jnp.exp(s2 - row(v2[0])).astype(BF16), jnp.uint32)
        return carry

    lax.fori_loop(0, PEER_HEADS, head, 0)


def _peer_scores(wqT_bf, h1T_bf, sk1_bf, sk2_bf, tl=256):
    D, T = h1T_bf.shape
    Q = wqT_bf.shape[0]
    big = jax.ShapeDtypeStruct((PEER_HEADS, N_KEYS, T), F32)
    big16 = jax.ShapeDtypeStruct((PEER_HEADS, N_KEYS // 2, T), jnp.uint32)
    bspec = pl.BlockSpec((PEER_HEADS, N_KEYS, tl), lambda i: (0, 0, i))
    pspec = pl.BlockSpec((PEER_HEADS, N_KEYS // 2, tl), lambda i: (0, 0, i))
    return pl.pallas_call(
        _peer_score_kernel,
        grid=(T // tl,),
        in_specs=[pl.BlockSpec((Q, D), lambda i: (0, 0)),
                  pl.BlockSpec((D, tl), lambda i: (0, i)),
                  pl.BlockSpec((N_KEYS, PEER_QHALF), lambda i: (0, 0)),
                  pl.BlockSpec((N_KEYS, PEER_QHALF), lambda i: (0, 0))],
        out_specs=[bspec, pspec, bspec, pspec],
        out_shape=[big, big16, big, big16],
        scratch_shapes=[pltpu.VMEM((Q, tl), F32)],
        compiler_params=_cparams(("parallel",)),
        name="peer_scores_topk",
    )(wqT_bf, h1T_bf, sk1_bf, sk2_bf)


W_ROWS_PER_GROUP = 64


def _peer_dense_kernel(hb_ref, u_ref, vt_ref, cnt_ref, r2_ref, e1_ref, e2_ref,
                       o_ref, acc_ref, a0_ref, a1_ref, w0_ref, w1_ref, *, te, tm, n_j, n_tiles):
    s = pl.program_id(0)
    sv = jnp.clip(s - 2, 0, n_tiles - 1)
    jv = sv % n_j
    jw = jnp.clip(s - 1, 0, n_tiles - 1) % n_j

    @pl.when(s == 0)
    def _():
        for ref in (a0_ref, a1_ref, w0_ref, w1_ref):
            ref[...] = jnp.zeros_like(ref)

    @pl.when(jv == 0)
    def _():
        acc_ref[...] = jnp.zeros_like(acc_ref)

    wide = 2 * LANES
    n_r, n_c = te // N_KEYS, tm // LANES
    d_model = acc_ref.shape[0]

    def body(a_cur, a_prv, w_cur, w_prv):
        def stage_a(q):
            ms = slice((q // 2) * wide, (q // 2 + 1) * wide)
            ls = slice((q % 2) * wide, (q % 2 + 1) * wide)
            a_cur[ms, ls] = jnp.dot(u_ref[ms, :], hb_ref[:, ls], preferred_element_type=F32)

        def stage_v(p):
            fs = slice((p // 2) * (d_model // 8), (p // 2 + 1) * (d_model // 8))
            ls = slice((p % 2) * wide, (p % 2 + 1) * wide)
            acc_ref[fs, ls] += jnp.dot(vt_ref[0, fs, :], w_cur[:, ls], preferred_element_type=F32)

        n_g = N_KEYS // W_ROWS_PER_GROUP

        grp = (W_ROWS_PER_GROUP, LANES)

        def stage_w(idx):
            p, g = idx // n_g, idx % n_g
            r, c = p // n_c, p % n_c
            i1 = jw * n_r + r
            cs = slice(c * LANES, (c + 1) * LANES)
            cntrows = cnt_ref[i1, :, cs]
            e1rows = e1_ref[i1, :, cs]
            ks = slice(g * W_ROWS_PER_GROUP // 2, (g + 1) * W_ROWS_PER_GROUP // 2)
            gate = jnp.zeros(grp, BF16)
            half = (W_ROWS_PER_GROUP // 2, LANES)
            for h in range(PEER_HEADS):
                cnt_b = pltpu.bitcast(jnp.broadcast_to(cntrows[h:h + 1], half), BF16)
                e1_b = pltpu.bitcast(jnp.broadcast_to(e1rows[h:h + 1], half), BF16)
                sel = pltpu.bitcast(r2_ref[h, ks, cs], BF16) < cnt_b
                val = pltpu.bitcast(e2_ref[h, ks, cs], BF16) * e1_b
                gate = gate + jnp.where(sel, val, jnp.zeros(grp, BF16))
            ws = slice(r * N_KEYS + g * W_ROWS_PER_GROUP, r * N_KEYS + (g + 1) * W_ROWS_PER_GROUP)
            a = a_prv[ws, cs]
            act = a * (1.0 + lax.erf(a * (1.0 / math.sqrt(2.0))))
            w_prv[ws, cs] = gate * act.astype(BF16)

        for q in range(4):
            stage_a(q)
            for t in range(4):
                stage_v(4 * q + t)
                for gg in range(2):
                    stage_w(8 * q + 2 * t + gg)

    @pl.when(s % 2 == 0)
    def _():
        body(a0_ref, a1_ref, w0_ref, w1_ref)

    @pl.when(s % 2 == 1)
    def _():
        body(a1_ref, a0_ref, w1_ref, w0_ref)

    @pl.when(jv == n_j - 1)
    def _():
        o_ref[...] = acc_ref[...]


def _peer_dense(h1T_bf, u_bf, v_bf, cnt, r2, e1, e2, tm=512, te=512):
    D, T = h1T_bf.shape
    E = u_bf.shape[0]
    assert te == 4 * N_KEYS and tm == 4 * LANES, "the stage interleave is written for 512 x 512 tiles"
    n_i, n_j = T // tm, E // te
    vT_bf = jnp.transpose(v_bf.reshape(n_j, te, D), (0, 2, 1))
    n_tiles = n_i * n_j
    kern = functools.partial(_peer_dense_kernel, te=te, tm=tm, n_j=n_j, n_tiles=n_tiles)
    ta = lambda s: jnp.minimum(s, n_tiles - 1)
    tw = lambda s: jnp.clip(s - 1, 0, n_tiles - 1)
    tv = lambda s: jnp.clip(s - 2, 0, n_tiles - 1)
    sspec = pl.BlockSpec((PEER_HEADS, N_KEYS // 2, tm), lambda s: (0, 0, tw(s) // n_j))
    xspec = pl.BlockSpec((N_KEYS, PEER_HEADS, tm), lambda s: (0, 0, tw(s) // n_j))
    def row_major_pairs(x):
        u = lax.bitcast_convert_type(x.astype(BF16), jnp.uint16).astype(jnp.uint32)
        return jnp.transpose((u << 16) | u, (1, 0, 2))

    cnt = row_major_pairs(cnt)
    e1 = row_major_pairs(e1)
    return pl.pallas_call(
        kern,
        grid=(n_tiles + 2,),
        in_specs=[pl.BlockSpec((D, tm), lambda s: (0, ta(s) // n_j)),
                  pl.BlockSpec((te, D), lambda s: (ta(s) % n_j, 0)),
                  pl.BlockSpec((1, D, te), lambda s: (tv(s) % n_j, 0, 0)),
                  xspec, sspec, xspec, sspec],
        out_specs=pl.BlockSpec((D, tm), lambda s: (0, tv(s) // n_j)),
        out_shape=jax.ShapeDtypeStruct((D, T), F32),
        scratch_shapes=[pltpu.VMEM((D, tm), F32),
                        pltpu.VMEM((te, tm), F32), pltpu.VMEM((te, tm), F32),
                        pltpu.VMEM((te, tm), BF16), pltpu.VMEM((te, tm), BF16)],
        compiler_params=_cparams(("arbitrary",)),
        name="peer_dense",
    )(h1T_bf, u_bf, vT_bf, cnt, r2, e1, e2)


def _res_ln_kernel(h_ref, f_ref, g_ref, b_ref, o_ref, *, alpha):
    x = alpha * h_ref[...] + f_ref[...]
    mu = jnp.mean(x, -1, keepdims=True)
    xc = x - mu
    var = jnp.mean(xc * xc, -1, keepdims=True)
    o_ref[...] = xc * lax.rsqrt(var + 1e-5) * g_ref[...] + b_ref[...]


def _res_ln(h, f, g, b, alpha, tm=512):
    T, D = h.shape
    row = pl.BlockSpec((tm, D), lambda i: (i, 0))
    vec = pl.BlockSpec((1, D), lambda i: (0, 0))
    return pl.pallas_call(
        functools.partial(_res_ln_kernel, alpha=alpha),
        grid=(T // tm,),
        in_specs=[row, row, vec, vec],
        out_specs=row,
        out_shape=jax.ShapeDtypeStruct((T, D), F32),
        compiler_params=_cparams(("parallel",)),
        name="res_ln2",
    )(h, f, g.reshape(1, D), b.reshape(1, D))


def kernel(x, ln0_g, ln0_b, rel_bias, w_in, sink, conv_w, conv_b, f_w1, f_b1, f_freq1, f_w2, f_b2,
           f_freq2, f_w3, hy_bias, mix_norm_g, w_out, ln1_g, ln1_b, peer_wq, peer_subkeys, peer_u,
           peer_v, ln2_g, ln2_b):
    B, S, D = x.shape
    T = B * S
    alpha = (2.0 * DEPTH) ** 0.25
    consts = _dft_constants()

    qi = jnp.arange(BLOCK, dtype=jnp.int32)
    kj = jnp.arange(3 * BLOCK, dtype=jnp.int32)
    rel = kj[None, :] - BLOCK - qi[:, None]
    onehot = (_t5_bucket(rel)[..., None] == jnp.arange(N_BUCKETS, dtype=jnp.int32)).astype(F32)
    bias = jnp.einsum("qkb,bh->hqk", onehot, rel_bias.astype(F32), precision=lax.Precision.HIGHEST)

    h, h_bf = _ln0(x.reshape(T, D), ln0_g, ln0_b)
    for l in range(DEPTH):
        proj = _matmul(h_bf, w_in[l].astype(BF16), tm=512, tn=1536)
        attn = _attention(proj, bias, sink[l], B, S)

        hfilt = _filters(S, f_w1[l], f_b1[l], f_freq1[l], f_w2[l], f_b2[l], f_freq2[l], f_w3[l])
        kf = _filter_spectra(hfilt, consts)
        u = _short_conv(proj.reshape(B, S, -1), conv_w[l], conv_b[l], ATTN_WIDTH + 2 * KV_WIDTH)
        nct = HY_WIDTH // LANES
        z1 = _long_conv(u, 0, u, nct, kf, 0, hy_bias[l, 0], consts)
        hyo = _long_conv(z1, 0, u, 2 * nct, kf, 1, hy_bias[l, 1], consts)

        h1 = _mix_out(attn, hyo.reshape(T, HY_WIDTH), h, mix_norm_g[l], w_out[l].astype(BF16),
                      ln1_g[l], ln1_b[l], alpha)

        h1T = h1.T
        h1T_bf = h1T.astype(BF16)
        cnt, r2, e1, e2 = _peer_scores(peer_wq[l].T.astype(BF16), h1T_bf,
                                       peer_subkeys[l, 0].astype(BF16), peer_subkeys[l, 1].astype(BF16))
        ffnT = _peer_dense(h1T_bf, peer_u[l].astype(BF16), peer_v[l].astype(BF16), cnt, r2, e1, e2)
        h = _res_ln(h1, ffnT.T, ln2_g[l], ln2_b[l], alpha)
        if l + 1 < DEPTH:
            h_bf = h.astype(BF16)
    return h.reshape(B, S, D)
```

```python
import functools
import math

import numpy as np
import jax
import jax.numpy as jnp
from jax import lax
from jax.experimental import pallas as pl
from jax.experimental.pallas import tpu as pltpu

F32 = jnp.float32
BF16 = jnp.bfloat16

D_MODEL = 2048
HEAD_DIM = 128
N_Q_HEADS = 8
N_KV_HEADS = 2
GQA_GROUP = N_Q_HEADS // N_KV_HEADS
ATTN_WIDTH = N_Q_HEADS * HEAD_DIM
KV_WIDTH = N_KV_HEADS * HEAD_DIM
WINDOW = 128
BLOCK = 128
N_BUCKETS = 32
MAX_DISTANCE = 128
HY_WIDTH = D_MODEL - ATTN_WIDTH
HY_ORDER = 2
POS_BANDS = 16
POS_EMB = 1 + 2 * POS_BANDS
FILTER_HIDDEN = 64
FAST_DECAY_PCT = 0.3
SLOW_DECAY_PCT = 1.5
DECAY_TARGET = 1e-2
NORM_GROUP = 128
N_KEYS = 128
PEER_HEADS = 8
PEER_QDIM = 256
PEER_QHALF = PEER_QDIM // 2
PEER_TOPK = 16
NEG = -1e30
DEPTH = 1

LANES = 128
VMEM_LIMIT = 56 * 1024 * 1024

FFT_N1 = 64
FFT_N2 = 128
FFT_K1 = FFT_N1 // 2 + 1
FFT_K1_PAD = 40
FFT_SLOT = 2 * FFT_N2
KF_ROWS = FFT_K1 * FFT_SLOT
SUBLANES = 8
W_PITCH = FFT_SLOT + SUBLANES
X_PITCH = FFT_N2 + SUBLANES
W_ROWS = FFT_K1_PAD * W_PITCH
X_ROWS = (FFT_N1 // 2) * X_PITCH


def _cparams(sem, vmem=VMEM_LIMIT):
    return pltpu.CompilerParams(dimension_semantics=sem, vmem_limit_bytes=vmem)


def _ln0_kernel(x_ref, g_ref, b_ref, h_ref, hb_ref):
    x = x_ref[...]
    mu = jnp.mean(x, -1, keepdims=True)
    xc = x - mu
    var = jnp.mean(xc * xc, -1, keepdims=True)
    y = xc * lax.rsqrt(var + 1e-5) * g_ref[...] + b_ref[...]
    h_ref[...] = y
    hb_ref[...] = y.astype(BF16)


def _ln0(x2d, g, b, tm=512):
    T, D = x2d.shape
    return pl.pallas_call(
        _ln0_kernel,
        grid=(T // tm,),
        in_specs=[pl.BlockSpec((tm, D), lambda i: (i, 0)),
                  pl.BlockSpec((1, D), lambda i: (0, 0)),
                  pl.BlockSpec((1, D), lambda i: (0, 0))],
        out_specs=[pl.BlockSpec((tm, D), lambda i: (i, 0)),
                   pl.BlockSpec((tm, D), lambda i: (i, 0))],
        out_shape=[jax.ShapeDtypeStruct((T, D), F32), jax.ShapeDtypeStruct((T, D), BF16)],
        compiler_params=_cparams(("parallel",)),
        name="ln0",
    )(x2d, g.reshape(1, D), b.reshape(1, D))


def _mm_kernel(a_ref, b_ref, o_ref):
    o_ref[...] = jnp.dot(a_ref[...], b_ref[...], preferred_element_type=F32)


def _matmul(a, b, tm, tn):
    M, K = a.shape
    N = b.shape[1]
    return pl.pallas_call(
        _mm_kernel,
        grid=(N // tn, M // tm),
        in_specs=[pl.BlockSpec((tm, K), lambda j, i: (i, 0)),
                  pl.BlockSpec((K, tn), lambda j, i: (0, j))],
        out_specs=pl.BlockSpec((tm, tn), lambda j, i: (i, j)),
        out_shape=jax.ShapeDtypeStruct((M, N), F32),
        compiler_params=_cparams(("parallel", "parallel")),
        name="in_proj",
    )(a, b)


def _t5_bucket(rel):
    nb = N_BUCKETS // 2
    ret = (rel > 0).astype(jnp.int32) * nb
    n = jnp.abs(rel)
    max_exact = nb // 2
    nf = jnp.maximum(n, 1).astype(F32)
    large = max_exact + (jnp.log(nf / max_exact) / math.log(MAX_DISTANCE / max_exact)
                         * (nb - max_exact)).astype(jnp.int32)
    large = jnp.minimum(large, nb - 1)
    return ret + jnp.where(n < max_exact, n, large)


def _attn_kernel(sink_ref, q_ref, kp_ref, kc_ref, kn_ref, vp_ref, vc_ref, vn_ref, bias_ref, o_ref,
                 *, nb, seq):
    n = pl.program_id(0) % nb
    qi = lax.broadcasted_iota(jnp.int32, (BLOCK, 3 * BLOCK), 0)
    kj = lax.broadcasted_iota(jnp.int32, (BLOCK, 3 * BLOCK), 1)
    rel = kj - BLOCK - qi
    kabs = n * BLOCK + kj - BLOCK
    valid = (jnp.abs(rel) <= WINDOW) & (kabs >= 0) & (kabs < seq)
    k = jnp.concatenate([kp_ref[...], kc_ref[...], kn_ref[...]], axis=0).astype(BF16)
    v = jnp.concatenate([vp_ref[...], vc_ref[...], vn_ref[...]], axis=0).astype(BF16)
    scale = 1.0 / math.sqrt(HEAD_DIM)
    for g in range(N_KV_HEADS):
        kg = k[:, g * HEAD_DIM:(g + 1) * HEAD_DIM]
        vg = v[:, g * HEAD_DIM:(g + 1) * HEAD_DIM]
        for r in range(GQA_GROUP):
            h = g * GQA_GROUP + r
            qh = q_ref[:, h * HEAD_DIM:(h + 1) * HEAD_DIM].astype(BF16)
            s = lax.dot_general(qh, kg, (((1,), (1,)), ((), ())), preferred_element_type=F32) * scale
            s = jnp.where(valid, s + bias_ref[h], NEG)
            sk = sink_ref[h]
            m = jnp.maximum(jnp.max(s, -1, keepdims=True), sk)
            p = jnp.exp(s - m)
            denom = jnp.sum(p, -1, keepdims=True) + jnp.exp(sk - m)
            p = p / denom
            o_ref[:, h * HEAD_DIM:(h + 1) * HEAD_DIM] = jnp.dot(p.astype(BF16), vg,
                                                                 preferred_element_type=F32)


def _attention(proj, bias, sink, batch, seq):
    T = proj.shape[0]
    nb = seq // BLOCK
    kcol = ATTN_WIDTH // KV_WIDTH
    vcol = kcol + 1

    def prev(i):
        return i - jnp.where(i % nb == 0, 0, 1)

    def nxt(i):
        return i + jnp.where(i % nb == nb - 1, 0, 1)

    kern = functools.partial(_attn_kernel, nb=nb, seq=seq)
    return pl.pallas_call(
        kern,
        grid=(T // BLOCK,),
        in_specs=[pl.BlockSpec(memory_space=pltpu.SMEM),
                  pl.BlockSpec((BLOCK, ATTN_WIDTH), lambda i: (i, 0)),
                  pl.BlockSpec((BLOCK, KV_WIDTH), lambda i: (prev(i), kcol)),
                  pl.BlockSpec((BLOCK, KV_WIDTH), lambda i: (i, kcol)),
                  pl.BlockSpec((BLOCK, KV_WIDTH), lambda i: (nxt(i), kcol)),
                  pl.BlockSpec((BLOCK, KV_WIDTH), lambda i: (prev(i), vcol)),
                  pl.BlockSpec((BLOCK, KV_WIDTH), lambda i: (i, vcol)),
                  pl.BlockSpec((BLOCK, KV_WIDTH), lambda i: (nxt(i), vcol)),
                  pl.BlockSpec((N_Q_HEADS, BLOCK, 3 * BLOCK), lambda i: (0, 0, 0))],
        out_specs=pl.BlockSpec((BLOCK, ATTN_WIDTH), lambda i: (i, 0)),
        out_shape=jax.ShapeDtypeStruct((T, ATTN_WIDTH), F32),
        compiler_params=_cparams(("parallel",)),
        name="window_attn",
    )(sink, proj, proj, proj, proj, proj, proj, proj, bias)


def _filter_kernel(z_ref, w1_ref, b1_ref, fr1_ref, w2_ref, b2_ref, fr2_ref, w3_ref, dl_ref, o_ref,
                   *, tt, seq):
    hi = lax.Precision.HIGHEST
    a = jnp.dot(z_ref[...], w1_ref[...], preferred_element_type=F32, precision=hi) + b1_ref[...]
    hid = jnp.sin(fr1_ref[...] * a)
    a = jnp.dot(hid, w2_ref[...], preferred_element_type=F32, precision=hi) + b2_ref[...]
    hid = jnp.sin(fr2_ref[...] * a)
    h = jnp.dot(hid, w3_ref[...], preferred_element_type=F32, precision=hi)
    rows = lax.broadcasted_iota(jnp.int32, h.shape, 0) + pl.program_id(0) * tt
    tn = rows.astype(F32) / float(max(seq - 1, 1))
    o_ref[...] = h * jnp.exp(-tn * dl_ref[...])


def _filters(seq, w1, b1, fr1, w2, b2, fr2, w3, tt=512):
    t = jnp.arange(seq, dtype=F32)
    tn = t / max(seq - 1, 1)
    w = 2.0 * math.pi * t / seq
    bands = jnp.linspace(1e-4, POS_BANDS - 1, POS_BANDS, dtype=F32)
    z = jnp.concatenate([tn[:, None], jnp.cos(w[:, None] * bands), -jnp.sin(w[:, None] * bands)], -1)
    zp = jnp.pad(z, ((0, 0), (0, FILTER_HIDDEN - POS_EMB)))
    w1p = jnp.pad(w1, ((0, FILTER_HIDDEN - POS_EMB), (0, 0)))
    max_decay = math.log(DECAY_TARGET) / FAST_DECAY_PCT
    min_decay = math.log(DECAY_TARGET) / SLOW_DECAY_PCT
    deltas = jnp.abs(jnp.linspace(min_decay, max_decay, HY_WIDTH, dtype=F32))
    ncol = HY_ORDER * 2 * HY_WIDTH
    dl = jnp.tile(deltas, HY_ORDER * 2).reshape(1, ncol)
    H = FILTER_HIDDEN
    kern = functools.partial(_filter_kernel, tt=tt, seq=seq)
    full = lambda r, c: pl.BlockSpec((r, c), lambda i: (0, 0))
    return pl.pallas_call(
        kern,
        grid=(seq // tt,),
        in_specs=[pl.BlockSpec((tt, H), lambda i: (i, 0)),
                  full(H, H), full(1, H), full(1, H), full(H, H), full(1, H), full(1, H),
                  full(H, ncol), full(1, ncol)],
        out_specs=pl.BlockSpec((tt, ncol), lambda i: (i, 0)),
        out_shape=jax.ShapeDtypeStruct((seq, ncol), F32),
        compiler_params=_cparams(("parallel",)),
        name="hyena_filter_mlp",
    )(zp, w1p, b1.reshape(1, H), fr1.reshape(1, H), w2, b2.reshape(1, H), fr2.reshape(1, H), w3, dl)


def _sconv_kernel(x_ref, w_ref, b_ref, o_ref):
    x = x_ref[0]
    L = x.shape[0]
    rows = lax.broadcasted_iota(jnp.int32, x.shape, 0)
    xm = jnp.where(rows == 0, 0.0, pltpu.roll(x, 1, 0))
    xp = jnp.where(rows == L - 1, 0.0, pltpu.roll(x, L - 1, 0))
    w = w_ref[...]
    o_ref[0] = xm * w[0:1] + x * w[1:2] + xp * w[2:3] + b_ref[...]


def _short_conv(proj3, conv_w, conv_b, col0, ct=256):
    B, L, _ = proj3.shape
    C = conv_w.shape[1]
    off = col0 // ct
    return pl.pallas_call(
        _sconv_kernel,
        grid=(B, C // ct),
        in_specs=[pl.BlockSpec((1, L, ct), lambda b, c: (b, 0, c + off)),
                  pl.BlockSpec((3, ct), lambda b, c: (0, c)),
                  pl.BlockSpec((1, ct), lambda b, c: (0, c))],
        out_specs=pl.BlockSpec((1, L, ct), lambda b, c: (b, 0, c)),
        out_shape=jax.ShapeDtypeStruct((B, L, C), F32),
        compiler_params=_cparams(("parallel", "parallel")),
        name="hyena_short_conv",
    )(proj3, conv_w, conv_b.reshape(1, C))


def _dft_constants():
    n1h = FFT_N1 // 2
    k1 = np.arange(FFT_K1_PAD)[:, None].astype(np.float64)
    n1 = np.arange(n1h)[None, :].astype(np.float64)
    ang = 2.0 * np.pi * k1 * n1 / FFT_N1
    live = (np.arange(FFT_K1_PAD) < FFT_K1)[:, None]
    f1 = np.concatenate([np.where(live, np.cos(ang), 0.0), np.where(live, -np.sin(ang), 0.0)], 0)
    kk1 = np.arange(FFT_K1)[:, None, None].astype(np.float64)
    k2 = np.arange(FFT_N2)[None, :, None].astype(np.float64)
    n2 = np.arange(FFT_N2)[None, None, :].astype(np.float64)
    phi = 2.0 * np.pi * (n2 * k2 / FFT_N2 + n2 * kk1 / (FFT_N1 * FFT_N2))
    c, s = np.cos(phi), np.sin(phi)
    g = np.concatenate([np.concatenate([c, s], 2), np.concatenate([-s, c], 2)], 1)
    ct_, st_ = np.transpose(c, (0, 2, 1)), np.transpose(s, (0, 2, 1))
    ginv = np.concatenate([np.concatenate([ct_, -st_], 2), np.concatenate([st_, ct_], 2)], 1)
    wk = np.where((np.arange(FFT_K1_PAD) == 0) | (np.arange(FFT_K1_PAD) == FFT_N1 // 2), 1.0, 2.0)
    wk = np.where(np.arange(FFT_K1_PAD) < FFT_K1, wk, 0.0)[None, :] / (FFT_N1 * FFT_N2)
    angi = 2.0 * np.pi * np.arange(n1h)[:, None] * np.arange(FFT_K1_PAD)[None, :] / FFT_N1
    finv = np.concatenate([wk * np.cos(angi), -wk * np.sin(angi)], 1)
    as_bf = lambda a: jnp.asarray(a.astype(np.float32)).astype(BF16)
    return as_bf(f1), as_bf(g), as_bf(finv), as_bf(ginv)


def _pad_rows_in(src_ref, xp_ref, zero_first_row=False):
    for n1 in range(FFT_N1 // 2):
        blk = src_ref[pl.ds(n1 * FFT_N2, FFT_N2), :]
        if zero_first_row and n1 == 0:
            rows = lax.broadcasted_iota(jnp.int32, blk.shape, 0)
            blk = jnp.where(rows == 0, 0.0, blk)
        xp_ref[pl.ds(n1 * X_PITCH, FFT_N2), :] = blk


def _fft_stage1(xp_ref, w_ref, f1_ref):
    n1h = FFT_N1 // 2
    kp = FFT_K1_PAD

    def body(i, carry):
        n2 = 2 * i
        xa = xp_ref[pl.ds(n2, n1h, stride=X_PITCH), :]
        xb = xp_ref[pl.ds(n2 + 1, n1h, stride=X_PITCH), :]
        xs = jnp.concatenate([xa, xb], axis=1).astype(BF16)
        r = jnp.dot(f1_ref[...], xs, preferred_element_type=F32)
        w_ref[pl.ds(n2, kp, stride=W_PITCH), :] = r[0:kp, 0:LANES]
        w_ref[pl.ds(n2 + 1, kp, stride=W_PITCH), :] = r[0:kp, LANES:2 * LANES]
        w_ref[pl.ds(FFT_N2 + n2, kp, stride=W_PITCH), :] = r[kp:2 * kp, 0:LANES]
        w_ref[pl.ds(FFT_N2 + n2 + 1, kp, stride=W_PITCH), :] = r[kp:2 * kp, LANES:2 * LANES]
        return carry

    lax.fori_loop(0, FFT_N2 // 2, body, 0, unroll=4)


def _kf_kernel(hf_ref, hb_ref, f1_ref, g_ref, kf_ref, wf_ref, wb_ref, xp_ref):
    l1 = (jnp.sum(jnp.abs(hf_ref[...]), 0, keepdims=True)
          + jnp.sum(jnp.abs(hb_ref[...]), 0, keepdims=True))
    scale = 1.0 / (l1 + 1e-6)
    _pad_rows_in(hf_ref, xp_ref)
    _fft_stage1(xp_ref, wf_ref, f1_ref)
    _pad_rows_in(hb_ref, xp_ref, zero_first_row=True)
    _fft_stage1(xp_ref, wb_ref, f1_ref)
    out = kf_ref.at[0]

    def body(k1, carry):
        src = pl.multiple_of(k1 * W_PITCH, 8)
        dst = pl.multiple_of(k1 * FFT_SLOT, FFT_SLOT)
        gk = g_ref[k1]
        sf = jnp.dot(gk, wf_ref[pl.ds(src, FFT_SLOT), :].astype(BF16), preferred_element_type=F32)
        sb = jnp.dot(gk, wb_ref[pl.ds(src, FFT_SLOT), :].astype(BF16), preferred_element_type=F32)
        out[pl.ds(dst, FFT_N2), :] = (sf[0:FFT_N2] + sb[0:FFT_N2]) * scale
        out[pl.ds(dst + FFT_N2, FFT_N2), :] = (sf[FFT_N2:FFT_SLOT] - sb[FFT_N2:FFT_SLOT]) * scale
        return carry

    lax.fori_loop(0, FFT_K1, body, 0, unroll=3)


def _filter_spectra(hfilt, consts):
    L = hfilt.shape[0]
    C = HY_WIDTH
    nct = C // LANES
    f1, g, _, _ = consts
    return pl.pallas_call(
        _kf_kernel,
        grid=(HY_ORDER, nct),
        in_specs=[pl.BlockSpec((L, LANES), lambda o, c: (0, o * 2 * nct + c)),
                  pl.BlockSpec((L, LANES), lambda o, c: (0, o * 2 * nct + nct + c)),
                  pl.BlockSpec(f1.shape, lambda o, c: (0, 0)),
                  pl.BlockSpec(g.shape, lambda o, c: (0, 0, 0))],
        out_specs=pl.BlockSpec((1, KF_ROWS, LANES), lambda o, c: (o, 0, c)),
        out_shape=jax.ShapeDtypeStruct((HY_ORDER, KF_ROWS, C), F32),
        scratch_shapes=[pltpu.VMEM((W_ROWS, LANES), F32), pltpu.VMEM((W_ROWS, LANES), F32),
                        pltpu.VMEM((X_ROWS, LANES), F32)],
        compiler_params=_cparams(("parallel", "parallel")),
        name="hyena_filter_fft",
    )(hfilt, hfilt, f1, g)


def _conv_kernel(z_ref, gate_ref, kf_ref, d_ref, f1_ref, g_ref, finv_ref, ginv_ref, o_ref, w_ref, xp_ref):
    zsrc = z_ref.at[0]
    kf = kf_ref.at[0]
    _pad_rows_in(zsrc, xp_ref)
    _fft_stage1(xp_ref, w_ref, f1_ref)

    def body(k1, carry):
        base = pl.multiple_of(k1 * W_PITCH, 8)
        kbase = pl.multiple_of(k1 * FFT_SLOT, FFT_SLOT)
        a = w_ref[pl.ds(base, FFT_SLOT), :].astype(BF16)
        s = jnp.dot(g_ref[k1], a, preferred_element_type=F32)
        sr, si = s[0:FFT_N2], s[FFT_N2:FFT_SLOT]
        kr = kf[pl.ds(kbase, FFT_N2), :]
        ki = kf[pl.ds(kbase + FFT_N2, FFT_N2), :]
        y = jnp.concatenate([sr * kr - si * ki, sr * ki + si * kr], axis=0).astype(BF16)
        w_ref[pl.ds(base, FFT_SLOT), :] = jnp.dot(ginv_ref[k1], y, preferred_element_type=F32)
        return carry

    lax.fori_loop(0, FFT_K1, body, 0, unroll=3)

    n1h = FFT_N1 // 2
    kp = FFT_K1_PAD

    def body2(i, carry):
        n2 = 2 * i
        zr = jnp.concatenate([w_ref[pl.ds(n2, kp, stride=W_PITCH), :],
                              w_ref[pl.ds(n2 + 1, kp, stride=W_PITCH), :]], axis=1)
        zi = jnp.concatenate([w_ref[pl.ds(FFT_N2 + n2, kp, stride=W_PITCH), :],
                              w_ref[pl.ds(FFT_N2 + n2 + 1, kp, stride=W_PITCH), :]], axis=1)
        zz = jnp.concatenate([zr, zi], axis=0).astype(BF16)
        x = jnp.dot(finv_ref[...], zz, preferred_element_type=F32)
        xp_ref[pl.ds(n2, n1h, stride=X_PITCH), :] = x[:, 0:LANES]
        xp_ref[pl.ds(n2 + 1, n1h, stride=X_PITCH), :] = x[:, LANES:2 * LANES]
        return carry

    lax.fori_loop(0, FFT_N2 // 2, body2, 0, unroll=4)
    d = d_ref[...]
    for n1 in range(n1h):
        rs = pl.ds(n1 * FFT_N2, FFT_N2)
        y = xp_ref[pl.ds(n1 * X_PITCH, FFT_N2), :]
        o_ref[0, rs, :] = gate_ref[0, rs, :] * (y + z_ref[0, rs, :] * d)


def _long_conv(z_arr, z_off, gate_arr, gate_off, kf_all, order, d, consts):
    B, L, _ = z_arr.shape
    C = HY_WIDTH
    nct = C // LANES
    f1, g, finv, ginv = consts
    cst2 = lambda a: pl.BlockSpec(a.shape, lambda c, b: (0, 0))
    cst3 = lambda a: pl.BlockSpec(a.shape, lambda c, b: (0, 0, 0))
    return pl.pallas_call(
        _conv_kernel,
        grid=(nct, B),
        in_specs=[pl.BlockSpec((1, L, LANES), lambda c, b: (b, 0, c + z_off)),
                  pl.BlockSpec((1, L, LANES), lambda c, b: (b, 0, c + gate_off)),
                  pl.BlockSpec((1, KF_ROWS, LANES), lambda c, b: (order, 0, c)),
                  pl.BlockSpec((1, LANES), lambda c, b: (0, c)),
                  cst2(f1), cst3(g), cst2(finv), cst3(ginv)],
        out_specs=pl.BlockSpec((1, L, LANES), lambda c, b: (b, 0, c)),
        out_shape=jax.ShapeDtypeStruct((B, L, C), F32),
        scratch_shapes=[pltpu.VMEM((W_ROWS, LANES), F32), pltpu.VMEM((X_ROWS, LANES), F32)],
        compiler_params=_cparams(("parallel", "parallel")),
        name=f"hyena_long_conv{order}",
    )(z_arr, gate_arr, kf_all, d.reshape(1, C), f1, g, finv, ginv)


def _mix_kernel(attn_ref, hyo_ref, h0_ref, mg_ref, w_ref, g_ref, b_ref, h1_ref, h1t_ref, *, alpha):
    def gnorm(x, goff):
        parts = []
        for gi in range(x.shape[1] // NORM_GROUP):
            xg = x[:, gi * NORM_GROUP:(gi + 1) * NORM_GROUP]
            ms = jnp.mean(xg * xg, -1, keepdims=True)
            gg = mg_ref[:, goff + gi * NORM_GROUP: goff + (gi + 1) * NORM_GROUP]
            parts.append((xg * lax.rsqrt(ms + 1e-6) * gg).astype(BF16))
        return jnp.concatenate(parts, axis=1)

    a = gnorm(attn_ref[...], 0)
    y = gnorm(hyo_ref[...], ATTN_WIDTH)
    mix = (jnp.dot(a, w_ref[0:ATTN_WIDTH, :], preferred_element_type=F32)
           + jnp.dot(y, w_ref[ATTN_WIDTH:, :], preferred_element_type=F32))
    x = alpha * h0_ref[...] + mix
    mu = jnp.mean(x, -1, keepdims=True)
    xc = x - mu
    var = jnp.mean(xc * xc, -1, keepdims=True)
    h1 = xc * lax.rsqrt(var + 1e-5) * g_ref[...] + b_ref[...]
    h1_ref[...] = h1
    h1t_ref[...] = h1.T.astype(BF16)


def _mix_out(attn, hyo, h0, mix_g, w_out_bf, ln_g, ln_b, alpha, tm=256):
    T, D = h0.shape
    kern = functools.partial(_mix_kernel, alpha=alpha)
    return pl.pallas_call(
        kern,
        grid=(T // tm,),
        in_specs=[pl.BlockSpec((tm, ATTN_WIDTH), lambda i: (i, 0)),
                  pl.BlockSpec((tm, HY_WIDTH), lambda i: (i, 0)),
                  pl.BlockSpec((tm, D), lambda i: (i, 0)),
                  pl.BlockSpec((1, D), lambda i: (0, 0)),
                  pl.BlockSpec((D, D), lambda i: (0, 0)),
                  pl.BlockSpec((1, D), lambda i: (0, 0)),
                  pl.BlockSpec((1, D), lambda i: (0, 0))],
        out_specs=[pl.BlockSpec((tm, D), lambda i: (i, 0)), pl.BlockSpec((D, tm), lambda i: (0, i))],
        out_shape=[jax.ShapeDtypeStruct((T, D), F32), jax.ShapeDtypeStruct((D, T), BF16)],
        compiler_params=_cparams(("parallel",)),
        name="mix_out_ln1",
    )(attn, hyo, h0, mix_g.reshape(1, D), w_out_bf, ln_g.reshape(1, D), ln_b.reshape(1, D))


def _oddeven_sort_pairs(n):
    pairs = []

    def merge(lo, m, r):
        step = r * 2
        if step < m:
            merge(lo, m, step)
            merge(lo + r, m, step)
            pairs.extend((i, i + r) for i in range(lo + r, lo + m - r, step))
        else:
            pairs.append((lo, lo + r))

    def sort(lo, m):
        if m > 1:
            sort(lo, m // 2)
            sort(lo + m // 2, m // 2)
            merge(lo, m, 1)

    sort(0, n)
    return pairs


def _bitonic_merge_pairs(n):
    pairs, d = [], n // 2
    while d >= 1:
        pairs.extend((i, i + d) for i in range(n) if (i & d) == 0)
        d //= 2
    return pairs


def _apply_network(xs, pairs):
    xs = list(xs)
    for i, j in pairs:
        xs[i], xs[j] = jnp.maximum(xs[i], xs[j]), jnp.minimum(xs[i], xs[j])
    return xs


def _top16_replicated(xs):
    xs = _apply_network(xs, _oddeven_sort_pairs(PEER_TOPK))
    merge = _bitonic_merge_pairs(PEER_TOPK)
    for shift in (4, 2, 1):
        other = [pltpu.roll(x, shift, 0) for x in xs]
        xs = [jnp.maximum(xs[i], other[PEER_TOPK - 1 - i]) for i in range(PEER_TOPK)]
        xs = _apply_network(xs, merge)
    return xs


def _pair_sum_candidates(v1, v2):
    sub = lax.broadcasted_iota(jnp.int32, v1[0].shape, 0)

    def by_sublane(vs):
        out = vs[SUBLANES - 1]
        for j in range(SUBLANES - 2, -1, -1):
            out = jnp.where(sub == j, vs[j], out)
        return out

    ninf = -jnp.inf
    v2lo, v2hi = by_sublane(v2[:SUBLANES]), by_sublane(v2[SUBLANES:])
    v1lo, v1hi = by_sublane(v1[:SUBLANES]), by_sublane(v1[SUBLANES:])
    cands = [v1[0] + v2lo, v1[0] + v2hi, v1[1] + v2lo]
    for a in (2, 3, 4):
        cands.append(jnp.where(sub < PEER_TOPK // (a + 1), v1[a] + v2lo, ninf))
    cands.append(jnp.where(sub >= 5, v1lo + v2[0], ninf))
    cands.append(v1hi + v2[0])
    cands.append(jnp.where(sub >= 5, v1lo + v2[1], ninf))
    pad = jnp.full(v1[0].shape, ninf, F32)
    return cands + [pad] * (PEER_TOPK - len(cands))


def _peer_score_kernel(wq_ref, h_ref, sk1_ref, sk2_ref, cnt_ref, r2_ref, e1_ref, e2_ref, q_ref):
    q_ref[...] = jnp.dot(wq_ref[...], h_ref[...], preferred_element_type=F32)

    def head(h, carry):
        base = pl.multiple_of(h * PEER_QDIM, PEER_QDIM)
        q1 = q_ref[pl.ds(base, PEER_QHALF), :].astype(BF16)
        q2 = q_ref[pl.ds(base + PEER_QHALF, PEER_QHALF), :].astype(BF16)
        s1 = jnp.dot(sk1_ref[...], q1, preferred_element_type=F32)
        s2 = jnp.dot(sk2_ref[...], q2, preferred_element_type=F32)
        slabs = lambda s: [s[SUBLANES * i:SUBLANES * (i + 1)] for i in range(N_KEYS // SUBLANES)]
        v1 = _top16_replicated(slabs(s1))
        v2 = _top16_replicated(slabs(s2))
        sc = _top16_replicated(_pair_sum_candidates(v1, v2))
        zsum = jnp.zeros_like(sc[0])
        for j in range(PEER_TOPK):
            zsum = zsum + jnp.exp(sc[j] - sc[0])
        row = lambda x: x[0:1]
        th = row(sc[PEER_TOPK - 1])
        cnt = jnp.zeros_like(s1)
        r2 = jnp.zeros_like(s2)
        for b in range(PEER_TOPK):
            cnt = cnt + jnp.where(s1 + row(v2[b]) >= th, 1.0, 0.0)
            r2 = r2 + jnp.where(row(v2[b]) > s2, 1.0, 0.0)
        def twice(x):
            bits = pltpu.bitcast(x.astype(BF16).astype(F32), jnp.uint32)
            return bits | (bits >> 16)

        cnt_ref[h] = twice(cnt)
        e1_ref[h] = twice(0.5 * jnp.exp(s1 - row(v1[0])) / row(zsum))
        r2_ref[h] = pltpu.bitcast(r2.astype(BF16), jnp.uint32)
        e2_ref[h] = pltpu.bitcast(jnp.exp(s2 - row(v2[0])).astype(BF16), jnp.uint32)
        return carry

    lax.fori_loop(0, PEER_HEADS, head, 0)


def _peer_scores(wqT_bf, h1T_bf, sk1_bf, sk2_bf, tl=256):
    D, T = h1T_bf.shape
    Q = wqT_bf.shape[0]
    big = jax.ShapeDtypeStruct((PEER_HEADS, N_KEYS, T), jnp.uint32)
    big16 = jax.ShapeDtypeStruct((PEER_HEADS, N_KEYS // 2, T), jnp.uint32)
    bspec = pl.BlockSpec((PEER_HEADS, N_KEYS, tl), lambda i: (0, 0, i))
    pspec = pl.BlockSpec((PEER_HEADS, N_KEYS // 2, tl), lambda i: (0, 0, i))
    return pl.pallas_call(
        _peer_score_kernel,
        grid=(T // tl,),
        in_specs=[pl.BlockSpec((Q, D), lambda i: (0, 0)),
                  pl.BlockSpec((D, tl), lambda i: (0, i)),
                  pl.BlockSpec((N_KEYS, PEER_QHALF), lambda i: (0, 0)),
                  pl.BlockSpec((N_KEYS, PEER_QHALF), lambda i: (0, 0))],
        out_specs=[bspec, pspec, bspec, pspec],
        out_shape=[big, big16, big, big16],
        scratch_shapes=[pltpu.VMEM((Q, tl), F32)],
        compiler_params=_cparams(("parallel",)),
        name="peer_scores_topk",
    )(wqT_bf, h1T_bf, sk1_bf, sk2_bf)


W_ROWS_PER_GROUP = 64


def _peer_dense_kernel(hb_ref, u_ref, vt_ref, cnt_ref, r2_ref, e1_ref, e2_ref,
                       o_ref, acc_ref, a0_ref, a1_ref, w0_ref, w1_ref, *, te, tm, n_j, n_tiles):
    s = pl.program_id(0)
    sv = jnp.clip(s - 2, 0, n_tiles - 1)
    jv = sv % n_j
    jw = jnp.clip(s - 1, 0, n_tiles - 1) % n_j

    @pl.when(s == 0)
    def _():
        for ref in (a0_ref, a1_ref, w0_ref, w1_ref):
            ref[...] = jnp.zeros_like(ref)

    @pl.when(jv == 0)
    def _():
        acc_ref[...] = jnp.zeros_like(acc_ref)

    wide = 2 * LANES
    n_r, n_c = te // N_KEYS, tm // LANES
    d_model = acc_ref.shape[0]

    def body(a_cur, a_prv, w_cur, w_prv):
        def stage_a(q):
            ms = slice((q // 2) * wide, (q // 2 + 1) * wide)
            ls = slice((q % 2) * wide, (q % 2 + 1) * wide)
            a_cur[ms, ls] = jnp.dot(u_ref[ms, :], hb_ref[:, ls], preferred_element_type=F32)

        def stage_v(p):
            fs = slice((p // 2) * wide, (p // 2 + 1) * wide)
            ls = slice((p % 2) * wide, (p % 2 + 1) * wide)
            acc_ref[fs, ls] += jnp.dot(vt_ref[0, fs, :], w_cur[:, ls], preferred_element_type=F32)

        n_g = N_KEYS // W_ROWS_PER_GROUP
        grp = (W_ROWS_PER_GROUP, LANES)

        def stage_w(idx):
            p, g = idx // n_g, idx % n_g
            r, c = p // n_c, p % n_c
            i1 = jw * n_r + r
            cs = slice(c * LANES, (c + 1) * LANES)
            cntrows = cnt_ref[i1, :, cs]
            e1rows = e1_ref[i1, :, cs]
            ks = slice(g * W_ROWS_PER_GROUP // 2, (g + 1) * W_ROWS_PER_GROUP // 2)
            gate = jnp.zeros(grp, BF16)
            half = (W_ROWS_PER_GROUP // 2, LANES)
            for h in range(PEER_HEADS):
                cnt_b = pltpu.bitcast(jnp.broadcast_to(cntrows[h:h + 1], half), BF16)
                e1_b = pltpu.bitcast(jnp.broadcast_to(e1rows[h:h + 1], half), BF16)
                sel = pltpu.bitcast(r2_ref[h, ks, cs], BF16) < cnt_b
                val = pltpu.bitcast(e2_ref[h, ks, cs], BF16) * e1_b
                gate = gate + jnp.where(sel, val, jnp.zeros(grp, BF16))
            ws = slice(r * N_KEYS + g * W_ROWS_PER_GROUP, r * N_KEYS + (g + 1) * W_ROWS_PER_GROUP)
            a = a_prv[ws, cs]
            act = a * (1.0 + lax.erf(a * (1.0 / math.sqrt(2.0))))
            w_prv[ws, cs] = gate * act.astype(BF16)

        for q in range(4):
            stage_a(q)
            for t in range(4):
                stage_v(4 * q + t)
                for gg in range(2):
                    stage_w(8 * q + 2 * t + gg)

    @pl.when(s % 2 == 0)
    def _():
        body(a0_ref, a1_ref, w0_ref, w1_ref)

    @pl.when(s % 2 == 1)
    def _():
        body(a1_ref, a0_ref, w1_ref, w0_ref)

    @pl.when(jv == n_j - 1)
    def _():
        o_ref[...] = acc_ref[...]


def _peer_dense(h1T_bf, u_bf, v_bf, cnt, r2, e1, e2, tm=512, te=512):
    D, T = h1T_bf.shape
    E = u_bf.shape[0]
    assert te == 4 * N_KEYS and tm == 4 * LANES and D == 2048, \
        "the stage interleave is written for 512 x 512 tiles"
    n_i, n_j = T // tm, E // te
    vT_bf = jnp.transpose(v_bf.reshape(n_j, te, D), (0, 2, 1))
    n_tiles = n_i * n_j
    kern = functools.partial(_peer_dense_kernel, te=te, tm=tm, n_j=n_j, n_tiles=n_tiles)
    ta = lambda s: jnp.minimum(s, n_tiles - 1)
    tw = lambda s: jnp.clip(s - 1, 0, n_tiles - 1)
    tv = lambda s: jnp.clip(s - 2, 0, n_tiles - 1)
    sspec = pl.BlockSpec((PEER_HEADS, N_KEYS // 2, tm), lambda s: (0, 0, tw(s) // n_j))
    xspec = pl.BlockSpec((N_KEYS, PEER_HEADS, tm), lambda s: (0, 0, tw(s) // n_j))
    cnt = jnp.transpose(cnt, (1, 0, 2))
    e1 = jnp.transpose(e1, (1, 0, 2))
    return pl.pallas_call(
        kern,
        grid=(n_tiles + 2,),
        in_specs=[pl.BlockSpec((D, tm), lambda s: (0, ta(s) // n_j)),
                  pl.BlockSpec((te, D), lambda s: (ta(s) % n_j, 0)),
                  pl.BlockSpec((1, D, te), lambda s: (tv(s) % n_j, 0, 0)),
                  xspec, sspec, xspec, sspec],
        out_specs=pl.BlockSpec((D, tm), lambda s: (0, tv(s) // n_j)),
        out_shape=jax.ShapeDtypeStruct((D, T), F32),
        scratch_shapes=[pltpu.VMEM((D, tm), F32),
                        pltpu.VMEM((te, tm), F32), pltpu.VMEM((te, tm), F32),
                        pltpu.VMEM((te, tm), BF16), pltpu.VMEM((te, tm), BF16)],
        compiler_params=_cparams(("arbitrary",)),
        name="peer_dense",
    )(h1T_bf, u_bf, vT_bf, cnt, r2, e1, e2)


def _res_ln_kernel(h_ref, ft_ref, g_ref, b_ref, o_ref, *, alpha):
    x = alpha * h_ref[...] + ft_ref[...].T
    mu = jnp.mean(x, -1, keepdims=True)
    xc = x - mu
    var = jnp.mean(xc * xc, -1, keepdims=True)
    o_ref[...] = xc * lax.rsqrt(var + 1e-5) * g_ref[...] + b_ref[...]


def _res_ln(h, fT, g, b, alpha, tm=512):
    T, D = h.shape
    row = pl.BlockSpec((tm, D), lambda i: (i, 0))
    vec = pl.BlockSpec((1, D), lambda i: (0, 0))
    return pl.pallas_call(
        functools.partial(_res_ln_kernel, alpha=alpha),
        grid=(T // tm,),
        in_specs=[row, pl.BlockSpec((D, tm), lambda i: (0, i)), vec, vec],
        out_specs=row,
        out_shape=jax.ShapeDtypeStruct((T, D), F32),
        compiler_params=_cparams(("parallel",)),
        name="res_ln2",
    )(h, fT, g.reshape(1, D), b.reshape(1, D))


def kernel(x, ln0_g, ln0_b, rel_bias, w_in, sink, conv_w, conv_b, f_w1, f_b1, f_freq1, f_w2, f_b2,
           f_freq2, f_w3, hy_bias, mix_norm_g, w_out, ln1_g, ln1_b, peer_wq, peer_subkeys, peer_u,
           peer_v, ln2_g, ln2_b):
    B, S, D = x.shape
    T = B * S
    alpha = (2.0 * DEPTH) ** 0.25
    consts = _dft_constants()

    qi = jnp.arange(BLOCK, dtype=jnp.int32)
    kj = jnp.arange(3 * BLOCK, dtype=jnp.int32)
    rel = kj[None, :] - BLOCK - qi[:, None]
    onehot = (_t5_bucket(rel)[..., None] == jnp.arange(N_BUCKETS, dtype=jnp.int32)).astype(F32)
    bias = jnp.einsum("qkb,bh->hqk", onehot, rel_bias.astype(F32), precision=lax.Precision.HIGHEST)

    h, h_bf = _ln0(x.reshape(T, D), ln0_g, ln0_b)
    for l in range(DEPTH):
        proj = _matmul(h_bf, w_in[l].astype(BF16), tm=512, tn=1536)
        attn = _attention(proj, bias, sink[l], B, S)

        hfilt = _filters(S, f_w1[l], f_b1[l], f_freq1[l], f_w2[l], f_b2[l], f_freq2[l], f_w3[l])
        kf = _filter_spectra(hfilt, consts)
        u = _short_conv(proj.reshape(B, S, -1), conv_w[l], conv_b[l], ATTN_WIDTH + 2 * KV_WIDTH)
        nct = HY_WIDTH // LANES
        z1 = _long_conv(u, 0, u, nct, kf, 0, hy_bias[l, 0], consts)
        hyo = _long_conv(z1, 0, u, 2 * nct, kf, 1, hy_bias[l, 1], consts)

        h1, h1T_bf = _mix_out(attn, hyo.reshape(T, HY_WIDTH), h, mix_norm_g[l], w_out[l].astype(BF16),
                              ln1_g[l], ln1_b[l], alpha)
        cnt, r2, e1, e2 = _peer_scores(peer_wq[l].T.astype(BF16), h1T_bf,
                                       peer_subkeys[l, 0].astype(BF16), peer_subkeys[l, 1].astype(BF16))
        ffnT = _peer_dense(h1T_bf, peer_u[l].astype(BF16), peer_v[l].astype(BF16), cnt, r2, e1, e2)
        h = _res_ln(h1, ffnT, ln2_g[l], ln2_b[l], alpha)
        if l + 1 < DEPTH:
            h_bf = h.astype(BF16)
    return h.reshape(B, S, D)
```

```python
import functools
import math

import numpy as np
import jax
import jax.numpy as jnp
from jax import lax
from jax.experimental import pallas as pl
from jax.experimental.pallas import tpu as pltpu

F32 = jnp.float32
BF16 = jnp.bfloat16

D_MODEL = 2048
HEAD_DIM = 128
N_Q_HEADS = 8
N_KV_HEADS = 2
GQA_GROUP = N_Q_HEADS // N_KV_HEADS
ATTN_WIDTH = N_Q_HEADS * HEAD_DIM
KV_WIDTH = N_KV_HEADS * HEAD_DIM
WINDOW = 128
BLOCK = 128
N_BUCKETS = 32
MAX_DISTANCE = 128
HY_WIDTH = D_MODEL - ATTN_WIDTH
HY_ORDER = 2
POS_BANDS = 16
POS_EMB = 1 + 2 * POS_BANDS
FILTER_HIDDEN = 64
FAST_DECAY_PCT = 0.3
SLOW_DECAY_PCT = 1.5
DECAY_TARGET = 1e-2
NORM_GROUP = 128
N_KEYS = 128
PEER_HEADS = 8
PEER_QDIM = 256
PEER_QHALF = PEER_QDIM // 2
PEER_TOPK = 16
NEG = -1e30
DEPTH = 1

LANES = 128
VMEM_LIMIT = 56 * 1024 * 1024

FFT_N1 = 64
FFT_N2 = 128
FFT_K1 = FFT_N1 // 2 + 1
FFT_K1_PAD = 40
FFT_SLOT = 2 * FFT_N2
KF_ROWS = FFT_K1 * FFT_SLOT
SUBLANES = 8
W_PITCH = FFT_SLOT + SUBLANES
X_PITCH = FFT_N2 + SUBLANES
W_ROWS = FFT_K1_PAD * W_PITCH
X_ROWS = (FFT_N1 // 2) * X_PITCH


def _cparams(sem, vmem=VMEM_LIMIT):
    return pltpu.CompilerParams(dimension_semantics=sem, vmem_limit_bytes=vmem)


def _ln0_kernel(x_ref, g_ref, b_ref, h_ref, hb_ref):
    x = x_ref[...]
    mu = jnp.mean(x, -1, keepdims=True)
    xc = x - mu
    var = jnp.mean(xc * xc, -1, keepdims=True)
    y = xc * lax.rsqrt(var + 1e-5) * g_ref[...] + b_ref[...]
    h_ref[...] = y
    hb_ref[...] = y.astype(BF16)


def _ln0(x2d, g, b, tm=512):
    T, D = x2d.shape
    return pl.pallas_call(
        _ln0_kernel,
        grid=(T // tm,),
        in_specs=[pl.BlockSpec((tm, D), lambda i: (i, 0)),
                  pl.BlockSpec((1, D), lambda i: (0, 0)),
                  pl.BlockSpec((1, D), lambda i: (0, 0))],
        out_specs=[pl.BlockSpec((tm, D), lambda i: (i, 0)),
                   pl.BlockSpec((tm, D), lambda i: (i, 0))],
        out_shape=[jax.ShapeDtypeStruct((T, D), F32), jax.ShapeDtypeStruct((T, D), BF16)],
        compiler_params=_cparams(("parallel",)),
        name="ln0",
    )(x2d, g.reshape(1, D), b.reshape(1, D))


def _mm_kernel(a_ref, b_ref, o_ref):
    o_ref[...] = jnp.dot(a_ref[...], b_ref[...], preferred_element_type=F32)


def _matmul(a, b, tm, tn):
    M, K = a.shape
    N = b.shape[1]
    return pl.pallas_call(
        _mm_kernel,
        grid=(N // tn, M // tm),
        in_specs=[pl.BlockSpec((tm, K), lambda j, i: (i, 0)),
                  pl.BlockSpec((K, tn), lambda j, i: (0, j))],
        out_specs=pl.BlockSpec((tm, tn), lambda j, i: (i, j)),
        out_shape=jax.ShapeDtypeStruct((M, N), F32),
        compiler_params=_cparams(("parallel", "parallel")),
        name="in_proj",
    )(a, b)


def _t5_bucket(rel):
    nb = N_BUCKETS // 2
    ret = (rel > 0).astype(jnp.int32) * nb
    n = jnp.abs(rel)
    max_exact = nb // 2
    nf = jnp.maximum(n, 1).astype(F32)
    large = max_exact + (jnp.log(nf / max_exact) / math.log(MAX_DISTANCE / max_exact)
                         * (nb - max_exact)).astype(jnp.int32)
    large = jnp.minimum(large, nb - 1)
    return ret + jnp.where(n < max_exact, n, large)


def _attn_kernel(sink_ref, q_ref, kp_ref, kc_ref, kn_ref, vp_ref, vc_ref, vn_ref, bias_ref, o_ref,
                 *, nb, seq):
    n = pl.program_id(0) % nb
    rows = GQA_GROUP * BLOCK
    row = lax.broadcasted_iota(jnp.int32, (rows, 3 * BLOCK), 0)
    kj = lax.broadcasted_iota(jnp.int32, (rows, 3 * BLOCK), 1)
    rel = kj - BLOCK - (row & (BLOCK - 1))
    kabs = n * BLOCK + kj - BLOCK
    valid = (jnp.abs(rel) <= WINDOW) & (kabs >= 0) & (kabs < seq)
    head_of_row = lax.broadcasted_iota(jnp.int32, (rows, 1), 0) // BLOCK
    k = jnp.concatenate([kp_ref[...], kc_ref[...], kn_ref[...]], axis=0).astype(BF16)
    v = jnp.concatenate([vp_ref[...], vc_ref[...], vn_ref[...]], axis=0).astype(BF16)
    scale = 1.0 / math.sqrt(HEAD_DIM)
    for g in range(N_KV_HEADS):
        kg = k[:, g * HEAD_DIM:(g + 1) * HEAD_DIM]
        vg = v[:, g * HEAD_DIM:(g + 1) * HEAD_DIM]
        h0 = g * GQA_GROUP
        qg = jnp.concatenate([q_ref[:, (h0 + r) * HEAD_DIM:(h0 + r + 1) * HEAD_DIM]
                              for r in range(GQA_GROUP)], axis=0).astype(BF16)
        s = lax.dot_general(qg, kg, (((1,), (1,)), ((), ())), preferred_element_type=F32) * scale
        bias_g = bias_ref[h0:h0 + GQA_GROUP].reshape(rows, 3 * BLOCK)
        s = jnp.where(valid, s + bias_g, NEG)
        sk = jnp.zeros((rows, 1), F32)
        for r in range(GQA_GROUP):
            sk = jnp.where(head_of_row == r, sink_ref[h0 + r], sk)
        m = jnp.maximum(jnp.max(s, -1, keepdims=True), sk)
        p = jnp.exp(s - m)
        denom = jnp.sum(p, -1, keepdims=True) + jnp.exp(sk - m)
        p = p / denom
        o = jnp.dot(p.astype(BF16), vg, preferred_element_type=F32)
        for r in range(GQA_GROUP):
            o_ref[:, (h0 + r) * HEAD_DIM:(h0 + r + 1) * HEAD_DIM] = o[r * BLOCK:(r + 1) * BLOCK]


def _attention(proj, bias, sink, batch, seq):
    T = proj.shape[0]
    nb = seq // BLOCK
    kcol = ATTN_WIDTH // KV_WIDTH
    vcol = kcol + 1

    def prev(i):
        return i - jnp.where(i % nb == 0, 0, 1)

    def nxt(i):
        return i + jnp.where(i % nb == nb - 1, 0, 1)

    kern = functools.partial(_attn_kernel, nb=nb, seq=seq)
    return pl.pallas_call(
        kern,
        grid=(T // BLOCK,),
        in_specs=[pl.BlockSpec(memory_space=pltpu.SMEM),
                  pl.BlockSpec((BLOCK, ATTN_WIDTH), lambda i: (i, 0)),
                  pl.BlockSpec((BLOCK, KV_WIDTH), lambda i: (prev(i), kcol)),
                  pl.BlockSpec((BLOCK, KV_WIDTH), lambda i: (i, kcol)),
                  pl.BlockSpec((BLOCK, KV_WIDTH), lambda i: (nxt(i), kcol)),
                  pl.BlockSpec((BLOCK, KV_WIDTH), lambda i: (prev(i), vcol)),
                  pl.BlockSpec((BLOCK, KV_WIDTH), lambda i: (i, vcol)),
                  pl.BlockSpec((BLOCK, KV_WIDTH), lambda i: (nxt(i), vcol)),
                  pl.BlockSpec((N_Q_HEADS, BLOCK, 3 * BLOCK), lambda i: (0, 0, 0))],
        out_specs=pl.BlockSpec((BLOCK, ATTN_WIDTH), lambda i: (i, 0)),
        out_shape=jax.ShapeDtypeStruct((T, ATTN_WIDTH), F32),
        compiler_params=_cparams(("parallel",)),
        name="window_attn",
    )(sink, proj, proj, proj, proj, proj, proj, proj, bias)


def _filter_kernel(z_ref, w1_ref, b1_ref, fr1_ref, w2_ref, b2_ref, fr2_ref, w3_ref, dl_ref, o_ref,
                   *, tt, seq):
    hi = lax.Precision.HIGHEST
    a = jnp.dot(z_ref[...], w1_ref[...], preferred_element_type=F32, precision=hi) + b1_ref[...]
    hid = jnp.sin(fr1_ref[...] * a)
    a = jnp.dot(hid, w2_ref[...], preferred_element_type=F32, precision=hi) + b2_ref[...]
    hid = jnp.sin(fr2_ref[...] * a)
    h = jnp.dot(hid, w3_ref[...], preferred_element_type=F32, precision=hi)
    rows = lax.broadcasted_iota(jnp.int32, h.shape, 0) + pl.program_id(0) * tt
    tn = rows.astype(F32) / float(max(seq - 1, 1))
    o_ref[...] = h * jnp.exp(-tn * dl_ref[...])


def _filters(seq, w1, b1, fr1, w2, b2, fr2, w3, tt=512):
    t = jnp.arange(seq, dtype=F32)
    tn = t / max(seq - 1, 1)
    w = 2.0 * math.pi * t / seq
    bands = jnp.linspace(1e-4, POS_BANDS - 1, POS_BANDS, dtype=F32)
    z = jnp.concatenate([tn[:, None], jnp.cos(w[:, None] * bands), -jnp.sin(w[:, None] * bands)], -1)
    zp = jnp.pad(z, ((0, 0), (0, FILTER_HIDDEN - POS_EMB)))
    w1p = jnp.pad(w1, ((0, FILTER_HIDDEN - POS_EMB), (0, 0)))
    max_decay = math.log(DECAY_TARGET) / FAST_DECAY_PCT
    min_decay = math.log(DECAY_TARGET) / SLOW_DECAY_PCT
    deltas = jnp.abs(jnp.linspace(min_decay, max_decay, HY_WIDTH, dtype=F32))
    ncol = HY_ORDER * 2 * HY_WIDTH
    dl = jnp.tile(deltas, HY_ORDER * 2).reshape(1, ncol)
    H = FILTER_HIDDEN
    kern = functools.partial(_filter_kernel, tt=tt, seq=seq)
    full = lambda r, c: pl.BlockSpec((r, c), lambda i: (0, 0))
    return pl.pallas_call(
        kern,
        grid=(seq // tt,),
        in_specs=[pl.BlockSpec((tt, H), lambda i: (i, 0)),
                  full(H, H), full(1, H), full(1, H), full(H, H), full(1, H), full(1, H),
                  full(H, ncol), full(1, ncol)],
        out_specs=pl.BlockSpec((tt, ncol), lambda i: (i, 0)),
        out_shape=jax.ShapeDtypeStruct((seq, ncol), F32),
        compiler_params=_cparams(("parallel",)),
        name="hyena_filter_mlp",
    )(zp, w1p, b1.reshape(1, H), fr1.reshape(1, H), w2, b2.reshape(1, H), fr2.reshape(1, H), w3, dl)


def _sconv_kernel(x_ref, w_ref, b_ref, o_ref):
    x = x_ref[0]
    L = x.shape[0]
    rows = lax.broadcasted_iota(jnp.int32, x.shape, 0)
    xm = jnp.where(rows == 0, 0.0, pltpu.roll(x, 1, 0))
    xp = jnp.where(rows == L - 1, 0.0, pltpu.roll(x, L - 1, 0))
    w = w_ref[...]
    o_ref[0] = xm * w[0:1] + x * w[1:2] + xp * w[2:3] + b_ref[...]


def _short_conv(proj3, conv_w, conv_b, col0, ct=256):
    B, L, _ = proj3.shape
    C = conv_w.shape[1]
    off = col0 // ct
    return pl.pallas_call(
        _sconv_kernel,
        grid=(B, C // ct),
        in_specs=[pl.BlockSpec((1, L, ct), lambda b, c: (b, 0, c + off)),
                  pl.BlockSpec((3, ct), lambda b, c: (0, c)),
                  pl.BlockSpec((1, ct), lambda b, c: (0, c))],
        out_specs=pl.BlockSpec((1, L, ct), lambda b, c: (b, 0, c)),
        out_shape=jax.ShapeDtypeStruct((B, L, C), F32),
        compiler_params=_cparams(("parallel", "parallel")),
        name="hyena_short_conv",
    )(proj3, conv_w, conv_b.reshape(1, C))


def _dft_constants():
    n1h = FFT_N1 // 2
    k1 = np.arange(FFT_K1_PAD)[:, None].astype(np.float64)
    n1 = np.arange(n1h)[None, :].astype(np.float64)
    ang = 2.0 * np.pi * k1 * n1 / FFT_N1
    live = (np.arange(FFT_K1_PAD) < FFT_K1)[:, None]
    f1 = np.concatenate([np.where(live, np.cos(ang), 0.0), np.where(live, -np.sin(ang), 0.0)], 0)
    kk1 = np.arange(FFT_K1)[:, None, None].astype(np.float64)
    k2 = np.arange(FFT_N2)[None, :, None].astype(np.float64)
    n2 = np.arange(FFT_N2)[None, None, :].astype(np.float64)
    phi = 2.0 * np.pi * (n2 * k2 / FFT_N2 + n2 * kk1 / (FFT_N1 * FFT_N2))
    c, s = np.cos(phi), np.sin(phi)
    g = np.concatenate([np.concatenate([c, s], 2), np.concatenate([-s, c], 2)], 1)
    ct_, st_ = np.transpose(c, (0, 2, 1)), np.transpose(s, (0, 2, 1))
    ginv = np.concatenate([np.concatenate([ct_, -st_], 2), np.concatenate([st_, ct_], 2)], 1)
    wk = np.where((np.arange(FFT_K1_PAD) == 0) | (np.arange(FFT_K1_PAD) == FFT_N1 // 2), 1.0, 2.0)
    wk = np.where(np.arange(FFT_K1_PAD) < FFT_K1, wk, 0.0)[None, :] / (FFT_N1 * FFT_N2)
    angi = 2.0 * np.pi * np.arange(n1h)[:, None] * np.arange(FFT_K1_PAD)[None, :] / FFT_N1
    finv = np.concatenate([wk * np.cos(angi), -wk * np.sin(angi)], 1)
    as_bf = lambda a: jnp.asarray(a.astype(np.float32)).astype(BF16)
    return as_bf(f1), as_bf(g), as_bf(finv), as_bf(ginv)


def _pad_rows_in(src_ref, xp_ref, zero_first_row=False):
    for n1 in range(FFT_N1 // 2):
        blk = src_ref[pl.ds(n1 * FFT_N2, FFT_N2), :]
        if zero_first_row and n1 == 0:
            rows = lax.broadcasted_iota(jnp.int32, blk.shape, 0)
            blk = jnp.where(rows == 0, 0.0, blk)
        xp_ref[pl.ds(n1 * X_PITCH, FFT_N2), :] = blk


def _fft_stage1(xp_ref, w_ref, f1_ref):
    n1h = FFT_N1 // 2
    kp = FFT_K1_PAD

    def body(i, carry):
        n2 = 2 * i
        xa = xp_ref[pl.ds(n2, n1h, stride=X_PITCH), :]
        xb = xp_ref[pl.ds(n2 + 1, n1h, stride=X_PITCH), :]
        xs = jnp.concatenate([xa, xb], axis=1).astype(BF16)
        r = jnp.dot(f1_ref[...], xs, preferred_element_type=F32)
        w_ref[pl.ds(n2, kp, stride=W_PITCH), :] = r[0:kp, 0:LANES]
        w_ref[pl.ds(n2 + 1, kp, stride=W_PITCH), :] = r[0:kp, LANES:2 * LANES]
        w_ref[pl.ds(FFT_N2 + n2, kp, stride=W_PITCH), :] = r[kp:2 * kp, 0:LANES]
        w_ref[pl.ds(FFT_N2 + n2 + 1, kp, stride=W_PITCH), :] = r[kp:2 * kp, LANES:2 * LANES]
        return carry

    lax.fori_loop(0, FFT_N2 // 2, body, 0, unroll=4)


def _kf_kernel(hf_ref, hb_ref, f1_ref, g_ref, kf_ref, wf_ref, wb_ref, xp_ref):
    l1 = (jnp.sum(jnp.abs(hf_ref[...]), 0, keepdims=True)
          + jnp.sum(jnp.abs(hb_ref[...]), 0, keepdims=True))
    scale = 1.0 / (l1 + 1e-6)
    _pad_rows_in(hf_ref, xp_ref)
    _fft_stage1(xp_ref, wf_ref, f1_ref)
    _pad_rows_in(hb_ref, xp_ref, zero_first_row=True)
    _fft_stage1(xp_ref, wb_ref, f1_ref)
    out = kf_ref.at[0]

    def body(k1, carry):
        src = pl.multiple_of(k1 * W_PITCH, 8)
        dst = pl.multiple_of(k1 * FFT_SLOT, FFT_SLOT)
        gk = g_ref[k1]
        sf = jnp.dot(gk, wf_ref[pl.ds(src, FFT_SLOT), :].astype(BF16), preferred_element_type=F32)
        sb = jnp.dot(gk, wb_ref[pl.ds(src, FFT_SLOT), :].astype(BF16), preferred_element_type=F32)
        out[pl.ds(dst, FFT_N2), :] = (sf[0:FFT_N2] + sb[0:FFT_N2]) * scale
        out[pl.ds(dst + FFT_N2, FFT_N2), :] = (sf[FFT_N2:FFT_SLOT] - sb[FFT_N2:FFT_SLOT]) * scale
        return carry

    lax.fori_loop(0, FFT_K1, body, 0, unroll=3)


def _filter_spectra(hfilt, consts):
    L = hfilt.shape[0]
    C = HY_WIDTH
    nct = C // LANES
    f1, g, _, _ = consts
    return pl.pallas_call(
        _kf_kernel,
        grid=(HY_ORDER, nct),
        in_specs=[pl.BlockSpec((L, LANES), lambda o, c: (0, o * 2 * nct + c)),
                  pl.BlockSpec((L, LANES), lambda o, c: (0, o * 2 * nct + nct + c)),
                  pl.BlockSpec(f1.shape, lambda o, c: (0, 0)),
                  pl.BlockSpec(g.shape, lambda o, c: (0, 0, 0))],
        out_specs=pl.BlockSpec((1, KF_ROWS, LANES), lambda o, c: (o, 0, c)),
        out_shape=jax.ShapeDtypeStruct((HY_ORDER, KF_ROWS, C), F32),
        scratch_shapes=[pltpu.VMEM((W_ROWS, LANES), F32), pltpu.VMEM((W_ROWS, LANES), F32),
                        pltpu.VMEM((X_ROWS, LANES), F32)],
        compiler_params=_cparams(("parallel", "parallel")),
        name="hyena_filter_fft",
    )(hfilt, hfilt, f1, g)


def _conv_kernel(z_ref, gate_ref, kf_ref, d_ref, f1_ref, g_ref, finv_ref, ginv_ref, o_ref, w_ref, xp_ref):
    zsrc = z_ref.at[0]
    kf = kf_ref.at[0]
    _pad_rows_in(zsrc, xp_ref)
    _fft_stage1(xp_ref, w_ref, f1_ref)

    def body(k1, carry):
        base = pl.multiple_of(k1 * W_PITCH, 8)
        kbase = pl.multiple_of(k1 * FFT_SLOT, FFT_SLOT)
        a = w_ref[pl.ds(base, FFT_SLOT), :].astype(BF16)
        s = jnp.dot(g_ref[k1], a, preferred_element_type=F32)
        sr, si = s[0:FFT_N2], s[FFT_N2:FFT_SLOT]
        kr = kf[pl.ds(kbase, FFT_N2), :]
        ki = kf[pl.ds(kbase + FFT_N2, FFT_N2), :]
        y = jnp.concatenate([sr * kr - si * ki, sr * ki + si * kr], axis=0).astype(BF16)
        w_ref[pl.ds(base, FFT_SLOT), :] = jnp.dot(ginv_ref[k1], y, preferred_element_type=F32)
        return carry

    lax.fori_loop(0, FFT_K1, body, 0, unroll=3)

    n1h = FFT_N1 // 2
    kp = FFT_K1_PAD

    def body2(i, carry):
        n2 = 2 * i
        zr = jnp.concatenate([w_ref[pl.ds(n2, kp, stride=W_PITCH), :],
                              w_ref[pl.ds(n2 + 1, kp, stride=W_PITCH), :]], axis=1)
        zi = jnp.concatenate([w_ref[pl.ds(FFT_N2 + n2, kp, stride=W_PITCH), :],
                              w_ref[pl.ds(FFT_N2 + n2 + 1, kp, stride=W_PITCH), :]], axis=1)
        zz = jnp.concatenate([zr, zi], axis=0).astype(BF16)
        x = jnp.dot(finv_ref[...], zz, preferred_element_type=F32)
        xp_ref[pl.ds(n2, n1h, stride=X_PITCH), :] = x[:, 0:LANES]
        xp_ref[pl.ds(n2 + 1, n1h, stride=X_PITCH), :] = x[:, LANES:2 * LANES]
        return carry

    lax.fori_loop(0, FFT_N2 // 2, body2, 0, unroll=4)
    d = d_ref[...]
    for n1 in range(n1h):
        rs = pl.ds(n1 * FFT_N2, FFT_N2)
        y = xp_ref[pl.ds(n1 * X_PITCH, FFT_N2), :]
        o_ref[0, rs, :] = gate_ref[0, rs, :] * (y + z_ref[0, rs, :] * d)


def _long_conv(z_arr, z_off, gate_arr, gate_off, kf_all, order, d, consts):
    B, L, _ = z_arr.shape
    C = HY_WIDTH
    nct = C // LANES
    f1, g, finv, ginv = consts
    cst2 = lambda a: pl.BlockSpec(a.shape, lambda c, b: (0, 0))
    cst3 = lambda a: pl.BlockSpec(a.shape, lambda c, b: (0, 0, 0))
    return pl.pallas_call(
        _conv_kernel,
        grid=(nct, B),
        in_specs=[pl.BlockSpec((1, L, LANES), lambda c, b: (b, 0, c + z_off)),
                  pl.BlockSpec((1, L, LANES), lambda c, b: (b, 0, c + gate_off)),
                  pl.BlockSpec((1, KF_ROWS, LANES), lambda c, b: (order, 0, c)),
                  pl.BlockSpec((1, LANES), lambda c, b: (0, c)),
                  cst2(f1), cst3(g), cst2(finv), cst3(ginv)],
        out_specs=pl.BlockSpec((1, L, LANES), lambda c, b: (b, 0, c)),
        out_shape=jax.ShapeDtypeStruct((B, L, C), F32),
        scratch_shapes=[pltpu.VMEM((W_ROWS, LANES), F32), pltpu.VMEM((X_ROWS, LANES), F32)],
        compiler_params=_cparams(("parallel", "parallel")),
        name=f"hyena_long_conv{order}",
    )(z_arr, gate_arr, kf_all, d.reshape(1, C), f1, g, finv, ginv)


def _mix_kernel(attn_ref, hyo_ref, h0_ref, mg_ref, w_ref, g_ref, b_ref, h1_ref, h1t_ref, *, alpha):
    def gnorm(x, goff):
        parts = []
        for gi in range(x.shape[1] // NORM_GROUP):
            xg = x[:, gi * NORM_GROUP:(gi + 1) * NORM_GROUP]
            ms = jnp.mean(xg * xg, -1, keepdims=True)
            gg = mg_ref[:, goff + gi * NORM_GROUP: goff + (gi + 1) * NORM_GROUP]
            parts.append((xg * lax.rsqrt(ms + 1e-6) * gg).astype(BF16))
        return jnp.concatenate(parts, axis=1)

    a = gnorm(attn_ref[...], 0)
    y = gnorm(hyo_ref[...], ATTN_WIDTH)
    mix = (jnp.dot(a, w_ref[0:ATTN_WIDTH, :], preferred_element_type=F32)
           + jnp.dot(y, w_ref[ATTN_WIDTH:, :], preferred_element_type=F32))
    x = alpha * h0_ref[...] + mix
    mu = jnp.mean(x, -1, keepdims=True)
    xc = x - mu
    var = jnp.mean(xc * xc, -1, keepdims=True)
    h1 = xc * lax.rsqrt(var + 1e-5) * g_ref[...] + b_ref[...]
    h1_ref[...] = h1
    h1t_ref[...] = h1.T.astype(BF16)


def _mix_out(attn, hyo, h0, mix_g, w_out_bf, ln_g, ln_b, alpha, tm=256):
    T, D = h0.shape
    kern = functools.partial(_mix_kernel, alpha=alpha)
    return pl.pallas_call(
        kern,
        grid=(T // tm,),
        in_specs=[pl.BlockSpec((tm, ATTN_WIDTH), lambda i: (i, 0)),
                  pl.BlockSpec((tm, HY_WIDTH), lambda i: (i, 0)),
                  pl.BlockSpec((tm, D), lambda i: (i, 0)),
                  pl.BlockSpec((1, D), lambda i: (0, 0)),
                  pl.BlockSpec((D, D), lambda i: (0, 0)),
                  pl.BlockSpec((1, D), lambda i: (0, 0)),
                  pl.BlockSpec((1, D), lambda i: (0, 0))],
        out_specs=[pl.BlockSpec((tm, D), lambda i: (i, 0)), pl.BlockSpec((D, tm), lambda i: (0, i))],
        out_shape=[jax.ShapeDtypeStruct((T, D), F32), jax.ShapeDtypeStruct((D, T), BF16)],
        compiler_params=_cparams(("parallel",)),
        name="mix_out_ln1",
    )(attn, hyo, h0, mix_g.reshape(1, D), w_out_bf, ln_g.reshape(1, D), ln_b.reshape(1, D))


def _oddeven_sort_pairs(n):
    pairs = []

    def merge(lo, m, r):
        step = r * 2
        if step < m:
            merge(lo, m, step)
            merge(lo + r, m, step)
            pairs.extend((i, i + r) for i in range(lo + r, lo + m - r, step))
        else:
            pairs.append((lo, lo + r))

    def sort(lo, m):
        if m > 1:
            sort(lo, m // 2)
            sort(lo + m // 2, m // 2)
            merge(lo, m, 1)

    sort(0, n)
    return pairs


def _bitonic_merge_pairs(n):
    pairs, d = [], n // 2
    while d >= 1:
        pairs.extend((i, i + d) for i in range(n) if (i & d) == 0)
        d //= 2
    return pairs


def _apply_network(xs, pairs):
    xs = list(xs)
    for i, j in pairs:
        xs[i], xs[j] = jnp.maximum(xs[i], xs[j]), jnp.minimum(xs[i], xs[j])
    return xs


def _top16_replicated(xs):
    xs = _apply_network(xs, _oddeven_sort_pairs(PEER_TOPK))
    merge = _bitonic_merge_pairs(PEER_TOPK)
    for shift in (4, 2, 1):
        other = [pltpu.roll(x, shift, 0) for x in xs]
        xs = [jnp.maximum(xs[i], other[PEER_TOPK - 1 - i]) for i in range(PEER_TOPK)]
        xs = _apply_network(xs, merge)
    return xs


def _pair_sum_candidates(v1, v2):
    sub = lax.broadcasted_iota(jnp.int32, v1[0].shape, 0)

    def by_sublane(vs):
        out = vs[SUBLANES - 1]
        for j in range(SUBLANES - 2, -1, -1):
            out = jnp.where(sub == j, vs[j], out)
        return out

    ninf = -jnp.inf
    v2lo, v2hi = by_sublane(v2[:SUBLANES]), by_sublane(v2[SUBLANES:])
    v1lo, v1hi = by_sublane(v1[:SUBLANES]), by_sublane(v1[SUBLANES:])
    cands = [v1[0] + v2lo, v1[0] + v2hi, v1[1] + v2lo]
    for a in (2, 3, 4):
        cands.append(jnp.where(sub < PEER_TOPK // (a + 1), v1[a] + v2lo, ninf))
    cands.append(jnp.where(sub >= 5, v1lo + v2[0], ninf))
    cands.append(v1hi + v2[0])
    cands.append(jnp.where(sub >= 5, v1lo + v2[1], ninf))
    pad = jnp.full(v1[0].shape, ninf, F32)
    return cands + [pad] * (PEER_TOPK - len(cands))


def _peer_score_kernel(wq_ref, h_ref, sk1_ref, sk2_ref, cnt_ref, r2_ref, e1_ref, e2_ref, q_ref):
    q_ref[...] = jnp.dot(wq_ref[...], h_ref[...], preferred_element_type=F32)

    def head(h, carry):
        base = pl.multiple_of(h * PEER_QDIM, PEER_QDIM)
        q1 = q_ref[pl.ds(base, PEER_QHALF), :].astype(BF16)
        q2 = q_ref[pl.ds(base + PEER_QHALF, PEER_QHALF), :].astype(BF16)
        s1 = jnp.dot(sk1_ref[...], q1, preferred_element_type=F32)
        s2 = jnp.dot(sk2_ref[...], q2, preferred_element_type=F32)
        slabs = lambda s: [s[SUBLANES * i:SUBLANES * (i + 1)] for i in range(N_KEYS // SUBLANES)]
        v1 = _top16_replicated(slabs(s1))
        v2 = _top16_replicated(slabs(s2))
        sc = _top16_replicated(_pair_sum_candidates(v1, v2))
        zsum = jnp.zeros_like(sc[0])
        for j in range(PEER_TOPK):
            zsum = zsum + jnp.exp(sc[j] - sc[0])
        row = lambda x: x[0:1]
        th = row(sc[PEER_TOPK - 1])
        cnt = jnp.zeros_like(s1)
        r2 = jnp.zeros_like(s2)
        for b in range(PEER_TOPK):
            cnt = cnt + jnp.where(s1 + row(v2[b]) >= th, 1.0, 0.0)
            r2 = r2 + jnp.where(row(v2[b]) > s2, 1.0, 0.0)
        def twice(x):
            bits = pltpu.bitcast(x.astype(BF16).astype(F32), jnp.uint32)
            return bits | (bits >> 16)

        cnt_w = twice(cnt)
        e1_w = twice(0.5 * jnp.exp(s1 - row(v1[0])) / row(zsum))
        for cc in range(cnt_ref.shape[1]):
            cnt_ref[h, cc] = cnt_w[:, cc * LANES:(cc + 1) * LANES]
            e1_ref[h, cc] = e1_w[:, cc * LANES:(cc + 1) * LANES]
        r2_ref[h] = pltpu.bitcast(r2.astype(BF16), jnp.uint32)
        e2_ref[h] = pltpu.bitcast(jnp.exp(s2 - row(v2[0])).astype(BF16), jnp.uint32)
        return carry

    lax.fori_loop(0, PEER_HEADS, head, 0)


def _peer_scores(wqT_bf, h1T_bf, sk1_bf, sk2_bf, tl=256):
    D, T = h1T_bf.shape
    Q = wqT_bf.shape[0]
    big = jax.ShapeDtypeStruct((PEER_HEADS, T // LANES, N_KEYS, LANES), jnp.uint32)
    big16 = jax.ShapeDtypeStruct((PEER_HEADS, N_KEYS // 2, T), jnp.uint32)
    bspec = pl.BlockSpec((PEER_HEADS, tl // LANES, N_KEYS, LANES), lambda i: (0, i, 0, 0))
    pspec = pl.BlockSpec((PEER_HEADS, N_KEYS // 2, tl), lambda i: (0, 0, i))
    return pl.pallas_call(
        _peer_score_kernel,
        grid=(T // tl,),
        in_specs=[pl.BlockSpec((Q, D), lambda i: (0, 0)),
                  pl.BlockSpec((D, tl), lambda i: (0, i)),
                  pl.BlockSpec((N_KEYS, PEER_QHALF), lambda i: (0, 0)),
                  pl.BlockSpec((N_KEYS, PEER_QHALF), lambda i: (0, 0))],
        out_specs=[bspec, pspec, bspec, pspec],
        out_shape=[big, big16, big, big16],
        scratch_shapes=[pltpu.VMEM((Q, tl), F32)],
        compiler_params=_cparams(("parallel",)),
        name="peer_scores_topk",
    )(wqT_bf, h1T_bf, sk1_bf, sk2_bf)


W_ROWS_PER_GROUP = 64


def _peer_dense_kernel(hb_ref, u_ref, vt_ref, cnt_ref, r2_ref, e1_ref, e2_ref,
                       o_ref, acc_ref, a0_ref, a1_ref, w0_ref, w1_ref, *, te, tm, n_j, n_tiles):
    s = pl.program_id(0)
    sv = jnp.clip(s - 2, 0, n_tiles - 1)
    jv = sv % n_j
    jw = jnp.clip(s - 1, 0, n_tiles - 1) % n_j

    @pl.when(s == 0)
    def _():
        for ref in (a0_ref, a1_ref, w0_ref, w1_ref):
            ref[...] = jnp.zeros_like(ref)

    @pl.when(jv == 0)
    def _():
        acc_ref[...] = jnp.zeros_like(acc_ref)

    wide = 2 * LANES
    n_r, n_c = te // N_KEYS, tm // LANES
    d_model = acc_ref.shape[0]

    def body(a_cur, a_prv, w_cur, w_prv):
        def stage_a(q):
            ms = slice((q // 2) * wide, (q // 2 + 1) * wide)
            ls = slice((q % 2) * wide, (q % 2 + 1) * wide)
            a_cur[ms, ls] = jnp.dot(u_ref[ms, :], hb_ref[:, ls], preferred_element_type=F32)

        def stage_v(p):
            fs = slice((p // 2) * wide, (p // 2 + 1) * wide)
            ls = slice((p % 2) * wide, (p % 2 + 1) * wide)
            acc_ref[fs, ls] += jnp.dot(vt_ref[0, fs, :], w_cur[:, ls], preferred_element_type=F32)

        n_g = N_KEYS // W_ROWS_PER_GROUP
        grp = (W_ROWS_PER_GROUP, LANES)

        def stage_w(idx):
            p, g = idx // n_g, idx % n_g
            r, c = p // n_c, p % n_c
            i1 = jw * n_r + r
            cs = slice(c * LANES, (c + 1) * LANES)
            ks = slice(g * W_ROWS_PER_GROUP // 2, (g + 1) * W_ROWS_PER_GROUP // 2)
            gate = jnp.zeros(grp, BF16)
            half = (W_ROWS_PER_GROUP // 2, LANES)
            tile8 = pl.ds(pl.multiple_of((i1 // SUBLANES) * SUBLANES, SUBLANES), SUBLANES)
            to_top = (SUBLANES - i1 % SUBLANES) % SUBLANES
            for h in range(PEER_HEADS):
                cnt_row = pltpu.roll(cnt_ref[h, c, tile8, :], to_top, 0)[0:1]
                e1_row = pltpu.roll(e1_ref[h, c, tile8, :], to_top, 0)[0:1]
                cnt_b = pltpu.bitcast(jnp.broadcast_to(cnt_row, half), BF16)
                e1_b = pltpu.bitcast(jnp.broadcast_to(e1_row, half), BF16)
                sel = pltpu.bitcast(r2_ref[h, ks, cs], BF16) < cnt_b
                val = pltpu.bitcast(e2_ref[h, ks, cs], BF16) * e1_b
                gate = gate + jnp.where(sel, val, jnp.zeros(grp, BF16))
            ws = slice(r * N_KEYS + g * W_ROWS_PER_GROUP, r * N_KEYS + (g + 1) * W_ROWS_PER_GROUP)
            a = a_prv[ws, cs]
            act = a * (1.0 + lax.erf(a * (1.0 / math.sqrt(2.0))))
            w_prv[ws, cs] = gate * act.astype(BF16)

        for q in range(4):
            stage_a(q)
            for t in range(4):
                stage_v(4 * q + t)
                for gg in range(2):
                    stage_w(8 * q + 2 * t + gg)

    @pl.when(s % 2 == 0)
    def _():
        body(a0_ref, a1_ref, w0_ref, w1_ref)

    @pl.when(s % 2 == 1)
    def _():
        body(a1_ref, a0_ref, w1_ref, w0_ref)

    @pl.when(jv == n_j - 1)
    def _():
        o_ref[...] = acc_ref[...]


def _peer_dense(h1T_bf, u_bf, v_bf, cnt, r2, e1, e2, tm=512, te=512):
    D, T = h1T_bf.shape
    E = u_bf.shape[0]
    assert te == 4 * N_KEYS and tm == 4 * LANES and D == 2048, \
        "the stage interleave is written for 512 x 512 tiles"
    n_i, n_j = T // tm, E // te
    vT_bf = jnp.transpose(v_bf.reshape(n_j, te, D), (0, 2, 1))
    n_tiles = n_i * n_j
    kern = functools.partial(_peer_dense_kernel, te=te, tm=tm, n_j=n_j, n_tiles=n_tiles)
    ta = lambda s: jnp.minimum(s, n_tiles - 1)
    tw = lambda s: jnp.clip(s - 1, 0, n_tiles - 1)
    tv = lambda s: jnp.clip(s - 2, 0, n_tiles - 1)
    sspec = pl.BlockSpec((PEER_HEADS, N_KEYS // 2, tm), lambda s: (0, 0, tw(s) // n_j))
    xspec = pl.BlockSpec((PEER_HEADS, tm // LANES, N_KEYS, LANES), lambda s: (0, tw(s) // n_j, 0, 0))
    return pl.pallas_call(
        kern,
        grid=(n_tiles + 2,),
        in_specs=[pl.BlockSpec((D, tm), lambda s: (0, ta(s) // n_j)),
                  pl.BlockSpec((te, D), lambda s: (ta(s) % n_j, 0)),
                  pl.BlockSpec((1, D, te), lambda s: (tv(s) % n_j, 0, 0)),
                  xspec, sspec, xspec, sspec],
        out_specs=pl.BlockSpec((D, tm), lambda s: (0, tv(s) // n_j)),
        out_shape=jax.ShapeDtypeStruct((D, T), F32),
        scratch_shapes=[pltpu.VMEM((D, tm), F32),
                        pltpu.VMEM((te, tm), F32), pltpu.VMEM((te, tm), F32),
                        pltpu.VMEM((te, tm), BF16), pltpu.VMEM((te, tm), BF16)],
        compiler_params=_cparams(("arbitrary",)),
        name="peer_dense",
    )(h1T_bf, u_bf, vT_bf, cnt, r2, e1, e2)


def _res_ln_kernel(h_ref, ft_ref, g_ref, b_ref, o_ref, *, alpha):
    x = alpha * h_ref[...] + ft_ref[...].T
    mu = jnp.mean(x, -1, keepdims=True)
    xc = x - mu
    var = jnp.mean(xc * xc, -1, keepdims=True)
    o_ref[...] = xc * lax.rsqrt(var + 1e-5) * g_ref[...] + b_ref[...]


def _res_ln(h, fT, g, b, alpha, tm=512):
    T, D = h.shape
    row = pl.BlockSpec((tm, D), lambda i: (i, 0))
    vec = pl.BlockSpec((1, D), lambda i: (0, 0))
    return pl.pallas_call(
        functools.partial(_res_ln_kernel, alpha=alpha),
        grid=(T // tm,),
        in_specs=[row, pl.BlockSpec((D, tm), lambda i: (0, i)), vec, vec],
        out_specs=row,
        out_shape=jax.ShapeDtypeStruct((T, D), F32),
        compiler_params=_cparams(("parallel",)),
        name="res_ln2",
    )(h, fT, g.reshape(1, D), b.reshape(1, D))


def kernel(x, ln0_g, ln0_b, rel_bias, w_in, sink, conv_w, conv_b, f_w1, f_b1, f_freq1, f_w2, f_b2,
           f_freq2, f_w3, hy_bias, mix_norm_g, w_out, ln1_g, ln1_b, peer_wq, peer_subkeys, peer_u,
           peer_v, ln2_g, ln2_b):
    B, S, D = x.shape
    T = B * S
    alpha = (2.0 * DEPTH) ** 0.25
    consts = _dft_constants()

    qi = jnp.arange(BLOCK, dtype=jnp.int32)
    kj = jnp.arange(3 * BLOCK, dtype=jnp.int32)
    rel = kj[None, :] - BLOCK - qi[:, None]
    onehot = (_t5_bucket(rel)[..., None] == jnp.arange(N_BUCKETS, dtype=jnp.int32)).astype(F32)
    bias = jnp.einsum("qkb,bh->hqk", onehot, rel_bias.astype(F32), precision=lax.Precision.HIGHEST)

    h, h_bf = _ln0(x.reshape(T, D), ln0_g, ln0_b)
    for l in range(DEPTH):
        proj = _matmul(h_bf, w_in[l].astype(BF16), tm=512, tn=1536)
        attn = _attention(proj, bias, sink[l], B, S)

        hfilt = _filters(S, f_w1[l], f_b1[l], f_freq1[l], f_w2[l], f_b2[l], f_freq2[l], f_w3[l])
        kf = _filter_spectra(hfilt, consts)
        u = _short_conv(proj.reshape(B, S, -1), conv_w[l], conv_b[l], ATTN_WIDTH + 2 * KV_WIDTH)
        nct = HY_WIDTH // LANES
        z1 = _long_conv(u, 0, u, nct, kf, 0, hy_bias[l, 0], consts)
        hyo = _long_conv(z1, 0, u, 2 * nct, kf, 1, hy_bias[l, 1], consts)

        h1, h1T_bf = _mix_out(attn, hyo.reshape(T, HY_WIDTH), h, mix_norm_g[l], w_out[l].astype(BF16),
                              ln1_g[l], ln1_b[l], alpha)
        cnt, r2, e1, e2 = _peer_scores(peer_wq[l].T.astype(BF16), h1T_bf,
                                       peer_subkeys[l, 0].astype(BF16), peer_subkeys[l, 1].astype(BF16))
        ffnT = _peer_dense(h1T_bf, peer_u[l].astype(BF16), peer_v[l].astype(BF16), cnt, r2, e1, e2)
        h = _res_ln(h1, ffnT, ln2_g[l], ln2_b[l], alpha)
        if l + 1 < DEPTH:
            h_bf = h.astype(BF16)
    return h.reshape(B, S, D)
```

```python
import functools
import math

import numpy as np
import jax
import jax.numpy as jnp
from jax import lax
from jax.experimental import pallas as pl
from jax.experimental.pallas import tpu as pltpu

F32 = jnp.float32
BF16 = jnp.bfloat16

D_MODEL = 2048
HEAD_DIM = 128
N_Q_HEADS = 8
N_KV_HEADS = 2
GQA_GROUP = N_Q_HEADS // N_KV_HEADS
ATTN_WIDTH = N_Q_HEADS * HEAD_DIM
KV_WIDTH = N_KV_HEADS * HEAD_DIM
WINDOW = 128
BLOCK = 128
N_BUCKETS = 32
MAX_DISTANCE = 128
HY_WIDTH = D_MODEL - ATTN_WIDTH
HY_ORDER = 2
POS_BANDS = 16
POS_EMB = 1 + 2 * POS_BANDS
FILTER_HIDDEN = 64
FAST_DECAY_PCT = 0.3
SLOW_DECAY_PCT = 1.5
DECAY_TARGET = 1e-2
NORM_GROUP = 128
N_KEYS = 128
PEER_HEADS = 8
PEER_QDIM = 256
PEER_QHALF = PEER_QDIM // 2
PEER_TOPK = 16
NEG = -1e30
DEPTH = 1

LANES = 128
VMEM_LIMIT = 56 * 1024 * 1024

FFT_N1 = 64
FFT_N2 = 128
FFT_K1 = FFT_N1 // 2 + 1
FFT_K1_PAD = 40
FFT_SLOT = 2 * FFT_N2
KF_ROWS = FFT_K1 * FFT_SLOT
SUBLANES = 8
W_PITCH = FFT_SLOT + SUBLANES
X_PITCH = FFT_N2 + SUBLANES
W_ROWS = FFT_K1_PAD * W_PITCH
X_ROWS = (FFT_N1 // 2) * X_PITCH


def _cparams(sem, vmem=VMEM_LIMIT):
    return pltpu.CompilerParams(dimension_semantics=sem, vmem_limit_bytes=vmem)


def _ln0_kernel(x_ref, g_ref, b_ref, h_ref, hb_ref):
    x = x_ref[...]
    mu = jnp.mean(x, -1, keepdims=True)
    xc = x - mu
    var = jnp.mean(xc * xc, -1, keepdims=True)
    y = xc * lax.rsqrt(var + 1e-5) * g_ref[...] + b_ref[...]
    h_ref[...] = y
    hb_ref[...] = y.astype(BF16)


def _ln0(x2d, g, b, tm=512):
    T, D = x2d.shape
    return pl.pallas_call(
        _ln0_kernel,
        grid=(T // tm,),
        in_specs=[pl.BlockSpec((tm, D), lambda i: (i, 0)),
                  pl.BlockSpec((1, D), lambda i: (0, 0)),
                  pl.BlockSpec((1, D), lambda i: (0, 0))],
        out_specs=[pl.BlockSpec((tm, D), lambda i: (i, 0)),
                   pl.BlockSpec((tm, D), lambda i: (i, 0))],
        out_shape=[jax.ShapeDtypeStruct((T, D), F32), jax.ShapeDtypeStruct((T, D), BF16)],
        compiler_params=_cparams(("parallel",)),
        name="ln0",
    )(x2d, g.reshape(1, D), b.reshape(1, D))


def _mm_kernel(a_ref, b_ref, o_ref):
    o_ref[...] = jnp.dot(a_ref[...], b_ref[...], preferred_element_type=F32)


def _matmul(a, b, tm, tn):
    M, K = a.shape
    N = b.shape[1]
    return pl.pallas_call(
        _mm_kernel,
        grid=(N // tn, M // tm),
        in_specs=[pl.BlockSpec((tm, K), lambda j, i: (i, 0)),
                  pl.BlockSpec((K, tn), lambda j, i: (0, j))],
        out_specs=pl.BlockSpec((tm, tn), lambda j, i: (i, j)),
        out_shape=jax.ShapeDtypeStruct((M, N), F32),
        compiler_params=_cparams(("parallel", "parallel")),
        name="in_proj",
    )(a, b)


def _t5_bucket(rel):
    nb = N_BUCKETS // 2
    ret = (rel > 0).astype(jnp.int32) * nb
    n = jnp.abs(rel)
    max_exact = nb // 2
    nf = jnp.maximum(n, 1).astype(F32)
    large = max_exact + (jnp.log(nf / max_exact) / math.log(MAX_DISTANCE / max_exact)
                         * (nb - max_exact)).astype(jnp.int32)
    large = jnp.minimum(large, nb - 1)
    return ret + jnp.where(n < max_exact, n, large)


def _attn_kernel(sink_ref, q_ref, kp_ref, kc_ref, kn_ref, vp_ref, vc_ref, vn_ref, bias_ref, o_ref,
                 *, nb, seq):
    n = pl.program_id(0) % nb
    rows = GQA_GROUP * BLOCK
    row = lax.broadcasted_iota(jnp.int32, (rows, 3 * BLOCK), 0)
    kj = lax.broadcasted_iota(jnp.int32, (rows, 3 * BLOCK), 1)
    rel = kj - BLOCK - (row & (BLOCK - 1))
    kabs = n * BLOCK + kj - BLOCK
    valid = (jnp.abs(rel) <= WINDOW) & (kabs >= 0) & (kabs < seq)
    head_of_row = lax.broadcasted_iota(jnp.int32, (rows, 1), 0) // BLOCK
    k = jnp.concatenate([kp_ref[...], kc_ref[...], kn_ref[...]], axis=0).astype(BF16)
    v = jnp.concatenate([vp_ref[...], vc_ref[...], vn_ref[...]], axis=0).astype(BF16)
    scale = 1.0 / math.sqrt(HEAD_DIM)
    for g in range(N_KV_HEADS):
        kg = k[:, g * HEAD_DIM:(g + 1) * HEAD_DIM]
        vg = v[:, g * HEAD_DIM:(g + 1) * HEAD_DIM]
        h0 = g * GQA_GROUP
        qg = jnp.concatenate([q_ref[:, (h0 + r) * HEAD_DIM:(h0 + r + 1) * HEAD_DIM]
                              for r in range(GQA_GROUP)], axis=0).astype(BF16)
        s = lax.dot_general(qg, kg, (((1,), (1,)), ((), ())), preferred_element_type=F32) * scale
        bias_g = bias_ref[h0:h0 + GQA_GROUP].reshape(rows, 3 * BLOCK)
        s = jnp.where(valid, s + bias_g, NEG)
        sk = jnp.zeros((rows, 1), F32)
        for r in range(GQA_GROUP):
            sk = jnp.where(head_of_row == r, sink_ref[h0 + r], sk)
        m = jnp.maximum(jnp.max(s, -1, keepdims=True), sk)
        p = jnp.exp(s - m)
        denom = jnp.sum(p, -1, keepdims=True) + jnp.exp(sk - m)
        p = p / denom
        o = jnp.dot(p.astype(BF16), vg, preferred_element_type=F32)
        for r in range(GQA_GROUP):
            o_ref[:, (h0 + r) * HEAD_DIM:(h0 + r + 1) * HEAD_DIM] = o[r * BLOCK:(r + 1) * BLOCK]


def _attention(proj, bias, sink, batch, seq):
    T = proj.shape[0]
    nb = seq // BLOCK
    kcol = ATTN_WIDTH // KV_WIDTH
    vcol = kcol + 1

    def prev(i):
        return i - jnp.where(i % nb == 0, 0, 1)

    def nxt(i):
        return i + jnp.where(i % nb == nb - 1, 0, 1)

    kern = functools.partial(_attn_kernel, nb=nb, seq=seq)
    return pl.pallas_call(
        kern,
        grid=(T // BLOCK,),
        in_specs=[pl.BlockSpec(memory_space=pltpu.SMEM),
                  pl.BlockSpec((BLOCK, ATTN_WIDTH), lambda i: (i, 0)),
                  pl.BlockSpec((BLOCK, KV_WIDTH), lambda i: (prev(i), kcol)),
                  pl.BlockSpec((BLOCK, KV_WIDTH), lambda i: (i, kcol)),
                  pl.BlockSpec((BLOCK, KV_WIDTH), lambda i: (nxt(i), kcol)),
                  pl.BlockSpec((BLOCK, KV_WIDTH), lambda i: (prev(i), vcol)),
                  pl.BlockSpec((BLOCK, KV_WIDTH), lambda i: (i, vcol)),
                  pl.BlockSpec((BLOCK, KV_WIDTH), lambda i: (nxt(i), vcol)),
                  pl.BlockSpec((N_Q_HEADS, BLOCK, 3 * BLOCK), lambda i: (0, 0, 0))],
        out_specs=pl.BlockSpec((BLOCK, ATTN_WIDTH), lambda i: (i, 0)),
        out_shape=jax.ShapeDtypeStruct((T, ATTN_WIDTH), F32),
        compiler_params=_cparams(("parallel",)),
        name="window_attn",
    )(sink, proj, proj, proj, proj, proj, proj, proj, bias)


def _filter_kernel(z_ref, w1_ref, b1_ref, fr1_ref, w2_ref, b2_ref, fr2_ref, w3_ref, dl_ref, o_ref,
                   *, tt, seq):
    hi = lax.Precision.HIGHEST
    a = jnp.dot(z_ref[...], w1_ref[...], preferred_element_type=F32, precision=hi) + b1_ref[...]
    hid = jnp.sin(fr1_ref[...] * a)
    a = jnp.dot(hid, w2_ref[...], preferred_element_type=F32, precision=hi) + b2_ref[...]
    hid = jnp.sin(fr2_ref[...] * a)
    h = jnp.dot(hid.astype(BF16), w3_ref[0], preferred_element_type=F32)
    j = lax.broadcasted_iota(jnp.int32, h.shape, 0) + pl.program_id(0) * tt
    lag = jnp.where(j < seq, j, (2 * seq - j) & (seq - 1))
    tn = lag.astype(F32) / float(max(seq - 1, 1))
    o_ref[...] = h * jnp.exp(-tn * dl_ref[...])


def _filters(seq, w1, b1, fr1, w2, b2, fr2, w3, tt=512):
    assert seq & (seq - 1) == 0
    t = jnp.arange(seq, dtype=F32)
    tn = t / max(seq - 1, 1)
    w = 2.0 * math.pi * t / seq
    bands = jnp.linspace(1e-4, POS_BANDS - 1, POS_BANDS, dtype=F32)
    z = jnp.concatenate([tn[:, None], jnp.cos(w[:, None] * bands), -jnp.sin(w[:, None] * bands)], -1)
    zp = jnp.pad(z, ((0, 0), (0, FILTER_HIDDEN - POS_EMB)))
    zfull = jnp.concatenate([zp, zp[0:1], zp[1:][::-1]], axis=0)
    w1p = jnp.pad(w1, ((0, FILTER_HIDDEN - POS_EMB), (0, 0)))
    max_decay = math.log(DECAY_TARGET) / FAST_DECAY_PCT
    min_decay = math.log(DECAY_TARGET) / SLOW_DECAY_PCT
    deltas = jnp.abs(jnp.linspace(min_decay, max_decay, HY_WIDTH, dtype=F32))
    ncol = HY_ORDER * HY_WIDTH
    dl = jnp.tile(deltas, HY_ORDER).reshape(1, ncol)
    H = FILTER_HIDDEN
    w3s = jnp.transpose(w3.reshape(H, HY_ORDER, 2, HY_WIDTH), (2, 0, 1, 3)).reshape(2, H, ncol).astype(BF16)
    half = seq // tt
    kern = functools.partial(_filter_kernel, tt=tt, seq=seq)
    full = lambda r, c: pl.BlockSpec((r, c), lambda i: (0, 0))
    return pl.pallas_call(
        kern,
        grid=(2 * half,),
        in_specs=[pl.BlockSpec((tt, H), lambda i: (i, 0)),
                  full(H, H), full(1, H), full(1, H), full(H, H), full(1, H), full(1, H),
                  pl.BlockSpec((1, H, ncol), lambda i: (i // half, 0, 0)), full(1, ncol)],
        out_specs=pl.BlockSpec((tt, ncol), lambda i: (i, 0)),
        out_shape=jax.ShapeDtypeStruct((2 * seq, ncol), F32),
        compiler_params=_cparams(("parallel",)),
        name="hyena_filter_mlp",
    )(zfull, w1p, b1.reshape(1, H), fr1.reshape(1, H), w2, b2.reshape(1, H), fr2.reshape(1, H), w3s, dl)


def _sconv_kernel(x_ref, w_ref, b_ref, o_ref):
    x = x_ref[0]
    L = x.shape[0]
    rows = lax.broadcasted_iota(jnp.int32, x.shape, 0)
    xm = jnp.where(rows == 0, 0.0, pltpu.roll(x, 1, 0))
    xp = jnp.where(rows == L - 1, 0.0, pltpu.roll(x, L - 1, 0))
    w = w_ref[...]
    o_ref[0] = xm * w[0:1] + x * w[1:2] + xp * w[2:3] + b_ref[...]


def _short_conv(proj3, conv_w, conv_b, col0, ct=256):
    B, L, _ = proj3.shape
    C = conv_w.shape[1]
    off = col0 // ct
    return pl.pallas_call(
        _sconv_kernel,
        grid=(B, C // ct),
        in_specs=[pl.BlockSpec((1, L, ct), lambda b, c: (b, 0, c + off)),
                  pl.BlockSpec((3, ct), lambda b, c: (0, c)),
                  pl.BlockSpec((1, ct), lambda b, c: (0, c))],
        out_specs=pl.BlockSpec((1, L, ct), lambda b, c: (b, 0, c)),
        out_shape=jax.ShapeDtypeStruct((B, L, C), F32),
        compiler_params=_cparams(("parallel", "parallel")),
        name="hyena_short_conv",
    )(proj3, conv_w, conv_b.reshape(1, C))


def _dft_constants():
    n1h = FFT_N1 // 2
    k1 = np.arange(FFT_K1_PAD)[:, None].astype(np.float64)
    n1 = np.arange(FFT_N1)[None, :].astype(np.float64)
    ang = 2.0 * np.pi * k1 * n1 / FFT_N1
    live = (np.arange(FFT_K1_PAD) < FFT_K1)[:, None]
    f1_full = np.concatenate([np.where(live, np.cos(ang), 0.0), np.where(live, -np.sin(ang), 0.0)], 0)
    f1 = f1_full[:, :n1h]
    kk1 = np.arange(FFT_K1)[:, None, None].astype(np.float64)
    k2 = np.arange(FFT_N2)[None, :, None].astype(np.float64)
    n2 = np.arange(FFT_N2)[None, None, :].astype(np.float64)
    phi = 2.0 * np.pi * (n2 * k2 / FFT_N2 + n2 * kk1 / (FFT_N1 * FFT_N2))
    c, s = np.cos(phi), np.sin(phi)
    g = np.concatenate([np.concatenate([c, s], 2), np.concatenate([-s, c], 2)], 1)
    ct_, st_ = np.transpose(c, (0, 2, 1)), np.transpose(s, (0, 2, 1))
    ginv = np.concatenate([np.concatenate([ct_, -st_], 2), np.concatenate([st_, ct_], 2)], 1)
    wk = np.where((np.arange(FFT_K1_PAD) == 0) | (np.arange(FFT_K1_PAD) == FFT_N1 // 2), 1.0, 2.0)
    wk = np.where(np.arange(FFT_K1_PAD) < FFT_K1, wk, 0.0)[None, :] / (FFT_N1 * FFT_N2)
    angi = 2.0 * np.pi * np.arange(n1h)[:, None] * np.arange(FFT_K1_PAD)[None, :] / FFT_N1
    finv = np.concatenate([wk * np.cos(angi), -wk * np.sin(angi)], 1)
    as_bf = lambda a: jnp.asarray(a.astype(np.float32)).astype(BF16)
    return as_bf(f1), as_bf(g), as_bf(finv), as_bf(ginv), as_bf(f1_full)


def _pad_rows_in(src_ref, xp_ref, n_blocks=FFT_N1 // 2, zero_row_of_block=None):
    for n1 in range(n_blocks):
        blk = src_ref[pl.ds(n1 * FFT_N2, FFT_N2), :]
        if n1 == zero_row_of_block:
            rows = lax.broadcasted_iota(jnp.int32, blk.shape, 0)
            blk = jnp.where(rows == 0, 0.0, blk)
        xp_ref[pl.ds(n1 * X_PITCH, FFT_N2), :] = blk


def _fft_stage1(xp_ref, w_ref, f1_ref):
    n1h = f1_ref.shape[1]
    kp = FFT_K1_PAD

    def body(i, carry):
        n2 = 2 * i
        xa = xp_ref[pl.ds(n2, n1h, stride=X_PITCH), :]
        xb = xp_ref[pl.ds(n2 + 1, n1h, stride=X_PITCH), :]
        xs = jnp.concatenate([xa, xb], axis=1).astype(BF16)
        r = jnp.dot(f1_ref[...], xs, preferred_element_type=F32)
        w_ref[pl.ds(n2, kp, stride=W_PITCH), :] = r[0:kp, 0:LANES]
        w_ref[pl.ds(n2 + 1, kp, stride=W_PITCH), :] = r[0:kp, LANES:2 * LANES]
        w_ref[pl.ds(FFT_N2 + n2, kp, stride=W_PITCH), :] = r[kp:2 * kp, 0:LANES]
        w_ref[pl.ds(FFT_N2 + n2 + 1, kp, stride=W_PITCH), :] = r[kp:2 * kp, LANES:2 * LANES]
        return carry

    lax.fori_loop(0, FFT_N2 // 2, body, 0, unroll=8)


def _kf_kernel(k_ref, f1_ref, g_ref, kf_ref, w_ref, xp_ref):
    scale = 1.0 / (jnp.sum(jnp.abs(k_ref[...]), 0, keepdims=True) + 1e-6)
    _pad_rows_in(k_ref, xp_ref, n_blocks=FFT_N1, zero_row_of_block=FFT_N1 // 2)
    _fft_stage1(xp_ref, w_ref, f1_ref)
    out = kf_ref.at[0]

    def body(k1, carry):
        src = pl.multiple_of(k1 * W_PITCH, 8)
        dst = pl.multiple_of(k1 * FFT_SLOT, FFT_SLOT)
        s = jnp.dot(g_ref[k1], w_ref[pl.ds(src, FFT_SLOT), :].astype(BF16), preferred_element_type=F32)
        out[pl.ds(dst, FFT_SLOT), :] = s * scale
        return carry

    lax.fori_loop(0, FFT_K1, body, 0, unroll=3)


def _filter_spectra(kfull, consts):
    L2 = kfull.shape[0]
    C = HY_WIDTH
    nct = C // LANES
    _, g, _, _, f1_full = consts
    return pl.pallas_call(
        _kf_kernel,
        grid=(HY_ORDER * nct,),
        in_specs=[pl.BlockSpec((L2, LANES), lambda j: (0, j)),
                  pl.BlockSpec(f1_full.shape, lambda j: (0, 0)),
                  pl.BlockSpec(g.shape, lambda j: (0, 0, 0))],
        out_specs=pl.BlockSpec((1, KF_ROWS, LANES), lambda j: (j // nct, 0, j % nct)),
        out_shape=jax.ShapeDtypeStruct((HY_ORDER, KF_ROWS, C), F32),
        scratch_shapes=[pltpu.VMEM((W_ROWS, LANES), F32), pltpu.VMEM((2 * X_ROWS, LANES), F32)],
        compiler_params=_cparams(("parallel",)),
        name="hyena_filter_fft",
    )(kfull, f1_full, g)


CONV_GROUP = 3


def _conv_kernel(z_ref, gate_ref, kf_ref, d_ref, f1_ref, g_ref, finv_ref, ginv_ref, o_ref,
                 w_ref, xp_ref, y_ref):
    zsrc = z_ref.at[0]
    kf = kf_ref.at[0]
    _pad_rows_in(zsrc, xp_ref)
    _fft_stage1(xp_ref, w_ref, f1_ref)

    def forward(k1):
        base = pl.multiple_of(k1 * W_PITCH, 8)
        kbase = pl.multiple_of(k1 * FFT_SLOT, FFT_SLOT)
        a = w_ref[pl.ds(base, FFT_SLOT), :].astype(BF16)
        s = jnp.dot(g_ref[k1], a, preferred_element_type=F32)
        sr, si = s[0:FFT_N2], s[FFT_N2:FFT_SLOT]
        kr = kf[pl.ds(kbase, FFT_N2), :]
        ki = kf[pl.ds(kbase + FFT_N2, FFT_N2), :]
        return jnp.concatenate([sr * kr - si * ki, sr * ki + si * kr], axis=0).astype(BF16)

    def inverse(k1, y):
        base = pl.multiple_of(k1 * W_PITCH, 8)
        w_ref[pl.ds(base, FFT_SLOT), :] = jnp.dot(ginv_ref[k1], y, preferred_element_type=F32)

    def step(i, do_forward, do_inverse):
        prev = [y_ref[k] for k in range(CONV_GROUP)] if do_inverse else None
        new = [forward(i * CONV_GROUP + k) for k in range(CONV_GROUP)] if do_forward else None
        if do_inverse:
            for k in range(CONV_GROUP):
                inverse((i - 1) * CONV_GROUP + k, prev[k])
        if do_forward:
            for k in range(CONV_GROUP):
                y_ref[k] = new[k]

    n_groups = FFT_K1 // CONV_GROUP
    step(0, True, False)

    def body(i, carry):
        step(i, True, True)
        return carry

    lax.fori_loop(1, n_groups, body, 0)
    step(n_groups, False, True)

    n1h = FFT_N1 // 2
    kp = FFT_K1_PAD

    def body2(i, carry):
        n2 = 2 * i
        zr = jnp.concatenate([w_ref[pl.ds(n2, kp, stride=W_PITCH), :],
                              w_ref[pl.ds(n2 + 1, kp, stride=W_PITCH), :]], axis=1)
        zi = jnp.concatenate([w_ref[pl.ds(FFT_N2 + n2, kp, stride=W_PITCH), :],
                              w_ref[pl.ds(FFT_N2 + n2 + 1, kp, stride=W_PITCH), :]], axis=1)
        zz = jnp.concatenate([zr, zi], axis=0).astype(BF16)
        x = jnp.dot(finv_ref[...], zz, preferred_element_type=F32)
        xp_ref[pl.ds(n2, n1h, stride=X_PITCH), :] = x[:, 0:LANES]
        xp_ref[pl.ds(n2 + 1, n1h, stride=X_PITCH), :] = x[:, LANES:2 * LANES]
        return carry

    lax.fori_loop(0, FFT_N2 // 2, body2, 0, unroll=8)
    d = d_ref[...]
    for n1 in range(n1h):
        rs = pl.ds(n1 * FFT_N2, FFT_N2)
        y = xp_ref[pl.ds(n1 * X_PITCH, FFT_N2), :]
        o_ref[0, rs, :] = gate_ref[0, rs, :] * (y + z_ref[0, rs, :] * d)


def _long_conv(z_arr, z_off, gate_arr, gate_off, kf_all, order, d, consts):
    B, L, _ = z_arr.shape
    C = HY_WIDTH
    nct = C // LANES
    f1, g, finv, ginv, _ = consts
    cst2 = lambda a: pl.BlockSpec(a.shape, lambda c, b: (0, 0))
    cst3 = lambda a: pl.BlockSpec(a.shape, lambda c, b: (0, 0, 0))
    return pl.pallas_call(
        _conv_kernel,
        grid=(nct, B),
        in_specs=[pl.BlockSpec((1, L, LANES), lambda c, b: (b, 0, c + z_off)),
                  pl.BlockSpec((1, L, LANES), lambda c, b: (b, 0, c + gate_off)),
                  pl.BlockSpec((1, KF_ROWS, LANES), lambda c, b: (order, 0, c)),
                  pl.BlockSpec((1, LANES), lambda c, b: (0, c)),
                  cst2(f1), cst3(g), cst2(finv), cst3(ginv)],
        out_specs=pl.BlockSpec((1, L, LANES), lambda c, b: (b, 0, c)),
        out_shape=jax.ShapeDtypeStruct((B, L, C), F32),
        scratch_shapes=[pltpu.VMEM((W_ROWS, LANES), F32), pltpu.VMEM((X_ROWS, LANES), F32),
                        pltpu.VMEM((CONV_GROUP, FFT_SLOT, LANES), BF16)],
        compiler_params=_cparams(("parallel", "parallel")),
        name=f"hyena_long_conv{order}",
    )(z_arr, gate_arr, kf_all, d.reshape(1, C), f1, g, finv, ginv)


def _mix_kernel(attn_ref, hyo_ref, h0_ref, mg_ref, w_ref, g_ref, b_ref, h1_ref, h1t_ref, *, alpha):
    def gnorm(x, goff):
        parts = []
        for gi in range(x.shape[1] // NORM_GROUP):
            xg = x[:, gi * NORM_GROUP:(gi + 1) * NORM_GROUP]
            ms = jnp.mean(xg * xg, -1, keepdims=True)
            gg = mg_ref[:, goff + gi * NORM_GROUP: goff + (gi + 1) * NORM_GROUP]
            parts.append((xg * lax.rsqrt(ms + 1e-6) * gg).astype(BF16))
        return jnp.concatenate(parts, axis=1)

    a = gnorm(attn_ref[...], 0)
    y = gnorm(hyo_ref[...], ATTN_WIDTH)
    mix = (jnp.dot(a, w_ref[0:ATTN_WIDTH, :], preferred_element_type=F32)
           + jnp.dot(y, w_ref[ATTN_WIDTH:, :], preferred_element_type=F32))
    x = alpha * h0_ref[...] + mix
    mu = jnp.mean(x, -1, keepdims=True)
    xc = x - mu
    var = jnp.mean(xc * xc, -1, keepdims=True)
    h1 = xc * lax.rsqrt(var + 1e-5) * g_ref[...] + b_ref[...]
    h1_ref[...] = h1
    h1t_ref[...] = h1.T.astype(BF16)


def _mix_out(attn, hyo, h0, mix_g, w_out_bf, ln_g, ln_b, alpha, tm=256):
    T, D = h0.shape
    kern = functools.partial(_mix_kernel, alpha=alpha)
    return pl.pallas_call(
        kern,
        grid=(T // tm,),
        in_specs=[pl.BlockSpec((tm, ATTN_WIDTH), lambda i: (i, 0)),
                  pl.BlockSpec((tm, HY_WIDTH), lambda i: (i, 0)),
                  pl.BlockSpec((tm, D), lambda i: (i, 0)),
                  pl.BlockSpec((1, D), lambda i: (0, 0)),
                  pl.BlockSpec((D, D), lambda i: (0, 0)),
                  pl.BlockSpec((1, D), lambda i: (0, 0)),
                  pl.BlockSpec((1, D), lambda i: (0, 0))],
        out_specs=[pl.BlockSpec((tm, D), lambda i: (i, 0)), pl.BlockSpec((D, tm), lambda i: (0, i))],
        out_shape=[jax.ShapeDtypeStruct((T, D), F32), jax.ShapeDtypeStruct((D, T), BF16)],
        compiler_params=_cparams(("parallel",)),
        name="mix_out_ln1",
    )(attn, hyo, h0, mix_g.reshape(1, D), w_out_bf, ln_g.reshape(1, D), ln_b.reshape(1, D))


def _oddeven_sort_pairs(n):
    pairs = []

    def merge(lo, m, r):
        step = r * 2
        if step < m:
            merge(lo, m, step)
            merge(lo + r, m, step)
            pairs.extend((i, i + r) for i in range(lo + r, lo + m - r, step))
        else:
            pairs.append((lo, lo + r))

    def sort(lo, m):
        if m > 1:
            sort(lo, m // 2)
            sort(lo + m // 2, m // 2)
            merge(lo, m, 1)

    sort(0, n)
    return pairs


def _bitonic_merge_pairs(n):
    pairs, d = [], n // 2
    while d >= 1:
        pairs.extend((i, i + d) for i in range(n) if (i & d) == 0)
        d //= 2
    return pairs


def _apply_network(xs, pairs):
    xs = list(xs)
    for i, j in pairs:
        xs[i], xs[j] = jnp.maximum(xs[i], xs[j]), jnp.minimum(xs[i], xs[j])
    return xs


def _top16_replicated(xs):
    xs = _apply_network(xs, _oddeven_sort_pairs(PEER_TOPK))
    merge = _bitonic_merge_pairs(PEER_TOPK)
    for shift in (4, 2, 1):
        other = [pltpu.roll(x, shift, 0) for x in xs]
        xs = [jnp.maximum(xs[i], other[PEER_TOPK - 1 - i]) for i in range(PEER_TOPK)]
        xs = _apply_network(xs, merge)
    return xs


def _pair_sum_candidates(v1, v2):
    sub = lax.broadcasted_iota(jnp.int32, v1[0].shape, 0)

    def by_sublane(vs):
        out = vs[SUBLANES - 1]
        for j in range(SUBLANES - 2, -1, -1):
            out = jnp.where(sub == j, vs[j], out)
        return out

    ninf = -jnp.inf
    v2lo, v2hi = by_sublane(v2[:SUBLANES]), by_sublane(v2[SUBLANES:])
    v1lo, v1hi = by_sublane(v1[:SUBLANES]), by_sublane(v1[SUBLANES:])
    cands = [v1[0] + v2lo, v1[0] + v2hi, v1[1] + v2lo]
    for a in (2, 3, 4):
        cands.append(jnp.where(sub < PEER_TOPK // (a + 1), v1[a] + v2lo, ninf))
    cands.append(jnp.where(sub >= 5, v1lo + v2[0], ninf))
    cands.append(v1hi + v2[0])
    cands.append(jnp.where(sub >= 5, v1lo + v2[1], ninf))
    pad = jnp.full(v1[0].shape, ninf, F32)
    return cands + [pad] * (PEER_TOPK - len(cands))


def _peer_score_kernel(wq_ref, h_ref, sk1_ref, sk2_ref, cnt_ref, r2_ref, e1_ref, e2_ref, q_ref):
    q_ref[...] = jnp.dot(wq_ref[...], h_ref[...], preferred_element_type=F32)

    def head(h, carry):
        base = pl.multiple_of(h * PEER_QDIM, PEER_QDIM)
        q1 = q_ref[pl.ds(base, PEER_QHALF), :].astype(BF16)
        q2 = q_ref[pl.ds(base + PEER_QHALF, PEER_QHALF), :].astype(BF16)
        s1 = jnp.dot(sk1_ref[...], q1, preferred_element_type=F32)
        s2 = jnp.dot(sk2_ref[...], q2, preferred_element_type=F32)
        slabs = lambda s: [s[SUBLANES * i:SUBLANES * (i + 1)] for i in range(N_KEYS // SUBLANES)]
        v1 = _top16_replicated(slabs(s1))
        v2 = _top16_replicated(slabs(s2))
        sc = _top16_replicated(_pair_sum_candidates(v1, v2))
        zsum = jnp.zeros_like(sc[0])
        for j in range(PEER_TOPK):
            zsum = zsum + jnp.exp(sc[j] - sc[0])
        row = lambda x: x[0:1]
        th = row(sc[PEER_TOPK - 1])
        cnt = jnp.zeros_like(s1)
        r2 = jnp.zeros_like(s2)
        for b in range(PEER_TOPK):
            cnt = cnt + jnp.where(s1 + row(v2[b]) >= th, 1.0, 0.0)
            r2 = r2 + jnp.where(row(v2[b]) > s2, 1.0, 0.0)
        def twice(x):
            bits = pltpu.bitcast(x.astype(BF16).astype(F32), jnp.uint32)
            return bits | (bits >> 16)

        cnt_w = twice(cnt)
        e1_w = twice(0.5 * jnp.exp(s1 - row(v1[0])) / row(zsum))
        for cc in range(cnt_ref.shape[1]):
            cnt_ref[h, cc] = cnt_w[:, cc * LANES:(cc + 1) * LANES]
            e1_ref[h, cc] = e1_w[:, cc * LANES:(cc + 1) * LANES]
        r2_ref[h] = pltpu.bitcast(r2.astype(BF16), jnp.uint32)
        e2_ref[h] = pltpu.bitcast(jnp.exp(s2 - row(v2[0])).astype(BF16), jnp.uint32)
        return carry

    lax.fori_loop(0, PEER_HEADS, head, 0)


def _peer_scores(wqT_bf, h1T_bf, sk1_bf, sk2_bf, tl=256):
    D, T = h1T_bf.shape
    Q = wqT_bf.shape[0]
    big = jax.ShapeDtypeStruct((PEER_HEADS, T // LANES, N_KEYS, LANES), jnp.uint32)
    big16 = jax.ShapeDtypeStruct((PEER_HEADS, N_KEYS // 2, T), jnp.uint32)
    bspec = pl.BlockSpec((PEER_HEADS, tl // LANES, N_KEYS, LANES), lambda i: (0, i, 0, 0))
    pspec = pl.BlockSpec((PEER_HEADS, N_KEYS // 2, tl), lambda i: (0, 0, i))
    return pl.pallas_call(
        _peer_score_kernel,
        grid=(T // tl,),
        in_specs=[pl.BlockSpec((Q, D), lambda i: (0, 0)),
                  pl.BlockSpec((D, tl), lambda i: (0, i)),
                  pl.BlockSpec((N_KEYS, PEER_QHALF), lambda i: (0, 0)),
                  pl.BlockSpec((N_KEYS, PEER_QHALF), lambda i: (0, 0))],
        out_specs=[bspec, pspec, bspec, pspec],
        out_shape=[big, big16, big, big16],
        scratch_shapes=[pltpu.VMEM((Q, tl), F32)],
        compiler_params=_cparams(("parallel",)),
        name="peer_scores_topk",
    )(wqT_bf, h1T_bf, sk1_bf, sk2_bf)


W_ROWS_PER_GROUP = 64


def _peer_dense_kernel(hb_ref, u_ref, vt_ref, cnt_ref, r2_ref, e1_ref, e2_ref,
                       o_ref, acc_ref, a0_ref, a1_ref, w0_ref, w1_ref, *, te, tm, n_j, n_tiles):
    s = pl.program_id(0)
    sv = jnp.clip(s - 2, 0, n_tiles - 1)
    jv = sv % n_j
    jw = jnp.clip(s - 1, 0, n_tiles - 1) % n_j

    @pl.when(s == 0)
    def _():
        for ref in (a0_ref, a1_ref, w0_ref, w1_ref):
            ref[...] = jnp.zeros_like(ref)

    @pl.when(jv == 0)
    def _():
        acc_ref[...] = jnp.zeros_like(acc_ref)

    wide = 2 * LANES
    n_r, n_c = te // N_KEYS, tm // LANES
    d_model = acc_ref.shape[0]

    def body(a_cur, a_prv, w_cur, w_prv):
        def stage_a(q):
            ms = slice((q // 2) * wide, (q // 2 + 1) * wide)
            ls = slice((q % 2) * wide, (q % 2 + 1) * wide)
            a_cur[ms, ls] = jnp.dot(u_ref[ms, :], hb_ref[:, ls], preferred_element_type=F32)

        def stage_v(p):
            fs = slice((p // 2) * wide, (p // 2 + 1) * wide)
            ls = slice((p % 2) * wide, (p % 2 + 1) * wide)
            acc_ref[fs, ls] += jnp.dot(vt_ref[0, fs, :], w_cur[:, ls], preferred_element_type=F32)

        n_g = N_KEYS // W_ROWS_PER_GROUP
        grp = (W_ROWS_PER_GROUP, LANES)

        def stage_w(idx):
            p, g = idx // n_g, idx % n_g
            r, c = p // n_c, p % n_c
            i1 = jw * n_r + r
            cs = slice(c * LANES, (c + 1) * LANES)
            ks = slice(g * W_ROWS_PER_GROUP // 2, (g + 1) * W_ROWS_PER_GROUP // 2)
            gate = jnp.zeros(grp, BF16)
            half = (W_ROWS_PER_GROUP // 2, LANES)
            tile8 = pl.ds(pl.multiple_of((i1 // SUBLANES) * SUBLANES, SUBLANES), SUBLANES)
            to_top = (SUBLANES - i1 % SUBLANES) % SUBLANES
            for h in range(PEER_HEADS):
                cnt_row = pltpu.roll(cnt_ref[h, c, tile8, :], to_top, 0)[0:1]
                e1_row = pltpu.roll(e1_ref[h, c, tile8, :], to_top, 0)[0:1]
                cnt_b = pltpu.bitcast(jnp.broadcast_to(cnt_row, half), BF16)
                e1_b = pltpu.bitcast(jnp.broadcast_to(e1_row, half), BF16)
                sel = pltpu.bitcast(r2_ref[h, ks, cs], BF16) < cnt_b
                val = pltpu.bitcast(e2_ref[h, ks, cs], BF16) * e1_b
                gate = gate + jnp.where(sel, val, jnp.zeros(grp, BF16))
            ws = slice(r * N_KEYS + g * W_ROWS_PER_GROUP, r * N_KEYS + (g + 1) * W_ROWS_PER_GROUP)
            a = a_prv[ws, cs]
            act = a * (1.0 + lax.erf(a * (1.0 / math.sqrt(2.0))))
            w_prv[ws, cs] = gate * act.astype(BF16)

        for q in range(4):
            stage_a(q)
            for t in range(4):
                stage_v(4 * q + t)
                for gg in range(2):
                    stage_w(8 * q + 2 * t + gg)

    @pl.when(s % 2 == 0)
    def _():
        body(a0_ref, a1_ref, w0_ref, w1_ref)

    @pl.when(s % 2 == 1)
    def _():
        body(a1_ref, a0_ref, w1_ref, w0_ref)

    @pl.when(jv == n_j - 1)
    def _():
        o_ref[...] = acc_ref[...]


def _peer_dense(h1T_bf, u_bf, v_bf, cnt, r2, e1, e2, tm=512, te=512):
    D, T = h1T_bf.shape
    E = u_bf.shape[0]
    assert te == 4 * N_KEYS and tm == 4 * LANES and D == 2048, \
        "the stage interleave is written for 512 x 512 tiles"
    n_i, n_j = T // tm, E // te
    vT_bf = jnp.transpose(v_bf.reshape(n_j, te, D), (0, 2, 1))
    n_tiles = n_i * n_j
    kern = functools.partial(_peer_dense_kernel, te=te, tm=tm, n_j=n_j, n_tiles=n_tiles)
    ta = lambda s: jnp.minimum(s, n_tiles - 1)
    tw = lambda s: jnp.clip(s - 1, 0, n_tiles - 1)
    tv = lambda s: jnp.clip(s - 2, 0, n_tiles - 1)
    sspec = pl.BlockSpec((PEER_HEADS, N_KEYS // 2, tm), lambda s: (0, 0, tw(s) // n_j))
    xspec = pl.BlockSpec((PEER_HEADS, tm // LANES, N_KEYS, LANES), lambda s: (0, tw(s) // n_j, 0, 0))
    return pl.pallas_call(
        kern,
        grid=(n_tiles + 2,),
        in_specs=[pl.BlockSpec((D, tm), lambda s: (0, ta(s) // n_j)),
                  pl.BlockSpec((te, D), lambda s: (ta(s) % n_j, 0)),
                  pl.BlockSpec((1, D, te), lambda s: (tv(s) % n_j, 0, 0)),
                  xspec, sspec, xspec, sspec],
        out_specs=pl.BlockSpec((D, tm), lambda s: (0, tv(s) // n_j)),
        out_shape=jax.ShapeDtypeStruct((D, T), F32),
        scratch_shapes=[pltpu.VMEM((D, tm), F32),
                        pltpu.VMEM((te, tm), F32), pltpu.VMEM((te, tm), F32),
                        pltpu.VMEM((te, tm), BF16), pltpu.VMEM((te, tm), BF16)],
        compiler_params=_cparams(("arbitrary",)),
        name="peer_dense",
    )(h1T_bf, u_bf, vT_bf, cnt, r2, e1, e2)


def _res_ln_kernel(h_ref, ft_ref, g_ref, b_ref, o_ref, *, alpha):
    x = alpha * h_ref[...] + ft_ref[...].T
    mu = jnp.mean(x, -1, keepdims=True)
    xc = x - mu
    var = jnp.mean(xc * xc, -1, keepdims=True)
    o_ref[...] = xc * lax.rsqrt(var + 1e-5) * g_ref[...] + b_ref[...]


def _res_ln(h, fT, g, b, alpha, tm=512):
    T, D = h.shape
    row = pl.BlockSpec((tm, D), lambda i: (i, 0))
    vec = pl.BlockSpec((1, D), lambda i: (0, 0))
    return pl.pallas_call(
        functools.partial(_res_ln_kernel, alpha=alpha),
        grid=(T // tm,),
        in_specs=[row, pl.BlockSpec((D, tm), lambda i: (0, i)), vec, vec],
        out_specs=row,
        out_shape=jax.ShapeDtypeStruct((T, D), F32),
        compiler_params=_cparams(("parallel",)),
        name="res_ln2",
    )(h, fT, g.reshape(1, D), b.reshape(1, D))


def kernel(x, ln0_g, ln0_b, rel_bias, w_in, sink, conv_w, conv_b, f_w1, f_b1, f_freq1, f_w2, f_b2,
           f_freq2, f_w3, hy_bias, mix_norm_g, w_out, ln1_g, ln1_b, peer_wq, peer_subkeys, peer_u,
           peer_v, ln2_g, ln2_b):
    B, S, D = x.shape
    T = B * S
    alpha = (2.0 * DEPTH) ** 0.25
    consts = _dft_constants()

    qi = jnp.arange(BLOCK, dtype=jnp.int32)
    kj = jnp.arange(3 * BLOCK, dtype=jnp.int32)
    rel = kj[None, :] - BLOCK - qi[:, None]
    onehot = (_t5_bucket(rel)[..., None] == jnp.arange(N_BUCKETS, dtype=jnp.int32)).astype(F32)
    bias = jnp.einsum("qkb,bh->hqk", onehot, rel_bias.astype(F32), precision=lax.Precision.HIGHEST)

    h, h_bf = _ln0(x.reshape(T, D), ln0_g, ln0_b)
    for l in range(DEPTH):
        proj = _matmul(h_bf, w_in[l].astype(BF16), tm=512, tn=1536)
        attn = _attention(proj, bias, sink[l], B, S)

        hfilt = _filters(S, f_w1[l], f_b1[l], f_freq1[l], f_w2[l], f_b2[l], f_freq2[l], f_w3[l])
        kf = _filter_spectra(hfilt, consts)
        u = _short_conv(proj.reshape(B, S, -1), conv_w[l], conv_b[l], ATTN_WIDTH + 2 * KV_WIDTH)
        nct = HY_WIDTH // LANES
        z1 = _long_conv(u, 0, u, nct, kf, 0, hy_bias[l, 0], consts)
        hyo = _long_conv(z1, 0, u, 2 * nct, kf, 1, hy_bias[l, 1], consts)

        h1, h1T_bf = _mix_out(attn, hyo.reshape(T, HY_WIDTH), h, mix_norm_g[l], w_out[l].astype(BF16),
                              ln1_g[l], ln1_b[l], alpha)
        cnt, r2, e1, e2 = _peer_scores(peer_wq[l].T.astype(BF16), h1T_bf,
                                       peer_subkeys[l, 0].astype(BF16), peer_subkeys[l, 1].astype(BF16))
        ffnT = _peer_dense(h1T_bf, peer_u[l].astype(BF16), peer_v[l].astype(BF16), cnt, r2, e1, e2)
        h = _res_ln(h1, ffnT, ln2_g[l], ln2_b[l], alpha)
        if l + 1 < DEPTH:
            h_bf = h.astype(BF16)
    return h.reshape(B, S, D)
```

```python
import functools
import math

import numpy as np
import jax
import jax.numpy as jnp
from jax import lax
from jax.experimental import pallas as pl
from jax.experimental.pallas import tpu as pltpu

F32 = jnp.float32
BF16 = jnp.bfloat16

D_MODEL = 2048
HEAD_DIM = 128
N_Q_HEADS = 8
N_KV_HEADS = 2
GQA_GROUP = N_Q_HEADS // N_KV_HEADS
ATTN_WIDTH = N_Q_HEADS * HEAD_DIM
KV_WIDTH = N_KV_HEADS * HEAD_DIM
WINDOW = 128
BLOCK = 128
N_BUCKETS = 32
MAX_DISTANCE = 128
HY_WIDTH = D_MODEL - ATTN_WIDTH
HY_ORDER = 2
POS_BANDS = 16
POS_EMB = 1 + 2 * POS_BANDS
FILTER_HIDDEN = 64
FAST_DECAY_PCT = 0.3
SLOW_DECAY_PCT = 1.5
DECAY_TARGET = 1e-2
NORM_GROUP = 128
N_KEYS = 128
PEER_HEADS = 8
PEER_QDIM = 256
PEER_QHALF = PEER_QDIM // 2
PEER_TOPK = 16
NEG = -1e30
DEPTH = 1

LANES = 128
VMEM_LIMIT = 56 * 1024 * 1024

FFT_N1 = 64
FFT_N2 = 128
FFT_K1 = FFT_N1 // 2 + 1
FFT_K1_PAD = 40
FFT_SLOT = 2 * FFT_N2
KF_ROWS = FFT_K1 * FFT_SLOT
SUBLANES = 8
W_PITCH = FFT_SLOT + SUBLANES
X_PITCH = FFT_N2 + SUBLANES
W_ROWS = FFT_K1_PAD * W_PITCH
X_ROWS = (FFT_N1 // 2) * X_PITCH


def _cparams(sem, vmem=VMEM_LIMIT):
    return pltpu.CompilerParams(dimension_semantics=sem, vmem_limit_bytes=vmem)


def _ln0_kernel(x_ref, g_ref, b_ref, h_ref, hb_ref):
    x = x_ref[...]
    mu = jnp.mean(x, -1, keepdims=True)
    xc = x - mu
    var = jnp.mean(xc * xc, -1, keepdims=True)
    y = xc * lax.rsqrt(var + 1e-5) * g_ref[...] + b_ref[...]
    h_ref[...] = y
    hb_ref[...] = y.astype(BF16)


def _ln0(x2d, g, b, tm=512):
    T, D = x2d.shape
    return pl.pallas_call(
        _ln0_kernel,
        grid=(T // tm,),
        in_specs=[pl.BlockSpec((tm, D), lambda i: (i, 0)),
                  pl.BlockSpec((1, D), lambda i: (0, 0)),
                  pl.BlockSpec((1, D), lambda i: (0, 0))],
        out_specs=[pl.BlockSpec((tm, D), lambda i: (i, 0)),
                   pl.BlockSpec((tm, D), lambda i: (i, 0))],
        out_shape=[jax.ShapeDtypeStruct((T, D), F32), jax.ShapeDtypeStruct((T, D), BF16)],
        compiler_params=_cparams(("parallel",)),
        name="ln0",
    )(x2d, g.reshape(1, D), b.reshape(1, D))


def _mm_kernel(a_ref, b_ref, o_ref):
    o_ref[...] = jnp.dot(a_ref[...], b_ref[...], preferred_element_type=F32)


def _matmul(a, b, tm, tn):
    M, K = a.shape
    N = b.shape[1]
    return pl.pallas_call(
        _mm_kernel,
        grid=(N // tn, M // tm),
        in_specs=[pl.BlockSpec((tm, K), lambda j, i: (i, 0)),
                  pl.BlockSpec((K, tn), lambda j, i: (0, j))],
        out_specs=pl.BlockSpec((tm, tn), lambda j, i: (i, j)),
        out_shape=jax.ShapeDtypeStruct((M, N), F32),
        compiler_params=_cparams(("parallel", "parallel")),
        name="in_proj",
    )(a, b)


def _t5_bucket(rel):
    nb = N_BUCKETS // 2
    ret = (rel > 0).astype(jnp.int32) * nb
    n = jnp.abs(rel)
    max_exact = nb // 2
    nf = jnp.maximum(n, 1).astype(F32)
    large = max_exact + (jnp.log(nf / max_exact) / math.log(MAX_DISTANCE / max_exact)
                         * (nb - max_exact)).astype(jnp.int32)
    large = jnp.minimum(large, nb - 1)
    return ret + jnp.where(n < max_exact, n, large)


def _attn_kernel(sink_ref, q_ref, kp_ref, kc_ref, kn_ref, vp_ref, vc_ref, vn_ref, bias_ref, o_ref,
                 *, nb, seq):
    n = pl.program_id(0) % nb
    rows = GQA_GROUP * BLOCK
    row = lax.broadcasted_iota(jnp.int32, (rows, 3 * BLOCK), 0)
    kj = lax.broadcasted_iota(jnp.int32, (rows, 3 * BLOCK), 1)
    rel = kj - BLOCK - (row & (BLOCK - 1))
    kabs = n * BLOCK + kj - BLOCK
    valid = (jnp.abs(rel) <= WINDOW) & (kabs >= 0) & (kabs < seq)
    head_of_row = lax.broadcasted_iota(jnp.int32, (rows, 1), 0) // BLOCK
    k = jnp.concatenate([kp_ref[...], kc_ref[...], kn_ref[...]], axis=0).astype(BF16)
    v = jnp.concatenate([vp_ref[...], vc_ref[...], vn_ref[...]], axis=0).astype(BF16)
    scale = 1.0 / math.sqrt(HEAD_DIM)
    for g in range(N_KV_HEADS):
        kg = k[:, g * HEAD_DIM:(g + 1) * HEAD_DIM]
        vg = v[:, g * HEAD_DIM:(g + 1) * HEAD_DIM]
        h0 = g * GQA_GROUP
        qg = jnp.concatenate([q_ref[:, (h0 + r) * HEAD_DIM:(h0 + r + 1) * HEAD_DIM]
                              for r in range(GQA_GROUP)], axis=0).astype(BF16)
        s = lax.dot_general(qg, kg, (((1,), (1,)), ((), ())), preferred_element_type=F32) * scale
        bias_g = bias_ref[h0:h0 + GQA_GROUP].reshape(rows, 3 * BLOCK)
        s = jnp.where(valid, s + bias_g, NEG)
        sk = jnp.zeros((rows, 1), F32)
        for r in range(GQA_GROUP):
            sk = jnp.where(head_of_row == r, sink_ref[h0 + r], sk)
        m = jnp.maximum(jnp.max(s, -1, keepdims=True), sk)
        p = jnp.exp(s - m)
        denom = jnp.sum(p, -1, keepdims=True) + jnp.exp(sk - m)
        p = p / denom
        o = jnp.dot(p.astype(BF16), vg, preferred_element_type=F32)
        for r in range(GQA_GROUP):
            o_ref[:, (h0 + r) * HEAD_DIM:(h0 + r + 1) * HEAD_DIM] = o[r * BLOCK:(r + 1) * BLOCK]


def _attention(proj, bias, sink, batch, seq):
    T = proj.shape[0]
    nb = seq // BLOCK
    kcol = ATTN_WIDTH // KV_WIDTH
    vcol = kcol + 1

    def prev(i):
        return i - jnp.where(i % nb == 0, 0, 1)

    def nxt(i):
        return i + jnp.where(i % nb == nb - 1, 0, 1)

    kern = functools.partial(_attn_kernel, nb=nb, seq=seq)
    return pl.pallas_call(
        kern,
        grid=(T // BLOCK,),
        in_specs=[pl.BlockSpec(memory_space=pltpu.SMEM),
                  pl.BlockSpec((BLOCK, ATTN_WIDTH), lambda i: (i, 0)),
                  pl.BlockSpec((BLOCK, KV_WIDTH), lambda i: (prev(i), kcol)),
                  pl.BlockSpec((BLOCK, KV_WIDTH), lambda i: (i, kcol)),
                  pl.BlockSpec((BLOCK, KV_WIDTH), lambda i: (nxt(i), kcol)),
                  pl.BlockSpec((BLOCK, KV_WIDTH), lambda i: (prev(i), vcol)),
                  pl.BlockSpec((BLOCK, KV_WIDTH), lambda i: (i, vcol)),
                  pl.BlockSpec((BLOCK, KV_WIDTH), lambda i: (nxt(i), vcol)),
                  pl.BlockSpec((N_Q_HEADS, BLOCK, 3 * BLOCK), lambda i: (0, 0, 0))],
        out_specs=pl.BlockSpec((BLOCK, ATTN_WIDTH), lambda i: (i, 0)),
        out_shape=jax.ShapeDtypeStruct((T, ATTN_WIDTH), F32),
        compiler_params=_cparams(("parallel",)),
        name="window_attn",
    )(sink, proj, proj, proj, proj, proj, proj, proj, bias)


def _filter_kernel(z_ref, w1_ref, b1_ref, fr1_ref, w2_ref, b2_ref, fr2_ref, w3_ref, dl_ref, o_ref,
                   *, tt, seq):
    hi = lax.Precision.HIGHEST
    a = jnp.dot(z_ref[...], w1_ref[...], preferred_element_type=F32, precision=hi) + b1_ref[...]
    hid = jnp.sin(fr1_ref[...] * a)
    a = jnp.dot(hid, w2_ref[...], preferred_element_type=F32, precision=hi) + b2_ref[...]
    hid = jnp.sin(fr2_ref[...] * a)
    h = jnp.dot(hid.astype(BF16), w3_ref[0], preferred_element_type=F32)
    j = lax.broadcasted_iota(jnp.int32, h.shape, 0) + pl.program_id(0) * tt
    lag = jnp.where(j < seq, j, (2 * seq - j) & (seq - 1))
    tn = lag.astype(F32) / float(max(seq - 1, 1))
    o_ref[...] = h * jnp.exp(-tn * dl_ref[...])


def _filters(seq, w1, b1, fr1, w2, b2, fr2, w3, tt=512):
    assert seq & (seq - 1) == 0
    t = np.arange(seq, dtype=np.float32)
    tn = t / np.float32(max(seq - 1, 1))
    w = (np.float32(2.0 * math.pi) * t / np.float32(seq))[:, None]
    bands = np.linspace(1e-4, POS_BANDS - 1, POS_BANDS, dtype=np.float32)
    z = np.concatenate([tn[:, None], np.cos(w * bands), -np.sin(w * bands)], -1).astype(np.float32)
    zp = np.pad(z, ((0, 0), (0, FILTER_HIDDEN - POS_EMB)))
    zfull = jnp.asarray(np.concatenate([zp, zp[0:1], zp[1:][::-1]], axis=0))
    w1p = jnp.pad(w1, ((0, FILTER_HIDDEN - POS_EMB), (0, 0)))
    max_decay = math.log(DECAY_TARGET) / FAST_DECAY_PCT
    min_decay = math.log(DECAY_TARGET) / SLOW_DECAY_PCT
    deltas = jnp.abs(jnp.linspace(min_decay, max_decay, HY_WIDTH, dtype=F32))
    ncol = HY_ORDER * HY_WIDTH
    dl = jnp.tile(deltas, HY_ORDER).reshape(1, ncol)
    H = FILTER_HIDDEN
    w3s = jnp.transpose(w3.reshape(H, HY_ORDER, 2, HY_WIDTH), (2, 0, 1, 3)).reshape(2, H, ncol).astype(BF16)
    half = seq // tt
    kern = functools.partial(_filter_kernel, tt=tt, seq=seq)
    full = lambda r, c: pl.BlockSpec((r, c), lambda i: (0, 0))
    return pl.pallas_call(
        kern,
        grid=(2 * half,),
        in_specs=[pl.BlockSpec((tt, H), lambda i: (i, 0)),
                  full(H, H), full(1, H), full(1, H), full(H, H), full(1, H), full(1, H),
                  pl.BlockSpec((1, H, ncol), lambda i: (i // half, 0, 0)), full(1, ncol)],
        out_specs=pl.BlockSpec((tt, ncol), lambda i: (i, 0)),
        out_shape=jax.ShapeDtypeStruct((2 * seq, ncol), F32),
        compiler_params=_cparams(("parallel",)),
        name="hyena_filter_mlp",
    )(zfull, w1p, b1.reshape(1, H), fr1.reshape(1, H), w2, b2.reshape(1, H), fr2.reshape(1, H), w3s, dl)


def _sconv_kernel(x_ref, w_ref, b_ref, o_ref):
    x = x_ref[0]
    L = x.shape[0]
    rows = lax.broadcasted_iota(jnp.int32, x.shape, 0)
    xm = jnp.where(rows == 0, 0.0, pltpu.roll(x, 1, 0))
    xp = jnp.where(rows == L - 1, 0.0, pltpu.roll(x, L - 1, 0))
    w = w_ref[...]
    o_ref[0] = xm * w[0:1] + x * w[1:2] + xp * w[2:3] + b_ref[...]


def _short_conv(proj3, conv_w, conv_b, col0, ct=256):
    B, L, _ = proj3.shape
    C = conv_w.shape[1]
    off = col0 // ct
    return pl.pallas_call(
        _sconv_kernel,
        grid=(B, C // ct),
        in_specs=[pl.BlockSpec((1, L, ct), lambda b, c: (b, 0, c + off)),
                  pl.BlockSpec((3, ct), lambda b, c: (0, c)),
                  pl.BlockSpec((1, ct), lambda b, c: (0, c))],
        out_specs=pl.BlockSpec((1, L, ct), lambda b, c: (b, 0, c)),
        out_shape=jax.ShapeDtypeStruct((B, L, C), F32),
        compiler_params=_cparams(("parallel", "parallel")),
        name="hyena_short_conv",
    )(proj3, conv_w, conv_b.reshape(1, C))


def _dft_constants():
    n1h = FFT_N1 // 2
    k1 = np.arange(FFT_K1_PAD)[:, None].astype(np.float64)
    n1 = np.arange(FFT_N1)[None, :].astype(np.float64)
    ang = 2.0 * np.pi * k1 * n1 / FFT_N1
    live = (np.arange(FFT_K1_PAD) < FFT_K1)[:, None]
    f1_full = np.concatenate([np.where(live, np.cos(ang), 0.0), np.where(live, -np.sin(ang), 0.0)], 0)
    f1 = f1_full[:, :n1h]
    kk1 = np.arange(FFT_K1)[:, None, None].astype(np.float64)
    k2 = np.arange(FFT_N2)[None, :, None].astype(np.float64)
    n2 = np.arange(FFT_N2)[None, None, :].astype(np.float64)
    phi = 2.0 * np.pi * (n2 * k2 / FFT_N2 + n2 * kk1 / (FFT_N1 * FFT_N2))
    c, s = np.cos(phi), np.sin(phi)
    g = np.concatenate([np.concatenate([c, s], 2), np.concatenate([-s, c], 2)], 1)
    ct_, st_ = np.transpose(c, (0, 2, 1)), np.transpose(s, (0, 2, 1))
    ginv = np.concatenate([np.concatenate([ct_, -st_], 2), np.concatenate([st_, ct_], 2)], 1)
    wk = np.where((np.arange(FFT_K1_PAD) == 0) | (np.arange(FFT_K1_PAD) == FFT_N1 // 2), 1.0, 2.0)
    wk = np.where(np.arange(FFT_K1_PAD) < FFT_K1, wk, 0.0)[None, :] / (FFT_N1 * FFT_N2)
    angi = 2.0 * np.pi * np.arange(n1h)[:, None] * np.arange(FFT_K1_PAD)[None, :] / FFT_N1
    finv = np.concatenate([wk * np.cos(angi), -wk * np.sin(angi)], 1)
    as_bf = lambda a: jnp.asarray(a.astype(np.float32)).astype(BF16)
    return as_bf(f1), as_bf(g), as_bf(finv), as_bf(ginv), as_bf(f1_full)


def _pad_rows_in(src_ref, xp_ref, n_blocks=FFT_N1 // 2, zero_row_of_block=None):
    for n1 in range(n_blocks):
        blk = src_ref[pl.ds(n1 * FFT_N2, FFT_N2), :]
        if n1 == zero_row_of_block:
            rows = lax.broadcasted_iota(jnp.int32, blk.shape, 0)
            blk = jnp.where(rows == 0, 0.0, blk)
        xp_ref[pl.ds(n1 * X_PITCH, FFT_N2), :] = blk


def _fft_stage1(xp_ref, w_ref, f1_ref):
    n1h = f1_ref.shape[1]
    kp = FFT_K1_PAD

    def body(i, carry):
        n2 = 2 * i
        xa = xp_ref[pl.ds(n2, n1h, stride=X_PITCH), :]
        xb = xp_ref[pl.ds(n2 + 1, n1h, stride=X_PITCH), :]
        xs = jnp.concatenate([xa, xb], axis=1).astype(BF16)
        r = jnp.dot(f1_ref[...], xs, preferred_element_type=F32)
        w_ref[pl.ds(n2, kp, stride=W_PITCH), :] = r[0:kp, 0:LANES]
        w_ref[pl.ds(n2 + 1, kp, stride=W_PITCH), :] = r[0:kp, LANES:2 * LANES]
        w_ref[pl.ds(FFT_N2 + n2, kp, stride=W_PITCH), :] = r[kp:2 * kp, 0:LANES]
        w_ref[pl.ds(FFT_N2 + n2 + 1, kp, stride=W_PITCH), :] = r[kp:2 * kp, LANES:2 * LANES]
        return carry

    lax.fori_loop(0, FFT_N2 // 2, body, 0, unroll=8)


def _kf_kernel(k_ref, f1_ref, g_ref, kf_ref, w_ref, xp_ref):
    scale = 1.0 / (jnp.sum(jnp.abs(k_ref[...]), 0, keepdims=True) + 1e-6)
    _pad_rows_in(k_ref, xp_ref, n_blocks=FFT_N1, zero_row_of_block=FFT_N1 // 2)
    _fft_stage1(xp_ref, w_ref, f1_ref)
    out = kf_ref.at[0]

    def body(k1, carry):
        src = pl.multiple_of(k1 * W_PITCH, 8)
        dst = pl.multiple_of(k1 * FFT_SLOT, FFT_SLOT)
        s = jnp.dot(g_ref[k1], w_ref[pl.ds(src, FFT_SLOT), :].astype(BF16), preferred_element_type=F32)
        out[pl.ds(dst, FFT_SLOT), :] = s * scale
        return carry

    lax.fori_loop(0, FFT_K1, body, 0, unroll=3)


def _filter_spectra(kfull, consts):
    L2 = kfull.shape[0]
    C = HY_WIDTH
    nct = C // LANES
    _, g, _, _, f1_full = consts
    return pl.pallas_call(
        _kf_kernel,
        grid=(HY_ORDER * nct,),
        in_specs=[pl.BlockSpec((L2, LANES), lambda j: (0, j)),
                  pl.BlockSpec(f1_full.shape, lambda j: (0, 0)),
                  pl.BlockSpec(g.shape, lambda j: (0, 0, 0))],
        out_specs=pl.BlockSpec((1, KF_ROWS, LANES), lambda j: (j // nct, 0, j % nct)),
        out_shape=jax.ShapeDtypeStruct((HY_ORDER, KF_ROWS, C), F32),
        scratch_shapes=[pltpu.VMEM((W_ROWS, LANES), F32), pltpu.VMEM((2 * X_ROWS, LANES), F32)],
        compiler_params=_cparams(("parallel",)),
        name="hyena_filter_fft",
    )(kfull, f1_full, g)


CONV_GROUP = 3


def _conv_kernel(z_ref, gate_ref, kf_ref, d_ref, f1_ref, g_ref, finv_ref, ginv_ref, o_ref,
                 w_ref, xp_ref, y_ref):
    zsrc = z_ref.at[0]
    kf = kf_ref.at[0]
    _pad_rows_in(zsrc, xp_ref)
    _fft_stage1(xp_ref, w_ref, f1_ref)

    def forward(k1):
        base = pl.multiple_of(k1 * W_PITCH, 8)
        kbase = pl.multiple_of(k1 * FFT_SLOT, FFT_SLOT)
        a = w_ref[pl.ds(base, FFT_SLOT), :].astype(BF16)
        s = jnp.dot(g_ref[k1], a, preferred_element_type=F32)
        sr, si = s[0:FFT_N2], s[FFT_N2:FFT_SLOT]
        kr = kf[pl.ds(kbase, FFT_N2), :]
        ki = kf[pl.ds(kbase + FFT_N2, FFT_N2), :]
        return jnp.concatenate([sr * kr - si * ki, sr * ki + si * kr], axis=0).astype(BF16)

    def inverse(k1, y):
        base = pl.multiple_of(k1 * W_PITCH, 8)
        w_ref[pl.ds(base, FFT_SLOT), :] = jnp.dot(ginv_ref[k1], y, preferred_element_type=F32)

    def step(i, do_forward, do_inverse):
        prev = [y_ref[k] for k in range(CONV_GROUP)] if do_inverse else None
        new = [forward(i * CONV_GROUP + k) for k in range(CONV_GROUP)] if do_forward else None
        if do_inverse:
            for k in range(CONV_GROUP):
                inverse((i - 1) * CONV_GROUP + k, prev[k])
        if do_forward:
            for k in range(CONV_GROUP):
                y_ref[k] = new[k]

    n_groups = FFT_K1 // CONV_GROUP
    step(0, True, False)

    def body(i, carry):
        step(i, True, True)
        return carry

    lax.fori_loop(1, n_groups, body, 0)
    step(n_groups, False, True)

    n1h = FFT_N1 // 2
    kp = FFT_K1_PAD

    def body2(i, carry):
        n2 = 2 * i
        zr = jnp.concatenate([w_ref[pl.ds(n2, kp, stride=W_PITCH), :],
                              w_ref[pl.ds(n2 + 1, kp, stride=W_PITCH), :]], axis=1)
        zi = jnp.concatenate([w_ref[pl.ds(FFT_N2 + n2, kp, stride=W_PITCH), :],
                              w_ref[pl.ds(FFT_N2 + n2 + 1, kp, stride=W_PITCH), :]], axis=1)
        zz = jnp.concatenate([zr, zi], axis=0).astype(BF16)
        x = jnp.dot(finv_ref[...], zz, preferred_element_type=F32)
        xp_ref[pl.ds(n2, n1h, stride=X_PITCH), :] = x[:, 0:LANES]
        xp_ref[pl.ds(n2 + 1, n1h, stride=X_PITCH), :] = x[:, LANES:2 * LANES]
        return carry

    lax.fori_loop(0, FFT_N2 // 2, body2, 0, unroll=8)
    d = d_ref[...]
    for n1 in range(n1h):
        rs = pl.ds(n1 * FFT_N2, FFT_N2)
        y = xp_ref[pl.ds(n1 * X_PITCH, FFT_N2), :]
        o_ref[0, rs, :] = gate_ref[0, rs, :] * (y + z_ref[0, rs, :] * d)


def _long_conv(z_arr, z_off, gate_arr, gate_off, kf_all, order, d, consts):
    B, L, _ = z_arr.shape
    C = HY_WIDTH
    nct = C // LANES
    f1, g, finv, ginv, _ = consts
    cst2 = lambda a: pl.BlockSpec(a.shape, lambda c, b: (0, 0))
    cst3 = lambda a: pl.BlockSpec(a.shape, lambda c, b: (0, 0, 0))
    return pl.pallas_call(
        _conv_kernel,
        grid=(nct, B),
        in_specs=[pl.BlockSpec((1, L, LANES), lambda c, b: (b, 0, c + z_off)),
                  pl.BlockSpec((1, L, LANES), lambda c, b: (b, 0, c + gate_off)),
                  pl.BlockSpec((1, KF_ROWS, LANES), lambda c, b: (order, 0, c)),
                  pl.BlockSpec((1, LANES), lambda c, b: (0, c)),
                  cst2(f1), cst3(g), cst2(finv), cst3(ginv)],
        out_specs=pl.BlockSpec((1, L, LANES), lambda c, b: (b, 0, c)),
        out_shape=jax.ShapeDtypeStruct((B, L, C), F32),
        scratch_shapes=[pltpu.VMEM((W_ROWS, LANES), F32), pltpu.VMEM((X_ROWS, LANES), F32),
                        pltpu.VMEM((CONV_GROUP, FFT_SLOT, LANES), BF16)],
        compiler_params=_cparams(("parallel", "parallel")),
        name=f"hyena_long_conv{order}",
    )(z_arr, gate_arr, kf_all, d.reshape(1, C), f1, g, finv, ginv)


def _mix_kernel(attn_ref, hyo_ref, h0_ref, mg_ref, w_ref, g_ref, b_ref, h1_ref, h1t_ref, *, alpha):
    def gnorm(x, goff):
        parts = []
        for gi in range(x.shape[1] // NORM_GROUP):
            xg = x[:, gi * NORM_GROUP:(gi + 1) * NORM_GROUP]
            ms = jnp.mean(xg * xg, -1, keepdims=True)
            gg = mg_ref[:, goff + gi * NORM_GROUP: goff + (gi + 1) * NORM_GROUP]
            parts.append((xg * lax.rsqrt(ms + 1e-6) * gg).astype(BF16))
        return jnp.concatenate(parts, axis=1)

    a = gnorm(attn_ref[...], 0)
    y = gnorm(hyo_ref[...], ATTN_WIDTH)
    mix = (jnp.dot(a, w_ref[0:ATTN_WIDTH, :], preferred_element_type=F32)
           + jnp.dot(y, w_ref[ATTN_WIDTH:, :], preferred_element_type=F32))
    x = alpha * h0_ref[...] + mix
    mu = jnp.mean(x, -1, keepdims=True)
    xc = x - mu
    var = jnp.mean(xc * xc, -1, keepdims=True)
    h1 = xc * lax.rsqrt(var + 1e-5) * g_ref[...] + b_ref[...]
    h1_ref[...] = h1
    h1t_ref[...] = h1.T.astype(BF16)


def _mix_out(attn, hyo, h0, mix_g, w_out_bf, ln_g, ln_b, alpha, tm=256):
    T, D = h0.shape
    kern = functools.partial(_mix_kernel, alpha=alpha)
    return pl.pallas_call(
        kern,
        grid=(T // tm,),
        in_specs=[pl.BlockSpec((tm, ATTN_WIDTH), lambda i: (i, 0)),
                  pl.BlockSpec((tm, HY_WIDTH), lambda i: (i, 0)),
                  pl.BlockSpec((tm, D), lambda i: (i, 0)),
                  pl.BlockSpec((1, D), lambda i: (0, 0)),
                  pl.BlockSpec((D, D), lambda i: (0, 0)),
                  pl.BlockSpec((1, D), lambda i: (0, 0)),
                  pl.BlockSpec((1, D), lambda i: (0, 0))],
        out_specs=[pl.BlockSpec((tm, D), lambda i: (i, 0)), pl.BlockSpec((D, tm), lambda i: (0, i))],
        out_shape=[jax.ShapeDtypeStruct((T, D), F32), jax.ShapeDtypeStruct((D, T), BF16)],
        compiler_params=_cparams(("parallel",)),
        name="mix_out_ln1",
    )(attn, hyo, h0, mix_g.reshape(1, D), w_out_bf, ln_g.reshape(1, D), ln_b.reshape(1, D))


def _oddeven_sort_pairs(n):
    pairs = []

    def merge(lo, m, r):
        step = r * 2
        if step < m:
            merge(lo, m, step)
            merge(lo + r, m, step)
            pairs.extend((i, i + r) for i in range(lo + r, lo + m - r, step))
        else:
            pairs.append((lo, lo + r))

    def sort(lo, m):
        if m > 1:
            sort(lo, m // 2)
            sort(lo + m // 2, m // 2)
            merge(lo, m, 1)

    sort(0, n)
    return pairs


def _bitonic_merge_pairs(n):
    pairs, d = [], n // 2
    while d >= 1:
        pairs.extend((i, i + d) for i in range(n) if (i & d) == 0)
        d //= 2
    return pairs


def _apply_network(xs, pairs):
    xs = list(xs)
    for i, j in pairs:
        xs[i], xs[j] = jnp.maximum(xs[i], xs[j]), jnp.minimum(xs[i], xs[j])
    return xs


def _top16_replicated(xs):
    xs = _apply_network(xs, _oddeven_sort_pairs(PEER_TOPK))
    merge = _bitonic_merge_pairs(PEER_TOPK)
    for shift in (4, 2, 1):
        other = [pltpu.roll(x, shift, 0) for x in xs]
        xs = [jnp.maximum(xs[i], other[PEER_TOPK - 1 - i]) for i in range(PEER_TOPK)]
        xs = _apply_network(xs, merge)
    return xs


def _pair_sum_candidates(v1, v2):
    sub = lax.broadcasted_iota(jnp.int32, v1[0].shape, 0)

    def by_sublane(vs):
        out = vs[SUBLANES - 1]
        for j in range(SUBLANES - 2, -1, -1):
            out = jnp.where(sub == j, vs[j], out)
        return out

    ninf = -jnp.inf
    v2lo, v2hi = by_sublane(v2[:SUBLANES]), by_sublane(v2[SUBLANES:])
    v1lo, v1hi = by_sublane(v1[:SUBLANES]), by_sublane(v1[SUBLANES:])
    cands = [v1[0] + v2lo, v1[0] + v2hi, v1[1] + v2lo]
    for a in (2, 3, 4):
        cands.append(jnp.where(sub < PEER_TOPK // (a + 1), v1[a] + v2lo, ninf))
    cands.append(jnp.where(sub >= 5, v1lo + v2[0], ninf))
    cands.append(v1hi + v2[0])
    cands.append(jnp.where(sub >= 5, v1lo + v2[1], ninf))
    pad = jnp.full(v1[0].shape, ninf, F32)
    return cands + [pad] * (PEER_TOPK - len(cands))


def _peer_score_kernel(wq_ref, h_ref, sk1_ref, sk2_ref, cnt_ref, r2_ref, e1_ref, e2_ref, q_ref):
    q_ref[...] = jnp.dot(wq_ref[...], h_ref[...], preferred_element_type=F32)

    def head(h, carry):
        base = pl.multiple_of(h * PEER_QDIM, PEER_QDIM)
        q1 = q_ref[pl.ds(base, PEER_QHALF), :].astype(BF16)
        q2 = q_ref[pl.ds(base + PEER_QHALF, PEER_QHALF), :].astype(BF16)
        s1 = jnp.dot(sk1_ref[...], q1, preferred_element_type=F32)
        s2 = jnp.dot(sk2_ref[...], q2, preferred_element_type=F32)
        slabs = lambda s: [s[SUBLANES * i:SUBLANES * (i + 1)] for i in range(N_KEYS // SUBLANES)]
        v1 = _top16_replicated(slabs(s1))
        v2 = _top16_replicated(slabs(s2))
        sc = _top16_replicated(_pair_sum_candidates(v1, v2))
        zsum = jnp.zeros_like(sc[0])
        for j in range(PEER_TOPK):
            zsum = zsum + jnp.exp(sc[j] - sc[0])
        row = lambda x: x[0:1]
        th = row(sc[PEER_TOPK - 1])
        cnt = jnp.zeros_like(s1)
        r2 = jnp.zeros_like(s2)
        for b in range(PEER_TOPK):
            cnt = cnt + jnp.where(s1 + row(v2[b]) >= th, 1.0, 0.0)
            r2 = r2 + jnp.where(row(v2[b]) > s2, 1.0, 0.0)
        def twice(x):
            bits = pltpu.bitcast(x.astype(BF16).astype(F32), jnp.uint32)
            return bits | (bits >> 16)

        cnt_w = twice(cnt)
        e1_w = twice(0.5 * jnp.exp(s1 - row(v1[0])) / row(zsum))
        for cc in range(cnt_ref.shape[1]):
            cnt_ref[h, cc] = cnt_w[:, cc * LANES:(cc + 1) * LANES]
            e1_ref[h, cc] = e1_w[:, cc * LANES:(cc + 1) * LANES]
        r2_ref[h] = pltpu.bitcast(r2.astype(BF16), jnp.uint32)
        e2_ref[h] = pltpu.bitcast(jnp.exp(s2 - row(v2[0])).astype(BF16), jnp.uint32)
        return carry

    lax.fori_loop(0, PEER_HEADS, head, 0)


def _peer_scores(wqT_bf, h1T_bf, sk1_bf, sk2_bf, tl=256):
    D, T = h1T_bf.shape
    Q = wqT_bf.shape[0]
    big = jax.ShapeDtypeStruct((PEER_HEADS, T // LANES, N_KEYS, LANES), jnp.uint32)
    big16 = jax.ShapeDtypeStruct((PEER_HEADS, N_KEYS // 2, T), jnp.uint32)
    bspec = pl.BlockSpec((PEER_HEADS, tl // LANES, N_KEYS, LANES), lambda i: (0, i, 0, 0))
    pspec = pl.BlockSpec((PEER_HEADS, N_KEYS // 2, tl), lambda i: (0, 0, i))
    return pl.pallas_call(
        _peer_score_kernel,
        grid=(T // tl,),
        in_specs=[pl.BlockSpec((Q, D), lambda i: (0, 0)),
                  pl.BlockSpec((D, tl), lambda i: (0, i)),
                  pl.BlockSpec((N_KEYS, PEER_QHALF), lambda i: (0, 0)),
                  pl.BlockSpec((N_KEYS, PEER_QHALF), lambda i: (0, 0))],
        out_specs=[bspec, pspec, bspec, pspec],
        out_shape=[big, big16, big, big16],
        scratch_shapes=[pltpu.VMEM((Q, tl), F32)],
        compiler_params=_cparams(("parallel",)),
        name="peer_scores_topk",
    )(wqT_bf, h1T_bf, sk1_bf, sk2_bf)


W_ROWS_PER_GROUP = 64


def _peer_dense_kernel(hb_ref, u_ref, vt_ref, cnt_ref, r2_ref, e1_ref, e2_ref,
                       o_ref, acc_ref, a0_ref, a1_ref, w0_ref, w1_ref, *, te, tm, n_j, n_tiles):
    s = pl.program_id(0)
    sv = jnp.clip(s - 2, 0, n_tiles - 1)
    jv = sv % n_j
    jw = jnp.clip(s - 1, 0, n_tiles - 1) % n_j

    @pl.when(s == 0)
    def _():
        for ref in (a0_ref, a1_ref, w0_ref, w1_ref):
            ref[...] = jnp.zeros_like(ref)

    @pl.when(jv == 0)
    def _():
        acc_ref[...] = jnp.zeros_like(acc_ref)

    wide = 2 * LANES
    n_r, n_c = te // N_KEYS, tm // LANES
    d_model = acc_ref.shape[0]

    def body(a_cur, a_prv, w_cur, w_prv):
        def stage_a(q):
            ms = slice((q // 2) * wide, (q // 2 + 1) * wide)
            ls = slice((q % 2) * wide, (q % 2 + 1) * wide)
            a_cur[ms, ls] = jnp.dot(u_ref[ms, :].astype(BF16), hb_ref[:, ls], preferred_element_type=F32)

        def stage_v(p):
            fs = slice((p // 2) * wide, (p // 2 + 1) * wide)
            ls = slice((p % 2) * wide, (p % 2 + 1) * wide)
            acc_ref[fs, ls] += jnp.dot(vt_ref[0, fs, :], w_cur[:, ls], preferred_element_type=F32)

        n_g = N_KEYS // W_ROWS_PER_GROUP
        grp = (W_ROWS_PER_GROUP, LANES)

        def stage_w(idx):
            p, g = idx // n_g, idx % n_g
            r, c = p // n_c, p % n_c
            i1 = jw * n_r + r
            cs = slice(c * LANES, (c + 1) * LANES)
            ks = slice(g * W_ROWS_PER_GROUP // 2, (g + 1) * W_ROWS_PER_GROUP // 2)
            gate = jnp.zeros(grp, BF16)
            half = (W_ROWS_PER_GROUP // 2, LANES)
            tile8 = pl.ds(pl.multiple_of((i1 // SUBLANES) * SUBLANES, SUBLANES), SUBLANES)
            to_top = (SUBLANES - i1 % SUBLANES) % SUBLANES
            for h in range(PEER_HEADS):
                cnt_row = pltpu.roll(cnt_ref[h, c, tile8, :], to_top, 0)[0:1]
                e1_row = pltpu.roll(e1_ref[h, c, tile8, :], to_top, 0)[0:1]
                cnt_b = pltpu.bitcast(jnp.broadcast_to(cnt_row, half), BF16)
                e1_b = pltpu.bitcast(jnp.broadcast_to(e1_row, half), BF16)
                sel = pltpu.bitcast(r2_ref[h, ks, cs], BF16) < cnt_b
                val = pltpu.bitcast(e2_ref[h, ks, cs], BF16) * e1_b
                gate = gate + jnp.where(sel, val, jnp.zeros(grp, BF16))
            ws = slice(r * N_KEYS + g * W_ROWS_PER_GROUP, r * N_KEYS + (g + 1) * W_ROWS_PER_GROUP)
            a = a_prv[ws, cs]
            act = a * (1.0 + lax.erf(a * (1.0 / math.sqrt(2.0))))
            w_prv[ws, cs] = gate * act.astype(BF16)

        for q in range(4):
            stage_a(q)
            for t in range(4):
                stage_v(4 * q + t)
                for gg in range(2):
                    stage_w(8 * q + 2 * t + gg)

    @pl.when(s % 2 == 0)
    def _():
        body(a0_ref, a1_ref, w0_ref, w1_ref)

    @pl.when(s % 2 == 1)
    def _():
        body(a1_ref, a0_ref, w1_ref, w0_ref)

    @pl.when(jv == n_j - 1)
    def _():
        o_ref[...] = acc_ref[...]


def _peer_dense(h1T_bf, u_bf, v_bf, cnt, r2, e1, e2, tm=512, te=512):
    D, T = h1T_bf.shape
    E = u_bf.shape[0]
    assert te == 4 * N_KEYS and tm == 4 * LANES and D == 2048, \
        "the stage interleave is written for 512 x 512 tiles"
    n_i, n_j = T // tm, E // te
    vT_bf = jnp.transpose(v_bf.reshape(n_j, te, D), (0, 2, 1))
    n_tiles = n_i * n_j
    kern = functools.partial(_peer_dense_kernel, te=te, tm=tm, n_j=n_j, n_tiles=n_tiles)
    ta = lambda s: jnp.minimum(s, n_tiles - 1)
    tw = lambda s: jnp.clip(s - 1, 0, n_tiles - 1)
    tv = lambda s: jnp.clip(s - 2, 0, n_tiles - 1)
    sspec = pl.BlockSpec((PEER_HEADS, N_KEYS // 2, tm), lambda s: (0, 0, tw(s) // n_j))
    xspec = pl.BlockSpec((PEER_HEADS, tm // LANES, N_KEYS, LANES), lambda s: (0, tw(s) // n_j, 0, 0))
    return pl.pallas_call(
        kern,
        grid=(n_tiles + 2,),
        in_specs=[pl.BlockSpec((D, tm), lambda s: (0, ta(s) // n_j)),
                  pl.BlockSpec((te, D), lambda s: (ta(s) % n_j, 0)),
                  pl.BlockSpec((1, D, te), lambda s: (tv(s) % n_j, 0, 0)),
                  xspec, sspec, xspec, sspec],
        out_specs=pl.BlockSpec((D, tm), lambda s: (0, tv(s) // n_j)),
        out_shape=jax.ShapeDtypeStruct((D, T), F32),
        scratch_shapes=[pltpu.VMEM((D, tm), F32),
                        pltpu.VMEM((te, tm), F32), pltpu.VMEM((te, tm), F32),
                        pltpu.VMEM((te, tm), BF16), pltpu.VMEM((te, tm), BF16)],
        compiler_params=_cparams(("arbitrary",)),
        name="peer_dense",
    )(h1T_bf, u_bf, vT_bf, cnt, r2, e1, e2)


def _res_ln_kernel(h_ref, ft_ref, g_ref, b_ref, o_ref, *, alpha):
    x = alpha * h_ref[...] + ft_ref[...].T
    mu = jnp.mean(x, -1, keepdims=True)
    xc = x - mu
    var = jnp.mean(xc * xc, -1, keepdims=True)
    o_ref[...] = xc * lax.rsqrt(var + 1e-5) * g_ref[...] + b_ref[...]


def _res_ln(h, fT, g, b, alpha, tm=512):
    T, D = h.shape
    row = pl.BlockSpec((tm, D), lambda i: (i, 0))
    vec = pl.BlockSpec((1, D), lambda i: (0, 0))
    return pl.pallas_call(
        functools.partial(_res_ln_kernel, alpha=alpha),
        grid=(T // tm,),
        in_specs=[row, pl.BlockSpec((D, tm), lambda i: (0, i)), vec, vec],
        out_specs=row,
        out_shape=jax.ShapeDtypeStruct((T, D), F32),
        compiler_params=_cparams(("parallel",)),
        name="res_ln2",
    )(h, fT, g.reshape(1, D), b.reshape(1, D))


def kernel(x, ln0_g, ln0_b, rel_bias, w_in, sink, conv_w, conv_b, f_w1, f_b1, f_freq1, f_w2, f_b2,
           f_freq2, f_w3, hy_bias, mix_norm_g, w_out, ln1_g, ln1_b, peer_wq, peer_subkeys, peer_u,
           peer_v, ln2_g, ln2_b):
    B, S, D = x.shape
    T = B * S
    alpha = (2.0 * DEPTH) ** 0.25
    consts = _dft_constants()

    qi = jnp.arange(BLOCK, dtype=jnp.int32)
    kj = jnp.arange(3 * BLOCK, dtype=jnp.int32)
    rel = kj[None, :] - BLOCK - qi[:, None]
    onehot = (_t5_bucket(rel)[..., None] == jnp.arange(N_BUCKETS, dtype=jnp.int32)).astype(F32)
    bias = jnp.einsum("qkb,bh->hqk", onehot, rel_bias.astype(F32), precision=lax.Precision.HIGHEST)

    h, h_bf = _ln0(x.reshape(T, D), ln0_g, ln0_b)
    for l in range(DEPTH):
        proj = _matmul(h_bf, w_in[l].astype(BF16), tm=512, tn=1536)
        attn = _attention(proj, bias, sink[l], B, S)

        hfilt = _filters(S, f_w1[l], f_b1[l], f_freq1[l], f_w2[l], f_b2[l], f_freq2[l], f_w3[l])
        kf = _filter_spectra(hfilt, consts)
        u = _short_conv(proj.reshape(B, S, -1), conv_w[l], conv_b[l], ATTN_WIDTH + 2 * KV_WIDTH)
        nct = HY_WIDTH // LANES
        z1 = _long_conv(u, 0, u, nct, kf, 0, hy_bias[l, 0], consts)
        hyo = _long_conv(z1, 0, u, 2 * nct, kf, 1, hy_bias[l, 1], consts)

        h1, h1T_bf = _mix_out(attn, hyo.reshape(T, HY_WIDTH), h, mix_norm_g[l], w_out[l].astype(BF16),
                              ln1_g[l], ln1_b[l], alpha)
        cnt, r2, e1, e2 = _peer_scores(peer_wq[l].T.astype(BF16), h1T_bf,
                                       peer_subkeys[l, 0].astype(BF16), peer_subkeys[l, 1].astype(BF16))
        ffnT = _peer_dense(h1T_bf, peer_u[l], peer_v[l].astype(BF16), cnt, r2, e1, e2)
        h = _res_ln(h1, ffnT, ln2_g[l], ln2_b[l], alpha)
        if l + 1 < DEPTH:
            h_bf = h.astype(BF16)
    return h.reshape(B, S, D)
```

```python
import functools
import math

import numpy as np
import jax
import jax.numpy as jnp
from jax import lax
from jax.experimental import pallas as pl
from jax.experimental.pallas import tpu as pltpu

F32 = jnp.float32
BF16 = jnp.bfloat16

D_MODEL = 2048
HEAD_DIM = 128
N_Q_HEADS = 8
N_KV_HEADS = 2
GQA_GROUP = N_Q_HEADS // N_KV_HEADS
ATTN_WIDTH = N_Q_HEADS * HEAD_DIM
KV_WIDTH = N_KV_HEADS * HEAD_DIM
WINDOW = 128
BLOCK = 128
N_BUCKETS = 32
MAX_DISTANCE = 128
HY_WIDTH = D_MODEL - ATTN_WIDTH
HY_ORDER = 2
POS_BANDS = 16
POS_EMB = 1 + 2 * POS_BANDS
FILTER_HIDDEN = 64
FAST_DECAY_PCT = 0.3
SLOW_DECAY_PCT = 1.5
DECAY_TARGET = 1e-2
NORM_GROUP = 128
N_KEYS = 128
PEER_HEADS = 8
PEER_QDIM = 256
PEER_QHALF = PEER_QDIM // 2
PEER_TOPK = 16
NEG = -1e30
DEPTH = 1

LANES = 128
VMEM_LIMIT = 56 * 1024 * 1024

FFT_N1 = 64
FFT_N2 = 128
FFT_K1 = FFT_N1 // 2 + 1
FFT_K1_PAD = 40
FFT_SLOT = 2 * FFT_N2
KF_ROWS = FFT_K1 * FFT_SLOT
SUBLANES = 8
W_PITCH = FFT_SLOT + SUBLANES
X_PITCH = FFT_N2 + SUBLANES
W_ROWS = FFT_K1_PAD * W_PITCH
X_ROWS = (FFT_N1 // 2) * X_PITCH


def _cparams(sem, vmem=VMEM_LIMIT):
    return pltpu.CompilerParams(dimension_semantics=sem, vmem_limit_bytes=vmem)


def _ln0_kernel(x_ref, g_ref, b_ref, h_ref, hb_ref):
    x = x_ref[...]
    mu = jnp.mean(x, -1, keepdims=True)
    xc = x - mu
    var = jnp.mean(xc * xc, -1, keepdims=True)
    y = xc * lax.rsqrt(var + 1e-5) * g_ref[...] + b_ref[...]
    h_ref[...] = y
    hb_ref[...] = y.astype(BF16)


def _ln0(x2d, g, b, tm=512):
    T, D = x2d.shape
    return pl.pallas_call(
        _ln0_kernel,
        grid=(T // tm,),
        in_specs=[pl.BlockSpec((tm, D), lambda i: (i, 0)),
                  pl.BlockSpec((1, D), lambda i: (0, 0)),
                  pl.BlockSpec((1, D), lambda i: (0, 0))],
        out_specs=[pl.BlockSpec((tm, D), lambda i: (i, 0)),
                   pl.BlockSpec((tm, D), lambda i: (i, 0))],
        out_shape=[jax.ShapeDtypeStruct((T, D), F32), jax.ShapeDtypeStruct((T, D), BF16)],
        compiler_params=_cparams(("parallel",)),
        name="ln0",
    )(x2d, g.reshape(1, D), b.reshape(1, D))


def _mm_kernel(a_ref, b_ref, o_ref):
    o_ref[...] = jnp.dot(a_ref[...], b_ref[...], preferred_element_type=F32)


def _matmul(a, b, tm, tn):
    M, K = a.shape
    N = b.shape[1]
    return pl.pallas_call(
        _mm_kernel,
        grid=(N // tn, M // tm),
        in_specs=[pl.BlockSpec((tm, K), lambda j, i: (i, 0)),
                  pl.BlockSpec((K, tn), lambda j, i: (0, j))],
        out_specs=pl.BlockSpec((tm, tn), lambda j, i: (i, j)),
        out_shape=jax.ShapeDtypeStruct((M, N), F32),
        compiler_params=_cparams(("parallel", "parallel")),
        name="in_proj",
    )(a, b)


def _t5_bucket(rel):
    nb = N_BUCKETS // 2
    ret = (rel > 0).astype(jnp.int32) * nb
    n = jnp.abs(rel)
    max_exact = nb // 2
    nf = jnp.maximum(n, 1).astype(F32)
    large = max_exact + (jnp.log(nf / max_exact) / math.log(MAX_DISTANCE / max_exact)
                         * (nb - max_exact)).astype(jnp.int32)
    large = jnp.minimum(large, nb - 1)
    return ret + jnp.where(n < max_exact, n, large)


def _attn_kernel(sink_ref, q_ref, kp_ref, kc_ref, kn_ref, vp_ref, vc_ref, vn_ref, bias_ref, o_ref,
                 *, nb, seq):
    n = pl.program_id(0) % nb
    rows = GQA_GROUP * BLOCK
    row = lax.broadcasted_iota(jnp.int32, (rows, 3 * BLOCK), 0)
    kj = lax.broadcasted_iota(jnp.int32, (rows, 3 * BLOCK), 1)
    rel = kj - BLOCK - (row & (BLOCK - 1))
    kabs = n * BLOCK + kj - BLOCK
    valid = (jnp.abs(rel) <= WINDOW) & (kabs >= 0) & (kabs < seq)
    head_of_row = lax.broadcasted_iota(jnp.int32, (rows, 1), 0) // BLOCK
    k = jnp.concatenate([kp_ref[...], kc_ref[...], kn_ref[...]], axis=0).astype(BF16)
    v = jnp.concatenate([vp_ref[...], vc_ref[...], vn_ref[...]], axis=0).astype(BF16)
    scale = 1.0 / math.sqrt(HEAD_DIM)
    for g in range(N_KV_HEADS):
        kg = k[:, g * HEAD_DIM:(g + 1) * HEAD_DIM]
        vg = v[:, g * HEAD_DIM:(g + 1) * HEAD_DIM]
        h0 = g * GQA_GROUP
        qg = jnp.concatenate([q_ref[:, (h0 + r) * HEAD_DIM:(h0 + r + 1) * HEAD_DIM]
                              for r in range(GQA_GROUP)], axis=0).astype(BF16)
        s = lax.dot_general(qg, kg, (((1,), (1,)), ((), ())), preferred_element_type=F32) * scale
        bias_g = bias_ref[h0:h0 + GQA_GROUP].reshape(rows, 3 * BLOCK)
        s = jnp.where(valid, s + bias_g, NEG)
        sk = jnp.zeros((rows, 1), F32)
        for r in range(GQA_GROUP):
            sk = jnp.where(head_of_row == r, sink_ref[h0 + r], sk)
        m = jnp.maximum(jnp.max(s, -1, keepdims=True), sk)
        p = jnp.exp(s - m)
        denom = jnp.sum(p, -1, keepdims=True) + jnp.exp(sk - m)
        p = p / denom
        o = jnp.dot(p.astype(BF16), vg, preferred_element_type=F32)
        for r in range(GQA_GROUP):
            o_ref[:, (h0 + r) * HEAD_DIM:(h0 + r + 1) * HEAD_DIM] = o[r * BLOCK:(r + 1) * BLOCK]


def _attention(proj, bias, sink, batch, seq):
    T = proj.shape[0]
    nb = seq // BLOCK
    kcol = ATTN_WIDTH // KV_WIDTH
    vcol = kcol + 1

    def prev(i):
        return i - jnp.where(i % nb == 0, 0, 1)

    def nxt(i):
        return i + jnp.where(i % nb == nb - 1, 0, 1)

    kern = functools.partial(_attn_kernel, nb=nb, seq=seq)
    return pl.pallas_call(
        kern,
        grid=(T // BLOCK,),
        in_specs=[pl.BlockSpec(memory_space=pltpu.SMEM),
                  pl.BlockSpec((BLOCK, ATTN_WIDTH), lambda i: (i, 0)),
                  pl.BlockSpec((BLOCK, KV_WIDTH), lambda i: (prev(i), kcol)),
                  pl.BlockSpec((BLOCK, KV_WIDTH), lambda i: (i, kcol)),
                  pl.BlockSpec((BLOCK, KV_WIDTH), lambda i: (nxt(i), kcol)),
                  pl.BlockSpec((BLOCK, KV_WIDTH), lambda i: (prev(i), vcol)),
                  pl.BlockSpec((BLOCK, KV_WIDTH), lambda i: (i, vcol)),
                  pl.BlockSpec((BLOCK, KV_WIDTH), lambda i: (nxt(i), vcol)),
                  pl.BlockSpec((N_Q_HEADS, BLOCK, 3 * BLOCK), lambda i: (0, 0, 0))],
        out_specs=pl.BlockSpec((BLOCK, ATTN_WIDTH), lambda i: (i, 0)),
        out_shape=jax.ShapeDtypeStruct((T, ATTN_WIDTH), F32),
        compiler_params=_cparams(("parallel",)),
        name="window_attn",
    )(sink, proj, proj, proj, proj, proj, proj, proj, bias)


def _filter_kernel(z_ref, w1_ref, b1_ref, fr1_ref, w2_ref, b2_ref, fr2_ref, w3_ref, dl_ref, o_ref,
                   *, tt, seq):
    hi = lax.Precision.HIGHEST
    a = jnp.dot(z_ref[...], w1_ref[...], preferred_element_type=F32, precision=hi) + b1_ref[...]
    hid = jnp.sin(fr1_ref[...] * a)
    a = jnp.dot(hid, w2_ref[...], preferred_element_type=F32, precision=hi) + b2_ref[...]
    hid = jnp.sin(fr2_ref[...] * a)
    h = jnp.dot(hid.astype(BF16), w3_ref[0], preferred_element_type=F32)
    j = lax.broadcasted_iota(jnp.int32, h.shape, 0) + pl.program_id(0) * tt
    lag = jnp.where(j < seq, j, (2 * seq - j) & (seq - 1))
    tn = lag.astype(F32) / float(max(seq - 1, 1))
    o_ref[...] = h * jnp.exp(-tn * dl_ref[...])


def _filters(seq, w1, b1, fr1, w2, b2, fr2, w3, tt=512):
    assert seq & (seq - 1) == 0
    t = np.arange(seq, dtype=np.float32)
    tn = t / np.float32(max(seq - 1, 1))
    w = (np.float32(2.0 * math.pi) * t / np.float32(seq))[:, None]
    bands = np.linspace(1e-4, POS_BANDS - 1, POS_BANDS, dtype=np.float32)
    z = np.concatenate([tn[:, None], np.cos(w * bands), -np.sin(w * bands)], -1).astype(np.float32)
    zp = np.pad(z, ((0, 0), (0, FILTER_HIDDEN - POS_EMB)))
    zfull = jnp.asarray(np.concatenate([zp, zp[0:1], zp[1:][::-1]], axis=0))
    w1p = jnp.pad(w1, ((0, FILTER_HIDDEN - POS_EMB), (0, 0)))
    max_decay = math.log(DECAY_TARGET) / FAST_DECAY_PCT
    min_decay = math.log(DECAY_TARGET) / SLOW_DECAY_PCT
    deltas = jnp.abs(jnp.linspace(min_decay, max_decay, HY_WIDTH, dtype=F32))
    ncol = HY_ORDER * HY_WIDTH
    dl = jnp.tile(deltas, HY_ORDER).reshape(1, ncol)
    H = FILTER_HIDDEN
    w3s = jnp.transpose(w3.reshape(H, HY_ORDER, 2, HY_WIDTH), (2, 0, 1, 3)).reshape(2, H, ncol).astype(BF16)
    half = seq // tt
    kern = functools.partial(_filter_kernel, tt=tt, seq=seq)
    full = lambda r, c: pl.BlockSpec((r, c), lambda i: (0, 0))
    return pl.pallas_call(
        kern,
        grid=(2 * half,),
        in_specs=[pl.BlockSpec((tt, H), lambda i: (i, 0)),
                  full(H, H), full(1, H), full(1, H), full(H, H), full(1, H), full(1, H),
                  pl.BlockSpec((1, H, ncol), lambda i: (i // half, 0, 0)), full(1, ncol)],
        out_specs=pl.BlockSpec((tt, ncol), lambda i: (i, 0)),
        out_shape=jax.ShapeDtypeStruct((2 * seq, ncol), F32),
        compiler_params=_cparams(("parallel",)),
        name="hyena_filter_mlp",
    )(zfull, w1p, b1.reshape(1, H), fr1.reshape(1, H), w2, b2.reshape(1, H), fr2.reshape(1, H), w3s, dl)


def _sconv_kernel(x_ref, w_ref, b_ref, o_ref):
    x = x_ref[0]
    L = x.shape[0]
    rows = lax.broadcasted_iota(jnp.int32, x.shape, 0)
    xm = jnp.where(rows == 0, 0.0, pltpu.roll(x, 1, 0))
    xp = jnp.where(rows == L - 1, 0.0, pltpu.roll(x, L - 1, 0))
    w = w_ref[...]
    o_ref[0] = xm * w[0:1] + x * w[1:2] + xp * w[2:3] + b_ref[...]


def _short_conv(proj3, conv_w, conv_b, col0, ct=256):
    B, L, _ = proj3.shape
    C = conv_w.shape[1]
    off = col0 // ct
    return pl.pallas_call(
        _sconv_kernel,
        grid=(B, C // ct),
        in_specs=[pl.BlockSpec((1, L, ct), lambda b, c: (b, 0, c + off)),
                  pl.BlockSpec((3, ct), lambda b, c: (0, c)),
                  pl.BlockSpec((1, ct), lambda b, c: (0, c))],
        out_specs=pl.BlockSpec((1, L, ct), lambda b, c: (b, 0, c)),
        out_shape=jax.ShapeDtypeStruct((B, L, C), F32),
        compiler_params=_cparams(("parallel", "parallel")),
        name="hyena_short_conv",
    )(proj3, conv_w, conv_b.reshape(1, C))


def _dft_constants():
    n1h = FFT_N1 // 2
    k1 = np.arange(FFT_K1_PAD)[:, None].astype(np.float64)
    n1 = np.arange(FFT_N1)[None, :].astype(np.float64)
    ang = 2.0 * np.pi * k1 * n1 / FFT_N1
    live = (np.arange(FFT_K1_PAD) < FFT_K1)[:, None]
    f1_full = np.concatenate([np.where(live, np.cos(ang), 0.0), np.where(live, -np.sin(ang), 0.0)], 0)
    f1 = f1_full[:, :n1h]
    kk1 = np.arange(FFT_K1)[:, None, None].astype(np.float64)
    k2 = np.arange(FFT_N2)[None, :, None].astype(np.float64)
    n2 = np.arange(FFT_N2)[None, None, :].astype(np.float64)
    phi = 2.0 * np.pi * (n2 * k2 / FFT_N2 + n2 * kk1 / (FFT_N1 * FFT_N2))
    c, s = np.cos(phi), np.sin(phi)
    g = np.concatenate([np.concatenate([c, s], 2), np.concatenate([-s, c], 2)], 1)
    ct_, st_ = np.transpose(c, (0, 2, 1)), np.transpose(s, (0, 2, 1))
    ginv = np.concatenate([np.concatenate([ct_, -st_], 2), np.concatenate([st_, ct_], 2)], 1)
    wk = np.where((np.arange(FFT_K1_PAD) == 0) | (np.arange(FFT_K1_PAD) == FFT_N1 // 2), 1.0, 2.0)
    wk = np.where(np.arange(FFT_K1_PAD) < FFT_K1, wk, 0.0)[None, :] / (FFT_N1 * FFT_N2)
    angi = 2.0 * np.pi * np.arange(n1h)[:, None] * np.arange(FFT_K1_PAD)[None, :] / FFT_N1
    finv = np.concatenate([wk * np.cos(angi), -wk * np.sin(angi)], 1)
    as_bf = lambda a: jnp.asarray(a.astype(np.float32)).astype(BF16)
    return as_bf(f1), as_bf(g), as_bf(finv), as_bf(ginv), as_bf(f1_full)


def _pad_rows_in(src_ref, xp_ref, n_blocks=FFT_N1 // 2, zero_row_of_block=None):
    for n1 in range(n_blocks):
        blk = src_ref[pl.ds(n1 * FFT_N2, FFT_N2), :]
        if n1 == zero_row_of_block:
            rows = lax.broadcasted_iota(jnp.int32, blk.shape, 0)
            blk = jnp.where(rows == 0, 0.0, blk)
        xp_ref[pl.ds(n1 * X_PITCH, FFT_N2), :] = blk


def _fft_stage1(xp_ref, w_ref, f1_ref):
    n1h = f1_ref.shape[1]
    kp = FFT_K1_PAD

    def body(i, carry):
        n2 = 2 * i
        xa = xp_ref[pl.ds(n2, n1h, stride=X_PITCH), :]
        xb = xp_ref[pl.ds(n2 + 1, n1h, stride=X_PITCH), :]
        xs = jnp.concatenate([xa, xb], axis=1).astype(BF16)
        r = jnp.dot(f1_ref[...], xs, preferred_element_type=F32)
        w_ref[pl.ds(n2, kp, stride=W_PITCH), :] = r[0:kp, 0:LANES]
        w_ref[pl.ds(n2 + 1, kp, stride=W_PITCH), :] = r[0:kp, LANES:2 * LANES]
        w_ref[pl.ds(FFT_N2 + n2, kp, stride=W_PITCH), :] = r[kp:2 * kp, 0:LANES]
        w_ref[pl.ds(FFT_N2 + n2 + 1, kp, stride=W_PITCH), :] = r[kp:2 * kp, LANES:2 * LANES]
        return carry

    lax.fori_loop(0, FFT_N2 // 2, body, 0, unroll=8)


def _kf_kernel(k_ref, f1_ref, g_ref, kf_ref, w_ref, xp_ref):
    scale = 1.0 / (jnp.sum(jnp.abs(k_ref[...]), 0, keepdims=True) + 1e-6)
    _pad_rows_in(k_ref, xp_ref, n_blocks=FFT_N1, zero_row_of_block=FFT_N1 // 2)
    _fft_stage1(xp_ref, w_ref, f1_ref)
    out = kf_ref.at[0]

    def body(k1, carry):
        src = pl.multiple_of(k1 * W_PITCH, 8)
        dst = pl.multiple_of(k1 * FFT_SLOT, FFT_SLOT)
        s = jnp.dot(g_ref[k1], w_ref[pl.ds(src, FFT_SLOT), :].astype(BF16), preferred_element_type=F32)
        out[pl.ds(dst, FFT_SLOT), :] = s * scale
        return carry

    lax.fori_loop(0, FFT_K1, body, 0, unroll=3)


def _filter_spectra(kfull, consts):
    L2 = kfull.shape[0]
    C = HY_WIDTH
    nct = C // LANES
    _, g, _, _, f1_full = consts
    return pl.pallas_call(
        _kf_kernel,
        grid=(HY_ORDER * nct,),
        in_specs=[pl.BlockSpec((L2, LANES), lambda j: (0, j)),
                  pl.BlockSpec(f1_full.shape, lambda j: (0, 0)),
                  pl.BlockSpec(g.shape, lambda j: (0, 0, 0))],
        out_specs=pl.BlockSpec((1, KF_ROWS, LANES), lambda j: (j // nct, 0, j % nct)),
        out_shape=jax.ShapeDtypeStruct((HY_ORDER, KF_ROWS, C), F32),
        scratch_shapes=[pltpu.VMEM((W_ROWS, LANES), F32), pltpu.VMEM((2 * X_ROWS, LANES), F32)],
        compiler_params=_cparams(("parallel",)),
        name="hyena_filter_fft",
    )(kfull, f1_full, g)


CONV_GROUP = 3


def _conv_kernel(z_ref, gate_ref, kf_ref, d_ref, f1_ref, g_ref, finv_ref, ginv_ref, o_ref,
                 w_ref, xp_ref, y_ref):
    zsrc = z_ref.at[0]
    kf = kf_ref.at[0]
    _pad_rows_in(zsrc, xp_ref)
    _fft_stage1(xp_ref, w_ref, f1_ref)

    def forward(k1):
        base = pl.multiple_of(k1 * W_PITCH, 8)
        kbase = pl.multiple_of(k1 * FFT_SLOT, FFT_SLOT)
        a = w_ref[pl.ds(base, FFT_SLOT), :].astype(BF16)
        s = jnp.dot(g_ref[k1], a, preferred_element_type=F32)
        sr, si = s[0:FFT_N2], s[FFT_N2:FFT_SLOT]
        kr = kf[pl.ds(kbase, FFT_N2), :]
        ki = kf[pl.ds(kbase + FFT_N2, FFT_N2), :]
        return jnp.concatenate([sr * kr - si * ki, sr * ki + si * kr], axis=0).astype(BF16)

    def inverse(k1, y):
        base = pl.multiple_of(k1 * W_PITCH, 8)
        w_ref[pl.ds(base, FFT_SLOT), :] = jnp.dot(ginv_ref[k1], y, preferred_element_type=F32)

    def step(i, do_forward, do_inverse):
        prev = [y_ref[k] for k in range(CONV_GROUP)] if do_inverse else None
        new = [forward(i * CONV_GROUP + k) for k in range(CONV_GROUP)] if do_forward else None
        if do_inverse:
            for k in range(CONV_GROUP):
                inverse((i - 1) * CONV_GROUP + k, prev[k])
        if do_forward:
            for k in range(CONV_GROUP):
                y_ref[k] = new[k]

    n_groups = FFT_K1 // CONV_GROUP
    step(0, True, False)

    def body(i, carry):
        step(i, True, True)
        return carry

    lax.fori_loop(1, n_groups, body, 0)
    step(n_groups, False, True)

    n1h = FFT_N1 // 2
    kp = FFT_K1_PAD

    def body2(i, carry):
        n2 = 2 * i
        zr = jnp.concatenate([w_ref[pl.ds(n2, kp, stride=W_PITCH), :],
                              w_ref[pl.ds(n2 + 1, kp, stride=W_PITCH), :]], axis=1)
        zi = jnp.concatenate([w_ref[pl.ds(FFT_N2 + n2, kp, stride=W_PITCH), :],
                              w_ref[pl.ds(FFT_N2 + n2 + 1, kp, stride=W_PITCH), :]], axis=1)
        zz = jnp.concatenate([zr, zi], axis=0).astype(BF16)
        x = jnp.dot(finv_ref[...], zz, preferred_element_type=F32)
        xp_ref[pl.ds(n2, n1h, stride=X_PITCH), :] = x[:, 0:LANES]
        xp_ref[pl.ds(n2 + 1, n1h, stride=X_PITCH), :] = x[:, LANES:2 * LANES]
        return carry

    lax.fori_loop(0, FFT_N2 // 2, body2, 0, unroll=8)
    d = d_ref[...]
    for n1 in range(n1h):
        rs = pl.ds(n1 * FFT_N2, FFT_N2)
        y = xp_ref[pl.ds(n1 * X_PITCH, FFT_N2), :]
        o_ref[0, rs, :] = gate_ref[0, rs, :] * (y + z_ref[0, rs, :] * d)


def _long_conv(z_arr, z_off, gate_arr, gate_off, kf_all, order, d, consts):
    B, L, _ = z_arr.shape
    C = HY_WIDTH
    nct = C // LANES
    f1, g, finv, ginv, _ = consts
    cst2 = lambda a: pl.BlockSpec(a.shape, lambda c, b: (0, 0))
    cst3 = lambda a: pl.BlockSpec(a.shape, lambda c, b: (0, 0, 0))
    return pl.pallas_call(
        _conv_kernel,
        grid=(nct, B),
        in_specs=[pl.BlockSpec((1, L, LANES), lambda c, b: (b, 0, c + z_off)),
                  pl.BlockSpec((1, L, LANES), lambda c, b: (b, 0, c + gate_off)),
                  pl.BlockSpec((1, KF_ROWS, LANES), lambda c, b: (order, 0, c)),
                  pl.BlockSpec((1, LANES), lambda c, b: (0, c)),
                  cst2(f1), cst3(g), cst2(finv), cst3(ginv)],
        out_specs=pl.BlockSpec((1, L, LANES), lambda c, b: (b, 0, c)),
        out_shape=jax.ShapeDtypeStruct((B, L, C), F32),
        scratch_shapes=[pltpu.VMEM((W_ROWS, LANES), F32), pltpu.VMEM((X_ROWS, LANES), F32),
                        pltpu.VMEM((CONV_GROUP, FFT_SLOT, LANES), BF16)],
        compiler_params=_cparams(("parallel", "parallel")),
        name=f"hyena_long_conv{order}",
    )(z_arr, gate_arr, kf_all, d.reshape(1, C), f1, g, finv, ginv)


def _mix_kernel(attn_ref, hyo_ref, h0_ref, mg_ref, w_ref, g_ref, b_ref, h1_ref, h1t_ref, *, alpha):
    def gnorm(x, goff):
        parts = []
        for gi in range(x.shape[1] // NORM_GROUP):
            xg = x[:, gi * NORM_GROUP:(gi + 1) * NORM_GROUP]
            ms = jnp.mean(xg * xg, -1, keepdims=True)
            gg = mg_ref[:, goff + gi * NORM_GROUP: goff + (gi + 1) * NORM_GROUP]
            parts.append((xg * lax.rsqrt(ms + 1e-6) * gg).astype(BF16))
        return jnp.concatenate(parts, axis=1)

    a = gnorm(attn_ref[...], 0)
    y = gnorm(hyo_ref[...], ATTN_WIDTH)
    mix = (jnp.dot(a, w_ref[0:ATTN_WIDTH, :], preferred_element_type=F32)
           + jnp.dot(y, w_ref[ATTN_WIDTH:, :], preferred_element_type=F32))
    x = alpha * h0_ref[...] + mix
    mu = jnp.mean(x, -1, keepdims=True)
    xc = x - mu
    var = jnp.mean(xc * xc, -1, keepdims=True)
    h1 = xc * lax.rsqrt(var + 1e-5) * g_ref[...] + b_ref[...]
    h1_ref[...] = h1
    h1t_ref[...] = h1.T.astype(BF16)


def _mix_out(attn, hyo, h0, mix_g, w_out_bf, ln_g, ln_b, alpha, tm=512):
    T, D = h0.shape
    kern = functools.partial(_mix_kernel, alpha=alpha)
    return pl.pallas_call(
        kern,
        grid=(T // tm,),
        in_specs=[pl.BlockSpec((tm, ATTN_WIDTH), lambda i: (i, 0)),
                  pl.BlockSpec((tm, HY_WIDTH), lambda i: (i, 0)),
                  pl.BlockSpec((tm, D), lambda i: (i, 0)),
                  pl.BlockSpec((1, D), lambda i: (0, 0)),
                  pl.BlockSpec((D, D), lambda i: (0, 0)),
                  pl.BlockSpec((1, D), lambda i: (0, 0)),
                  pl.BlockSpec((1, D), lambda i: (0, 0))],
        out_specs=[pl.BlockSpec((tm, D), lambda i: (i, 0)), pl.BlockSpec((D, tm), lambda i: (0, i))],
        out_shape=[jax.ShapeDtypeStruct((T, D), F32), jax.ShapeDtypeStruct((D, T), BF16)],
        compiler_params=_cparams(("parallel",)),
        name="mix_out_ln1",
    )(attn, hyo, h0, mix_g.reshape(1, D), w_out_bf, ln_g.reshape(1, D), ln_b.reshape(1, D))


def _oddeven_sort_pairs(n):
    pairs = []

    def merge(lo, m, r):
        step = r * 2
        if step < m:
            merge(lo, m, step)
            merge(lo + r, m, step)
            pairs.extend((i, i + r) for i in range(lo + r, lo + m - r, step))
        else:
            pairs.append((lo, lo + r))

    def sort(lo, m):
        if m > 1:
            sort(lo, m // 2)
            sort(lo + m // 2, m // 2)
            merge(lo, m, 1)

    sort(0, n)
    return pairs


def _bitonic_merge_pairs(n):
    pairs, d = [], n // 2
    while d >= 1:
        pairs.extend((i, i + d) for i in range(n) if (i & d) == 0)
        d //= 2
    return pairs


def _apply_network(xs, pairs):
    xs = list(xs)
    for i, j in pairs:
        xs[i], xs[j] = jnp.maximum(xs[i], xs[j]), jnp.minimum(xs[i], xs[j])
    return xs


def _top16_replicated(xs):
    xs = _apply_network(xs, _oddeven_sort_pairs(PEER_TOPK))
    merge = _bitonic_merge_pairs(PEER_TOPK)
    for shift in (4, 2, 1):
        other = [pltpu.roll(x, shift, 0) for x in xs]
        xs = [jnp.maximum(xs[i], other[PEER_TOPK - 1 - i]) for i in range(PEER_TOPK)]
        xs = _apply_network(xs, merge)
    return xs


def _pair_sum_candidates(v1, v2):
    sub = lax.broadcasted_iota(jnp.int32, v1[0].shape, 0)

    def by_sublane(vs):
        out = vs[SUBLANES - 1]
        for j in range(SUBLANES - 2, -1, -1):
            out = jnp.where(sub == j, vs[j], out)
        return out

    ninf = -jnp.inf
    v2lo, v2hi = by_sublane(v2[:SUBLANES]), by_sublane(v2[SUBLANES:])
    v1lo, v1hi = by_sublane(v1[:SUBLANES]), by_sublane(v1[SUBLANES:])
    cands = [v1[0] + v2lo, v1[0] + v2hi, v1[1] + v2lo]
    for a in (2, 3, 4):
        cands.append(jnp.where(sub < PEER_TOPK // (a + 1), v1[a] + v2lo, ninf))
    cands.append(jnp.where(sub >= 5, v1lo + v2[0], ninf))
    cands.append(v1hi + v2[0])
    cands.append(jnp.where(sub >= 5, v1lo + v2[1], ninf))
    pad = jnp.full(v1[0].shape, ninf, F32)
    return cands + [pad] * (PEER_TOPK - len(cands))


def _peer_score_kernel(wq_ref, h_ref, sk1_ref, sk2_ref, cnt_ref, r2_ref, e1_ref, e2_ref, q_ref):
    q_ref[...] = jnp.dot(wq_ref[...], h_ref[...], preferred_element_type=F32)

    def head(h, carry):
        base = pl.multiple_of(h * PEER_QDIM, PEER_QDIM)
        q1 = q_ref[pl.ds(base, PEER_QHALF), :].astype(BF16)
        q2 = q_ref[pl.ds(base + PEER_QHALF, PEER_QHALF), :].astype(BF16)
        s1 = jnp.dot(sk1_ref[...], q1, preferred_element_type=F32)
        s2 = jnp.dot(sk2_ref[...], q2, preferred_element_type=F32)
        slabs = lambda s: [s[SUBLANES * i:SUBLANES * (i + 1)] for i in range(N_KEYS // SUBLANES)]
        v1 = _top16_replicated(slabs(s1))
        v2 = _top16_replicated(slabs(s2))
        sc = _top16_replicated(_pair_sum_candidates(v1, v2))
        zsum = jnp.zeros_like(sc[0])
        for j in range(PEER_TOPK):
            zsum = zsum + jnp.exp(sc[j] - sc[0])
        row = lambda x: x[0:1]
        th = row(sc[PEER_TOPK - 1])
        cnt = jnp.zeros_like(s1)
        r2 = jnp.zeros_like(s2)
        for b in range(PEER_TOPK):
            cnt = cnt + jnp.where(s1 + row(v2[b]) >= th, 1.0, 0.0)
            r2 = r2 + jnp.where(row(v2[b]) > s2, 1.0, 0.0)
        def twice(x):
            bits = pltpu.bitcast(x.astype(BF16).astype(F32), jnp.uint32)
            return bits | (bits >> 16)

        cnt_w = twice(cnt)
        e1_w = twice(0.5 * jnp.exp(s1 - row(v1[0])) / row(zsum))
        for cc in range(cnt_ref.shape[1]):
            cnt_ref[h, cc] = cnt_w[:, cc * LANES:(cc + 1) * LANES]
            e1_ref[h, cc] = e1_w[:, cc * LANES:(cc + 1) * LANES]
        r2_ref[h] = pltpu.bitcast(r2.astype(BF16), jnp.uint32)
        e2_ref[h] = pltpu.bitcast(jnp.exp(s2 - row(v2[0])).astype(BF16), jnp.uint32)
        return carry

    lax.fori_loop(0, PEER_HEADS, head, 0)


def _peer_scores(wqT_bf, h1T_bf, sk1_bf, sk2_bf, tl=256):
    D, T = h1T_bf.shape
    Q = wqT_bf.shape[0]
    big = jax.ShapeDtypeStruct((PEER_HEADS, T // LANES, N_KEYS, LANES), jnp.uint32)
    big16 = jax.ShapeDtypeStruct((PEER_HEADS, N_KEYS // 2, T), jnp.uint32)
    bspec = pl.BlockSpec((PEER_HEADS, tl // LANES, N_KEYS, LANES), lambda i: (0, i, 0, 0))
    pspec = pl.BlockSpec((PEER_HEADS, N_KEYS // 2, tl), lambda i: (0, 0, i))
    return pl.pallas_call(
        _peer_score_kernel,
        grid=(T // tl,),
        in_specs=[pl.BlockSpec((Q, D), lambda i: (0, 0)),
                  pl.BlockSpec((D, tl), lambda i: (0, i)),
                  pl.BlockSpec((N_KEYS, PEER_QHALF), lambda i: (0, 0)),
                  pl.BlockSpec((N_KEYS, PEER_QHALF), lambda i: (0, 0))],
        out_specs=[bspec, pspec, bspec, pspec],
        out_shape=[big, big16, big, big16],
        scratch_shapes=[pltpu.VMEM((Q, tl), F32)],
        compiler_params=_cparams(("parallel",)),
        name="peer_scores_topk",
    )(wqT_bf, h1T_bf, sk1_bf, sk2_bf)


W_ROWS_PER_GROUP = 64


PEER_HALF = 4 * N_KEYS


def _peer_dense_kernel(hb_ref, u_ref, vt_ref, cnt_ref, r2_ref, e1_ref, e2_ref, o_ref, acc_ref,
                       a00, a01, a10, a11, w00, w01, w10, w11, *, te, tm, n_j, n_tiles):
    s = pl.program_id(0)
    sv = jnp.clip(s - 2, 0, n_tiles - 1)
    jv = sv % n_j
    jw = jnp.clip(s - 1, 0, n_tiles - 1) % n_j
    a_bufs = ((a00, a01), (a10, a11))
    w_bufs = ((w00, w01), (w10, w11))

    @pl.when(s == 0)
    def _():
        for ref in (a00, a01, a10, a11, w00, w01, w10, w11):
            ref[...] = jnp.zeros_like(ref)

    @pl.when(jv == 0)
    def _():
        acc_ref[...] = jnp.zeros_like(acc_ref)

    wide = 2 * LANES
    n_r, n_c = PEER_HALF // N_KEYS, tm // LANES
    key0 = pl.multiple_of(jw * (te // N_KEYS), SUBLANES)

    def half_body(half, a_cur, a_prv, w_cur, w_prv):
        def stage_a(q):
            ms = slice(half * PEER_HALF + (q // 2) * wide, half * PEER_HALF + (q // 2 + 1) * wide)
            ls = slice((q % 2) * wide, (q % 2 + 1) * wide)
            a_cur[(q // 2) * wide:(q // 2 + 1) * wide, ls] = jnp.dot(
                u_ref[ms, :], hb_ref[:, ls], preferred_element_type=F32)

        def stage_v(p):
            fs = slice((p // 2) * wide, (p // 2 + 1) * wide)
            ls = slice((p % 2) * wide, (p % 2 + 1) * wide)
            es = slice(half * PEER_HALF, (half + 1) * PEER_HALF)
            acc_ref[fs, ls] += jnp.dot(vt_ref[0, fs, es], w_cur[:, ls], preferred_element_type=F32)

        n_g = N_KEYS // W_ROWS_PER_GROUP
        grp = (W_ROWS_PER_GROUP, LANES)

        def stage_w(idx):
            p, g = idx // n_g, idx % n_g
            r, c = p // n_c, p % n_c
            krow = half * n_r + r
            cs = slice(c * LANES, (c + 1) * LANES)
            ks = slice(g * W_ROWS_PER_GROUP // 2, (g + 1) * W_ROWS_PER_GROUP // 2)
            gate = jnp.zeros(grp, BF16)
            pair = (W_ROWS_PER_GROUP // 2, LANES)
            for h in range(PEER_HEADS):
                cnt_row = cnt_ref[h, c, pl.ds(key0, SUBLANES), :][krow:krow + 1]
                e1_row = e1_ref[h, c, pl.ds(key0, SUBLANES), :][krow:krow + 1]
                cnt_b = pltpu.bitcast(jnp.broadcast_to(cnt_row, pair), BF16)
                e1_b = pltpu.bitcast(jnp.broadcast_to(e1_row, pair), BF16)
                sel = pltpu.bitcast(r2_ref[h, ks, cs], BF16) < cnt_b
                val = pltpu.bitcast(e2_ref[h, ks, cs], BF16) * e1_b
                gate = gate + jnp.where(sel, val, jnp.zeros(grp, BF16))
            ws = slice(r * N_KEYS + g * W_ROWS_PER_GROUP, r * N_KEYS + (g + 1) * W_ROWS_PER_GROUP)
            a = a_prv[ws, cs]
            act = a * (1.0 + lax.erf(a * (1.0 / math.sqrt(2.0))))
            w_prv[ws, cs] = gate * act.astype(BF16)

        for q in range(4):
            stage_a(q)
            for t in range(4):
                stage_v(4 * q + t)
                for gg in range(2):
                    stage_w(8 * q + 2 * t + gg)

    for par in range(2):
        @pl.when(s % 2 == par)
        def _():
            for half in range(2):
                half_body(half, a_bufs[par][half], a_bufs[1 - par][half],
                          w_bufs[par][half], w_bufs[1 - par][half])

    @pl.when(jv == n_j - 1)
    def _():
        o_ref[...] = acc_ref[...]


def _peer_dense(h1T_bf, u_bf, v_bf, cnt, r2, e1, e2, tm=512, te=2 * PEER_HALF):
    D, T = h1T_bf.shape
    E = u_bf.shape[0]
    assert te == SUBLANES * N_KEYS and tm == 4 * LANES and D == 2048, \
        "the stage interleave is written for two 512-expert halves x 512 tokens"
    n_i, n_j = T // tm, E // te
    vT_bf = jnp.transpose(v_bf.reshape(n_j, te, D), (0, 2, 1))
    n_tiles = n_i * n_j
    kern = functools.partial(_peer_dense_kernel, te=te, tm=tm, n_j=n_j, n_tiles=n_tiles)
    ta = lambda s: jnp.minimum(s, n_tiles - 1)
    tw = lambda s: jnp.clip(s - 1, 0, n_tiles - 1)
    tv = lambda s: jnp.clip(s - 2, 0, n_tiles - 1)
    sspec = pl.BlockSpec((PEER_HEADS, N_KEYS // 2, tm), lambda s: (0, 0, tw(s) // n_j))
    xspec = pl.BlockSpec((PEER_HEADS, tm // LANES, N_KEYS, LANES), lambda s: (0, tw(s) // n_j, 0, 0))
    return pl.pallas_call(
        kern,
        grid=(n_tiles + 2,),
        in_specs=[pl.BlockSpec((D, tm), lambda s: (0, ta(s) // n_j)),
                  pl.BlockSpec((te, D), lambda s: (ta(s) % n_j, 0)),
                  pl.BlockSpec((1, D, te), lambda s: (tv(s) % n_j, 0, 0)),
                  xspec, sspec, xspec, sspec],
        out_specs=pl.BlockSpec((D, tm), lambda s: (0, tv(s) // n_j)),
        out_shape=jax.ShapeDtypeStruct((D, T), F32),
        scratch_shapes=([pltpu.VMEM((D, tm), F32)]
                        + [pltpu.VMEM((PEER_HALF, tm), F32)] * 4
                        + [pltpu.VMEM((PEER_HALF, tm), BF16)] * 4),
        compiler_params=_cparams(("arbitrary",)),
        name="peer_dense",
    )(h1T_bf, u_bf, vT_bf, cnt, r2, e1, e2)


def _res_ln_kernel(h_ref, ft_ref, g_ref, b_ref, o_ref, *, alpha):
    x = alpha * h_ref[...] + ft_ref[...].T
    mu = jnp.mean(x, -1, keepdims=True)
    xc = x - mu
    var = jnp.mean(xc * xc, -1, keepdims=True)
    o_ref[...] = xc * lax.rsqrt(var + 1e-5) * g_ref[...] + b_ref[...]


def _res_ln(h, fT, g, b, alpha, tm=512):
    T, D = h.shape
    row = pl.BlockSpec((tm, D), lambda i: (i, 0))
    vec = pl.BlockSpec((1, D), lambda i: (0, 0))
    return pl.pallas_call(
        functools.partial(_res_ln_kernel, alpha=alpha),
        grid=(T // tm,),
        in_specs=[row, pl.BlockSpec((D, tm), lambda i: (0, i)), vec, vec],
        out_specs=row,
        out_shape=jax.ShapeDtypeStruct((T, D), F32),
        compiler_params=_cparams(("parallel",)),
        name="res_ln2",
    )(h, fT, g.reshape(1, D), b.reshape(1, D))


def kernel(x, ln0_g, ln0_b, rel_bias, w_in, sink, conv_w, conv_b, f_w1, f_b1, f_freq1, f_w2, f_b2,
           f_freq2, f_w3, hy_bias, mix_norm_g, w_out, ln1_g, ln1_b, peer_wq, peer_subkeys, peer_u,
           peer_v, ln2_g, ln2_b):
    B, S, D = x.shape
    T = B * S
    alpha = (2.0 * DEPTH) ** 0.25
    consts = _dft_constants()

    qi = jnp.arange(BLOCK, dtype=jnp.int32)
    kj = jnp.arange(3 * BLOCK, dtype=jnp.int32)
    rel = kj[None, :] - BLOCK - qi[:, None]
    onehot = (_t5_bucket(rel)[..., None] == jnp.arange(N_BUCKETS, dtype=jnp.int32)).astype(F32)
    bias = jnp.einsum("qkb,bh->hqk", onehot, rel_bias.astype(F32), precision=lax.Precision.HIGHEST)

    h, h_bf = _ln0(x.reshape(T, D), ln0_g, ln0_b)
    for l in range(DEPTH):
        proj = _matmul(h_bf, w_in[l].astype(BF16), tm=1024, tn=1536)
        attn = _attention(proj, bias, sink[l], B, S)

        hfilt = _filters(S, f_w1[l], f_b1[l], f_freq1[l], f_w2[l], f_b2[l], f_freq2[l], f_w3[l])
        kf = _filter_spectra(hfilt, consts)
        u = _short_conv(proj.reshape(B, S, -1), conv_w[l], conv_b[l], ATTN_WIDTH + 2 * KV_WIDTH)
        nct = HY_WIDTH // LANES
        z1 = _long_conv(u, 0, u, nct, kf, 0, hy_bias[l, 0], consts)
        hyo = _long_conv(z1, 0, u, 2 * nct, kf, 1, hy_bias[l, 1], consts)

        h1, h1T_bf = _mix_out(attn, hyo.reshape(T, HY_WIDTH), h, mix_norm_g[l], w_out[l].astype(BF16),
                              ln1_g[l], ln1_b[l], alpha)
        cnt, r2, e1, e2 = _peer_scores(peer_wq[l].T.astype(BF16), h1T_bf,
                                       peer_subkeys[l, 0].astype(BF16), peer_subkeys[l, 1].astype(BF16))
        ffnT = _peer_dense(h1T_bf, peer_u[l].astype(BF16), peer_v[l].astype(BF16), cnt, r2, e1, e2)
        h = _res_ln(h1, ffnT, ln2_g[l], ln2_b[l], alpha)
        if l + 1 < DEPTH:
            h_bf = h.astype(BF16)
    return h.reshape(B, S, D)
```

```python
import functools
import math

import numpy as np
import jax
import jax.numpy as jnp
from jax import lax
from jax.experimental import pallas as pl
from jax.experimental.pallas import tpu as pltpu

F32 = jnp.float32
BF16 = jnp.bfloat16

D_MODEL = 2048
HEAD_DIM = 128
N_Q_HEADS = 8
N_KV_HEADS = 2
GQA_GROUP = N_Q_HEADS // N_KV_HEADS
ATTN_WIDTH = N_Q_HEADS * HEAD_DIM
KV_WIDTH = N_KV_HEADS * HEAD_DIM
WINDOW = 128
BLOCK = 128
N_BUCKETS = 32
MAX_DISTANCE = 128
HY_WIDTH = D_MODEL - ATTN_WIDTH
HY_ORDER = 2
POS_BANDS = 16
POS_EMB = 1 + 2 * POS_BANDS
FILTER_HIDDEN = 64
FAST_DECAY_PCT = 0.3
SLOW_DECAY_PCT = 1.5
DECAY_TARGET = 1e-2
NORM_GROUP = 128
N_KEYS = 128
PEER_HEADS = 8
PEER_QDIM = 256
PEER_QHALF = PEER_QDIM // 2
PEER_TOPK = 16
NEG = -1e30
DEPTH = 1

LANES = 128
VMEM_LIMIT = 56 * 1024 * 1024

FFT_N1 = 64
FFT_N2 = 128
FFT_K1 = FFT_N1 // 2 + 1
FFT_K1_PAD = 40
FFT_SLOT = 2 * FFT_N2
KF_ROWS = FFT_K1 * FFT_SLOT
SUBLANES = 8
W_PITCH = FFT_SLOT + SUBLANES
X_PITCH = FFT_N2 + SUBLANES
W_ROWS = FFT_K1_PAD * W_PITCH
X_ROWS = (FFT_N1 // 2) * X_PITCH


def _cparams(sem, vmem=VMEM_LIMIT):
    return pltpu.CompilerParams(dimension_semantics=sem, vmem_limit_bytes=vmem)


def _ln0_kernel(x_ref, g_ref, b_ref, h_ref, hb_ref):
    x = x_ref[...]
    mu = jnp.mean(x, -1, keepdims=True)
    xc = x - mu
    var = jnp.mean(xc * xc, -1, keepdims=True)
    y = xc * lax.rsqrt(var + 1e-5) * g_ref[...] + b_ref[...]
    h_ref[...] = y
    hb_ref[...] = y.astype(BF16)


def _ln0(x2d, g, b, tm=512):
    T, D = x2d.shape
    return pl.pallas_call(
        _ln0_kernel,
        grid=(T // tm,),
        in_specs=[pl.BlockSpec((tm, D), lambda i: (i, 0)),
                  pl.BlockSpec((1, D), lambda i: (0, 0)),
                  pl.BlockSpec((1, D), lambda i: (0, 0))],
        out_specs=[pl.BlockSpec((tm, D), lambda i: (i, 0)),
                   pl.BlockSpec((tm, D), lambda i: (i, 0))],
        out_shape=[jax.ShapeDtypeStruct((T, D), F32), jax.ShapeDtypeStruct((T, D), BF16)],
        compiler_params=_cparams(("parallel",)),
        name="ln0",
    )(x2d, g.reshape(1, D), b.reshape(1, D))


def _mm_kernel(a_ref, b_ref, o_ref):
    o_ref[...] = jnp.dot(a_ref[...], b_ref[...], preferred_element_type=F32)


def _matmul(a, b, tm, tn):
    M, K = a.shape
    N = b.shape[1]
    return pl.pallas_call(
        _mm_kernel,
        grid=(N // tn, M // tm),
        in_specs=[pl.BlockSpec((tm, K), lambda j, i: (i, 0)),
                  pl.BlockSpec((K, tn), lambda j, i: (0, j))],
        out_specs=pl.BlockSpec((tm, tn), lambda j, i: (i, j)),
        out_shape=jax.ShapeDtypeStruct((M, N), F32),
        compiler_params=_cparams(("parallel", "parallel")),
        name="in_proj",
    )(a, b)


def _t5_bucket(rel):
    nb = N_BUCKETS // 2
    ret = (rel > 0).astype(jnp.int32) * nb
    n = jnp.abs(rel)
    max_exact = nb // 2
    nf = jnp.maximum(n, 1).astype(F32)
    large = max_exact + (jnp.log(nf / max_exact) / math.log(MAX_DISTANCE / max_exact)
                         * (nb - max_exact)).astype(jnp.int32)
    large = jnp.minimum(large, nb - 1)
    return ret + jnp.where(n < max_exact, n, large)


def _attn_kernel(sink_ref, q_ref, kp_ref, kc_ref, kn_ref, vp_ref, vc_ref, vn_ref, bias_ref, o_ref,
                 *, nb, seq):
    n = pl.program_id(0) % nb
    rows = GQA_GROUP * BLOCK
    row = lax.broadcasted_iota(jnp.int32, (rows, 3 * BLOCK), 0)
    kj = lax.broadcasted_iota(jnp.int32, (rows, 3 * BLOCK), 1)
    rel = kj - BLOCK - (row & (BLOCK - 1))
    kabs = n * BLOCK + kj - BLOCK
    valid = (jnp.abs(rel) <= WINDOW) & (kabs >= 0) & (kabs < seq)
    head_of_row = lax.broadcasted_iota(jnp.int32, (rows, 1), 0) // BLOCK
    k = jnp.concatenate([kp_ref[...], kc_ref[...], kn_ref[...]], axis=0).astype(BF16)
    v = jnp.concatenate([vp_ref[...], vc_ref[...], vn_ref[...]], axis=0).astype(BF16)
    scale = 1.0 / math.sqrt(HEAD_DIM)
    for g in range(N_KV_HEADS):
        kg = k[:, g * HEAD_DIM:(g + 1) * HEAD_DIM]
        vg = v[:, g * HEAD_DIM:(g + 1) * HEAD_DIM]
        h0 = g * GQA_GROUP
        qg = jnp.concatenate([q_ref[:, (h0 + r) * HEAD_DIM:(h0 + r + 1) * HEAD_DIM]
                              for r in range(GQA_GROUP)], axis=0).astype(BF16)
        s = lax.dot_general(qg, kg, (((1,), (1,)), ((), ())), preferred_element_type=F32) * scale
        bias_g = bias_ref[h0:h0 + GQA_GROUP].reshape(rows, 3 * BLOCK)
        s = jnp.where(valid, s + bias_g, NEG)
        sk = jnp.zeros((rows, 1), F32)
        for r in range(GQA_GROUP):
            sk = jnp.where(head_of_row == r, sink_ref[h0 + r], sk)
        m = jnp.maximum(jnp.max(s, -1, keepdims=True), sk)
        p = jnp.exp(s - m)
        denom = jnp.sum(p, -1, keepdims=True) + jnp.exp(sk - m)
        p = p / denom
        o = jnp.dot(p.astype(BF16), vg, preferred_element_type=F32)
        for r in range(GQA_GROUP):
            o_ref[:, (h0 + r) * HEAD_DIM:(h0 + r + 1) * HEAD_DIM] = o[r * BLOCK:(r + 1) * BLOCK]


def _attention(proj, bias, sink, batch, seq):
    T = proj.shape[0]
    nb = seq // BLOCK
    kcol = ATTN_WIDTH // KV_WIDTH
    vcol = kcol + 1

    def prev(i):
        return i - jnp.where(i % nb == 0, 0, 1)

    def nxt(i):
        return i + jnp.where(i % nb == nb - 1, 0, 1)

    kern = functools.partial(_attn_kernel, nb=nb, seq=seq)
    return pl.pallas_call(
        kern,
        grid=(T // BLOCK,),
        in_specs=[pl.BlockSpec(memory_space=pltpu.SMEM),
                  pl.BlockSpec((BLOCK, ATTN_WIDTH), lambda i: (i, 0)),
                  pl.BlockSpec((BLOCK, KV_WIDTH), lambda i: (prev(i), kcol)),
                  pl.BlockSpec((BLOCK, KV_WIDTH), lambda i: (i, kcol)),
                  pl.BlockSpec((BLOCK, KV_WIDTH), lambda i: (nxt(i), kcol)),
                  pl.BlockSpec((BLOCK, KV_WIDTH), lambda i: (prev(i), vcol)),
                  pl.BlockSpec((BLOCK, KV_WIDTH), lambda i: (i, vcol)),
                  pl.BlockSpec((BLOCK, KV_WIDTH), lambda i: (nxt(i), vcol)),
                  pl.BlockSpec((N_Q_HEADS, BLOCK, 3 * BLOCK), lambda i: (0, 0, 0))],
        out_specs=pl.BlockSpec((BLOCK, ATTN_WIDTH), lambda i: (i, 0)),
        out_shape=jax.ShapeDtypeStruct((T, ATTN_WIDTH), F32),
        compiler_params=_cparams(("parallel",)),
        name="window_attn",
    )(sink, proj, proj, proj, proj, proj, proj, proj, bias)


def _filter_kernel(z_ref, w1_ref, b1_ref, fr1_ref, w2_ref, b2_ref, fr2_ref, w3_ref, dl_ref, o_ref,
                   *, tt, seq):
    hi = lax.Precision.HIGHEST
    a = jnp.dot(z_ref[...], w1_ref[...], preferred_element_type=F32, precision=hi) + b1_ref[...]
    hid = jnp.sin(fr1_ref[...] * a)
    a = jnp.dot(hid, w2_ref[...], preferred_element_type=F32, precision=hi) + b2_ref[...]
    hid = jnp.sin(fr2_ref[...] * a)
    h = jnp.dot(hid.astype(BF16), w3_ref[0], preferred_element_type=F32)
    j = lax.broadcasted_iota(jnp.int32, h.shape, 0) + pl.program_id(0) * tt
    lag = jnp.where(j < seq, j, (2 * seq - j) & (seq - 1))
    tn = lag.astype(F32) / float(max(seq - 1, 1))
    o_ref[...] = h * jnp.exp(-tn * dl_ref[...])


def _filters(seq, w1, b1, fr1, w2, b2, fr2, w3, tt=512):
    assert seq & (seq - 1) == 0
    t = np.arange(seq, dtype=np.float32)
    tn = t / np.float32(max(seq - 1, 1))
    w = (np.float32(2.0 * math.pi) * t / np.float32(seq))[:, None]
    bands = np.linspace(1e-4, POS_BANDS - 1, POS_BANDS, dtype=np.float32)
    z = np.concatenate([tn[:, None], np.cos(w * bands), -np.sin(w * bands)], -1).astype(np.float32)
    zp = np.pad(z, ((0, 0), (0, FILTER_HIDDEN - POS_EMB)))
    zfull = jnp.asarray(np.concatenate([zp, zp[0:1], zp[1:][::-1]], axis=0))
    w1p = jnp.pad(w1, ((0, FILTER_HIDDEN - POS_EMB), (0, 0)))
    max_decay = math.log(DECAY_TARGET) / FAST_DECAY_PCT
    min_decay = math.log(DECAY_TARGET) / SLOW_DECAY_PCT
    deltas = jnp.abs(jnp.linspace(min_decay, max_decay, HY_WIDTH, dtype=F32))
    ncol = HY_ORDER * HY_WIDTH
    dl = jnp.tile(deltas, HY_ORDER).reshape(1, ncol)
    H = FILTER_HIDDEN
    w3s = jnp.transpose(w3.reshape(H, HY_ORDER, 2, HY_WIDTH), (2, 0, 1, 3)).reshape(2, H, ncol).astype(BF16)
    half = seq // tt
    kern = functools.partial(_filter_kernel, tt=tt, seq=seq)
    full = lambda r, c: pl.BlockSpec((r, c), lambda i: (0, 0))
    return pl.pallas_call(
        kern,
        grid=(2 * half,),
        in_specs=[pl.BlockSpec((tt, H), lambda i: (i, 0)),
                  full(H, H), full(1, H), full(1, H), full(H, H), full(1, H), full(1, H),
                  pl.BlockSpec((1, H, ncol), lambda i: (i // half, 0, 0)), full(1, ncol)],
        out_specs=pl.BlockSpec((tt, ncol), lambda i: (i, 0)),
        out_shape=jax.ShapeDtypeStruct((2 * seq, ncol), F32),
        compiler_params=_cparams(("parallel",)),
        name="hyena_filter_mlp",
    )(zfull, w1p, b1.reshape(1, H), fr1.reshape(1, H), w2, b2.reshape(1, H), fr2.reshape(1, H), w3s, dl)


def _sconv_kernel(x_ref, w_ref, b_ref, o_ref):
    x = x_ref[0]
    L = x.shape[0]
    rows = lax.broadcasted_iota(jnp.int32, x.shape, 0)
    xm = jnp.where(rows == 0, 0.0, pltpu.roll(x, 1, 0))
    xp = jnp.where(rows == L - 1, 0.0, pltpu.roll(x, L - 1, 0))
    w = w_ref[...]
    o_ref[0] = xm * w[0:1] + x * w[1:2] + xp * w[2:3] + b_ref[...]


def _short_conv(proj3, conv_w, conv_b, col0, ct=256):
    B, L, _ = proj3.shape
    C = conv_w.shape[1]
    off = col0 // ct
    return pl.pallas_call(
        _sconv_kernel,
        grid=(B, C // ct),
        in_specs=[pl.BlockSpec((1, L, ct), lambda b, c: (b, 0, c + off)),
                  pl.BlockSpec((3, ct), lambda b, c: (0, c)),
                  pl.BlockSpec((1, ct), lambda b, c: (0, c))],
        out_specs=pl.BlockSpec((1, L, ct), lambda b, c: (b, 0, c)),
        out_shape=jax.ShapeDtypeStruct((B, L, C), F32),
        compiler_params=_cparams(("parallel", "parallel")),
        name="hyena_short_conv",
    )(proj3, conv_w, conv_b.reshape(1, C))


def _dft_constants():
    n1h = FFT_N1 // 2
    k1 = np.arange(FFT_K1_PAD)[:, None].astype(np.float64)
    n1 = np.arange(FFT_N1)[None, :].astype(np.float64)
    ang = 2.0 * np.pi * k1 * n1 / FFT_N1
    live = (np.arange(FFT_K1_PAD) < FFT_K1)[:, None]
    f1_full = np.concatenate([np.where(live, np.cos(ang), 0.0), np.where(live, -np.sin(ang), 0.0)], 0)
    f1 = f1_full[:, :n1h]
    kk1 = np.arange(FFT_K1)[:, None, None].astype(np.float64)
    k2 = np.arange(FFT_N2)[None, :, None].astype(np.float64)
    n2 = np.arange(FFT_N2)[None, None, :].astype(np.float64)
    phi = 2.0 * np.pi * (n2 * k2 / FFT_N2 + n2 * kk1 / (FFT_N1 * FFT_N2))
    c, s = np.cos(phi), np.sin(phi)
    g = np.concatenate([np.concatenate([c, s], 2), np.concatenate([-s, c], 2)], 1)
    ct_, st_ = np.transpose(c, (0, 2, 1)), np.transpose(s, (0, 2, 1))
    ginv = np.concatenate([np.concatenate([ct_, -st_], 2), np.concatenate([st_, ct_], 2)], 1)
    wk = np.where((np.arange(FFT_K1_PAD) == 0) | (np.arange(FFT_K1_PAD) == FFT_N1 // 2), 1.0, 2.0)
    wk = np.where(np.arange(FFT_K1_PAD) < FFT_K1, wk, 0.0)[None, :] / (FFT_N1 * FFT_N2)
    angi = 2.0 * np.pi * np.arange(n1h)[:, None] * np.arange(FFT_K1_PAD)[None, :] / FFT_N1
    finv = np.concatenate([wk * np.cos(angi), -wk * np.sin(angi)], 1)
    as_bf = lambda a: jnp.asarray(a.astype(np.float32)).astype(BF16)
    return as_bf(f1), as_bf(g), as_bf(finv), as_bf(ginv), as_bf(f1_full)


def _pad_rows_in(src_ref, xp_ref, n_blocks=FFT_N1 // 2, zero_row_of_block=None):
    for n1 in range(n_blocks):
        blk = src_ref[pl.ds(n1 * FFT_N2, FFT_N2), :]
        if n1 == zero_row_of_block:
            rows = lax.broadcasted_iota(jnp.int32, blk.shape, 0)
            blk = jnp.where(rows == 0, 0.0, blk)
        xp_ref[pl.ds(n1 * X_PITCH, FFT_N2), :] = blk


def _fft_stage1(xp_ref, w_ref, f1_ref):
    n1h = f1_ref.shape[1]
    kp = FFT_K1_PAD

    def body(i, carry):
        n2 = 2 * i
        xa = xp_ref[pl.ds(n2, n1h, stride=X_PITCH), :]
        xb = xp_ref[pl.ds(n2 + 1, n1h, stride=X_PITCH), :]
        xs = jnp.concatenate([xa, xb], axis=1).astype(BF16)
        r = jnp.dot(f1_ref[...], xs, preferred_element_type=F32)
        w_ref[pl.ds(n2, kp, stride=W_PITCH), :] = r[0:kp, 0:LANES]
        w_ref[pl.ds(n2 + 1, kp, stride=W_PITCH), :] = r[0:kp, LANES:2 * LANES]
        w_ref[pl.ds(FFT_N2 + n2, kp, stride=W_PITCH), :] = r[kp:2 * kp, 0:LANES]
        w_ref[pl.ds(FFT_N2 + n2 + 1, kp, stride=W_PITCH), :] = r[kp:2 * kp, LANES:2 * LANES]
        return carry

    lax.fori_loop(0, FFT_N2 // 2, body, 0, unroll=8)


def _kf_kernel(k_ref, f1_ref, g_ref, kf_ref, w_ref, xp_ref):
    scale = 1.0 / (jnp.sum(jnp.abs(k_ref[...]), 0, keepdims=True) + 1e-6)
    _pad_rows_in(k_ref, xp_ref, n_blocks=FFT_N1, zero_row_of_block=FFT_N1 // 2)
    _fft_stage1(xp_ref, w_ref, f1_ref)
    out = kf_ref.at[0]

    def body(k1, carry):
        src = pl.multiple_of(k1 * W_PITCH, 8)
        dst = pl.multiple_of(k1 * FFT_SLOT, FFT_SLOT)
        s = jnp.dot(g_ref[k1], w_ref[pl.ds(src, FFT_SLOT), :].astype(BF16), preferred_element_type=F32)
        out[pl.ds(dst, FFT_SLOT), :] = s * scale
        return carry

    lax.fori_loop(0, FFT_K1, body, 0, unroll=3)


def _filter_spectra(kfull, consts):
    L2 = kfull.shape[0]
    C = HY_WIDTH
    nct = C // LANES
    _, g, _, _, f1_full = consts
    return pl.pallas_call(
        _kf_kernel,
        grid=(HY_ORDER * nct,),
        in_specs=[pl.BlockSpec((L2, LANES), lambda j: (0, j)),
                  pl.BlockSpec(f1_full.shape, lambda j: (0, 0)),
                  pl.BlockSpec(g.shape, lambda j: (0, 0, 0))],
        out_specs=pl.BlockSpec((1, KF_ROWS, LANES), lambda j: (j // nct, 0, j % nct)),
        out_shape=jax.ShapeDtypeStruct((HY_ORDER, KF_ROWS, C), F32),
        scratch_shapes=[pltpu.VMEM((W_ROWS, LANES), F32), pltpu.VMEM((2 * X_ROWS, LANES), F32)],
        compiler_params=_cparams(("parallel",)),
        name="hyena_filter_fft",
    )(kfull, f1_full, g)


CONV_GROUP = 11


def _conv_kernel(z_ref, gate_ref, kf_ref, d_ref, f1_ref, g_ref, finv_ref, ginv_ref, o_ref,
                 w_ref, xp_ref, y_ref):
    zsrc = z_ref.at[0]
    kf = kf_ref.at[0]
    _pad_rows_in(zsrc, xp_ref)
    _fft_stage1(xp_ref, w_ref, f1_ref)

    def forward(k1):
        base = pl.multiple_of(k1 * W_PITCH, 8)
        kbase = pl.multiple_of(k1 * FFT_SLOT, FFT_SLOT)
        a = w_ref[pl.ds(base, FFT_SLOT), :].astype(BF16)
        s = jnp.dot(g_ref[k1], a, preferred_element_type=F32)
        sr, si = s[0:FFT_N2], s[FFT_N2:FFT_SLOT]
        kr = kf[pl.ds(kbase, FFT_N2), :]
        ki = kf[pl.ds(kbase + FFT_N2, FFT_N2), :]
        return jnp.concatenate([sr * kr - si * ki, sr * ki + si * kr], axis=0).astype(BF16)

    def inverse(k1, y):
        base = pl.multiple_of(k1 * W_PITCH, 8)
        w_ref[pl.ds(base, FFT_SLOT), :] = jnp.dot(ginv_ref[k1], y, preferred_element_type=F32)

    def step(i, do_forward, do_inverse):
        prev = [y_ref[k] for k in range(CONV_GROUP)] if do_inverse else None
        new = [forward(i * CONV_GROUP + k) for k in range(CONV_GROUP)] if do_forward else None
        if do_inverse:
            for k in range(CONV_GROUP):
                inverse((i - 1) * CONV_GROUP + k, prev[k])
        if do_forward:
            for k in range(CONV_GROUP):
                y_ref[k] = new[k]

    n_groups = FFT_K1 // CONV_GROUP
    step(0, True, False)

    def body(i, carry):
        step(i, True, True)
        return carry

    lax.fori_loop(1, n_groups, body, 0)
    step(n_groups, False, True)

    n1h = FFT_N1 // 2
    kp = FFT_K1_PAD

    def body2(i, carry):
        n2 = 2 * i
        zr = jnp.concatenate([w_ref[pl.ds(n2, kp, stride=W_PITCH), :],
                              w_ref[pl.ds(n2 + 1, kp, stride=W_PITCH), :]], axis=1)
        zi = jnp.concatenate([w_ref[pl.ds(FFT_N2 + n2, kp, stride=W_PITCH), :],
                              w_ref[pl.ds(FFT_N2 + n2 + 1, kp, stride=W_PITCH), :]], axis=1)
        zz = jnp.concatenate([zr, zi], axis=0).astype(BF16)
        x = jnp.dot(finv_ref[...], zz, preferred_element_type=F32)
        xp_ref[pl.ds(n2, n1h, stride=X_PITCH), :] = x[:, 0:LANES]
        xp_ref[pl.ds(n2 + 1, n1h, stride=X_PITCH), :] = x[:, LANES:2 * LANES]
        return carry

    lax.fori_loop(0, FFT_N2 // 2, body2, 0, unroll=8)
    d = d_ref[...]
    for n1 in range(n1h):
        rs = pl.ds(n1 * FFT_N2, FFT_N2)
        y = xp_ref[pl.ds(n1 * X_PITCH, FFT_N2), :]
        o_ref[0, rs, :] = gate_ref[0, rs, :] * (y + z_ref[0, rs, :] * d)


def _long_conv(z_arr, z_off, gate_arr, gate_off, kf_all, order, d, consts):
    B, L, _ = z_arr.shape
    C = HY_WIDTH
    nct = C // LANES
    f1, g, finv, ginv, _ = consts
    cst2 = lambda a: pl.BlockSpec(a.shape, lambda c, b: (0, 0))
    cst3 = lambda a: pl.BlockSpec(a.shape, lambda c, b: (0, 0, 0))
    return pl.pallas_call(
        _conv_kernel,
        grid=(nct, B),
        in_specs=[pl.BlockSpec((1, L, LANES), lambda c, b: (b, 0, c + z_off)),
                  pl.BlockSpec((1, L, LANES), lambda c, b: (b, 0, c + gate_off)),
                  pl.BlockSpec((1, KF_ROWS, LANES), lambda c, b: (order, 0, c)),
                  pl.BlockSpec((1, LANES), lambda c, b: (0, c)),
                  cst2(f1), cst3(g), cst2(finv), cst3(ginv)],
        out_specs=pl.BlockSpec((1, L, LANES), lambda c, b: (b, 0, c)),
        out_shape=jax.ShapeDtypeStruct((B, L, C), F32),
        scratch_shapes=[pltpu.VMEM((W_ROWS, LANES), F32), pltpu.VMEM((X_ROWS, LANES), F32),
                        pltpu.VMEM((CONV_GROUP, FFT_SLOT, LANES), BF16)],
        compiler_params=_cparams(("parallel", "parallel")),
        name=f"hyena_long_conv{order}",
    )(z_arr, gate_arr, kf_all, d.reshape(1, C), f1, g, finv, ginv)


def _mix_kernel(attn_ref, hyo_ref, h0_ref, mg_ref, w_ref, g_ref, b_ref, h1_ref, h1t_ref, *, alpha):
    def gnorm(x, goff):
        parts = []
        for gi in range(x.shape[1] // NORM_GROUP):
            xg = x[:, gi * NORM_GROUP:(gi + 1) * NORM_GROUP]
            ms = jnp.mean(xg * xg, -1, keepdims=True)
            gg = mg_ref[:, goff + gi * NORM_GROUP: goff + (gi + 1) * NORM_GROUP]
            parts.append((xg * lax.rsqrt(ms + 1e-6) * gg).astype(BF16))
        return jnp.concatenate(parts, axis=1)

    a = gnorm(attn_ref[...], 0)
    y = gnorm(hyo_ref[...], ATTN_WIDTH)
    mix = (jnp.dot(a, w_ref[0:ATTN_WIDTH, :], preferred_element_type=F32)
           + jnp.dot(y, w_ref[ATTN_WIDTH:, :], preferred_element_type=F32))
    x = alpha * h0_ref[...] + mix
    mu = jnp.mean(x, -1, keepdims=True)
    xc = x - mu
    var = jnp.mean(xc * xc, -1, keepdims=True)
    h1 = xc * lax.rsqrt(var + 1e-5) * g_ref[...] + b_ref[...]
    h1_ref[...] = h1
    h1t_ref[...] = h1.T.astype(BF16)


def _mix_out(attn, hyo, h0, mix_g, w_out_bf, ln_g, ln_b, alpha, tm=512):
    T, D = h0.shape
    kern = functools.partial(_mix_kernel, alpha=alpha)
    return pl.pallas_call(
        kern,
        grid=(T // tm,),
        in_specs=[pl.BlockSpec((tm, ATTN_WIDTH), lambda i: (i, 0)),
                  pl.BlockSpec((tm, HY_WIDTH), lambda i: (i, 0)),
                  pl.BlockSpec((tm, D), lambda i: (i, 0)),
                  pl.BlockSpec((1, D), lambda i: (0, 0)),
                  pl.BlockSpec((D, D), lambda i: (0, 0)),
                  pl.BlockSpec((1, D), lambda i: (0, 0)),
                  pl.BlockSpec((1, D), lambda i: (0, 0))],
        out_specs=[pl.BlockSpec((tm, D), lambda i: (i, 0)), pl.BlockSpec((D, tm), lambda i: (0, i))],
        out_shape=[jax.ShapeDtypeStruct((T, D), F32), jax.ShapeDtypeStruct((D, T), BF16)],
        compiler_params=_cparams(("parallel",)),
        name="mix_out_ln1",
    )(attn, hyo, h0, mix_g.reshape(1, D), w_out_bf, ln_g.reshape(1, D), ln_b.reshape(1, D))


def _oddeven_sort_pairs(n):
    pairs = []

    def merge(lo, m, r):
        step = r * 2
        if step < m:
            merge(lo, m, step)
            merge(lo + r, m, step)
            pairs.extend((i, i + r) for i in range(lo + r, lo + m - r, step))
        else:
            pairs.append((lo, lo + r))

    def sort(lo, m):
        if m > 1:
            sort(lo, m // 2)
            sort(lo + m // 2, m // 2)
            merge(lo, m, 1)

    sort(0, n)
    return pairs


def _bitonic_merge_pairs(n):
    pairs, d = [], n // 2
    while d >= 1:
        pairs.extend((i, i + d) for i in range(n) if (i & d) == 0)
        d //= 2
    return pairs


def _apply_network(xs, pairs):
    xs = list(xs)
    for i, j in pairs:
        xs[i], xs[j] = jnp.maximum(xs[i], xs[j]), jnp.minimum(xs[i], xs[j])
    return xs


def _top16_replicated(xs):
    xs = _apply_network(xs, _oddeven_sort_pairs(PEER_TOPK))
    merge = _bitonic_merge_pairs(PEER_TOPK)
    for shift in (4, 2, 1):
        other = [pltpu.roll(x, shift, 0) for x in xs]
        xs = [jnp.maximum(xs[i], other[PEER_TOPK - 1 - i]) for i in range(PEER_TOPK)]
        xs = _apply_network(xs, merge)
    return xs


def _pair_sum_candidates(v1, v2):
    sub = lax.broadcasted_iota(jnp.int32, v1[0].shape, 0)

    def by_sublane(vs):
        out = vs[SUBLANES - 1]
        for j in range(SUBLANES - 2, -1, -1):
            out = jnp.where(sub == j, vs[j], out)
        return out

    ninf = -jnp.inf
    v2lo, v2hi = by_sublane(v2[:SUBLANES]), by_sublane(v2[SUBLANES:])
    v1lo, v1hi = by_sublane(v1[:SUBLANES]), by_sublane(v1[SUBLANES:])
    cands = [v1[0] + v2lo, v1[0] + v2hi, v1[1] + v2lo]
    for a in (2, 3, 4):
        cands.append(jnp.where(sub < PEER_TOPK // (a + 1), v1[a] + v2lo, ninf))
    cands.append(jnp.where(sub >= 5, v1lo + v2[0], ninf))
    cands.append(v1hi + v2[0])
    cands.append(jnp.where(sub >= 5, v1lo + v2[1], ninf))
    pad = jnp.full(v1[0].shape, ninf, F32)
    return cands + [pad] * (PEER_TOPK - len(cands))


def _peer_score_kernel(wq_ref, h_ref, sk1_ref, sk2_ref, cnt_ref, r2_ref, e1_ref, e2_ref, q_ref):
    q_ref[...] = jnp.dot(wq_ref[...], h_ref[...], preferred_element_type=F32)

    def head(h, carry):
        base = pl.multiple_of(h * PEER_QDIM, PEER_QDIM)
        q1 = q_ref[pl.ds(base, PEER_QHALF), :].astype(BF16)
        q2 = q_ref[pl.ds(base + PEER_QHALF, PEER_QHALF), :].astype(BF16)
        s1 = jnp.dot(sk1_ref[...], q1, preferred_element_type=F32)
        s2 = jnp.dot(sk2_ref[...], q2, preferred_element_type=F32)
        slabs = lambda s: [s[SUBLANES * i:SUBLANES * (i + 1)] for i in range(N_KEYS // SUBLANES)]
        v1 = _top16_replicated(slabs(s1))
        v2 = _top16_replicated(slabs(s2))
        sc = _top16_replicated(_pair_sum_candidates(v1, v2))
        zsum = jnp.zeros_like(sc[0])
        for j in range(PEER_TOPK):
            zsum = zsum + jnp.exp(sc[j] - sc[0])
        row = lambda x: x[0:1]
        th = row(sc[PEER_TOPK - 1])
        cnt = jnp.zeros_like(s1)
        r2 = jnp.zeros_like(s2)
        for b in range(PEER_TOPK):
            cnt = cnt + jnp.where(s1 + row(v2[b]) >= th, 1.0, 0.0)
            r2 = r2 + jnp.where(row(v2[b]) > s2, 1.0, 0.0)
        def twice(x):
            bits = pltpu.bitcast(x.astype(BF16).astype(F32), jnp.uint32)
            return bits | (bits >> 16)

        cnt_w = twice(cnt)
        e1_w = twice(0.5 * jnp.exp(s1 - row(v1[0])) / row(zsum))
        for cc in range(cnt_ref.shape[1]):
            cnt_ref[h, cc] = cnt_w[:, cc * LANES:(cc + 1) * LANES]
            e1_ref[h, cc] = e1_w[:, cc * LANES:(cc + 1) * LANES]
        r2_ref[h] = pltpu.bitcast(r2.astype(BF16), jnp.uint32)
        e2_ref[h] = pltpu.bitcast(jnp.exp(s2 - row(v2[0])).astype(BF16), jnp.uint32)
        return carry

    lax.fori_loop(0, PEER_HEADS, head, 0)


def _peer_scores(wqT_bf, h1T_bf, sk1_bf, sk2_bf, tl=256):
    D, T = h1T_bf.shape
    Q = wqT_bf.shape[0]
    big = jax.ShapeDtypeStruct((PEER_HEADS, T // LANES, N_KEYS, LANES), jnp.uint32)
    big16 = jax.ShapeDtypeStruct((PEER_HEADS, N_KEYS // 2, T), jnp.uint32)
    bspec = pl.BlockSpec((PEER_HEADS, tl // LANES, N_KEYS, LANES), lambda i: (0, i, 0, 0))
    pspec = pl.BlockSpec((PEER_HEADS, N_KEYS // 2, tl), lambda i: (0, 0, i))
    return pl.pallas_call(
        _peer_score_kernel,
        grid=(T // tl,),
        in_specs=[pl.BlockSpec((Q, D), lambda i: (0, 0)),
                  pl.BlockSpec((D, tl), lambda i: (0, i)),
                  pl.BlockSpec((N_KEYS, PEER_QHALF), lambda i: (0, 0)),
                  pl.BlockSpec((N_KEYS, PEER_QHALF), lambda i: (0, 0))],
        out_specs=[bspec, pspec, bspec, pspec],
        out_shape=[big, big16, big, big16],
        scratch_shapes=[pltpu.VMEM((Q, tl), F32)],
        compiler_params=_cparams(("parallel",)),
        name="peer_scores_topk",
    )(wqT_bf, h1T_bf, sk1_bf, sk2_bf)


W_ROWS_PER_GROUP = 64


PEER_HALF = 4 * N_KEYS


def _peer_dense_kernel(hb_ref, u_ref, vt_ref, cnt_ref, r2_ref, e1_ref, e2_ref, o_ref, acc_ref,
                       a00, a01, a10, a11, w00, w01, w10, w11, *, te, tm, n_j, n_tiles):
    s = pl.program_id(0)
    sv = jnp.clip(s - 2, 0, n_tiles - 1)
    jv = sv % n_j
    jw = jnp.clip(s - 1, 0, n_tiles - 1) % n_j
    a_bufs = ((a00, a01), (a10, a11))
    w_bufs = ((w00, w01), (w10, w11))

    @pl.when(s == 0)
    def _():
        for ref in (a00, a01, a10, a11, w00, w01, w10, w11):
            ref[...] = jnp.zeros_like(ref)

    @pl.when(jv == 0)
    def _():
        acc_ref[...] = jnp.zeros_like(acc_ref)

    wide = 2 * LANES
    n_r, n_c = PEER_HALF // N_KEYS, tm // LANES
    key0 = pl.multiple_of(jw * (te // N_KEYS), SUBLANES)

    def half_body(half, a_cur, a_prv, w_cur, w_prv):
        def stage_a(q):
            ms = slice(half * PEER_HALF + (q // 2) * wide, half * PEER_HALF + (q // 2 + 1) * wide)
            ls = slice((q % 2) * wide, (q % 2 + 1) * wide)
            a_cur[(q // 2) * wide:(q // 2 + 1) * wide, ls] = jnp.dot(
                u_ref[ms, :], hb_ref[:, ls], preferred_element_type=F32)

        def stage_v(p):
            fs = slice((p // 2) * wide, (p // 2 + 1) * wide)
            ls = slice((p % 2) * wide, (p % 2 + 1) * wide)
            es = slice(half * PEER_HALF, (half + 1) * PEER_HALF)
            acc_ref[fs, ls] += jnp.dot(vt_ref[0, fs, es], w_cur[:, ls], preferred_element_type=F32)

        n_g = N_KEYS // W_ROWS_PER_GROUP
        grp = (W_ROWS_PER_GROUP, LANES)

        def stage_w(idx):
            p, g = idx // n_g, idx % n_g
            r, c = p // n_c, p % n_c
            krow = half * n_r + r
            cs = slice(c * LANES, (c + 1) * LANES)
            ks = slice(g * W_ROWS_PER_GROUP // 2, (g + 1) * W_ROWS_PER_GROUP // 2)
            gate = jnp.zeros(grp, BF16)
            pair = (W_ROWS_PER_GROUP // 2, LANES)
            for h in range(PEER_HEADS):
                cnt_row = cnt_ref[h, c, pl.ds(key0, SUBLANES), :][krow:krow + 1]
                e1_row = e1_ref[h, c, pl.ds(key0, SUBLANES), :][krow:krow + 1]
                cnt_b = pltpu.bitcast(jnp.broadcast_to(cnt_row, pair), BF16)
                e1_b = pltpu.bitcast(jnp.broadcast_to(e1_row, pair), BF16)
                sel = pltpu.bitcast(r2_ref[h, ks, cs], BF16) < cnt_b
                val = pltpu.bitcast(e2_ref[h, ks, cs], BF16) * e1_b
                gate = gate + jnp.where(sel, val, jnp.zeros(grp, BF16))
            ws = slice(r * N_KEYS + g * W_ROWS_PER_GROUP, r * N_KEYS + (g + 1) * W_ROWS_PER_GROUP)
            a = a_prv[ws, cs]
            act = a * (1.0 + lax.erf(a * (1.0 / math.sqrt(2.0))))
            w_prv[ws, cs] = gate * act.astype(BF16)

        per_v = n_r * n_c * n_g // 16
        for q in range(4):
            stage_a(q)
            for t in range(4):
                stage_v(4 * q + t)
                for gg in range(per_v):
                    stage_w((4 * q + t) * per_v + gg)

    for par in range(2):
        @pl.when(s % 2 == par)
        def _():
            for half in range(2):
                half_body(half, a_bufs[par][half], a_bufs[1 - par][half],
                          w_bufs[par][half], w_bufs[1 - par][half])

    @pl.when(jv == n_j - 1)
    def _():
        o_ref[...] = acc_ref[...]


def _peer_dense(h1T_bf, u_bf, v_bf, cnt, r2, e1, e2, tm=512, te=2 * PEER_HALF):
    D, T = h1T_bf.shape
    E = u_bf.shape[0]
    assert te == SUBLANES * N_KEYS and tm == 4 * LANES and D == 2048, \
        "the stage interleave is written for two 512-expert halves x 512 tokens"
    n_i, n_j = T // tm, E // te
    vT_bf = jnp.transpose(v_bf.reshape(n_j, te, D), (0, 2, 1))
    n_tiles = n_i * n_j
    kern = functools.partial(_peer_dense_kernel, te=te, tm=tm, n_j=n_j, n_tiles=n_tiles)
    ta = lambda s: jnp.minimum(s, n_tiles - 1)
    tw = lambda s: jnp.clip(s - 1, 0, n_tiles - 1)
    tv = lambda s: jnp.clip(s - 2, 0, n_tiles - 1)
    sspec = pl.BlockSpec((PEER_HEADS, N_KEYS // 2, tm), lambda s: (0, 0, tw(s) // n_j))
    xspec = pl.BlockSpec((PEER_HEADS, tm // LANES, N_KEYS, LANES), lambda s: (0, tw(s) // n_j, 0, 0))
    return pl.pallas_call(
        kern,
        grid=(n_tiles + 2,),
        in_specs=[pl.BlockSpec((D, tm), lambda s: (0, ta(s) // n_j)),
                  pl.BlockSpec((te, D), lambda s: (ta(s) % n_j, 0)),
                  pl.BlockSpec((1, D, te), lambda s: (tv(s) % n_j, 0, 0)),
                  xspec, sspec, xspec, sspec],
        out_specs=pl.BlockSpec((D, tm), lambda s: (0, tv(s) // n_j)),
        out_shape=jax.ShapeDtypeStruct((D, T), F32),
        scratch_shapes=([pltpu.VMEM((D, tm), F32)]
                        + [pltpu.VMEM((PEER_HALF, tm), F32)] * 4
                        + [pltpu.VMEM((PEER_HALF, tm), BF16)] * 4),
        compiler_params=_cparams(("arbitrary",)),
        name="peer_dense",
    )(h1T_bf, u_bf, vT_bf, cnt, r2, e1, e2)


def _res_ln_kernel(h_ref, ft_ref, g_ref, b_ref, o_ref, *, alpha):
    x = alpha * h_ref[...] + ft_ref[...].T
    mu = jnp.mean(x, -1, keepdims=True)
    xc = x - mu
    var = jnp.mean(xc * xc, -1, keepdims=True)
    o_ref[...] = xc * lax.rsqrt(var + 1e-5) * g_ref[...] + b_ref[...]


def _res_ln(h, fT, g, b, alpha, tm=512):
    T, D = h.shape
    row = pl.BlockSpec((tm, D), lambda i: (i, 0))
    vec = pl.BlockSpec((1, D), lambda i: (0, 0))
    return pl.pallas_call(
        functools.partial(_res_ln_kernel, alpha=alpha),
        grid=(T // tm,),
        in_specs=[row, pl.BlockSpec((D, tm), lambda i: (0, i)), vec, vec],
        out_specs=row,
        out_shape=jax.ShapeDtypeStruct((T, D), F32),
        compiler_params=_cparams(("parallel",)),
        name="res_ln2",
    )(h, fT, g.reshape(1, D), b.reshape(1, D))


def kernel(x, ln0_g, ln0_b, rel_bias, w_in, sink, conv_w, conv_b, f_w1, f_b1, f_freq1, f_w2, f_b2,
           f_freq2, f_w3, hy_bias, mix_norm_g, w_out, ln1_g, ln1_b, peer_wq, peer_subkeys, peer_u,
           peer_v, ln2_g, ln2_b):
    B, S, D = x.shape
    T = B * S
    alpha = (2.0 * DEPTH) ** 0.25
    consts = _dft_constants()

    qi = jnp.arange(BLOCK, dtype=jnp.int32)
    kj = jnp.arange(3 * BLOCK, dtype=jnp.int32)
    rel = kj[None, :] - BLOCK - qi[:, None]
    onehot = (_t5_bucket(rel)[..., None] == jnp.arange(N_BUCKETS, dtype=jnp.int32)).astype(F32)
    bias = jnp.einsum("qkb,bh->hqk", onehot, rel_bias.astype(F32), precision=lax.Precision.HIGHEST)

    h, h_bf = _ln0(x.reshape(T, D), ln0_g, ln0_b)
    for l in range(DEPTH):
        proj = _matmul(h_bf, w_in[l].astype(BF16), tm=1024, tn=1536)
        attn = _attention(proj, bias, sink[l], B, S)

        hfilt = _filters(S, f_w1[l], f_b1[l], f_freq1[l], f_w2[l], f_b2[l], f_freq2[l], f_w3[l])
        kf = _filter_spectra(hfilt, consts)
        u = _short_conv(proj.reshape(B, S, -1), conv_w[l], conv_b[l], ATTN_WIDTH + 2 * KV_WIDTH)
        nct = HY_WIDTH // LANES
        z1 = _long_conv(u, 0, u, nct, kf, 0, hy_bias[l, 0], consts)
        hyo = _long_conv(z1, 0, u, 2 * nct, kf, 1, hy_bias[l, 1], consts)

        h1, h1T_bf = _mix_out(attn, hyo.reshape(T, HY_WIDTH), h, mix_norm_g[l], w_out[l].astype(BF16),
                              ln1_g[l], ln1_b[l], alpha)
        cnt, r2, e1, e2 = _peer_scores(peer_wq[l].T.astype(BF16), h1T_bf,
                                       peer_subkeys[l, 0].astype(BF16), peer_subkeys[l, 1].astype(BF16))
        ffnT = _peer_dense(h1T_bf, peer_u[l].astype(BF16), peer_v[l].astype(BF16), cnt, r2, e1, e2)
        h = _res_ln(h1, ffnT, ln2_g[l], ln2_b[l], alpha)
        if l + 1 < DEPTH:
            h_bf = h.astype(BF16)
    return h.reshape(B, S, D)
```

```python
import functools
import math

import numpy as np
import jax
import jax.numpy as jnp
from jax import lax
from jax.experimental import pallas as pl
from jax.experimental.pallas import tpu as pltpu

F32 = jnp.float32
BF16 = jnp.bfloat16

D_MODEL = 2048
HEAD_DIM = 128
N_Q_HEADS = 8
N_KV_HEADS = 2
GQA_GROUP = N_Q_HEADS // N_KV_HEADS
ATTN_WIDTH = N_Q_HEADS * HEAD_DIM
KV_WIDTH = N_KV_HEADS * HEAD_DIM
WINDOW = 128
BLOCK = 128
N_BUCKETS = 32
MAX_DISTANCE = 128
HY_WIDTH = D_MODEL - ATTN_WIDTH
HY_ORDER = 2
POS_BANDS = 16
POS_EMB = 1 + 2 * POS_BANDS
FILTER_HIDDEN = 64
FAST_DECAY_PCT = 0.3
SLOW_DECAY_PCT = 1.5
DECAY_TARGET = 1e-2
NORM_GROUP = 128
N_KEYS = 128
PEER_HEADS = 8
PEER_QDIM = 256
PEER_QHALF = PEER_QDIM // 2
PEER_TOPK = 16
NEG = -1e30
DEPTH = 1

LANES = 128
VMEM_LIMIT = 56 * 1024 * 1024

FFT_N1 = 64
FFT_N2 = 128
FFT_K1 = FFT_N1 // 2 + 1
FFT_K1_PAD = 40
FFT_SLOT = 2 * FFT_N2
KF_ROWS = FFT_K1 * FFT_SLOT
SUBLANES = 8
W_PITCH = FFT_SLOT + SUBLANES
X_PITCH = FFT_N2 + SUBLANES
W_ROWS = FFT_K1_PAD * W_PITCH
X_ROWS = (FFT_N1 // 2) * X_PITCH


def _cparams(sem, vmem=VMEM_LIMIT):
    return pltpu.CompilerParams(dimension_semantics=sem, vmem_limit_bytes=vmem)


def _ln0_kernel(x_ref, g_ref, b_ref, h_ref, hb_ref):
    x = x_ref[...]
    mu = jnp.mean(x, -1, keepdims=True)
    xc = x - mu
    var = jnp.mean(xc * xc, -1, keepdims=True)
    y = xc * lax.rsqrt(var + 1e-5) * g_ref[...] + b_ref[...]
    h_ref[...] = y
    hb_ref[...] = y.astype(BF16)


def _ln0(x2d, g, b, tm=1024):
    T, D = x2d.shape
    return pl.pallas_call(
        _ln0_kernel,
        grid=(T // tm,),
        in_specs=[pl.BlockSpec((tm, D), lambda i: (i, 0)),
                  pl.BlockSpec((1, D), lambda i: (0, 0)),
                  pl.BlockSpec((1, D), lambda i: (0, 0))],
        out_specs=[pl.BlockSpec((tm, D), lambda i: (i, 0)),
                   pl.BlockSpec((tm, D), lambda i: (i, 0))],
        out_shape=[jax.ShapeDtypeStruct((T, D), F32), jax.ShapeDtypeStruct((T, D), BF16)],
        compiler_params=_cparams(("parallel",)),
        name="ln0",
    )(x2d, g.reshape(1, D), b.reshape(1, D))


def _mm_kernel(a_ref, b_ref, o_ref):
    o_ref[...] = jnp.dot(a_ref[...], b_ref[...], preferred_element_type=F32)


def _matmul(a, b, tm, tn):
    M, K = a.shape
    N = b.shape[1]
    return pl.pallas_call(
        _mm_kernel,
        grid=(N // tn, M // tm),
        in_specs=[pl.BlockSpec((tm, K), lambda j, i: (i, 0)),
                  pl.BlockSpec((K, tn), lambda j, i: (0, j))],
        out_specs=pl.BlockSpec((tm, tn), lambda j, i: (i, j)),
        out_shape=jax.ShapeDtypeStruct((M, N), F32),
        compiler_params=_cparams(("parallel", "parallel")),
        name="in_proj",
    )(a, b)


def _t5_bucket(rel):
    nb = N_BUCKETS // 2
    ret = (rel > 0).astype(jnp.int32) * nb
    n = jnp.abs(rel)
    max_exact = nb // 2
    nf = jnp.maximum(n, 1).astype(F32)
    large = max_exact + (jnp.log(nf / max_exact) / math.log(MAX_DISTANCE / max_exact)
                         * (nb - max_exact)).astype(jnp.int32)
    large = jnp.minimum(large, nb - 1)
    return ret + jnp.where(n < max_exact, n, large)


def _attn_kernel(sink_ref, q_ref, kp_ref, kc_ref, kn_ref, vp_ref, vc_ref, vn_ref, bias_ref, o_ref,
                 *, nb, seq):
    n = pl.program_id(0) % nb
    rows = GQA_GROUP * BLOCK
    row = lax.broadcasted_iota(jnp.int32, (rows, 3 * BLOCK), 0)
    kj = lax.broadcasted_iota(jnp.int32, (rows, 3 * BLOCK), 1)
    rel = kj - BLOCK - (row & (BLOCK - 1))
    kabs = n * BLOCK + kj - BLOCK
    valid = (jnp.abs(rel) <= WINDOW) & (kabs >= 0) & (kabs < seq)
    head_of_row = lax.broadcasted_iota(jnp.int32, (rows, 1), 0) // BLOCK
    k = jnp.concatenate([kp_ref[...], kc_ref[...], kn_ref[...]], axis=0).astype(BF16)
    v = jnp.concatenate([vp_ref[...], vc_ref[...], vn_ref[...]], axis=0).astype(BF16)
    scale = 1.0 / math.sqrt(HEAD_DIM)
    for g in range(N_KV_HEADS):
        kg = k[:, g * HEAD_DIM:(g + 1) * HEAD_DIM]
        vg = v[:, g * HEAD_DIM:(g + 1) * HEAD_DIM]
        h0 = g * GQA_GROUP
        qg = jnp.concatenate([q_ref[:, (h0 + r) * HEAD_DIM:(h0 + r + 1) * HEAD_DIM]
                              for r in range(GQA_GROUP)], axis=0).astype(BF16)
        s = lax.dot_general(qg, kg, (((1,), (1,)), ((), ())), preferred_element_type=F32) * scale
        bias_g = bias_ref[h0:h0 + GQA_GROUP].reshape(rows, 3 * BLOCK)
        s = jnp.where(valid, s + bias_g, NEG)
        sk = jnp.zeros((rows, 1), F32)
        for r in range(GQA_GROUP):
            sk = jnp.where(head_of_row == r, sink_ref[h0 + r], sk)
        m = jnp.maximum(jnp.max(s, -1, keepdims=True), sk)
        p = jnp.exp(s - m)
        denom = jnp.sum(p, -1, keepdims=True) + jnp.exp(sk - m)
        p = p / denom
        o = jnp.dot(p.astype(BF16), vg, preferred_element_type=F32)
        for r in range(GQA_GROUP):
            o_ref[:, (h0 + r) * HEAD_DIM:(h0 + r + 1) * HEAD_DIM] = o[r * BLOCK:(r + 1) * BLOCK]


def _attention(proj, bias, sink, seq):
    T = proj.shape[0]
    nb = seq // BLOCK
    kcol = ATTN_WIDTH // KV_WIDTH
    vcol = kcol + 1

    def prev(i):
        return i - jnp.where(i % nb == 0, 0, 1)

    def nxt(i):
        return i + jnp.where(i % nb == nb - 1, 0, 1)

    kern = functools.partial(_attn_kernel, nb=nb, seq=seq)
    return pl.pallas_call(
        kern,
        grid=(T // BLOCK,),
        in_specs=[pl.BlockSpec(memory_space=pltpu.SMEM),
                  pl.BlockSpec((BLOCK, ATTN_WIDTH), lambda i: (i, 0)),
                  pl.BlockSpec((BLOCK, KV_WIDTH), lambda i: (prev(i), kcol)),
                  pl.BlockSpec((BLOCK, KV_WIDTH), lambda i: (i, kcol)),
                  pl.BlockSpec((BLOCK, KV_WIDTH), lambda i: (nxt(i), kcol)),
                  pl.BlockSpec((BLOCK, KV_WIDTH), lambda i: (prev(i), vcol)),
                  pl.BlockSpec((BLOCK, KV_WIDTH), lambda i: (i, vcol)),
                  pl.BlockSpec((BLOCK, KV_WIDTH), lambda i: (nxt(i), vcol)),
                  pl.BlockSpec((N_Q_HEADS, BLOCK, 3 * BLOCK), lambda i: (0, 0, 0))],
        out_specs=pl.BlockSpec((BLOCK, ATTN_WIDTH), lambda i: (i, 0)),
        out_shape=jax.ShapeDtypeStruct((T, ATTN_WIDTH), F32),
        compiler_params=_cparams(("parallel",)),
        name="window_attn",
    )(sink, proj, proj, proj, proj, proj, proj, proj, bias)


def _filter_kernel(z_ref, w1_ref, b1_ref, fr1_ref, w2_ref, b2_ref, fr2_ref, w3_ref, dl_ref, o_ref,
                   *, tt, seq):
    hi = lax.Precision.HIGHEST
    a = jnp.dot(z_ref[...], w1_ref[...], preferred_element_type=F32, precision=hi) + b1_ref[...]
    hid = jnp.sin(fr1_ref[...] * a)
    a = jnp.dot(hid, w2_ref[...], preferred_element_type=F32, precision=hi) + b2_ref[...]
    hid = jnp.sin(fr2_ref[...] * a)
    h = jnp.dot(hid.astype(BF16), w3_ref[0], preferred_element_type=F32)
    j = lax.broadcasted_iota(jnp.int32, h.shape, 0) + pl.program_id(0) * tt
    lag = jnp.where(j < seq, j, (2 * seq - j) & (seq - 1))
    tn = lag.astype(F32) / float(max(seq - 1, 1))
    o_ref[...] = h * jnp.exp(-tn * dl_ref[...])


def _filters(seq, w1, b1, fr1, w2, b2, fr2, w3, tt=512):
    assert seq & (seq - 1) == 0
    t = np.arange(seq, dtype=np.float32)
    tn = t / np.float32(max(seq - 1, 1))
    w = (np.float32(2.0 * math.pi) * t / np.float32(seq))[:, None]
    bands = np.linspace(1e-4, POS_BANDS - 1, POS_BANDS, dtype=np.float32)
    z = np.concatenate([tn[:, None], np.cos(w * bands), -np.sin(w * bands)], -1).astype(np.float32)
    zp = np.pad(z, ((0, 0), (0, FILTER_HIDDEN - POS_EMB)))
    zfull = jnp.asarray(np.concatenate([zp, zp[0:1], zp[1:][::-1]], axis=0))
    w1p = jnp.pad(w1, ((0, FILTER_HIDDEN - POS_EMB), (0, 0)))
    max_decay = math.log(DECAY_TARGET) / FAST_DECAY_PCT
    min_decay = math.log(DECAY_TARGET) / SLOW_DECAY_PCT
    deltas = jnp.abs(jnp.linspace(min_decay, max_decay, HY_WIDTH, dtype=F32))
    ncol = HY_ORDER * HY_WIDTH
    dl = jnp.tile(deltas, HY_ORDER).reshape(1, ncol)
    H = FILTER_HIDDEN
    w3s = jnp.transpose(w3.reshape(H, HY_ORDER, 2, HY_WIDTH), (2, 0, 1, 3)).reshape(2, H, ncol).astype(BF16)
    half = seq // tt
    kern = functools.partial(_filter_kernel, tt=tt, seq=seq)
    full = lambda r, c: pl.BlockSpec((r, c), lambda i: (0, 0))
    return pl.pallas_call(
        kern,
        grid=(2 * half,),
        in_specs=[pl.BlockSpec((tt, H), lambda i: (i, 0)),
                  full(H, H), full(1, H), full(1, H), full(H, H), full(1, H), full(1, H),
                  pl.BlockSpec((1, H, ncol), lambda i: (i // half, 0, 0)), full(1, ncol)],
        out_specs=pl.BlockSpec((tt, ncol), lambda i: (i, 0)),
        out_shape=jax.ShapeDtypeStruct((2 * seq, ncol), F32),
        compiler_params=_cparams(("parallel",)),
        name="hyena_filter_mlp",
    )(zfull, w1p, b1.reshape(1, H), fr1.reshape(1, H), w2, b2.reshape(1, H), fr2.reshape(1, H), w3s, dl)


def _sconv_kernel(x_ref, w_ref, b_ref, o_ref):
    x = x_ref[0]
    L = x.shape[0]
    rows = lax.broadcasted_iota(jnp.int32, x.shape, 0)
    xm = jnp.where(rows == 0, 0.0, pltpu.roll(x, 1, 0))
    xp = jnp.where(rows == L - 1, 0.0, pltpu.roll(x, L - 1, 0))
    w = w_ref[...]
    o_ref[0] = xm * w[0:1] + x * w[1:2] + xp * w[2:3] + b_ref[...]


def _short_conv(proj3, conv_w, conv_b, col0, ct=256):
    B, L, _ = proj3.shape
    C = conv_w.shape[1]
    off = col0 // ct
    return pl.pallas_call(
        _sconv_kernel,
        grid=(B, C // ct),
        in_specs=[pl.BlockSpec((1, L, ct), lambda b, c: (b, 0, c + off)),
                  pl.BlockSpec((3, ct), lambda b, c: (0, c)),
                  pl.BlockSpec((1, ct), lambda b, c: (0, c))],
        out_specs=pl.BlockSpec((1, L, ct), lambda b, c: (b, 0, c)),
        out_shape=jax.ShapeDtypeStruct((B, L, C), F32),
        compiler_params=_cparams(("parallel", "parallel")),
        name="hyena_short_conv",
    )(proj3, conv_w, conv_b.reshape(1, C))


def _dft_constants():
    n1h = FFT_N1 // 2
    k1 = np.arange(FFT_K1_PAD)[:, None].astype(np.float64)
    n1 = np.arange(FFT_N1)[None, :].astype(np.float64)
    ang = 2.0 * np.pi * k1 * n1 / FFT_N1
    live = (np.arange(FFT_K1_PAD) < FFT_K1)[:, None]
    f1_full = np.concatenate([np.where(live, np.cos(ang), 0.0), np.where(live, -np.sin(ang), 0.0)], 0)
    f1 = f1_full[:, :n1h]
    kk1 = np.arange(FFT_K1)[:, None, None].astype(np.float64)
    k2 = np.arange(FFT_N2)[None, :, None].astype(np.float64)
    n2 = np.arange(FFT_N2)[None, None, :].astype(np.float64)
    phi = 2.0 * np.pi * (n2 * k2 / FFT_N2 + n2 * kk1 / (FFT_N1 * FFT_N2))
    c, s = np.cos(phi), np.sin(phi)
    g = np.concatenate([np.concatenate([c, s], 2), np.concatenate([-s, c], 2)], 1)
    ct_, st_ = np.transpose(c, (0, 2, 1)), np.transpose(s, (0, 2, 1))
    ginv = np.concatenate([np.concatenate([ct_, -st_], 2), np.concatenate([st_, ct_], 2)], 1)
    wk = np.where((np.arange(FFT_K1_PAD) == 0) | (np.arange(FFT_K1_PAD) == FFT_N1 // 2), 1.0, 2.0)
    wk = np.where(np.arange(FFT_K1_PAD) < FFT_K1, wk, 0.0)[None, :] / (FFT_N1 * FFT_N2)
    angi = 2.0 * np.pi * np.arange(n1h)[:, None] * np.arange(FFT_K1_PAD)[None, :] / FFT_N1
    finv = np.concatenate([wk * np.cos(angi), -wk * np.sin(angi)], 1)
    as_bf = lambda a: jnp.asarray(a.astype(np.float32)).astype(BF16)
    return as_bf(f1), as_bf(g), as_bf(finv), as_bf(ginv), as_bf(f1_full)


def _pad_rows_in(src_ref, xp_ref, n_blocks=FFT_N1 // 2, zero_row_of_block=None):
    for n1 in range(n_blocks):
        blk = src_ref[pl.ds(n1 * FFT_N2, FFT_N2), :]
        if n1 == zero_row_of_block:
            rows = lax.broadcasted_iota(jnp.int32, blk.shape, 0)
            blk = jnp.where(rows == 0, 0.0, blk)
        xp_ref[pl.ds(n1 * X_PITCH, FFT_N2), :] = blk


def _fft_stage1(xp_ref, w_ref, f1_ref):
    n1h = f1_ref.shape[1]
    kp = FFT_K1_PAD

    def body(i, carry):
        n2 = 2 * i
        xa = xp_ref[pl.ds(n2, n1h, stride=X_PITCH), :]
        xb = xp_ref[pl.ds(n2 + 1, n1h, stride=X_PITCH), :]
        xs = jnp.concatenate([xa, xb], axis=1).astype(BF16)
        r = jnp.dot(f1_ref[...], xs, preferred_element_type=F32)
        w_ref[pl.ds(n2, kp, stride=W_PITCH), :] = r[0:kp, 0:LANES]
        w_ref[pl.ds(n2 + 1, kp, stride=W_PITCH), :] = r[0:kp, LANES:2 * LANES]
        w_ref[pl.ds(FFT_N2 + n2, kp, stride=W_PITCH), :] = r[kp:2 * kp, 0:LANES]
        w_ref[pl.ds(FFT_N2 + n2 + 1, kp, stride=W_PITCH), :] = r[kp:2 * kp, LANES:2 * LANES]
        return carry

    lax.fori_loop(0, FFT_N2 // 2, body, 0, unroll=8)


def _kf_kernel(k_ref, f1_ref, g_ref, kf_ref, w_ref, xp_ref):
    scale = 1.0 / (jnp.sum(jnp.abs(k_ref[...]), 0, keepdims=True) + 1e-6)
    _pad_rows_in(k_ref, xp_ref, n_blocks=FFT_N1, zero_row_of_block=FFT_N1 // 2)
    _fft_stage1(xp_ref, w_ref, f1_ref)
    out = kf_ref.at[0]

    def body(k1, carry):
        src = pl.multiple_of(k1 * W_PITCH, 8)
        dst = pl.multiple_of(k1 * FFT_SLOT, FFT_SLOT)
        s = jnp.dot(g_ref[k1], w_ref[pl.ds(src, FFT_SLOT), :].astype(BF16), preferred_element_type=F32)
        out[pl.ds(dst, FFT_SLOT), :] = s * scale
        return carry

    lax.fori_loop(0, FFT_K1, body, 0, unroll=3)


def _filter_spectra(kfull, consts):
    L2 = kfull.shape[0]
    assert L2 == FFT_N1 * FFT_N2, "the DFT factorisation is written for 2L = 64 * 128"
    C = HY_WIDTH
    nct = C // LANES
    _, g, _, _, f1_full = consts
    return pl.pallas_call(
        _kf_kernel,
        grid=(HY_ORDER * nct,),
        in_specs=[pl.BlockSpec((L2, LANES), lambda j: (0, j)),
                  pl.BlockSpec(f1_full.shape, lambda j: (0, 0)),
                  pl.BlockSpec(g.shape, lambda j: (0, 0, 0))],
        out_specs=pl.BlockSpec((1, KF_ROWS, LANES), lambda j: (j // nct, 0, j % nct)),
        out_shape=jax.ShapeDtypeStruct((HY_ORDER, KF_ROWS, C), F32),
        scratch_shapes=[pltpu.VMEM((W_ROWS, LANES), F32), pltpu.VMEM((2 * X_ROWS, LANES), F32)],
        compiler_params=_cparams(("parallel",)),
        name="hyena_filter_fft",
    )(kfull, f1_full, g)


CONV_GROUP = 11


def _conv_kernel(z_ref, gate_ref, kf_ref, d_ref, f1_ref, g_ref, finv_ref, ginv_ref, o_ref,
                 w_ref, xp_ref, y_ref):
    zsrc = z_ref.at[0]
    kf = kf_ref.at[0]
    _pad_rows_in(zsrc, xp_ref)
    _fft_stage1(xp_ref, w_ref, f1_ref)

    def forward(k1):
        base = pl.multiple_of(k1 * W_PITCH, 8)
        kbase = pl.multiple_of(k1 * FFT_SLOT, FFT_SLOT)
        a = w_ref[pl.ds(base, FFT_SLOT), :].astype(BF16)
        s = jnp.dot(g_ref[k1], a, preferred_element_type=F32)
        sr, si = s[0:FFT_N2], s[FFT_N2:FFT_SLOT]
        kr = kf[pl.ds(kbase, FFT_N2), :]
        ki = kf[pl.ds(kbase + FFT_N2, FFT_N2), :]
        return jnp.concatenate([sr * kr - si * ki, sr * ki + si * kr], axis=0).astype(BF16)

    def inverse(k1, y):
        base = pl.multiple_of(k1 * W_PITCH, 8)
        w_ref[pl.ds(base, FFT_SLOT), :] = jnp.dot(ginv_ref[k1], y, preferred_element_type=F32)

    def step(i, do_forward, do_inverse):
        prev = [y_ref[k] for k in range(CONV_GROUP)] if do_inverse else None
        new = [forward(i * CONV_GROUP + k) for k in range(CONV_GROUP)] if do_forward else None
        if do_inverse:
            for k in range(CONV_GROUP):
                inverse((i - 1) * CONV_GROUP + k, prev[k])
        if do_forward:
            for k in range(CONV_GROUP):
                y_ref[k] = new[k]

    n_groups = FFT_K1 // CONV_GROUP
    step(0, True, False)

    def body(i, carry):
        step(i, True, True)
        return carry

    lax.fori_loop(1, n_groups, body, 0)
    step(n_groups, False, True)

    n1h = FFT_N1 // 2
    kp = FFT_K1_PAD

    def body2(i, carry):
        n2 = 2 * i
        zr = jnp.concatenate([w_ref[pl.ds(n2, kp, stride=W_PITCH), :],
                              w_ref[pl.ds(n2 + 1, kp, stride=W_PITCH), :]], axis=1)
        zi = jnp.concatenate([w_ref[pl.ds(FFT_N2 + n2, kp, stride=W_PITCH), :],
                              w_ref[pl.ds(FFT_N2 + n2 + 1, kp, stride=W_PITCH), :]], axis=1)
        zz = jnp.concatenate([zr, zi], axis=0).astype(BF16)
        x = jnp.dot(finv_ref[...], zz, preferred_element_type=F32)
        xp_ref[pl.ds(n2, n1h, stride=X_PITCH), :] = x[:, 0:LANES]
        xp_ref[pl.ds(n2 + 1, n1h, stride=X_PITCH), :] = x[:, LANES:2 * LANES]
        return carry

    lax.fori_loop(0, FFT_N2 // 2, body2, 0, unroll=8)
    d = d_ref[...]
    for n1 in range(n1h):
        rs = pl.ds(n1 * FFT_N2, FFT_N2)
        y = xp_ref[pl.ds(n1 * X_PITCH, FFT_N2), :]
        o_ref[0, rs, :] = gate_ref[0, rs, :] * (y + z_ref[0, rs, :] * d)


def _long_conv(z_arr, z_off, gate_arr, gate_off, kf_all, order, d, consts):
    B, L, _ = z_arr.shape
    assert 2 * L == FFT_N1 * FFT_N2, "the DFT factorisation is written for 2L = 64 * 128"
    C = HY_WIDTH
    nct = C // LANES
    f1, g, finv, ginv, _ = consts
    cst2 = lambda a: pl.BlockSpec(a.shape, lambda c, b: (0, 0))
    cst3 = lambda a: pl.BlockSpec(a.shape, lambda c, b: (0, 0, 0))
    return pl.pallas_call(
        _conv_kernel,
        grid=(nct, B),
        in_specs=[pl.BlockSpec((1, L, LANES), lambda c, b: (b, 0, c + z_off)),
                  pl.BlockSpec((1, L, LANES), lambda c, b: (b, 0, c + gate_off)),
                  pl.BlockSpec((1, KF_ROWS, LANES), lambda c, b: (order, 0, c)),
                  pl.BlockSpec((1, LANES), lambda c, b: (0, c)),
                  cst2(f1), cst3(g), cst2(finv), cst3(ginv)],
        out_specs=pl.BlockSpec((1, L, LANES), lambda c, b: (b, 0, c)),
        out_shape=jax.ShapeDtypeStruct((B, L, C), F32),
        scratch_shapes=[pltpu.VMEM((W_ROWS, LANES), F32), pltpu.VMEM((X_ROWS, LANES), F32),
                        pltpu.VMEM((CONV_GROUP, FFT_SLOT, LANES), BF16)],
        compiler_params=_cparams(("parallel", "parallel")),
        name=f"hyena_long_conv{order}",
    )(z_arr, gate_arr, kf_all, d.reshape(1, C), f1, g, finv, ginv)


def _mix_kernel(attn_ref, hyo_ref, h0_ref, mg_ref, w_ref, g_ref, b_ref, h1_ref, h1t_ref, *, alpha):
    def gnorm(x, goff):
        parts = []
        for gi in range(x.shape[1] // NORM_GROUP):
            xg = x[:, gi * NORM_GROUP:(gi + 1) * NORM_GROUP]
            ms = jnp.mean(xg * xg, -1, keepdims=True)
            gg = mg_ref[:, goff + gi * NORM_GROUP: goff + (gi + 1) * NORM_GROUP]
            parts.append((xg * lax.rsqrt(ms + 1e-6) * gg).astype(BF16))
        return jnp.concatenate(parts, axis=1)

    a = gnorm(attn_ref[...], 0)
    y = gnorm(hyo_ref[...], ATTN_WIDTH)
    mix = (jnp.dot(a, w_ref[0:ATTN_WIDTH, :], preferred_element_type=F32)
           + jnp.dot(y, w_ref[ATTN_WIDTH:, :], preferred_element_type=F32))
    x = alpha * h0_ref[...] + mix
    mu = jnp.mean(x, -1, keepdims=True)
    xc = x - mu
    var = jnp.mean(xc * xc, -1, keepdims=True)
    h1 = xc * lax.rsqrt(var + 1e-5) * g_ref[...] + b_ref[...]
    h1_ref[...] = h1
    h1t_ref[...] = h1.T.astype(BF16)


def _mix_out(attn, hyo, h0, mix_g, w_out_bf, ln_g, ln_b, alpha, tm=512):
    T, D = h0.shape
    kern = functools.partial(_mix_kernel, alpha=alpha)
    return pl.pallas_call(
        kern,
        grid=(T // tm,),
        in_specs=[pl.BlockSpec((tm, ATTN_WIDTH), lambda i: (i, 0)),
                  pl.BlockSpec((tm, HY_WIDTH), lambda i: (i, 0)),
                  pl.BlockSpec((tm, D), lambda i: (i, 0)),
                  pl.BlockSpec((1, D), lambda i: (0, 0)),
                  pl.BlockSpec((D, D), lambda i: (0, 0)),
                  pl.BlockSpec((1, D), lambda i: (0, 0)),
                  pl.BlockSpec((1, D), lambda i: (0, 0))],
        out_specs=[pl.BlockSpec((tm, D), lambda i: (i, 0)), pl.BlockSpec((D, tm), lambda i: (0, i))],
        out_shape=[jax.ShapeDtypeStruct((T, D), F32), jax.ShapeDtypeStruct((D, T), BF16)],
        compiler_params=_cparams(("parallel",)),
        name="mix_out_ln1",
    )(attn, hyo, h0, mix_g.reshape(1, D), w_out_bf, ln_g.reshape(1, D), ln_b.reshape(1, D))


def _oddeven_sort_pairs(n):
    pairs = []

    def merge(lo, m, r):
        step = r * 2
        if step < m:
            merge(lo, m, step)
            merge(lo + r, m, step)
            pairs.extend((i, i + r) for i in range(lo + r, lo + m - r, step))
        else:
            pairs.append((lo, lo + r))

    def sort(lo, m):
        if m > 1:
            sort(lo, m // 2)
            sort(lo + m // 2, m // 2)
            merge(lo, m, 1)

    sort(0, n)
    return pairs


def _bitonic_merge_pairs(n):
    pairs, d = [], n // 2
    while d >= 1:
        pairs.extend((i, i + d) for i in range(n) if (i & d) == 0)
        d //= 2
    return pairs


def _apply_network(xs, pairs):
    xs = list(xs)
    for i, j in pairs:
        xs[i], xs[j] = jnp.maximum(xs[i], xs[j]), jnp.minimum(xs[i], xs[j])
    return xs


def _top16_replicated(xs, live=PEER_TOPK):
    first = [p for p in _oddeven_sort_pairs(PEER_TOPK) if p[1] < live]
    xs = _apply_network(xs, first)
    merge = _bitonic_merge_pairs(PEER_TOPK)
    for shift in (4, 2, 1):
        other = [pltpu.roll(x, shift, 0) for x in xs]
        xs = [jnp.maximum(xs[i], other[PEER_TOPK - 1 - i]) for i in range(PEER_TOPK)]
        xs = _apply_network(xs, merge)
    return xs


def _pair_sum_candidates(v1, v2):
    sub = lax.broadcasted_iota(jnp.int32, v1[0].shape, 0)

    def by_sublane(vs):
        out = vs[SUBLANES - 1]
        for j in range(SUBLANES - 2, -1, -1):
            out = jnp.where(sub == j, vs[j], out)
        return out

    ninf = -jnp.inf
    v2lo, v2hi = by_sublane(v2[:SUBLANES]), by_sublane(v2[SUBLANES:])
    v1lo, v1hi = by_sublane(v1[:SUBLANES]), by_sublane(v1[SUBLANES:])
    cands = [v1[0] + v2lo, v1[0] + v2hi, v1[1] + v2lo]
    for a in (2, 3, 4):
        cands.append(jnp.where(sub < PEER_TOPK // (a + 1), v1[a] + v2lo, ninf))
    cands.append(jnp.where(sub >= 5, v1lo + v2[0], ninf))
    cands.append(v1hi + v2[0])
    cands.append(jnp.where(sub >= 5, v1lo + v2[1], ninf))
    pad = jnp.full(v1[0].shape, ninf, F32)
    return cands + [pad] * (PEER_TOPK - len(cands)), len(cands)


def _peer_score_kernel(wq_ref, h_ref, sk1_ref, sk2_ref, cnt_ref, r2_ref, e1_ref, e2_ref, q_ref):
    q_ref[...] = jnp.dot(wq_ref[...], h_ref[...], preferred_element_type=F32)

    def head(h, carry):
        base = pl.multiple_of(h * PEER_QDIM, PEER_QDIM)
        q1 = q_ref[pl.ds(base, PEER_QHALF), :].astype(BF16)
        q2 = q_ref[pl.ds(base + PEER_QHALF, PEER_QHALF), :].astype(BF16)
        s1 = jnp.dot(sk1_ref[...], q1, preferred_element_type=F32)
        s2 = jnp.dot(sk2_ref[...], q2, preferred_element_type=F32)
        slabs = lambda s: [s[SUBLANES * i:SUBLANES * (i + 1)] for i in range(N_KEYS // SUBLANES)]
        v1 = _top16_replicated(slabs(s1))
        v2 = _top16_replicated(slabs(s2))
        cands, n_live = _pair_sum_candidates(v1, v2)
        sc = _top16_replicated(cands, live=n_live)
        zsum = jnp.zeros_like(sc[0])
        for j in range(PEER_TOPK):
            zsum = zsum + jnp.exp(sc[j] - sc[0])
        row = lambda x: x[0:1]
        th = row(sc[PEER_TOPK - 1])
        cnt = jnp.zeros_like(s1)
        r2 = jnp.zeros_like(s2)
        for b in range(PEER_TOPK):
            cnt = cnt + jnp.where(s1 + row(v2[b]) >= th, 1.0, 0.0)
            r2 = r2 + jnp.where(row(v2[b]) > s2, 1.0, 0.0)
        def twice(x):
            bits = pltpu.bitcast(x.astype(BF16).astype(F32), jnp.uint32)
            return bits | (bits >> 16)

        cnt_w = twice(cnt)
        e1_w = twice(0.5 * jnp.exp(s1 - row(v1[0])) / row(zsum))
        for cc in range(cnt_ref.shape[1]):
            cnt_ref[h, cc] = cnt_w[:, cc * LANES:(cc + 1) * LANES]
            e1_ref[h, cc] = e1_w[:, cc * LANES:(cc + 1) * LANES]
        r2_ref[h] = pltpu.bitcast(r2.astype(BF16), jnp.uint32)
        e2_ref[h] = pltpu.bitcast(jnp.exp(s2 - row(v2[0])).astype(BF16), jnp.uint32)
        return carry

    lax.fori_loop(0, PEER_HEADS, head, 0)


def _peer_scores(wqT_bf, h1T_bf, sk1_bf, sk2_bf, tl=256):
    D, T = h1T_bf.shape
    Q = wqT_bf.shape[0]
    big = jax.ShapeDtypeStruct((PEER_HEADS, T // LANES, N_KEYS, LANES), jnp.uint32)
    big16 = jax.ShapeDtypeStruct((PEER_HEADS, N_KEYS // 2, T), jnp.uint32)
    bspec = pl.BlockSpec((PEER_HEADS, tl // LANES, N_KEYS, LANES), lambda i: (0, i, 0, 0))
    pspec = pl.BlockSpec((PEER_HEADS, N_KEYS // 2, tl), lambda i: (0, 0, i))
    return pl.pallas_call(
        _peer_score_kernel,
        grid=(T // tl,),
        in_specs=[pl.BlockSpec((Q, D), lambda i: (0, 0)),
                  pl.BlockSpec((D, tl), lambda i: (0, i)),
                  pl.BlockSpec((N_KEYS, PEER_QHALF), lambda i: (0, 0)),
                  pl.BlockSpec((N_KEYS, PEER_QHALF), lambda i: (0, 0))],
        out_specs=[bspec, pspec, bspec, pspec],
        out_shape=[big, big16, big, big16],
        scratch_shapes=[pltpu.VMEM((Q, tl), F32)],
        compiler_params=_cparams(("parallel",)),
        name="peer_scores_topk",
    )(wqT_bf, h1T_bf, sk1_bf, sk2_bf)


W_ROWS_PER_GROUP = 64


PEER_HALF = 4 * N_KEYS


def _peer_dense_kernel(hb_ref, u_ref, vt_ref, cnt_ref, r2_ref, e1_ref, e2_ref, o_ref, acc_ref,
                       a00, a01, a10, a11, w00, w01, w10, w11, *, te, tm, n_j, n_tiles):
    s = pl.program_id(0)
    sv = jnp.clip(s - 2, 0, n_tiles - 1)
    jv = sv % n_j
    jw = jnp.clip(s - 1, 0, n_tiles - 1) % n_j
    a_bufs = ((a00, a01), (a10, a11))
    w_bufs = ((w00, w01), (w10, w11))

    @pl.when(s == 0)
    def _():
        for ref in (a00, a01, a10, a11, w00, w01, w10, w11):
            ref[...] = jnp.zeros_like(ref)

    @pl.when(jv == 0)
    def _():
        acc_ref[...] = jnp.zeros_like(acc_ref)

    wide = 2 * LANES
    n_r, n_c = PEER_HALF // N_KEYS, tm // LANES
    key0 = pl.multiple_of(jw * (te // N_KEYS), SUBLANES)

    def half_body(half, a_cur, a_prv, w_cur, w_prv):
        def stage_a(q):
            ms = slice(half * PEER_HALF + (q // 2) * wide, half * PEER_HALF + (q // 2 + 1) * wide)
            ls = slice((q % 2) * wide, (q % 2 + 1) * wide)
            a_cur[(q // 2) * wide:(q // 2 + 1) * wide, ls] = jnp.dot(
                u_ref[ms, :], hb_ref[:, ls], preferred_element_type=F32)

        def stage_v(p):
            fs = slice((p // 2) * wide, (p // 2 + 1) * wide)
            ls = slice((p % 2) * wide, (p % 2 + 1) * wide)
            es = slice(half * PEER_HALF, (half + 1) * PEER_HALF)
            acc_ref[fs, ls] += jnp.dot(vt_ref[0, fs, es], w_cur[:, ls], preferred_element_type=F32)

        n_g = N_KEYS // W_ROWS_PER_GROUP
        grp = (W_ROWS_PER_GROUP, LANES)

        def stage_w(idx):
            p, g = idx // n_g, idx % n_g
            r, c = p // n_c, p % n_c
            krow = half * n_r + r
            cs = slice(c * LANES, (c + 1) * LANES)
            ks = slice(g * W_ROWS_PER_GROUP // 2, (g + 1) * W_ROWS_PER_GROUP // 2)
            gate = jnp.zeros(grp, BF16)
            pair = (W_ROWS_PER_GROUP // 2, LANES)
            for h in range(PEER_HEADS):
                cnt_row = cnt_ref[h, c, pl.ds(key0, SUBLANES), :][krow:krow + 1]
                e1_row = e1_ref[h, c, pl.ds(key0, SUBLANES), :][krow:krow + 1]
                cnt_b = pltpu.bitcast(jnp.broadcast_to(cnt_row, pair), BF16)
                e1_b = pltpu.bitcast(jnp.broadcast_to(e1_row, pair), BF16)
                sel = pltpu.bitcast(r2_ref[h, ks, cs], BF16) < cnt_b
                val = pltpu.bitcast(e2_ref[h, ks, cs], BF16) * e1_b
                gate = gate + jnp.where(sel, val, jnp.zeros(grp, BF16))
            ws = slice(r * N_KEYS + g * W_ROWS_PER_GROUP, r * N_KEYS + (g + 1) * W_ROWS_PER_GROUP)
            a = a_prv[ws, cs]
            act = a * (1.0 + lax.erf(a * (1.0 / math.sqrt(2.0))))
            w_prv[ws, cs] = gate * act.astype(BF16)

        per_v = n_r * n_c * n_g // 16
        for q in range(4):
            stage_a(q)
            for t in range(4):
                stage_v(4 * q + t)
                for gg in range(per_v):
                    stage_w((4 * q + t) * per_v + gg)

    for par in range(2):
        @pl.when(s % 2 == par)
        def _():
            for half in range(2):
                half_body(half, a_bufs[par][half], a_bufs[1 - par][half],
                          w_bufs[par][half], w_bufs[1 - par][half])

    @pl.when(jv == n_j - 1)
    def _():
        o_ref[...] = acc_ref[...]


def _peer_dense(h1T_bf, u_bf, v_bf, cnt, r2, e1, e2, tm=512, te=2 * PEER_HALF):
    D, T = h1T_bf.shape
    E = u_bf.shape[0]
    assert te == SUBLANES * N_KEYS and tm == 4 * LANES and D == 2048, \
        "the stage interleave is written for two 512-expert halves x 512 tokens"
    n_i, n_j = T // tm, E // te
    vT_bf = jnp.transpose(v_bf.reshape(n_j, te, D), (0, 2, 1))
    n_tiles = n_i * n_j
    kern = functools.partial(_peer_dense_kernel, te=te, tm=tm, n_j=n_j, n_tiles=n_tiles)
    ta = lambda s: jnp.minimum(s, n_tiles - 1)
    tw = lambda s: jnp.clip(s - 1, 0, n_tiles - 1)
    tv = lambda s: jnp.clip(s - 2, 0, n_tiles - 1)
    sspec = pl.BlockSpec((PEER_HEADS, N_KEYS // 2, tm), lambda s: (0, 0, tw(s) // n_j))
    xspec = pl.BlockSpec((PEER_HEADS, tm // LANES, N_KEYS, LANES), lambda s: (0, tw(s) // n_j, 0, 0))
    return pl.pallas_call(
        kern,
        grid=(n_tiles + 2,),
        in_specs=[pl.BlockSpec((D, tm), lambda s: (0, ta(s) // n_j)),
                  pl.BlockSpec((te, D), lambda s: (ta(s) % n_j, 0)),
                  pl.BlockSpec((1, D, te), lambda s: (tv(s) % n_j, 0, 0)),
                  xspec, sspec, xspec, sspec],
        out_specs=pl.BlockSpec((D, tm), lambda s: (0, tv(s) // n_j)),
        out_shape=jax.ShapeDtypeStruct((D, T), F32),
        scratch_shapes=([pltpu.VMEM((D, tm), F32)]
                        + [pltpu.VMEM((PEER_HALF, tm), F32)] * 4
                        + [pltpu.VMEM((PEER_HALF, tm), BF16)] * 4),
        compiler_params=_cparams(("arbitrary",)),
        name="peer_dense",
    )(h1T_bf, u_bf, vT_bf, cnt, r2, e1, e2)


def _res_ln_kernel(h_ref, ft_ref, g_ref, b_ref, o_ref, *, alpha):
    x = alpha * h_ref[...] + ft_ref[...].T
    mu = jnp.mean(x, -1, keepdims=True)
    xc = x - mu
    var = jnp.mean(xc * xc, -1, keepdims=True)
    o_ref[...] = xc * lax.rsqrt(var + 1e-5) * g_ref[...] + b_ref[...]


def _res_ln(h, fT, g, b, alpha, tm=512):
    T, D = h.shape
    row = pl.BlockSpec((tm, D), lambda i: (i, 0))
    vec = pl.BlockSpec((1, D), lambda i: (0, 0))
    return pl.pallas_call(
        functools.partial(_res_ln_kernel, alpha=alpha),
        grid=(T // tm,),
        in_specs=[row, pl.BlockSpec((D, tm), lambda i: (0, i)), vec, vec],
        out_specs=row,
        out_shape=jax.ShapeDtypeStruct((T, D), F32),
        compiler_params=_cparams(("parallel",)),
        name="res_ln2",
    )(h, fT, g.reshape(1, D), b.reshape(1, D))


def kernel(x, ln0_g, ln0_b, rel_bias, w_in, sink, conv_w, conv_b, f_w1, f_b1, f_freq1, f_w2, f_b2,
           f_freq2, f_w3, hy_bias, mix_norm_g, w_out, ln1_g, ln1_b, peer_wq, peer_subkeys, peer_u,
           peer_v, ln2_g, ln2_b):
    B, S, D = x.shape
    T = B * S
    alpha = (2.0 * DEPTH) ** 0.25
    consts = _dft_constants()

    qi = jnp.arange(BLOCK, dtype=jnp.int32)
    kj = jnp.arange(3 * BLOCK, dtype=jnp.int32)
    rel = kj[None, :] - BLOCK - qi[:, None]
    onehot = (_t5_bucket(rel)[..., None] == jnp.arange(N_BUCKETS, dtype=jnp.int32)).astype(F32)
    bias = jnp.einsum("qkb,bh->hqk", onehot, rel_bias.astype(F32), precision=lax.Precision.HIGHEST)

    h, h_bf = _ln0(x.reshape(T, D), ln0_g, ln0_b)
    for l in range(DEPTH):
        proj = _matmul(h_bf, w_in[l].astype(BF16), tm=1024, tn=1536)
        attn = _attention(proj, bias, sink[l], S)

        kfull = _filters(S, f_w1[l], f_b1[l], f_freq1[l], f_w2[l], f_b2[l], f_freq2[l], f_w3[l])
        kf = _filter_spectra(kfull, consts)
        u = _short_conv(proj.reshape(B, S, -1), conv_w[l], conv_b[l], ATTN_WIDTH + 2 * KV_WIDTH)
        nct = HY_WIDTH // LANES
        z1 = _long_conv(u, 0, u, nct, kf, 0, hy_bias[l, 0], consts)
        hyo = _long_conv(z1, 0, u, 2 * nct, kf, 1, hy_bias[l, 1], consts)

        h1, h1T_bf = _mix_out(attn, hyo.reshape(T, HY_WIDTH), h, mix_norm_g[l], w_out[l].astype(BF16),
                              ln1_g[l], ln1_b[l], alpha)
        cnt, r2, e1, e2 = _peer_scores(peer_wq[l].T.astype(BF16), h1T_bf,
                                       peer_subkeys[l, 0].astype(BF16), peer_subkeys[l, 1].astype(BF16))
        ffnT = _peer_dense(h1T_bf, peer_u[l].astype(BF16), peer_v[l].astype(BF16), cnt, r2, e1, e2)
        h = _res_ln(h1, ffnT, ln2_g[l], ln2_b[l], alpha)
        if l + 1 < DEPTH:
            h_bf = h.astype(BF16)
    return h.reshape(B, S, D)
```

```python
import functools
import math

import numpy as np
import jax
import jax.numpy as jnp
from jax import lax
from jax.experimental import pallas as pl
from jax.experimental.pallas import tpu as pltpu

F32 = jnp.float32
BF16 = jnp.bfloat16

D_MODEL = 2048
HEAD_DIM = 128
N_Q_HEADS = 8
N_KV_HEADS = 2
GQA_GROUP = N_Q_HEADS // N_KV_HEADS
ATTN_WIDTH = N_Q_HEADS * HEAD_DIM
KV_WIDTH = N_KV_HEADS * HEAD_DIM
WINDOW = 128
BLOCK = 128
N_BUCKETS = 32
MAX_DISTANCE = 128
HY_WIDTH = D_MODEL - ATTN_WIDTH
HY_ORDER = 2
POS_BANDS = 16
POS_EMB = 1 + 2 * POS_BANDS
FILTER_HIDDEN = 64
FAST_DECAY_PCT = 0.3
SLOW_DECAY_PCT = 1.5
DECAY_TARGET = 1e-2
NORM_GROUP = 128
N_KEYS = 128
PEER_HEADS = 8
PEER_QDIM = 256
PEER_QHALF = PEER_QDIM // 2
PEER_TOPK = 16
NEG = -1e30
DEPTH = 1

LANES = 128
VMEM_LIMIT = 56 * 1024 * 1024

FFT_N1 = 64
FFT_N2 = 128
FFT_K1 = FFT_N1 // 2 + 1
FFT_K1_PAD = 40
FFT_SLOT = 2 * FFT_N2
KF_ROWS = FFT_K1 * FFT_SLOT
SUBLANES = 8
W_PITCH = FFT_SLOT + SUBLANES
X_PITCH = FFT_N2 + SUBLANES
W_ROWS = FFT_K1_PAD * W_PITCH
X_ROWS = (FFT_N1 // 2) * X_PITCH


def _cparams(sem, vmem=VMEM_LIMIT):
    return pltpu.CompilerParams(dimension_semantics=sem, vmem_limit_bytes=vmem)


def _ln0_kernel(x_ref, g_ref, b_ref, h_ref, hb_ref):
    x = x_ref[...]
    mu = jnp.mean(x, -1, keepdims=True)
    xc = x - mu
    var = jnp.mean(xc * xc, -1, keepdims=True)
    y = xc * lax.rsqrt(var + 1e-5) * g_ref[...] + b_ref[...]
    h_ref[...] = y
    hb_ref[...] = y.astype(BF16)


def _ln0(x2d, g, b, tm=1024):
    T, D = x2d.shape
    return pl.pallas_call(
        _ln0_kernel,
        grid=(T // tm,),
        in_specs=[pl.BlockSpec((tm, D), lambda i: (i, 0)),
                  pl.BlockSpec((1, D), lambda i: (0, 0)),
                  pl.BlockSpec((1, D), lambda i: (0, 0))],
        out_specs=[pl.BlockSpec((tm, D), lambda i: (i, 0)),
                   pl.BlockSpec((tm, D), lambda i: (i, 0))],
        out_shape=[jax.ShapeDtypeStruct((T, D), F32), jax.ShapeDtypeStruct((T, D), BF16)],
        compiler_params=_cparams(("parallel",)),
        name="ln0",
    )(x2d, g.reshape(1, D), b.reshape(1, D))


def _mm_kernel(a_ref, b_ref, o_ref):
    o_ref[...] = jnp.dot(a_ref[...], b_ref[...], preferred_element_type=F32)


def _matmul(a, b, tm, tn):
    M, K = a.shape
    N = b.shape[1]
    return pl.pallas_call(
        _mm_kernel,
        grid=(N // tn, M // tm),
        in_specs=[pl.BlockSpec((tm, K), lambda j, i: (i, 0)),
                  pl.BlockSpec((K, tn), lambda j, i: (0, j))],
        out_specs=pl.BlockSpec((tm, tn), lambda j, i: (i, j)),
        out_shape=jax.ShapeDtypeStruct((M, N), F32),
        compiler_params=_cparams(("parallel", "parallel")),
        name="in_proj",
    )(a, b)


def _t5_bucket(rel):
    nb = N_BUCKETS // 2
    ret = (rel > 0).astype(jnp.int32) * nb
    n = jnp.abs(rel)
    max_exact = nb // 2
    nf = jnp.maximum(n, 1).astype(F32)
    large = max_exact + (jnp.log(nf / max_exact) / math.log(MAX_DISTANCE / max_exact)
                         * (nb - max_exact)).astype(jnp.int32)
    large = jnp.minimum(large, nb - 1)
    return ret + jnp.where(n < max_exact, n, large)


def _attn_kernel(sink_ref, q_ref, kp_ref, kc_ref, kn_ref, vp_ref, vc_ref, vn_ref, bias_ref, o_ref,
                 *, nb, seq):
    n = pl.program_id(0) % nb
    rows = GQA_GROUP * BLOCK
    row = lax.broadcasted_iota(jnp.int32, (rows, 3 * BLOCK), 0)
    kj = lax.broadcasted_iota(jnp.int32, (rows, 3 * BLOCK), 1)
    rel = kj - BLOCK - (row & (BLOCK - 1))
    kabs = n * BLOCK + kj - BLOCK
    valid = (jnp.abs(rel) <= WINDOW) & (kabs >= 0) & (kabs < seq)
    head_of_row = lax.broadcasted_iota(jnp.int32, (rows, 1), 0) // BLOCK
    k = jnp.concatenate([kp_ref[...], kc_ref[...], kn_ref[...]], axis=0).astype(BF16)
    v = jnp.concatenate([vp_ref[...], vc_ref[...], vn_ref[...]], axis=0).astype(BF16)
    scale = 1.0 / math.sqrt(HEAD_DIM)
    for g in range(N_KV_HEADS):
        kg = k[:, g * HEAD_DIM:(g + 1) * HEAD_DIM]
        vg = v[:, g * HEAD_DIM:(g + 1) * HEAD_DIM]
        h0 = g * GQA_GROUP
        qg = jnp.concatenate([q_ref[:, (h0 + r) * HEAD_DIM:(h0 + r + 1) * HEAD_DIM]
                              for r in range(GQA_GROUP)], axis=0).astype(BF16)
        s = lax.dot_general(qg, kg, (((1,), (1,)), ((), ())), preferred_element_type=F32) * scale
        bias_g = bias_ref[h0:h0 + GQA_GROUP].reshape(rows, 3 * BLOCK)
        s = jnp.where(valid, s + bias_g, NEG)
        sk = jnp.zeros((rows, 1), F32)
        for r in range(GQA_GROUP):
            sk = jnp.where(head_of_row == r, sink_ref[h0 + r], sk)
        m = jnp.maximum(jnp.max(s, -1, keepdims=True), sk)
        p = jnp.exp(s - m)
        denom = jnp.sum(p, -1, keepdims=True) + jnp.exp(sk - m)
        p = p / denom
        o = jnp.dot(p.astype(BF16), vg, preferred_element_type=F32)
        for r in range(GQA_GROUP):
            o_ref[:, (h0 + r) * HEAD_DIM:(h0 + r + 1) * HEAD_DIM] = o[r * BLOCK:(r + 1) * BLOCK]


def _attention(proj, bias, sink, seq):
    T = proj.shape[0]
    nb = seq // BLOCK
    kcol = ATTN_WIDTH // KV_WIDTH
    vcol = kcol + 1

    def prev(i):
        return i - jnp.where(i % nb == 0, 0, 1)

    def nxt(i):
        return i + jnp.where(i % nb == nb - 1, 0, 1)

    kern = functools.partial(_attn_kernel, nb=nb, seq=seq)
    return pl.pallas_call(
        kern,
        grid=(T // BLOCK,),
        in_specs=[pl.BlockSpec(memory_space=pltpu.SMEM),
                  pl.BlockSpec((BLOCK, ATTN_WIDTH), lambda i: (i, 0)),
                  pl.BlockSpec((BLOCK, KV_WIDTH), lambda i: (prev(i), kcol)),
                  pl.BlockSpec((BLOCK, KV_WIDTH), lambda i: (i, kcol)),
                  pl.BlockSpec((BLOCK, KV_WIDTH), lambda i: (nxt(i), kcol)),
                  pl.BlockSpec((BLOCK, KV_WIDTH), lambda i: (prev(i), vcol)),
                  pl.BlockSpec((BLOCK, KV_WIDTH), lambda i: (i, vcol)),
                  pl.BlockSpec((BLOCK, KV_WIDTH), lambda i: (nxt(i), vcol)),
                  pl.BlockSpec((N_Q_HEADS, BLOCK, 3 * BLOCK), lambda i: (0, 0, 0))],
        out_specs=pl.BlockSpec((BLOCK, ATTN_WIDTH), lambda i: (i, 0)),
        out_shape=jax.ShapeDtypeStruct((T, ATTN_WIDTH), F32),
        compiler_params=_cparams(("parallel",)),
        name="window_attn",
    )(sink, proj, proj, proj, proj, proj, proj, proj, bias)


def _filter_kernel(z_ref, w1_ref, b1_ref, fr1_ref, w2_ref, b2_ref, fr2_ref, w3_ref, dl_ref, o_ref,
                   *, tt, seq):
    hi = lax.Precision.HIGHEST
    a = jnp.dot(z_ref[...], w1_ref[...], preferred_element_type=F32, precision=hi) + b1_ref[...]
    hid = jnp.sin(fr1_ref[...] * a)
    a = jnp.dot(hid, w2_ref[...], preferred_element_type=F32, precision=hi) + b2_ref[...]
    hid = jnp.sin(fr2_ref[...] * a)
    h = jnp.dot(hid.astype(BF16), w3_ref[0], preferred_element_type=F32)
    j = lax.broadcasted_iota(jnp.int32, h.shape, 0) + pl.program_id(0) * tt
    lag = jnp.where(j < seq, j, (2 * seq - j) & (seq - 1))
    tn = lag.astype(F32) / float(max(seq - 1, 1))
    o_ref[...] = h * jnp.exp(-tn * dl_ref[...])


def _filters(seq, w1, b1, fr1, w2, b2, fr2, w3, tt=512):
    assert seq & (seq - 1) == 0
    t = np.arange(seq, dtype=np.float32)
    tn = t / np.float32(max(seq - 1, 1))
    w = (np.float32(2.0 * math.pi) * t / np.float32(seq))[:, None]
    bands = np.linspace(1e-4, POS_BANDS - 1, POS_BANDS, dtype=np.float32)
    z = np.concatenate([tn[:, None], np.cos(w * bands), -np.sin(w * bands)], -1).astype(np.float32)
    zp = np.pad(z, ((0, 0), (0, FILTER_HIDDEN - POS_EMB)))
    zfull = jnp.asarray(np.concatenate([zp, zp[0:1], zp[1:][::-1]], axis=0))
    w1p = jnp.pad(w1, ((0, FILTER_HIDDEN - POS_EMB), (0, 0)))
    max_decay = math.log(DECAY_TARGET) / FAST_DECAY_PCT
    min_decay = math.log(DECAY_TARGET) / SLOW_DECAY_PCT
    deltas = jnp.abs(jnp.linspace(min_decay, max_decay, HY_WIDTH, dtype=F32))
    ncol = HY_ORDER * HY_WIDTH
    dl = jnp.tile(deltas, HY_ORDER).reshape(1, ncol)
    H = FILTER_HIDDEN
    w3s = jnp.transpose(w3.reshape(H, HY_ORDER, 2, HY_WIDTH), (2, 0, 1, 3)).reshape(2, H, ncol).astype(BF16)
    half = seq // tt
    kern = functools.partial(_filter_kernel, tt=tt, seq=seq)
    full = lambda r, c: pl.BlockSpec((r, c), lambda i: (0, 0))
    return pl.pallas_call(
        kern,
        grid=(2 * half,),
        in_specs=[pl.BlockSpec((tt, H), lambda i: (i, 0)),
                  full(H, H), full(1, H), full(1, H), full(H, H), full(1, H), full(1, H),
                  pl.BlockSpec((1, H, ncol), lambda i: (i // half, 0, 0)), full(1, ncol)],
        out_specs=pl.BlockSpec((tt, ncol), lambda i: (i, 0)),
        out_shape=jax.ShapeDtypeStruct((2 * seq, ncol), F32),
        compiler_params=_cparams(("parallel",)),
        name="hyena_filter_mlp",
    )(zfull, w1p, b1.reshape(1, H), fr1.reshape(1, H), w2, b2.reshape(1, H), fr2.reshape(1, H), w3s, dl)


def _sconv_kernel(x_ref, w_ref, b_ref, o_ref):
    x = x_ref[0]
    L = x.shape[0]
    rows = lax.broadcasted_iota(jnp.int32, x.shape, 0)
    xm = jnp.where(rows == 0, 0.0, pltpu.roll(x, 1, 0))
    xp = jnp.where(rows == L - 1, 0.0, pltpu.roll(x, L - 1, 0))
    w = w_ref[...]
    o_ref[0] = xm * w[0:1] + x * w[1:2] + xp * w[2:3] + b_ref[...]


def _short_conv(proj3, conv_w, conv_b, col0, ct=256):
    B, L, _ = proj3.shape
    C = conv_w.shape[1]
    off = col0 // ct
    return pl.pallas_call(
        _sconv_kernel,
        grid=(B, C // ct),
        in_specs=[pl.BlockSpec((1, L, ct), lambda b, c: (b, 0, c + off)),
                  pl.BlockSpec((3, ct), lambda b, c: (0, c)),
                  pl.BlockSpec((1, ct), lambda b, c: (0, c))],
        out_specs=pl.BlockSpec((1, L, ct), lambda b, c: (b, 0, c)),
        out_shape=jax.ShapeDtypeStruct((B, L, C), F32),
        compiler_params=_cparams(("parallel", "parallel")),
        name="hyena_short_conv",
    )(proj3, conv_w, conv_b.reshape(1, C))


def _dft_constants():
    n1h = FFT_N1 // 2
    k1 = np.arange(FFT_K1_PAD)[:, None].astype(np.float64)
    n1 = np.arange(FFT_N1)[None, :].astype(np.float64)
    ang = 2.0 * np.pi * k1 * n1 / FFT_N1
    live = (np.arange(FFT_K1_PAD) < FFT_K1)[:, None]
    f1_full = np.concatenate([np.where(live, np.cos(ang), 0.0), np.where(live, -np.sin(ang), 0.0)], 0)
    f1 = f1_full[:, :n1h]
    kk1 = np.arange(FFT_K1)[:, None, None].astype(np.float64)
    k2 = np.arange(FFT_N2)[None, :, None].astype(np.float64)
    n2 = np.arange(FFT_N2)[None, None, :].astype(np.float64)
    phi = 2.0 * np.pi * (n2 * k2 / FFT_N2 + n2 * kk1 / (FFT_N1 * FFT_N2))
    c, s = np.cos(phi), np.sin(phi)
    g = np.concatenate([np.concatenate([c, s], 2), np.concatenate([-s, c], 2)], 1)
    ct_, st_ = np.transpose(c, (0, 2, 1)), np.transpose(s, (0, 2, 1))
    ginv = np.concatenate([np.concatenate([ct_, -st_], 2), np.concatenate([st_, ct_], 2)], 1)
    wk = np.where((np.arange(FFT_K1_PAD) == 0) | (np.arange(FFT_K1_PAD) == FFT_N1 // 2), 1.0, 2.0)
    wk = np.where(np.arange(FFT_K1_PAD) < FFT_K1, wk, 0.0)[None, :] / (FFT_N1 * FFT_N2)
    angi = 2.0 * np.pi * np.arange(n1h)[:, None] * np.arange(FFT_K1_PAD)[None, :] / FFT_N1
    finv = np.concatenate([wk * np.cos(angi), -wk * np.sin(angi)], 1)
    as_bf = lambda a: jnp.asarray(a.astype(np.float32)).astype(BF16)
    return as_bf(f1), as_bf(g), as_bf(finv), as_bf(ginv), as_bf(f1_full)


def _pad_rows_in(src_ref, xp_ref, n_blocks=FFT_N1 // 2, zero_row_of_block=None):
    for n1 in range(n_blocks):
        blk = src_ref[pl.ds(n1 * FFT_N2, FFT_N2), :]
        if n1 == zero_row_of_block:
            rows = lax.broadcasted_iota(jnp.int32, blk.shape, 0)
            blk = jnp.where(rows == 0, 0.0, blk)
        xp_ref[pl.ds(n1 * X_PITCH, FFT_N2), :] = blk


def _fft_stage1(xp_ref, w_ref, f1_ref):
    n1h = f1_ref.shape[1]
    kp = FFT_K1_PAD

    def body(i, carry):
        n2 = 2 * i
        xa = xp_ref[pl.ds(n2, n1h, stride=X_PITCH), :]
        xb = xp_ref[pl.ds(n2 + 1, n1h, stride=X_PITCH), :]
        xs = jnp.concatenate([xa, xb], axis=1).astype(BF16)
        r = jnp.dot(f1_ref[...], xs, preferred_element_type=F32)
        w_ref[pl.ds(n2, kp, stride=W_PITCH), :] = r[0:kp, 0:LANES]
        w_ref[pl.ds(n2 + 1, kp, stride=W_PITCH), :] = r[0:kp, LANES:2 * LANES]
        w_ref[pl.ds(FFT_N2 + n2, kp, stride=W_PITCH), :] = r[kp:2 * kp, 0:LANES]
        w_ref[pl.ds(FFT_N2 + n2 + 1, kp, stride=W_PITCH), :] = r[kp:2 * kp, LANES:2 * LANES]
        return carry

    lax.fori_loop(0, FFT_N2 // 2, body, 0, unroll=8)


def _kf_kernel(k_ref, f1_ref, g_ref, kf_ref, w_ref, xp_ref):
    scale = 1.0 / (jnp.sum(jnp.abs(k_ref[...]), 0, keepdims=True) + 1e-6)
    _pad_rows_in(k_ref, xp_ref, n_blocks=FFT_N1, zero_row_of_block=FFT_N1 // 2)
    _fft_stage1(xp_ref, w_ref, f1_ref)
    out = kf_ref.at[0]

    def body(k1, carry):
        src = pl.multiple_of(k1 * W_PITCH, 8)
        dst = pl.multiple_of(k1 * FFT_SLOT, FFT_SLOT)
        s = jnp.dot(g_ref[k1], w_ref[pl.ds(src, FFT_SLOT), :].astype(BF16), preferred_element_type=F32)
        out[pl.ds(dst, FFT_SLOT), :] = s * scale
        return carry

    lax.fori_loop(0, FFT_K1, body, 0, unroll=3)


def _filter_spectra(kfull, consts):
    L2 = kfull.shape[0]
    assert L2 == FFT_N1 * FFT_N2, "the DFT factorisation is written for 2L = 64 * 128"
    C = HY_WIDTH
    nct = C // LANES
    _, g, _, _, f1_full = consts
    return pl.pallas_call(
        _kf_kernel,
        grid=(HY_ORDER * nct,),
        in_specs=[pl.BlockSpec((L2, LANES), lambda j: (0, j)),
                  pl.BlockSpec(f1_full.shape, lambda j: (0, 0)),
                  pl.BlockSpec(g.shape, lambda j: (0, 0, 0))],
        out_specs=pl.BlockSpec((1, KF_ROWS, LANES), lambda j: (j // nct, 0, j % nct)),
        out_shape=jax.ShapeDtypeStruct((HY_ORDER, KF_ROWS, C), F32),
        scratch_shapes=[pltpu.VMEM((W_ROWS, LANES), F32), pltpu.VMEM((2 * X_ROWS, LANES), F32)],
        compiler_params=_cparams(("parallel",)),
        name="hyena_filter_fft",
    )(kfull, f1_full, g)


CONV_GROUP = 11


def _conv_kernel(z_ref, gate_ref, kf_ref, d_ref, f1_ref, g_ref, finv_ref, ginv_ref, o_ref,
                 w_ref, xp_ref, y_ref):
    zsrc = z_ref.at[0]
    kf = kf_ref.at[0]
    _pad_rows_in(zsrc, xp_ref)
    _fft_stage1(xp_ref, w_ref, f1_ref)

    def forward(k1):
        base = pl.multiple_of(k1 * W_PITCH, 8)
        kbase = pl.multiple_of(k1 * FFT_SLOT, FFT_SLOT)
        a = w_ref[pl.ds(base, FFT_SLOT), :].astype(BF16)
        s = jnp.dot(g_ref[k1], a, preferred_element_type=F32)
        sr, si = s[0:FFT_N2], s[FFT_N2:FFT_SLOT]
        kr = kf[pl.ds(kbase, FFT_N2), :]
        ki = kf[pl.ds(kbase + FFT_N2, FFT_N2), :]
        return jnp.concatenate([sr * kr - si * ki, sr * ki + si * kr], axis=0).astype(BF16)

    def inverse(k1, y):
        base = pl.multiple_of(k1 * W_PITCH, 8)
        w_ref[pl.ds(base, FFT_SLOT), :] = jnp.dot(ginv_ref[k1], y, preferred_element_type=F32)

    def step(i, do_forward, do_inverse):
        prev = [y_ref[k] for k in range(CONV_GROUP)] if do_inverse else None
        new = [forward(i * CONV_GROUP + k) for k in range(CONV_GROUP)] if do_forward else None
        if do_inverse:
            for k in range(CONV_GROUP):
                inverse((i - 1) * CONV_GROUP + k, prev[k])
        if do_forward:
            for k in range(CONV_GROUP):
                y_ref[k] = new[k]

    n_groups = FFT_K1 // CONV_GROUP
    step(0, True, False)

    def body(i, carry):
        step(i, True, True)
        return carry

    lax.fori_loop(1, n_groups, body, 0)
    step(n_groups, False, True)

    n1h = FFT_N1 // 2
    kp = FFT_K1_PAD

    def body2(i, carry):
        n2 = 2 * i
        zr = jnp.concatenate([w_ref[pl.ds(n2, kp, stride=W_PITCH), :],
                              w_ref[pl.ds(n2 + 1, kp, stride=W_PITCH), :]], axis=1)
        zi = jnp.concatenate([w_ref[pl.ds(FFT_N2 + n2, kp, stride=W_PITCH), :],
                              w_ref[pl.ds(FFT_N2 + n2 + 1, kp, stride=W_PITCH), :]], axis=1)
        zz = jnp.concatenate([zr, zi], axis=0).astype(BF16)
        x = jnp.dot(finv_ref[...], zz, preferred_element_type=F32)
        xp_ref[pl.ds(n2, n1h, stride=X_PITCH), :] = x[:, 0:LANES]
        xp_ref[pl.ds(n2 + 1, n1h, stride=X_PITCH), :] = x[:, LANES:2 * LANES]
        return carry

    lax.fori_loop(0, FFT_N2 // 2, body2, 0, unroll=8)
    d = d_ref[...]
    for n1 in range(n1h):
        rs = pl.ds(n1 * FFT_N2, FFT_N2)
        y = xp_ref[pl.ds(n1 * X_PITCH, FFT_N2), :]
        o_ref[0, rs, :] = gate_ref[0, rs, :] * (y + z_ref[0, rs, :] * d)


def _long_conv(z_arr, z_off, gate_arr, gate_off, kf_all, order, d, consts):
    B, L, _ = z_arr.shape
    assert 2 * L == FFT_N1 * FFT_N2, "the DFT factorisation is written for 2L = 64 * 128"
    C = HY_WIDTH
    nct = C // LANES
    f1, g, finv, ginv, _ = consts
    cst2 = lambda a: pl.BlockSpec(a.shape, lambda c, b: (0, 0))
    cst3 = lambda a: pl.BlockSpec(a.shape, lambda c, b: (0, 0, 0))
    return pl.pallas_call(
        _conv_kernel,
        grid=(nct, B),
        in_specs=[pl.BlockSpec((1, L, LANES), lambda c, b: (b, 0, c + z_off)),
                  pl.BlockSpec((1, L, LANES), lambda c, b: (b, 0, c + gate_off)),
                  pl.BlockSpec((1, KF_ROWS, LANES), lambda c, b: (order, 0, c)),
                  pl.BlockSpec((1, LANES), lambda c, b: (0, c)),
                  cst2(f1), cst3(g), cst2(finv), cst3(ginv)],
        out_specs=pl.BlockSpec((1, L, LANES), lambda c, b: (b, 0, c)),
        out_shape=jax.ShapeDtypeStruct((B, L, C), F32),
        scratch_shapes=[pltpu.VMEM((W_ROWS, LANES), F32), pltpu.VMEM((X_ROWS, LANES), F32),
                        pltpu.VMEM((CONV_GROUP, FFT_SLOT, LANES), BF16)],
        compiler_params=_cparams(("parallel", "parallel")),
        name=f"hyena_long_conv{order}",
    )(z_arr, gate_arr, kf_all, d.reshape(1, C), f1, g, finv, ginv)


def _mix_kernel(attn_ref, hyo_ref, h0_ref, mg_ref, w_ref, g_ref, b_ref, h1_ref, h1t_ref, *, alpha):
    def gnorm(x, goff):
        parts = []
        for gi in range(x.shape[1] // NORM_GROUP):
            xg = x[:, gi * NORM_GROUP:(gi + 1) * NORM_GROUP]
            ms = jnp.mean(xg * xg, -1, keepdims=True)
            gg = mg_ref[:, goff + gi * NORM_GROUP: goff + (gi + 1) * NORM_GROUP]
            parts.append((xg * lax.rsqrt(ms + 1e-6) * gg).astype(BF16))
        return jnp.concatenate(parts, axis=1)

    a = gnorm(attn_ref[...], 0)
    y = gnorm(hyo_ref[...], ATTN_WIDTH)
    mix = (jnp.dot(a, w_ref[0:ATTN_WIDTH, :], preferred_element_type=F32)
           + jnp.dot(y, w_ref[ATTN_WIDTH:, :], preferred_element_type=F32))
    x = alpha * h0_ref[...] + mix
    mu = jnp.mean(x, -1, keepdims=True)
    xc = x - mu
    var = jnp.mean(xc * xc, -1, keepdims=True)
    h1 = xc * lax.rsqrt(var + 1e-5) * g_ref[...] + b_ref[...]
    h1_ref[...] = h1
    h1t_ref[...] = h1.T.astype(BF16)


def _mix_out(attn, hyo, h0, mix_g, w_out_bf, ln_g, ln_b, alpha, tm=512):
    T, D = h0.shape
    kern = functools.partial(_mix_kernel, alpha=alpha)
    return pl.pallas_call(
        kern,
        grid=(T // tm,),
        in_specs=[pl.BlockSpec((tm, ATTN_WIDTH), lambda i: (i, 0)),
                  pl.BlockSpec((tm, HY_WIDTH), lambda i: (i, 0)),
                  pl.BlockSpec((tm, D), lambda i: (i, 0)),
                  pl.BlockSpec((1, D), lambda i: (0, 0)),
                  pl.BlockSpec((D, D), lambda i: (0, 0)),
                  pl.BlockSpec((1, D), lambda i: (0, 0)),
                  pl.BlockSpec((1, D), lambda i: (0, 0))],
        out_specs=[pl.BlockSpec((tm, D), lambda i: (i, 0)), pl.BlockSpec((D, tm), lambda i: (0, i))],
        out_shape=[jax.ShapeDtypeStruct((T, D), F32), jax.ShapeDtypeStruct((D, T), BF16)],
        compiler_params=_cparams(("parallel",)),
        name="mix_out_ln1",
    )(attn, hyo, h0, mix_g.reshape(1, D), w_out_bf, ln_g.reshape(1, D), ln_b.reshape(1, D))


def _oddeven_sort_pairs(n):
    pairs = []

    def merge(lo, m, r):
        step = r * 2
        if step < m:
            merge(lo, m, step)
            merge(lo + r, m, step)
            pairs.extend((i, i + r) for i in range(lo + r, lo + m - r, step))
        else:
            pairs.append((lo, lo + r))

    def sort(lo, m):
        if m > 1:
            sort(lo, m // 2)
            sort(lo + m // 2, m // 2)
            merge(lo, m, 1)

    sort(0, n)
    return pairs


def _bitonic_merge_pairs(n):
    pairs, d = [], n // 2
    while d >= 1:
        pairs.extend((i, i + d) for i in range(n) if (i & d) == 0)
        d //= 2
    return pairs


def _apply_network(xs, pairs):
    xs = list(xs)
    for i, j in pairs:
        xs[i], xs[j] = jnp.maximum(xs[i], xs[j]), jnp.minimum(xs[i], xs[j])
    return xs


def _top16_replicated(xs, live=PEER_TOPK):
    first = [p for p in _oddeven_sort_pairs(PEER_TOPK) if p[1] < live]
    xs = _apply_network(xs, first)
    merge = _bitonic_merge_pairs(PEER_TOPK)
    for shift in (4, 2, 1):
        other = [pltpu.roll(x, shift, 0) for x in xs]
        xs = [jnp.maximum(xs[i], other[PEER_TOPK - 1 - i]) for i in range(PEER_TOPK)]
        xs = _apply_network(xs, merge)
    return xs


def _pair_sum_candidates(v1, v2):
    sub = lax.broadcasted_iota(jnp.int32, v1[0].shape, 0)

    def by_sublane(vs):
        out = vs[SUBLANES - 1]
        for j in range(SUBLANES - 2, -1, -1):
            out = jnp.where(sub == j, vs[j], out)
        return out

    ninf = -jnp.inf
    v2lo, v2hi = by_sublane(v2[:SUBLANES]), by_sublane(v2[SUBLANES:])
    v1lo, v1hi = by_sublane(v1[:SUBLANES]), by_sublane(v1[SUBLANES:])
    cands = [v1[0] + v2lo, v1[0] + v2hi, v1[1] + v2lo]
    for a in (2, 3, 4):
        cands.append(jnp.where(sub < PEER_TOPK // (a + 1), v1[a] + v2lo, ninf))
    cands.append(jnp.where(sub >= 5, v1lo + v2[0], ninf))
    cands.append(v1hi + v2[0])
    cands.append(jnp.where(sub >= 5, v1lo + v2[1], ninf))
    pad = jnp.full(v1[0].shape, ninf, F32)
    return cands + [pad] * (PEER_TOPK - len(cands)), len(cands)


def _peer_score_kernel(wq_ref, h_ref, sk1_ref, sk2_ref, cnt_ref, r2_ref, e1_ref, e2_ref, q_ref, wqt_ref):
    @pl.when(pl.program_id(0) == 0)
    def _():
        chunk = 2 * LANES
        for c in range(wq_ref.shape[1] // chunk):
            wqt_ref[c * chunk:(c + 1) * chunk, :] = wq_ref[:, c * chunk:(c + 1) * chunk].T.astype(BF16)

    q_ref[...] = jnp.dot(wqt_ref[...], h_ref[...], preferred_element_type=F32)

    def head(h, carry):
        base = pl.multiple_of(h * PEER_QDIM, PEER_QDIM)
        q1 = q_ref[pl.ds(base, PEER_QHALF), :].astype(BF16)
        q2 = q_ref[pl.ds(base + PEER_QHALF, PEER_QHALF), :].astype(BF16)
        s1 = jnp.dot(sk1_ref[...], q1, preferred_element_type=F32)
        s2 = jnp.dot(sk2_ref[...], q2, preferred_element_type=F32)
        slabs = lambda s: [s[SUBLANES * i:SUBLANES * (i + 1)] for i in range(N_KEYS // SUBLANES)]
        v1 = _top16_replicated(slabs(s1))
        v2 = _top16_replicated(slabs(s2))
        cands, n_live = _pair_sum_candidates(v1, v2)
        sc = _top16_replicated(cands, live=n_live)
        zsum = jnp.zeros_like(sc[0])
        for j in range(PEER_TOPK):
            zsum = zsum + jnp.exp(sc[j] - sc[0])
        row = lambda x: x[0:1]
        th = row(sc[PEER_TOPK - 1])
        cnt = jnp.zeros_like(s1)
        r2 = jnp.zeros_like(s2)
        for b in range(PEER_TOPK):
            cnt = cnt + jnp.where(s1 + row(v2[b]) >= th, 1.0, 0.0)
            r2 = r2 + jnp.where(row(v2[b]) > s2, 1.0, 0.0)
        def twice(x):
            bits = pltpu.bitcast(x.astype(BF16).astype(F32), jnp.uint32)
            return bits | (bits >> 16)

        cnt_w = twice(cnt)
        e1_w = twice(0.5 * jnp.exp(s1 - row(v1[0])) / row(zsum))
        for cc in range(cnt_ref.shape[1]):
            cnt_ref[h, cc] = cnt_w[:, cc * LANES:(cc + 1) * LANES]
            e1_ref[h, cc] = e1_w[:, cc * LANES:(cc + 1) * LANES]
        r2_ref[h] = pltpu.bitcast(r2.astype(BF16), jnp.uint32)
        e2_ref[h] = pltpu.bitcast(jnp.exp(s2 - row(v2[0])).astype(BF16), jnp.uint32)
        return carry

    lax.fori_loop(0, PEER_HEADS, head, 0)


def _peer_scores(wq, h1T_bf, sk1_bf, sk2_bf, tl=256):
    D, T = h1T_bf.shape
    Q = wq.shape[1]
    big = jax.ShapeDtypeStruct((PEER_HEADS, T // LANES, N_KEYS, LANES), jnp.uint32)
    big16 = jax.ShapeDtypeStruct((PEER_HEADS, N_KEYS // 2, T), jnp.uint32)
    bspec = pl.BlockSpec((PEER_HEADS, tl // LANES, N_KEYS, LANES), lambda i: (0, i, 0, 0))
    pspec = pl.BlockSpec((PEER_HEADS, N_KEYS // 2, tl), lambda i: (0, 0, i))
    return pl.pallas_call(
        _peer_score_kernel,
        grid=(T // tl,),
        in_specs=[pl.BlockSpec((D, Q), lambda i: (0, 0), pipeline_mode=pl.Buffered(1)),
                  pl.BlockSpec((D, tl), lambda i: (0, i)),
                  pl.BlockSpec((N_KEYS, PEER_QHALF), lambda i: (0, 0)),
                  pl.BlockSpec((N_KEYS, PEER_QHALF), lambda i: (0, 0))],
        out_specs=[bspec, pspec, bspec, pspec],
        out_shape=[big, big16, big, big16],
        scratch_shapes=[pltpu.VMEM((Q, tl), F32), pltpu.VMEM((Q, D), BF16)],
        compiler_params=_cparams(("arbitrary",)),
        name="peer_scores_topk",
    )(wq, h1T_bf, sk1_bf, sk2_bf)


W_ROWS_PER_GROUP = 64


PEER_HALF = 4 * N_KEYS


def _peer_dense_kernel(hb_ref, u_ref, vt_ref, cnt_ref, r2_ref, e1_ref, e2_ref, o_ref, acc_ref,
                       a00, a01, a10, a11, w00, w01, w10, w11, *, te, tm, n_j, n_tiles):
    s = pl.program_id(0)
    sv = jnp.clip(s - 2, 0, n_tiles - 1)
    jv = sv % n_j
    jw = jnp.clip(s - 1, 0, n_tiles - 1) % n_j
    a_bufs = ((a00, a01), (a10, a11))
    w_bufs = ((w00, w01), (w10, w11))

    @pl.when(s == 0)
    def _():
        for ref in (a00, a01, a10, a11, w00, w01, w10, w11):
            ref[...] = jnp.zeros_like(ref)

    @pl.when(jv == 0)
    def _():
        acc_ref[...] = jnp.zeros_like(acc_ref)

    wide = 2 * LANES
    n_r, n_c = PEER_HALF // N_KEYS, tm // LANES
    key0 = pl.multiple_of(jw * (te // N_KEYS), SUBLANES)

    def half_body(half, a_cur, a_prv, w_cur, w_prv):
        def stage_a(q):
            ms = slice(half * PEER_HALF + (q // 2) * wide, half * PEER_HALF + (q // 2 + 1) * wide)
            ls = slice((q % 2) * wide, (q % 2 + 1) * wide)
            a_cur[(q // 2) * wide:(q // 2 + 1) * wide, ls] = jnp.dot(
                u_ref[ms, :], hb_ref[:, ls], preferred_element_type=F32)

        def stage_v(p):
            fs = slice((p // 2) * wide, (p // 2 + 1) * wide)
            ls = slice((p % 2) * wide, (p % 2 + 1) * wide)
            es = slice(half * PEER_HALF, (half + 1) * PEER_HALF)
            acc_ref[fs, ls] += jnp.dot(vt_ref[0, fs, es], w_cur[:, ls], preferred_element_type=F32)

        n_g = N_KEYS // W_ROWS_PER_GROUP
        grp = (W_ROWS_PER_GROUP, LANES)

        def stage_w(idx):
            p, g = idx // n_g, idx % n_g
            r, c = p // n_c, p % n_c
            krow = half * n_r + r
            cs = slice(c * LANES, (c + 1) * LANES)
            ks = slice(g * W_ROWS_PER_GROUP // 2, (g + 1) * W_ROWS_PER_GROUP // 2)
            gate = jnp.zeros(grp, BF16)
            pair = (W_ROWS_PER_GROUP // 2, LANES)
            for h in range(PEER_HEADS):
                cnt_row = cnt_ref[h, c, pl.ds(key0, SUBLANES), :][krow:krow + 1]
                e1_row = e1_ref[h, c, pl.ds(key0, SUBLANES), :][krow:krow + 1]
                cnt_b = pltpu.bitcast(jnp.broadcast_to(cnt_row, pair), BF16)
                e1_b = pltpu.bitcast(jnp.broadcast_to(e1_row, pair), BF16)
                sel = pltpu.bitcast(r2_ref[h, ks, cs], BF16) < cnt_b
                val = pltpu.bitcast(e2_ref[h, ks, cs], BF16) * e1_b
                gate = gate + jnp.where(sel, val, jnp.zeros(grp, BF16))
            ws = slice(r * N_KEYS + g * W_ROWS_PER_GROUP, r * N_KEYS + (g + 1) * W_ROWS_PER_GROUP)
            a = a_prv[ws, cs]
            act = a * (1.0 + lax.erf(a * (1.0 / math.sqrt(2.0))))
            w_prv[ws, cs] = gate * act.astype(BF16)

        per_v = n_r * n_c * n_g // 16
        for q in range(4):
            stage_a(q)
            for t in range(4):
                stage_v(4 * q + t)
                for gg in range(per_v):
                    stage_w((4 * q + t) * per_v + gg)

    for par in range(2):
        @pl.when(s % 2 == par)
        def _():
            for half in range(2):
                half_body(half, a_bufs[par][half], a_bufs[1 - par][half],
                          w_bufs[par][half], w_bufs[1 - par][half])

    @pl.when(jv == n_j - 1)
    def _():
        o_ref[...] = acc_ref[...]


def _peer_dense(h1T_bf, u_bf, v_bf, cnt, r2, e1, e2, tm=512, te=2 * PEER_HALF):
    D, T = h1T_bf.shape
    E = u_bf.shape[0]
    assert te == SUBLANES * N_KEYS and tm == 4 * LANES and D == 2048, \
        "the stage interleave is written for two 512-expert halves x 512 tokens"
    n_i, n_j = T // tm, E // te
    vT_bf = jnp.transpose(v_bf.reshape(n_j, te, D), (0, 2, 1))
    n_tiles = n_i * n_j
    kern = functools.partial(_peer_dense_kernel, te=te, tm=tm, n_j=n_j, n_tiles=n_tiles)
    ta = lambda s: jnp.minimum(s, n_tiles - 1)
    tw = lambda s: jnp.clip(s - 1, 0, n_tiles - 1)
    tv = lambda s: jnp.clip(s - 2, 0, n_tiles - 1)
    sspec = pl.BlockSpec((PEER_HEADS, N_KEYS // 2, tm), lambda s: (0, 0, tw(s) // n_j))
    xspec = pl.BlockSpec((PEER_HEADS, tm // LANES, N_KEYS, LANES), lambda s: (0, tw(s) // n_j, 0, 0))
    return pl.pallas_call(
        kern,
        grid=(n_tiles + 2,),
        in_specs=[pl.BlockSpec((D, tm), lambda s: (0, ta(s) // n_j)),
                  pl.BlockSpec((te, D), lambda s: (ta(s) % n_j, 0)),
                  pl.BlockSpec((1, D, te), lambda s: (tv(s) % n_j, 0, 0)),
                  xspec, sspec, xspec, sspec],
        out_specs=pl.BlockSpec((D, tm), lambda s: (0, tv(s) // n_j)),
        out_shape=jax.ShapeDtypeStruct((D, T), F32),
        scratch_shapes=([pltpu.VMEM((D, tm), F32)]
                        + [pltpu.VMEM((PEER_HALF, tm), F32)] * 4
                        + [pltpu.VMEM((PEER_HALF, tm), BF16)] * 4),
        compiler_params=_cparams(("arbitrary",)),
        name="peer_dense",
    )(h1T_bf, u_bf, vT_bf, cnt, r2, e1, e2)


def _res_ln_kernel(h_ref, ft_ref, g_ref, b_ref, o_ref, *, alpha):
    x = alpha * h_ref[...] + ft_ref[...].T
    mu = jnp.mean(x, -1, keepdims=True)
    xc = x - mu
    var = jnp.mean(xc * xc, -1, keepdims=True)
    o_ref[...] = xc * lax.rsqrt(var + 1e-5) * g_ref[...] + b_ref[...]


def _res_ln(h, fT, g, b, alpha, tm=512):
    T, D = h.shape
    row = pl.BlockSpec((tm, D), lambda i: (i, 0))
    vec = pl.BlockSpec((1, D), lambda i: (0, 0))
    return pl.pallas_call(
        functools.partial(_res_ln_kernel, alpha=alpha),
        grid=(T // tm,),
        in_specs=[row, pl.BlockSpec((D, tm), lambda i: (0, i)), vec, vec],
        out_specs=row,
        out_shape=jax.ShapeDtypeStruct((T, D), F32),
        compiler_params=_cparams(("parallel",)),
        name="res_ln2",
    )(h, fT, g.reshape(1, D), b.reshape(1, D))


def kernel(x, ln0_g, ln0_b, rel_bias, w_in, sink, conv_w, conv_b, f_w1, f_b1, f_freq1, f_w2, f_b2,
           f_freq2, f_w3, hy_bias, mix_norm_g, w_out, ln1_g, ln1_b, peer_wq, peer_subkeys, peer_u,
           peer_v, ln2_g, ln2_b):
    B, S, D = x.shape
    T = B * S
    alpha = (2.0 * DEPTH) ** 0.25
    consts = _dft_constants()

    qi = jnp.arange(BLOCK, dtype=jnp.int32)
    kj = jnp.arange(3 * BLOCK, dtype=jnp.int32)
    rel = kj[None, :] - BLOCK - qi[:, None]
    onehot = (_t5_bucket(rel)[..., None] == jnp.arange(N_BUCKETS, dtype=jnp.int32)).astype(F32)
    bias = jnp.einsum("qkb,bh->hqk", onehot, rel_bias.astype(F32), precision=lax.Precision.HIGHEST)

    h, h_bf = _ln0(x.reshape(T, D), ln0_g, ln0_b)
    for l in range(DEPTH):
        proj = _matmul(h_bf, w_in[l].astype(BF16), tm=1024, tn=1536)
        attn = _attention(proj, bias, sink[l], S)

        kfull = _filters(S, f_w1[l], f_b1[l], f_freq1[l], f_w2[l], f_b2[l], f_freq2[l], f_w3[l])
        kf = _filter_spectra(kfull, consts)
        u = _short_conv(proj.reshape(B, S, -1), conv_w[l], conv_b[l], ATTN_WIDTH + 2 * KV_WIDTH)
        nct = HY_WIDTH // LANES
        z1 = _long_conv(u, 0, u, nct, kf, 0, hy_bias[l, 0], consts)
        hyo = _long_conv(z1, 0, u, 2 * nct, kf, 1, hy_bias[l, 1], consts)

        h1, h1T_bf = _mix_out(attn, hyo.reshape(T, HY_WIDTH), h, mix_norm_g[l], w_out[l].astype(BF16),
                              ln1_g[l], ln1_b[l], alpha)
        cnt, r2, e1, e2 = _peer_scores(peer_wq[l], h1T_bf,
                                       peer_subkeys[l, 0].astype(BF16), peer_subkeys[l, 1].astype(BF16))
        ffnT = _peer_dense(h1T_bf, peer_u[l].astype(BF16), peer_v[l].astype(BF16), cnt, r2, e1, e2)
        h = _res_ln(h1, ffnT, ln2_g[l], ln2_b[l], alpha)
        if l + 1 < DEPTH:
            h_bf = h.astype(BF16)
    return h.reshape(B, S, D)
```

```python
import functools
import math

import numpy as np
import jax
import jax.numpy as jnp
from jax import lax
from jax.experimental import pallas as pl
from jax.experimental.pallas import tpu as pltpu

F32 = jnp.float32
BF16 = jnp.bfloat16

D_MODEL = 2048
HEAD_DIM = 128
N_Q_HEADS = 8
N_KV_HEADS = 2
GQA_GROUP = N_Q_HEADS // N_KV_HEADS
ATTN_WIDTH = N_Q_HEADS * HEAD_DIM
KV_WIDTH = N_KV_HEADS * HEAD_DIM
WINDOW = 128
BLOCK = 128
N_BUCKETS = 32
MAX_DISTANCE = 128
HY_WIDTH = D_MODEL - ATTN_WIDTH
HY_ORDER = 2
POS_BANDS = 16
POS_EMB = 1 + 2 * POS_BANDS
FILTER_HIDDEN = 64
FAST_DECAY_PCT = 0.3
SLOW_DECAY_PCT = 1.5
DECAY_TARGET = 1e-2
NORM_GROUP = 128
N_KEYS = 128
PEER_HEADS = 8
PEER_QDIM = 256
PEER_QHALF = PEER_QDIM // 2
PEER_TOPK = 16
NEG = -1e30
DEPTH = 1

LANES = 128
VMEM_LIMIT = 56 * 1024 * 1024

FFT_N1 = 64
FFT_N2 = 128
FFT_K1 = FFT_N1 // 2 + 1
FFT_K1_PAD = 40
FFT_SLOT = 2 * FFT_N2
KF_ROWS = FFT_K1 * FFT_SLOT
SUBLANES = 8
W_PITCH = FFT_SLOT + SUBLANES
X_PITCH = FFT_N2 + SUBLANES
W_ROWS = FFT_K1_PAD * W_PITCH
X_ROWS = (FFT_N1 // 2) * X_PITCH


def _cparams(sem, vmem=VMEM_LIMIT):
    return pltpu.CompilerParams(dimension_semantics=sem, vmem_limit_bytes=vmem)


def _ln0_kernel(x_ref, g_ref, b_ref, h_ref, hb_ref):
    x = x_ref[...]
    mu = jnp.mean(x, -1, keepdims=True)
    xc = x - mu
    var = jnp.mean(xc * xc, -1, keepdims=True)
    y = xc * lax.rsqrt(var + 1e-5) * g_ref[...] + b_ref[...]
    h_ref[...] = y
    hb_ref[...] = y.astype(BF16)


def _ln0(x2d, g, b, tm=1024):
    T, D = x2d.shape
    return pl.pallas_call(
        _ln0_kernel,
        grid=(T // tm,),
        in_specs=[pl.BlockSpec((tm, D), lambda i: (i, 0)),
                  pl.BlockSpec((1, D), lambda i: (0, 0)),
                  pl.BlockSpec((1, D), lambda i: (0, 0))],
        out_specs=[pl.BlockSpec((tm, D), lambda i: (i, 0)),
                   pl.BlockSpec((tm, D), lambda i: (i, 0))],
        out_shape=[jax.ShapeDtypeStruct((T, D), F32), jax.ShapeDtypeStruct((T, D), BF16)],
        compiler_params=_cparams(("parallel",)),
        name="ln0",
    )(x2d, g.reshape(1, D), b.reshape(1, D))


def _mm_kernel(a_ref, b_ref, o_ref):
    o_ref[...] = jnp.dot(a_ref[...], b_ref[...], preferred_element_type=F32)


def _matmul(a, b, tm, tn):
    M, K = a.shape
    N = b.shape[1]
    return pl.pallas_call(
        _mm_kernel,
        grid=(N // tn, M // tm),
        in_specs=[pl.BlockSpec((tm, K), lambda j, i: (i, 0)),
                  pl.BlockSpec((K, tn), lambda j, i: (0, j))],
        out_specs=pl.BlockSpec((tm, tn), lambda j, i: (i, j)),
        out_shape=jax.ShapeDtypeStruct((M, N), F32),
        compiler_params=_cparams(("parallel", "parallel")),
        name="in_proj",
    )(a, b)


def _t5_bucket(rel):
    nb = N_BUCKETS // 2
    ret = (rel > 0).astype(jnp.int32) * nb
    n = jnp.abs(rel)
    max_exact = nb // 2
    nf = jnp.maximum(n, 1).astype(F32)
    large = max_exact + (jnp.log(nf / max_exact) / math.log(MAX_DISTANCE / max_exact)
                         * (nb - max_exact)).astype(jnp.int32)
    large = jnp.minimum(large, nb - 1)
    return ret + jnp.where(n < max_exact, n, large)


def _attn_kernel(sink_ref, q_ref, kp_ref, kc_ref, kn_ref, vp_ref, vc_ref, vn_ref, bias_ref, o_ref,
                 *, nb, seq):
    n = pl.program_id(0) % nb
    rows = GQA_GROUP * BLOCK
    row = lax.broadcasted_iota(jnp.int32, (rows, 3 * BLOCK), 0)
    kj = lax.broadcasted_iota(jnp.int32, (rows, 3 * BLOCK), 1)
    rel = kj - BLOCK - (row & (BLOCK - 1))
    kabs = n * BLOCK + kj - BLOCK
    valid = (jnp.abs(rel) <= WINDOW) & (kabs >= 0) & (kabs < seq)
    head_of_row = lax.broadcasted_iota(jnp.int32, (rows, 1), 0) // BLOCK
    k = jnp.concatenate([kp_ref[...], kc_ref[...], kn_ref[...]], axis=0).astype(BF16)
    v = jnp.concatenate([vp_ref[...], vc_ref[...], vn_ref[...]], axis=0).astype(BF16)
    scale = 1.0 / math.sqrt(HEAD_DIM)
    for g in range(N_KV_HEADS):
        kg = k[:, g * HEAD_DIM:(g + 1) * HEAD_DIM]
        vg = v[:, g * HEAD_DIM:(g + 1) * HEAD_DIM]
        h0 = g * GQA_GROUP
        qg = jnp.concatenate([q_ref[:, (h0 + r) * HEAD_DIM:(h0 + r + 1) * HEAD_DIM]
                              for r in range(GQA_GROUP)], axis=0).astype(BF16)
        s = lax.dot_general(qg, kg, (((1,), (1,)), ((), ())), preferred_element_type=F32) * scale
        bias_g = bias_ref[h0:h0 + GQA_GROUP].reshape(rows, 3 * BLOCK)
        s = jnp.where(valid, s + bias_g, NEG)
        sk = jnp.zeros((rows, 1), F32)
        for r in range(GQA_GROUP):
            sk = jnp.where(head_of_row == r, sink_ref[h0 + r], sk)
        m = jnp.maximum(jnp.max(s, -1, keepdims=True), sk)
        p = jnp.exp(s - m)
        denom = jnp.sum(p, -1, keepdims=True) + jnp.exp(sk - m)
        p = p / denom
        o = jnp.dot(p.astype(BF16), vg, preferred_element_type=F32)
        for r in range(GQA_GROUP):
            o_ref[:, (h0 + r) * HEAD_DIM:(h0 + r + 1) * HEAD_DIM] = o[r * BLOCK:(r + 1) * BLOCK]


def _attention(proj, bias, sink, seq):
    T = proj.shape[0]
    nb = seq // BLOCK
    kcol = ATTN_WIDTH // KV_WIDTH
    vcol = kcol + 1

    def prev(i):
        return i - jnp.where(i % nb == 0, 0, 1)

    def nxt(i):
        return i + jnp.where(i % nb == nb - 1, 0, 1)

    kern = functools.partial(_attn_kernel, nb=nb, seq=seq)
    return pl.pallas_call(
        kern,
        grid=(T // BLOCK,),
        in_specs=[pl.BlockSpec(memory_space=pltpu.SMEM),
                  pl.BlockSpec((BLOCK, ATTN_WIDTH), lambda i: (i, 0)),
                  pl.BlockSpec((BLOCK, KV_WIDTH), lambda i: (prev(i), kcol)),
                  pl.BlockSpec((BLOCK, KV_WIDTH), lambda i: (i, kcol)),
                  pl.BlockSpec((BLOCK, KV_WIDTH), lambda i: (nxt(i), kcol)),
                  pl.BlockSpec((BLOCK, KV_WIDTH), lambda i: (prev(i), vcol)),
                  pl.BlockSpec((BLOCK, KV_WIDTH), lambda i: (i, vcol)),
                  pl.BlockSpec((BLOCK, KV_WIDTH), lambda i: (nxt(i), vcol)),
                  pl.BlockSpec((N_Q_HEADS, BLOCK, 3 * BLOCK), lambda i: (0, 0, 0))],
        out_specs=pl.BlockSpec((BLOCK, ATTN_WIDTH), lambda i: (i, 0)),
        out_shape=jax.ShapeDtypeStruct((T, ATTN_WIDTH), F32),
        compiler_params=_cparams(("parallel",)),
        name="window_attn",
    )(sink, proj, proj, proj, proj, proj, proj, proj, bias)


def _filter_kernel(z_ref, w1_ref, b1_ref, fr1_ref, w2_ref, b2_ref, fr2_ref, w3_ref, dl_ref, o_ref,
                   *, tt, seq):
    hi = lax.Precision.HIGHEST
    a = jnp.dot(z_ref[...], w1_ref[...], preferred_element_type=F32, precision=hi) + b1_ref[...]
    hid = jnp.sin(fr1_ref[...] * a)
    a = jnp.dot(hid, w2_ref[...], preferred_element_type=F32, precision=hi) + b2_ref[...]
    hid = jnp.sin(fr2_ref[...] * a)
    h = jnp.dot(hid.astype(BF16), w3_ref[0], preferred_element_type=F32)
    j = lax.broadcasted_iota(jnp.int32, h.shape, 0) + pl.program_id(0) * tt
    lag = jnp.where(j < seq, j, (2 * seq - j) & (seq - 1))
    tn = lag.astype(F32) / float(max(seq - 1, 1))
    o_ref[...] = h * jnp.exp(-tn * dl_ref[...])


def _filters(seq, w1, b1, fr1, w2, b2, fr2, w3, tt=512):
    assert seq & (seq - 1) == 0
    t = np.arange(seq, dtype=np.float32)
    tn = t / np.float32(max(seq - 1, 1))
    w = (np.float32(2.0 * math.pi) * t / np.float32(seq))[:, None]
    bands = np.linspace(1e-4, POS_BANDS - 1, POS_BANDS, dtype=np.float32)
    z = np.concatenate([tn[:, None], np.cos(w * bands), -np.sin(w * bands)], -1).astype(np.float32)
    zp = np.pad(z, ((0, 0), (0, FILTER_HIDDEN - POS_EMB)))
    zfull = jnp.asarray(np.concatenate([zp, zp[0:1], zp[1:][::-1]], axis=0))
    w1p = jnp.pad(w1, ((0, FILTER_HIDDEN - POS_EMB), (0, 0)))
    max_decay = math.log(DECAY_TARGET) / FAST_DECAY_PCT
    min_decay = math.log(DECAY_TARGET) / SLOW_DECAY_PCT
    deltas = jnp.abs(jnp.linspace(min_decay, max_decay, HY_WIDTH, dtype=F32))
    ncol = HY_ORDER * HY_WIDTH
    dl = jnp.tile(deltas, HY_ORDER).reshape(1, ncol)
    H = FILTER_HIDDEN
    w3s = jnp.transpose(w3.reshape(H, HY_ORDER, 2, HY_WIDTH), (2, 0, 1, 3)).reshape(2, H, ncol).astype(BF16)
    half = seq // tt
    kern = functools.partial(_filter_kernel, tt=tt, seq=seq)
    full = lambda r, c: pl.BlockSpec((r, c), lambda i: (0, 0))
    return pl.pallas_call(
        kern,
        grid=(2 * half,),
        in_specs=[pl.BlockSpec((tt, H), lambda i: (i, 0)),
                  full(H, H), full(1, H), full(1, H), full(H, H), full(1, H), full(1, H),
                  pl.BlockSpec((1, H, ncol), lambda i: (i // half, 0, 0)), full(1, ncol)],
        out_specs=pl.BlockSpec((tt, ncol), lambda i: (i, 0)),
        out_shape=jax.ShapeDtypeStruct((2 * seq, ncol), F32),
        compiler_params=_cparams(("parallel",)),
        name="hyena_filter_mlp",
    )(zfull, w1p, b1.reshape(1, H), fr1.reshape(1, H), w2, b2.reshape(1, H), fr2.reshape(1, H), w3s, dl)


def _sconv_kernel(x_ref, w_ref, b_ref, o_ref):
    x = x_ref[0]
    L = x.shape[0]
    rows = lax.broadcasted_iota(jnp.int32, x.shape, 0)
    xm = jnp.where(rows == 0, 0.0, pltpu.roll(x, 1, 0))
    xp = jnp.where(rows == L - 1, 0.0, pltpu.roll(x, L - 1, 0))
    w = w_ref[...]
    o_ref[0] = xm * w[0:1] + x * w[1:2] + xp * w[2:3] + b_ref[...]


def _short_conv(proj3, conv_w, conv_b, col0, ct=256):
    B, L, _ = proj3.shape
    C = conv_w.shape[1]
    off = col0 // ct
    return pl.pallas_call(
        _sconv_kernel,
        grid=(B, C // ct),
        in_specs=[pl.BlockSpec((1, L, ct), lambda b, c: (b, 0, c + off)),
                  pl.BlockSpec((3, ct), lambda b, c: (0, c)),
                  pl.BlockSpec((1, ct), lambda b, c: (0, c))],
        out_specs=pl.BlockSpec((1, L, ct), lambda b, c: (b, 0, c)),
        out_shape=jax.ShapeDtypeStruct((B, L, C), F32),
        compiler_params=_cparams(("parallel", "parallel")),
        name="hyena_short_conv",
    )(proj3, conv_w, conv_b.reshape(1, C))


def _dft_constants():
    n1h = FFT_N1 // 2
    k1 = np.arange(FFT_K1_PAD)[:, None].astype(np.float64)
    n1 = np.arange(FFT_N1)[None, :].astype(np.float64)
    ang = 2.0 * np.pi * k1 * n1 / FFT_N1
    live = (np.arange(FFT_K1_PAD) < FFT_K1)[:, None]
    f1_full = np.concatenate([np.where(live, np.cos(ang), 0.0), np.where(live, -np.sin(ang), 0.0)], 0)
    f1 = f1_full[:, :n1h]
    kk1 = np.arange(FFT_K1)[:, None, None].astype(np.float64)
    k2 = np.arange(FFT_N2)[None, :, None].astype(np.float64)
    n2 = np.arange(FFT_N2)[None, None, :].astype(np.float64)
    phi = 2.0 * np.pi * (n2 * k2 / FFT_N2 + n2 * kk1 / (FFT_N1 * FFT_N2))
    c, s = np.cos(phi), np.sin(phi)
    g = np.concatenate([np.concatenate([c, s], 2), np.concatenate([-s, c], 2)], 1)
    ct_, st_ = np.transpose(c, (0, 2, 1)), np.transpose(s, (0, 2, 1))
    ginv = np.concatenate([np.concatenate([ct_, -st_], 2), np.concatenate([st_, ct_], 2)], 1)
    wk = np.where((np.arange(FFT_K1_PAD) == 0) | (np.arange(FFT_K1_PAD) == FFT_N1 // 2), 1.0, 2.0)
    wk = np.where(np.arange(FFT_K1_PAD) < FFT_K1, wk, 0.0)[None, :] / (FFT_N1 * FFT_N2)
    angi = 2.0 * np.pi * np.arange(n1h)[:, None] * np.arange(FFT_K1_PAD)[None, :] / FFT_N1
    finv = np.concatenate([wk * np.cos(angi), -wk * np.sin(angi)], 1)
    as_bf = lambda a: jnp.asarray(a.astype(np.float32)).astype(BF16)
    return as_bf(f1), as_bf(g), as_bf(finv), as_bf(ginv), as_bf(f1_full)


def _pad_rows_in(src_ref, xp_ref, n_blocks=FFT_N1 // 2, zero_row_of_block=None):
    for n1 in range(n_blocks):
        blk = src_ref[pl.ds(n1 * FFT_N2, FFT_N2), :]
        if n1 == zero_row_of_block:
            rows = lax.broadcasted_iota(jnp.int32, blk.shape, 0)
            blk = jnp.where(rows == 0, 0.0, blk)
        xp_ref[pl.ds(n1 * X_PITCH, FFT_N2), :] = blk


def _fft_stage1(xp_ref, w_ref, f1_ref):
    n1h = f1_ref.shape[1]
    kp = FFT_K1_PAD

    def body(i, carry):
        n2 = 2 * i
        xa = xp_ref[pl.ds(n2, n1h, stride=X_PITCH), :]
        xb = xp_ref[pl.ds(n2 + 1, n1h, stride=X_PITCH), :]
        xs = jnp.concatenate([xa, xb], axis=1).astype(BF16)
        r = jnp.dot(f1_ref[...], xs, preferred_element_type=F32)
        w_ref[pl.ds(n2, kp, stride=W_PITCH), :] = r[0:kp, 0:LANES]
        w_ref[pl.ds(n2 + 1, kp, stride=W_PITCH), :] = r[0:kp, LANES:2 * LANES]
        w_ref[pl.ds(FFT_N2 + n2, kp, stride=W_PITCH), :] = r[kp:2 * kp, 0:LANES]
        w_ref[pl.ds(FFT_N2 + n2 + 1, kp, stride=W_PITCH), :] = r[kp:2 * kp, LANES:2 * LANES]
        return carry

    lax.fori_loop(0, FFT_N2 // 2, body, 0, unroll=8)


def _kf_kernel(k_ref, f1_ref, g_ref, kf_ref, w_ref, xp_ref):
    scale = 1.0 / (jnp.sum(jnp.abs(k_ref[...]), 0, keepdims=True) + 1e-6)
    _pad_rows_in(k_ref, xp_ref, n_blocks=FFT_N1, zero_row_of_block=FFT_N1 // 2)
    _fft_stage1(xp_ref, w_ref, f1_ref)
    out = kf_ref.at[0]

    def body(k1, carry):
        src = pl.multiple_of(k1 * W_PITCH, 8)
        dst = pl.multiple_of(k1 * FFT_SLOT, FFT_SLOT)
        s = jnp.dot(g_ref[k1], w_ref[pl.ds(src, FFT_SLOT), :].astype(BF16), preferred_element_type=F32)
        out[pl.ds(dst, FFT_SLOT), :] = s * scale
        return carry

    lax.fori_loop(0, FFT_K1, body, 0, unroll=3)


def _filter_spectra(kfull, consts):
    L2 = kfull.shape[0]
    assert L2 == FFT_N1 * FFT_N2, "the DFT factorisation is written for 2L = 64 * 128"
    C = HY_WIDTH
    nct = C // LANES
    _, g, _, _, f1_full = consts
    return pl.pallas_call(
        _kf_kernel,
        grid=(HY_ORDER * nct,),
        in_specs=[pl.BlockSpec((L2, LANES), lambda j: (0, j)),
                  pl.BlockSpec(f1_full.shape, lambda j: (0, 0)),
                  pl.BlockSpec(g.shape, lambda j: (0, 0, 0))],
        out_specs=pl.BlockSpec((1, KF_ROWS, LANES), lambda j: (j // nct, 0, j % nct)),
        out_shape=jax.ShapeDtypeStruct((HY_ORDER, KF_ROWS, C), F32),
        scratch_shapes=[pltpu.VMEM((W_ROWS, LANES), F32), pltpu.VMEM((2 * X_ROWS, LANES), F32)],
        compiler_params=_cparams(("parallel",)),
        name="hyena_filter_fft",
    )(kfull, f1_full, g)


CONV_GROUP = 11


def _conv_kernel(z_ref, gate_ref, kf_ref, d_ref, f1_ref, g_ref, finv_ref, ginv_ref, o_ref,
                 w_ref, xp_ref, y_ref):
    zsrc = z_ref.at[0]
    kf = kf_ref.at[0]
    _pad_rows_in(zsrc, xp_ref)
    _fft_stage1(xp_ref, w_ref, f1_ref)

    def forward(k1):
        base = pl.multiple_of(k1 * W_PITCH, 8)
        kbase = pl.multiple_of(k1 * FFT_SLOT, FFT_SLOT)
        a = w_ref[pl.ds(base, FFT_SLOT), :].astype(BF16)
        s = jnp.dot(g_ref[k1], a, preferred_element_type=F32)
        sr, si = s[0:FFT_N2], s[FFT_N2:FFT_SLOT]
        kr = kf[pl.ds(kbase, FFT_N2), :]
        ki = kf[pl.ds(kbase + FFT_N2, FFT_N2), :]
        return jnp.concatenate([sr * kr - si * ki, sr * ki + si * kr], axis=0).astype(BF16)

    def inverse(k1, y):
        base = pl.multiple_of(k1 * W_PITCH, 8)
        w_ref[pl.ds(base, FFT_SLOT), :] = jnp.dot(ginv_ref[k1], y, preferred_element_type=F32)

    def step(i, do_forward, do_inverse):
        prev = [y_ref[k] for k in range(CONV_GROUP)] if do_inverse else None
        new = [forward(i * CONV_GROUP + k) for k in range(CONV_GROUP)] if do_forward else None
        if do_inverse:
            for k in range(CONV_GROUP):
                inverse((i - 1) * CONV_GROUP + k, prev[k])
        if do_forward:
            for k in range(CONV_GROUP):
                y_ref[k] = new[k]

    n_groups = FFT_K1 // CONV_GROUP
    step(0, True, False)

    def body(i, carry):
        step(i, True, True)
        return carry

    lax.fori_loop(1, n_groups, body, 0)
    step(n_groups, False, True)

    n1h = FFT_N1 // 2
    kp = FFT_K1_PAD

    def body2(i, carry):
        n2 = 2 * i
        zr = jnp.concatenate([w_ref[pl.ds(n2, kp, stride=W_PITCH), :],
                              w_ref[pl.ds(n2 + 1, kp, stride=W_PITCH), :]], axis=1)
        zi = jnp.concatenate([w_ref[pl.ds(FFT_N2 + n2, kp, stride=W_PITCH), :],
                              w_ref[pl.ds(FFT_N2 + n2 + 1, kp, stride=W_PITCH), :]], axis=1)
        zz = jnp.concatenate([zr, zi], axis=0).astype(BF16)
        x = jnp.dot(finv_ref[...], zz, preferred_element_type=F32)
        xp_ref[pl.ds(n2, n1h, stride=X_PITCH), :] = x[:, 0:LANES]
        xp_ref[pl.ds(n2 + 1, n1h, stride=X_PITCH), :] = x[:, LANES:2 * LANES]
        return carry

    lax.fori_loop(0, FFT_N2 // 2, body2, 0, unroll=8)
    d = d_ref[...]
    for n1 in range(n1h):
        rs = pl.ds(n1 * FFT_N2, FFT_N2)
        y = xp_ref[pl.ds(n1 * X_PITCH, FFT_N2), :]
        o_ref[0, rs, :] = gate_ref[0, rs, :] * (y + z_ref[0, rs, :] * d)


def _long_conv(z_arr, z_off, gate_arr, gate_off, kf_all, order, d, consts):
    B, L, _ = z_arr.shape
    assert 2 * L == FFT_N1 * FFT_N2, "the DFT factorisation is written for 2L = 64 * 128"
    C = HY_WIDTH
    nct = C // LANES
    f1, g, finv, ginv, _ = consts
    cst2 = lambda a: pl.BlockSpec(a.shape, lambda c, b: (0, 0))
    cst3 = lambda a: pl.BlockSpec(a.shape, lambda c, b: (0, 0, 0))
    return pl.pallas_call(
        _conv_kernel,
        grid=(nct, B),
        in_specs=[pl.BlockSpec((1, L, LANES), lambda c, b: (b, 0, c + z_off)),
                  pl.BlockSpec((1, L, LANES), lambda c, b: (b, 0, c + gate_off)),
                  pl.BlockSpec((1, KF_ROWS, LANES), lambda c, b: (order, 0, c)),
                  pl.BlockSpec((1, LANES), lambda c, b: (0, c)),
                  cst2(f1), cst3(g), cst2(finv), cst3(ginv)],
        out_specs=pl.BlockSpec((1, L, LANES), lambda c, b: (b, 0, c)),
        out_shape=jax.ShapeDtypeStruct((B, L, C), F32),
        scratch_shapes=[pltpu.VMEM((W_ROWS, LANES), F32), pltpu.VMEM((X_ROWS, LANES), F32),
                        pltpu.VMEM((CONV_GROUP, FFT_SLOT, LANES), BF16)],
        compiler_params=_cparams(("parallel", "parallel")),
        name=f"hyena_long_conv{order}",
    )(z_arr, gate_arr, kf_all, d.reshape(1, C), f1, g, finv, ginv)


def _mix_kernel(attn_ref, hyo_ref, h0_ref, mg_ref, w_ref, g_ref, b_ref, h1_ref, h1t_ref, *, alpha):
    def gnorm(x, goff):
        parts = []
        for gi in range(x.shape[1] // NORM_GROUP):
            xg = x[:, gi * NORM_GROUP:(gi + 1) * NORM_GROUP]
            ms = jnp.mean(xg * xg, -1, keepdims=True)
            gg = mg_ref[:, goff + gi * NORM_GROUP: goff + (gi + 1) * NORM_GROUP]
            parts.append((xg * lax.rsqrt(ms + 1e-6) * gg).astype(BF16))
        return jnp.concatenate(parts, axis=1)

    half = attn_ref.shape[0] // 2
    for p in range(2):
        rs = slice(p * half, (p + 1) * half)
        a = gnorm(attn_ref[rs, :], 0)
        y = gnorm(hyo_ref[rs, :], ATTN_WIDTH)
        mix = (jnp.dot(a, w_ref[0:ATTN_WIDTH, :], preferred_element_type=F32)
               + jnp.dot(y, w_ref[ATTN_WIDTH:, :], preferred_element_type=F32))
        x = alpha * h0_ref[rs, :] + mix
        mu = jnp.mean(x, -1, keepdims=True)
        xc = x - mu
        var = jnp.mean(xc * xc, -1, keepdims=True)
        h1 = xc * lax.rsqrt(var + 1e-5) * g_ref[...] + b_ref[...]
        h1_ref[rs, :] = h1
        h1t_ref[:, rs] = h1.T.astype(BF16)


def _mix_out(attn, hyo, h0, mix_g, w_out_bf, ln_g, ln_b, alpha, tm=512):
    T, D = h0.shape
    kern = functools.partial(_mix_kernel, alpha=alpha)
    return pl.pallas_call(
        kern,
        grid=(T // tm,),
        in_specs=[pl.BlockSpec((tm, ATTN_WIDTH), lambda i: (i, 0)),
                  pl.BlockSpec((tm, HY_WIDTH), lambda i: (i, 0)),
                  pl.BlockSpec((tm, D), lambda i: (i, 0)),
                  pl.BlockSpec((1, D), lambda i: (0, 0)),
                  pl.BlockSpec((D, D), lambda i: (0, 0)),
                  pl.BlockSpec((1, D), lambda i: (0, 0)),
                  pl.BlockSpec((1, D), lambda i: (0, 0))],
        out_specs=[pl.BlockSpec((tm, D), lambda i: (i, 0)), pl.BlockSpec((D, tm), lambda i: (0, i))],
        out_shape=[jax.ShapeDtypeStruct((T, D), F32), jax.ShapeDtypeStruct((D, T), BF16)],
        compiler_params=_cparams(("parallel",)),
        name="mix_out_ln1",
    )(attn, hyo, h0, mix_g.reshape(1, D), w_out_bf, ln_g.reshape(1, D), ln_b.reshape(1, D))


def _oddeven_sort_pairs(n):
    pairs = []

    def merge(lo, m, r):
        step = r * 2
        if step < m:
            merge(lo, m, step)
            merge(lo + r, m, step)
            pairs.extend((i, i + r) for i in range(lo + r, lo + m - r, step))
        else:
            pairs.append((lo, lo + r))

    def sort(lo, m):
        if m > 1:
            sort(lo, m // 2)
            sort(lo + m // 2, m // 2)
            merge(lo, m, 1)

    sort(0, n)
    return pairs


def _bitonic_merge_pairs(n):
    pairs, d = [], n // 2
    while d >= 1:
        pairs.extend((i, i + d) for i in range(n) if (i & d) == 0)
        d //= 2
    return pairs


def _apply_network(xs, pairs):
    xs = list(xs)
    for i, j in pairs:
        xs[i], xs[j] = jnp.maximum(xs[i], xs[j]), jnp.minimum(xs[i], xs[j])
    return xs


def _top16_replicated(xs, live=PEER_TOPK):
    first = [p for p in _oddeven_sort_pairs(PEER_TOPK) if p[1] < live]
    xs = _apply_network(xs, first)
    merge = _bitonic_merge_pairs(PEER_TOPK)
    for shift in (4, 2, 1):
        other = [pltpu.roll(x, shift, 0) for x in xs]
        xs = [jnp.maximum(xs[i], other[PEER_TOPK - 1 - i]) for i in range(PEER_TOPK)]
        xs = _apply_network(xs, merge)
    return xs


def _pair_sum_candidates(v1, v2):
    sub = lax.broadcasted_iota(jnp.int32, v1[0].shape, 0)

    def by_sublane(vs):
        out = vs[SUBLANES - 1]
        for j in range(SUBLANES - 2, -1, -1):
            out = jnp.where(sub == j, vs[j], out)
        return out

    ninf = -jnp.inf
    v2lo, v2hi = by_sublane(v2[:SUBLANES]), by_sublane(v2[SUBLANES:])
    v1lo, v1hi = by_sublane(v1[:SUBLANES]), by_sublane(v1[SUBLANES:])
    cands = [v1[0] + v2lo, v1[0] + v2hi, v1[1] + v2lo]
    for a in (2, 3, 4):
        cands.append(jnp.where(sub < PEER_TOPK // (a + 1), v1[a] + v2lo, ninf))
    cands.append(jnp.where(sub >= 5, v1lo + v2[0], ninf))
    cands.append(v1hi + v2[0])
    cands.append(jnp.where(sub >= 5, v1lo + v2[1], ninf))
    pad = jnp.full(v1[0].shape, ninf, F32)
    return cands + [pad] * (PEER_TOPK - len(cands)), len(cands)


def _peer_score_kernel(wq_ref, h_ref, sk1_ref, sk2_ref, cnt_ref, r2_ref, e1_ref, e2_ref, q_ref, wqt_ref):
    @pl.when(pl.program_id(0) == 0)
    def _():
        chunk = 2 * LANES
        for c in range(wq_ref.shape[1] // chunk):
            wqt_ref[c * chunk:(c + 1) * chunk, :] = wq_ref[:, c * chunk:(c + 1) * chunk].T.astype(BF16)

    q_ref[...] = jnp.dot(wqt_ref[...], h_ref[...], preferred_element_type=F32)

    def head(h, carry):
        base = pl.multiple_of(h * PEER_QDIM, PEER_QDIM)
        q1 = q_ref[pl.ds(base, PEER_QHALF), :].astype(BF16)
        q2 = q_ref[pl.ds(base + PEER_QHALF, PEER_QHALF), :].astype(BF16)
        s1 = jnp.dot(sk1_ref[...], q1, preferred_element_type=F32)
        s2 = jnp.dot(sk2_ref[...], q2, preferred_element_type=F32)
        slabs = lambda s: [s[SUBLANES * i:SUBLANES * (i + 1)] for i in range(N_KEYS // SUBLANES)]
        v1 = _top16_replicated(slabs(s1))
        v2 = _top16_replicated(slabs(s2))
        cands, n_live = _pair_sum_candidates(v1, v2)
        sc = _top16_replicated(cands, live=n_live)
        zsum = jnp.zeros_like(sc[0])
        for j in range(PEER_TOPK):
            zsum = zsum + jnp.exp(sc[j] - sc[0])
        row = lambda x: x[0:1]
        th = row(sc[PEER_TOPK - 1])
        cnt = jnp.zeros_like(s1)
        r2 = jnp.zeros_like(s2)
        for b in range(PEER_TOPK):
            cnt = cnt + jnp.where(s1 + row(v2[b]) >= th, 1.0, 0.0)
            r2 = r2 + jnp.where(row(v2[b]) > s2, 1.0, 0.0)
        def twice(x):
            bits = pltpu.bitcast(x.astype(BF16).astype(F32), jnp.uint32)
            return bits | (bits >> 16)

        cnt_w = twice(cnt)
        e1_w = twice(0.5 * jnp.exp(s1 - row(v1[0])) / row(zsum))
        for cc in range(cnt_ref.shape[1]):
            cnt_ref[h, cc] = cnt_w[:, cc * LANES:(cc + 1) * LANES]
            e1_ref[h, cc] = e1_w[:, cc * LANES:(cc + 1) * LANES]
        r2_ref[h] = pltpu.bitcast(r2.astype(BF16), jnp.uint32)
        e2_ref[h] = pltpu.bitcast(jnp.exp(s2 - row(v2[0])).astype(BF16), jnp.uint32)
        return carry

    lax.fori_loop(0, PEER_HEADS, head, 0)


def _peer_scores(wq, h1T_bf, sk1_bf, sk2_bf, tl=256):
    D, T = h1T_bf.shape
    Q = wq.shape[1]
    big = jax.ShapeDtypeStruct((PEER_HEADS, T // LANES, N_KEYS, LANES), jnp.uint32)
    big16 = jax.ShapeDtypeStruct((PEER_HEADS, N_KEYS // 2, T), jnp.uint32)
    bspec = pl.BlockSpec((PEER_HEADS, tl // LANES, N_KEYS, LANES), lambda i: (0, i, 0, 0))
    pspec = pl.BlockSpec((PEER_HEADS, N_KEYS // 2, tl), lambda i: (0, 0, i))
    return pl.pallas_call(
        _peer_score_kernel,
        grid=(T // tl,),
        in_specs=[pl.BlockSpec((D, Q), lambda i: (0, 0), pipeline_mode=pl.Buffered(1)),
                  pl.BlockSpec((D, tl), lambda i: (0, i)),
                  pl.BlockSpec((N_KEYS, PEER_QHALF), lambda i: (0, 0)),
                  pl.BlockSpec((N_KEYS, PEER_QHALF), lambda i: (0, 0))],
        out_specs=[bspec, pspec, bspec, pspec],
        out_shape=[big, big16, big, big16],
        scratch_shapes=[pltpu.VMEM((Q, tl), F32), pltpu.VMEM((Q, D), BF16)],
        compiler_params=_cparams(("arbitrary",)),
        name="peer_scores_topk",
    )(wq, h1T_bf, sk1_bf, sk2_bf)


W_ROWS_PER_GROUP = 64


PEER_HALF = 4 * N_KEYS


def _peer_dense_kernel(hb_ref, u_ref, vt_ref, cnt_ref, r2_ref, e1_ref, e2_ref, o_ref, acc_ref,
                       a00, a01, a10, a11, w00, w01, w10, w11, *, te, tm, n_j, n_tiles):
    s = pl.program_id(0)
    sv = jnp.clip(s - 2, 0, n_tiles - 1)
    jv = sv % n_j
    jw = jnp.clip(s - 1, 0, n_tiles - 1) % n_j
    a_bufs = ((a00, a01), (a10, a11))
    w_bufs = ((w00, w01), (w10, w11))

    @pl.when(s == 0)
    def _():
        for ref in (a00, a01, a10, a11, w00, w01, w10, w11):
            ref[...] = jnp.zeros_like(ref)

    @pl.when(jv == 0)
    def _():
        acc_ref[...] = jnp.zeros_like(acc_ref)

    wide = 2 * LANES
    n_r, n_c = PEER_HALF // N_KEYS, tm // LANES
    key0 = pl.multiple_of(jw * (te // N_KEYS), SUBLANES)

    def half_body(half, a_cur, a_prv, w_cur, w_prv):
        def stage_a(q):
            ms = slice(half * PEER_HALF + (q // 2) * wide, half * PEER_HALF + (q // 2 + 1) * wide)
            ls = slice((q % 2) * wide, (q % 2 + 1) * wide)
            a_cur[(q // 2) * wide:(q // 2 + 1) * wide, ls] = jnp.dot(
                u_ref[ms, :], hb_ref[:, ls], preferred_element_type=F32)

        def stage_v(p):
            fs = slice((p // 2) * wide, (p // 2 + 1) * wide)
            ls = slice((p % 2) * wide, (p % 2 + 1) * wide)
            es = slice(half * PEER_HALF, (half + 1) * PEER_HALF)
            acc_ref[fs, ls] += jnp.dot(vt_ref[0, fs, es], w_cur[:, ls], preferred_element_type=F32)

        n_g = N_KEYS // W_ROWS_PER_GROUP
        grp = (W_ROWS_PER_GROUP, LANES)

        def stage_w(idx):
            p, g = idx // n_g, idx % n_g
            r, c = p // n_c, p % n_c
            krow = half * n_r + r
            cs = slice(c * LANES, (c + 1) * LANES)
            ks = slice(g * W_ROWS_PER_GROUP // 2, (g + 1) * W_ROWS_PER_GROUP // 2)
            gate = jnp.zeros(grp, BF16)
            pair = (W_ROWS_PER_GROUP // 2, LANES)
            for h in range(PEER_HEADS):
                cnt_row = cnt_ref[h, c, pl.ds(key0, SUBLANES), :][krow:krow + 1]
                e1_row = e1_ref[h, c, pl.ds(key0, SUBLANES), :][krow:krow + 1]
                cnt_b = pltpu.bitcast(jnp.broadcast_to(cnt_row, pair), BF16)
                e1_b = pltpu.bitcast(jnp.broadcast_to(e1_row, pair), BF16)
                sel = pltpu.bitcast(r2_ref[h, ks, cs], BF16) < cnt_b
                val = pltpu.bitcast(e2_ref[h, ks, cs], BF16) * e1_b
                gate = gate + jnp.where(sel, val, jnp.zeros(grp, BF16))
            ws = slice(r * N_KEYS + g * W_ROWS_PER_GROUP, r * N_KEYS + (g + 1) * W_ROWS_PER_GROUP)
            a = a_prv[ws, cs]
            act = a * (1.0 + lax.erf(a * (1.0 / math.sqrt(2.0))))
            w_prv[ws, cs] = gate * act.astype(BF16)

        per_v = n_r * n_c * n_g // 16
        for q in range(4):
            stage_a(q)
            for t in range(4):
                stage_v(4 * q + t)
                for gg in range(per_v):
                    stage_w((4 * q + t) * per_v + gg)

    for par in range(2):
        @pl.when(s % 2 == par)
        def _():
            for half in range(2):
                half_body(half, a_bufs[par][half], a_bufs[1 - par][half],
                          w_bufs[par][half], w_bufs[1 - par][half])

    @pl.when(jv == n_j - 1)
    def _():
        o_ref[...] = acc_ref[...]


def _peer_dense(h1T_bf, u_bf, v_bf, cnt, r2, e1, e2, tm=512, te=2 * PEER_HALF):
    D, T = h1T_bf.shape
    E = u_bf.shape[0]
    assert te == SUBLANES * N_KEYS and tm == 4 * LANES and D == 2048, \
        "the stage interleave is written for two 512-expert halves x 512 tokens"
    n_i, n_j = T // tm, E // te
    vT_bf = jnp.transpose(v_bf.reshape(n_j, te, D), (0, 2, 1))
    n_tiles = n_i * n_j
    kern = functools.partial(_peer_dense_kernel, te=te, tm=tm, n_j=n_j, n_tiles=n_tiles)
    ta = lambda s: jnp.minimum(s, n_tiles - 1)
    tw = lambda s: jnp.clip(s - 1, 0, n_tiles - 1)
    tv = lambda s: jnp.clip(s - 2, 0, n_tiles - 1)
    sspec = pl.BlockSpec((PEER_HEADS, N_KEYS // 2, tm), lambda s: (0, 0, tw(s) // n_j))
    xspec = pl.BlockSpec((PEER_HEADS, tm // LANES, N_KEYS, LANES), lambda s: (0, tw(s) // n_j, 0, 0))
    return pl.pallas_call(
        kern,
        grid=(n_tiles + 2,),
        in_specs=[pl.BlockSpec((D, tm), lambda s: (0, ta(s) // n_j)),
                  pl.BlockSpec((te, D), lambda s: (ta(s) % n_j, 0)),
                  pl.BlockSpec((1, D, te), lambda s: (tv(s) % n_j, 0, 0)),
                  xspec, sspec, xspec, sspec],
        out_specs=pl.BlockSpec((D, tm), lambda s: (0, tv(s) // n_j)),
        out_shape=jax.ShapeDtypeStruct((D, T), F32),
        scratch_shapes=([pltpu.VMEM((D, tm), F32)]
                        + [pltpu.VMEM((PEER_HALF, tm), F32)] * 4
                        + [pltpu.VMEM((PEER_HALF, tm), BF16)] * 4),
        compiler_params=_cparams(("arbitrary",)),
        name="peer_dense",
    )(h1T_bf, u_bf, vT_bf, cnt, r2, e1, e2)


def _res_ln_kernel(h_ref, ft_ref, g_ref, b_ref, o_ref, *, alpha):
    x = alpha * h_ref[...] + ft_ref[...].T
    mu = jnp.mean(x, -1, keepdims=True)
    xc = x - mu
    var = jnp.mean(xc * xc, -1, keepdims=True)
    o_ref[...] = xc * lax.rsqrt(var + 1e-5) * g_ref[...] + b_ref[...]


def _res_ln(h, fT, g, b, alpha, tm=512):
    T, D = h.shape
    row = pl.BlockSpec((tm, D), lambda i: (i, 0))
    vec = pl.BlockSpec((1, D), lambda i: (0, 0))
    return pl.pallas_call(
        functools.partial(_res_ln_kernel, alpha=alpha),
        grid=(T // tm,),
        in_specs=[row, pl.BlockSpec((D, tm), lambda i: (0, i)), vec, vec],
        out_specs=row,
        out_shape=jax.ShapeDtypeStruct((T, D), F32),
        compiler_params=_cparams(("parallel",)),
        name="res_ln2",
    )(h, fT, g.reshape(1, D), b.reshape(1, D))


def kernel(x, ln0_g, ln0_b, rel_bias, w_in, sink, conv_w, conv_b, f_w1, f_b1, f_freq1, f_w2, f_b2,
           f_freq2, f_w3, hy_bias, mix_norm_g, w_out, ln1_g, ln1_b, peer_wq, peer_subkeys, peer_u,
           peer_v, ln2_g, ln2_b):
    B, S, D = x.shape
    T = B * S
    alpha = (2.0 * DEPTH) ** 0.25
    consts = _dft_constants()

    qi = jnp.arange(BLOCK, dtype=jnp.int32)
    kj = jnp.arange(3 * BLOCK, dtype=jnp.int32)
    rel = kj[None, :] - BLOCK - qi[:, None]
    onehot = (_t5_bucket(rel)[..., None] == jnp.arange(N_BUCKETS, dtype=jnp.int32)).astype(F32)
    bias = jnp.einsum("qkb,bh->hqk", onehot, rel_bias.astype(F32), precision=lax.Precision.HIGHEST)

    h, h_bf = _ln0(x.reshape(T, D), ln0_g, ln0_b)
    for l in range(DEPTH):
        proj = _matmul(h_bf, w_in[l].astype(BF16), tm=1024, tn=1536)
        attn = _attention(proj, bias, sink[l], S)

        kfull = _filters(S, f_w1[l], f_b1[l], f_freq1[l], f_w2[l], f_b2[l], f_freq2[l], f_w3[l])
        kf = _filter_spectra(kfull, consts)
        u = _short_conv(proj.reshape(B, S, -1), conv_w[l], conv_b[l], ATTN_WIDTH + 2 * KV_WIDTH)
        nct = HY_WIDTH // LANES
        z1 = _long_conv(u, 0, u, nct, kf, 0, hy_bias[l, 0], consts)
        hyo = _long_conv(z1, 0, u, 2 * nct, kf, 1, hy_bias[l, 1], consts)

        h1, h1T_bf = _mix_out(attn, hyo.reshape(T, HY_WIDTH), h, mix_norm_g[l], w_out[l].astype(BF16),
                              ln1_g[l], ln1_b[l], alpha)
        cnt, r2, e1, e2 = _peer_scores(peer_wq[l], h1T_bf,
                                       peer_subkeys[l, 0].astype(BF16), peer_subkeys[l, 1].astype(BF16))
        ffnT = _peer_dense(h1T_bf, peer_u[l].astype(BF16), peer_v[l].astype(BF16), cnt, r2, e1, e2)
        h = _res_ln(h1, ffnT, ln2_g[l], ln2_b[l], alpha)
        if l + 1 < DEPTH:
            h_bf = h.astype(BF16)
    return h.reshape(B, S, D)
```

```python
import functools
import math

import numpy as np
import jax
import jax.numpy as jnp
from jax import lax
from jax.experimental import pallas as pl
from jax.experimental.pallas import tpu as pltpu

F32 = jnp.float32
BF16 = jnp.bfloat16

D_MODEL = 2048
HEAD_DIM = 128
N_Q_HEADS = 8
N_KV_HEADS = 2
GQA_GROUP = N_Q_HEADS // N_KV_HEADS
ATTN_WIDTH = N_Q_HEADS * HEAD_DIM
KV_WIDTH = N_KV_HEADS * HEAD_DIM
WINDOW = 128
BLOCK = 128
N_BUCKETS = 32
MAX_DISTANCE = 128
HY_WIDTH = D_MODEL - ATTN_WIDTH
HY_ORDER = 2
POS_BANDS = 16
POS_EMB = 1 + 2 * POS_BANDS
FILTER_HIDDEN = 64
FAST_DECAY_PCT = 0.3
SLOW_DECAY_PCT = 1.5
DECAY_TARGET = 1e-2
NORM_GROUP = 128
N_KEYS = 128
PEER_HEADS = 8
PEER_QDIM = 256
PEER_QHALF = PEER_QDIM // 2
PEER_TOPK = 16
NEG = -1e30
DEPTH = 1

LANES = 128
VMEM_LIMIT = 56 * 1024 * 1024

FFT_N1 = 64
FFT_N2 = 128
FFT_K1 = FFT_N1 // 2 + 1
FFT_K1_PAD = 40
FFT_SLOT = 2 * FFT_N2
KF_ROWS = FFT_K1 * FFT_SLOT
SUBLANES = 8
W_PITCH = FFT_SLOT + SUBLANES
X_PITCH = FFT_N2 + SUBLANES
W_ROWS = FFT_K1_PAD * W_PITCH
X_ROWS = (FFT_N1 // 2) * X_PITCH


def _cparams(sem, vmem=VMEM_LIMIT):
    return pltpu.CompilerParams(dimension_semantics=sem, vmem_limit_bytes=vmem)


def _ln0_kernel(x_ref, g_ref, b_ref, h_ref, hb_ref):
    x = x_ref[...]
    mu = jnp.mean(x, -1, keepdims=True)
    xc = x - mu
    var = jnp.mean(xc * xc, -1, keepdims=True)
    y = xc * lax.rsqrt(var + 1e-5) * g_ref[...] + b_ref[...]
    h_ref[...] = y
    hb_ref[...] = y.astype(BF16)


def _ln0(x2d, g, b, tm=1024):
    T, D = x2d.shape
    return pl.pallas_call(
        _ln0_kernel,
        grid=(T // tm,),
        in_specs=[pl.BlockSpec((tm, D), lambda i: (i, 0)),
                  pl.BlockSpec((1, D), lambda i: (0, 0)),
                  pl.BlockSpec((1, D), lambda i: (0, 0))],
        out_specs=[pl.BlockSpec((tm, D), lambda i: (i, 0)),
                   pl.BlockSpec((tm, D), lambda i: (i, 0))],
        out_shape=[jax.ShapeDtypeStruct((T, D), F32), jax.ShapeDtypeStruct((T, D), BF16)],
        compiler_params=_cparams(("parallel",)),
        name="ln0",
    )(x2d, g.reshape(1, D), b.reshape(1, D))


def _mm_kernel(a_ref, b_ref, o_ref):
    o_ref[...] = jnp.dot(a_ref[...], b_ref[...], preferred_element_type=F32)


def _matmul(a, b, tm, tn):
    M, K = a.shape
    N = b.shape[1]
    return pl.pallas_call(
        _mm_kernel,
        grid=(N // tn, M // tm),
        in_specs=[pl.BlockSpec((tm, K), lambda j, i: (i, 0)),
                  pl.BlockSpec((K, tn), lambda j, i: (0, j))],
        out_specs=pl.BlockSpec((tm, tn), lambda j, i: (i, j)),
        out_shape=jax.ShapeDtypeStruct((M, N), F32),
        compiler_params=_cparams(("parallel", "parallel")),
        name="in_proj",
    )(a, b)


def _t5_bucket(rel):
    nb = N_BUCKETS // 2
    ret = (rel > 0).astype(jnp.int32) * nb
    n = jnp.abs(rel)
    max_exact = nb // 2
    nf = jnp.maximum(n, 1).astype(F32)
    large = max_exact + (jnp.log(nf / max_exact) / math.log(MAX_DISTANCE / max_exact)
                         * (nb - max_exact)).astype(jnp.int32)
    large = jnp.minimum(large, nb - 1)
    return ret + jnp.where(n < max_exact, n, large)


def _attn_kernel(sink_ref, q_ref, kp_ref, kc_ref, kn_ref, vp_ref, vc_ref, vn_ref, bias_ref, o_ref,
                 *, nb, seq):
    n = pl.program_id(0) % nb
    rows = GQA_GROUP * BLOCK
    row = lax.broadcasted_iota(jnp.int32, (rows, 3 * BLOCK), 0)
    kj = lax.broadcasted_iota(jnp.int32, (rows, 3 * BLOCK), 1)
    rel = kj - BLOCK - (row & (BLOCK - 1))
    kabs = n * BLOCK + kj - BLOCK
    valid = (jnp.abs(rel) <= WINDOW) & (kabs >= 0) & (kabs < seq)
    head_of_row = lax.broadcasted_iota(jnp.int32, (rows, 1), 0) // BLOCK
    k = jnp.concatenate([kp_ref[...], kc_ref[...], kn_ref[...]], axis=0).astype(BF16)
    v = jnp.concatenate([vp_ref[...], vc_ref[...], vn_ref[...]], axis=0).astype(BF16)
    scale = 1.0 / math.sqrt(HEAD_DIM)
    for g in range(N_KV_HEADS):
        kg = k[:, g * HEAD_DIM:(g + 1) * HEAD_DIM]
        vg = v[:, g * HEAD_DIM:(g + 1) * HEAD_DIM]
        h0 = g * GQA_GROUP
        qg = jnp.concatenate([q_ref[:, (h0 + r) * HEAD_DIM:(h0 + r + 1) * HEAD_DIM]
                              for r in range(GQA_GROUP)], axis=0).astype(BF16)
        s = lax.dot_general(qg, kg, (((1,), (1,)), ((), ())), preferred_element_type=F32) * scale
        bias_g = bias_ref[h0:h0 + GQA_GROUP].reshape(rows, 3 * BLOCK)
        s = jnp.where(valid, s + bias_g, NEG)
        sk = jnp.zeros((rows, 1), F32)
        for r in range(GQA_GROUP):
            sk = jnp.where(head_of_row == r, sink_ref[h0 + r], sk)
        m = jnp.maximum(jnp.max(s, -1, keepdims=True), sk)
        p = jnp.exp(s - m)
        denom = jnp.sum(p, -1, keepdims=True) + jnp.exp(sk - m)
        p = p / denom
        o = jnp.dot(p.astype(BF16), vg, preferred_element_type=F32)
        for r in range(GQA_GROUP):
            o_ref[:, (h0 + r) * HEAD_DIM:(h0 + r + 1) * HEAD_DIM] = o[r * BLOCK:(r + 1) * BLOCK]


def _attention(proj, bias, sink, seq):
    T = proj.shape[0]
    nb = seq // BLOCK
    kcol = ATTN_WIDTH // KV_WIDTH
    vcol = kcol + 1

    def prev(i):
        return i - jnp.where(i % nb == 0, 0, 1)

    def nxt(i):
        return i + jnp.where(i % nb == nb - 1, 0, 1)

    kern = functools.partial(_attn_kernel, nb=nb, seq=seq)
    return pl.pallas_call(
        kern,
        grid=(T // BLOCK,),
        in_specs=[pl.BlockSpec(memory_space=pltpu.SMEM),
                  pl.BlockSpec((BLOCK, ATTN_WIDTH), lambda i: (i, 0)),
                  pl.BlockSpec((BLOCK, KV_WIDTH), lambda i: (prev(i), kcol)),
                  pl.BlockSpec((BLOCK, KV_WIDTH), lambda i: (i, kcol)),
                  pl.BlockSpec((BLOCK, KV_WIDTH), lambda i: (nxt(i), kcol)),
                  pl.BlockSpec((BLOCK, KV_WIDTH), lambda i: (prev(i), vcol)),
                  pl.BlockSpec((BLOCK, KV_WIDTH), lambda i: (i, vcol)),
                  pl.BlockSpec((BLOCK, KV_WIDTH), lambda i: (nxt(i), vcol)),
                  pl.BlockSpec((N_Q_HEADS, BLOCK, 3 * BLOCK), lambda i: (0, 0, 0))],
        out_specs=pl.BlockSpec((BLOCK, ATTN_WIDTH), lambda i: (i, 0)),
        out_shape=jax.ShapeDtypeStruct((T, ATTN_WIDTH), F32),
        compiler_params=_cparams(("parallel",)),
        name="window_attn",
    )(sink, proj, proj, proj, proj, proj, proj, proj, bias)


def _filter_kernel(z_ref, w1_ref, b1_ref, fr1_ref, w2_ref, b2_ref, fr2_ref, w3_ref, dl_ref, o_ref,
                   *, tt, seq):
    hi = lax.Precision.HIGHEST
    a = jnp.dot(z_ref[...], w1_ref[...], preferred_element_type=F32, precision=hi) + b1_ref[...]
    hid = jnp.sin(fr1_ref[...] * a)
    a = jnp.dot(hid, w2_ref[...], preferred_element_type=F32, precision=hi) + b2_ref[...]
    hid = jnp.sin(fr2_ref[...] * a)
    h = jnp.dot(hid.astype(BF16), w3_ref[0], preferred_element_type=F32)
    j = lax.broadcasted_iota(jnp.int32, h.shape, 0) + pl.program_id(0) * tt
    lag = jnp.where(j < seq, j, (2 * seq - j) & (seq - 1))
    tn = lag.astype(F32) / float(max(seq - 1, 1))
    o_ref[...] = h * jnp.exp(-tn * dl_ref[...])


def _filters(seq, w1, b1, fr1, w2, b2, fr2, w3, tt=512):
    assert seq & (seq - 1) == 0
    t = np.arange(seq, dtype=np.float32)
    tn = t / np.float32(max(seq - 1, 1))
    w = (np.float32(2.0 * math.pi) * t / np.float32(seq))[:, None]
    bands = np.linspace(1e-4, POS_BANDS - 1, POS_BANDS, dtype=np.float32)
    z = np.concatenate([tn[:, None], np.cos(w * bands), -np.sin(w * bands)], -1).astype(np.float32)
    zp = np.pad(z, ((0, 0), (0, FILTER_HIDDEN - POS_EMB)))
    zfull = jnp.asarray(np.concatenate([zp, zp[0:1], zp[1:][::-1]], axis=0))
    w1p = jnp.pad(w1, ((0, FILTER_HIDDEN - POS_EMB), (0, 0)))
    max_decay = math.log(DECAY_TARGET) / FAST_DECAY_PCT
    min_decay = math.log(DECAY_TARGET) / SLOW_DECAY_PCT
    deltas = jnp.abs(jnp.linspace(min_decay, max_decay, HY_WIDTH, dtype=F32))
    ncol = HY_ORDER * HY_WIDTH
    dl = jnp.tile(deltas, HY_ORDER).reshape(1, ncol)
    H = FILTER_HIDDEN
    w3s = jnp.transpose(w3.reshape(H, HY_ORDER, 2, HY_WIDTH), (2, 0, 1, 3)).reshape(2, H, ncol).astype(BF16)
    half = seq // tt
    kern = functools.partial(_filter_kernel, tt=tt, seq=seq)
    full = lambda r, c: pl.BlockSpec((r, c), lambda i: (0, 0))
    return pl.pallas_call(
        kern,
        grid=(2 * half,),
        in_specs=[pl.BlockSpec((tt, H), lambda i: (i, 0)),
                  full(H, H), full(1, H), full(1, H), full(H, H), full(1, H), full(1, H),
                  pl.BlockSpec((1, H, ncol), lambda i: (i // half, 0, 0)), full(1, ncol)],
        out_specs=pl.BlockSpec((tt, ncol), lambda i: (i, 0)),
        out_shape=jax.ShapeDtypeStruct((2 * seq, ncol), F32),
        compiler_params=_cparams(("parallel",)),
        name="hyena_filter_mlp",
    )(zfull, w1p, b1.reshape(1, H), fr1.reshape(1, H), w2, b2.reshape(1, H), fr2.reshape(1, H), w3s, dl)


def _sconv_kernel(x_ref, w_ref, b_ref, o_ref):
    x = x_ref[0]
    L = x.shape[0]
    rows = lax.broadcasted_iota(jnp.int32, x.shape, 0)
    xm = jnp.where(rows == 0, 0.0, pltpu.roll(x, 1, 0))
    xp = jnp.where(rows == L - 1, 0.0, pltpu.roll(x, L - 1, 0))
    w = w_ref[...]
    o_ref[0] = xm * w[0:1] + x * w[1:2] + xp * w[2:3] + b_ref[...]


def _short_conv(proj3, conv_w, conv_b, col0, ct=256):
    B, L, _ = proj3.shape
    C = conv_w.shape[1]
    off = col0 // ct
    return pl.pallas_call(
        _sconv_kernel,
        grid=(B, C // ct),
        in_specs=[pl.BlockSpec((1, L, ct), lambda b, c: (b, 0, c + off)),
                  pl.BlockSpec((3, ct), lambda b, c: (0, c)),
                  pl.BlockSpec((1, ct), lambda b, c: (0, c))],
        out_specs=pl.BlockSpec((1, L, ct), lambda b, c: (b, 0, c)),
        out_shape=jax.ShapeDtypeStruct((B, L, C), F32),
        compiler_params=_cparams(("parallel", "parallel")),
        name="hyena_short_conv",
    )(proj3, conv_w, conv_b.reshape(1, C))


def _dft_constants():
    n1h = FFT_N1 // 2
    k1 = np.arange(FFT_K1_PAD)[:, None].astype(np.float64)
    n1 = np.arange(FFT_N1)[None, :].astype(np.float64)
    ang = 2.0 * np.pi * k1 * n1 / FFT_N1
    live = (np.arange(FFT_K1_PAD) < FFT_K1)[:, None]
    f1_full = np.concatenate([np.where(live, np.cos(ang), 0.0), np.where(live, -np.sin(ang), 0.0)], 0)
    f1 = f1_full[:, :n1h]
    kk1 = np.arange(FFT_K1)[:, None, None].astype(np.float64)
    k2 = np.arange(FFT_N2)[None, :, None].astype(np.float64)
    n2 = np.arange(FFT_N2)[None, None, :].astype(np.float64)
    phi = 2.0 * np.pi * (n2 * k2 / FFT_N2 + n2 * kk1 / (FFT_N1 * FFT_N2))
    c, s = np.cos(phi), np.sin(phi)
    g = np.concatenate([np.concatenate([c, s], 2), np.concatenate([-s, c], 2)], 1)
    ct_, st_ = np.transpose(c, (0, 2, 1)), np.transpose(s, (0, 2, 1))
    ginv = np.concatenate([np.concatenate([ct_, -st_], 2), np.concatenate([st_, ct_], 2)], 1)
    wk = np.where((np.arange(FFT_K1_PAD) == 0) | (np.arange(FFT_K1_PAD) == FFT_N1 // 2), 1.0, 2.0)
    wk = np.where(np.arange(FFT_K1_PAD) < FFT_K1, wk, 0.0)[None, :] / (FFT_N1 * FFT_N2)
    angi = 2.0 * np.pi * np.arange(n1h)[:, None] * np.arange(FFT_K1_PAD)[None, :] / FFT_N1
    finv = np.concatenate([wk * np.cos(angi), -wk * np.sin(angi)], 1)
    as_bf = lambda a: jnp.asarray(a.astype(np.float32)).astype(BF16)
    return as_bf(f1), as_bf(g), as_bf(finv), as_bf(ginv), as_bf(f1_full)


def _pad_rows_in(src_ref, xp_ref, n_blocks=FFT_N1 // 2, zero_row_of_block=None):
    for n1 in range(n_blocks):
        blk = src_ref[pl.ds(n1 * FFT_N2, FFT_N2), :]
        if n1 == zero_row_of_block:
            rows = lax.broadcasted_iota(jnp.int32, blk.shape, 0)
            blk = jnp.where(rows == 0, 0.0, blk)
        xp_ref[pl.ds(n1 * X_PITCH, FFT_N2), :] = blk


def _fft_stage1(xp_ref, w_ref, f1_ref):
    n1h = f1_ref.shape[1]
    kp = FFT_K1_PAD

    def body(i, carry):
        n2 = 2 * i
        xa = xp_ref[pl.ds(n2, n1h, stride=X_PITCH), :]
        xb = xp_ref[pl.ds(n2 + 1, n1h, stride=X_PITCH), :]
        xs = jnp.concatenate([xa, xb], axis=1).astype(BF16)
        r = jnp.dot(f1_ref[...], xs, preferred_element_type=F32)
        w_ref[pl.ds(n2, kp, stride=W_PITCH), :] = r[0:kp, 0:LANES]
        w_ref[pl.ds(n2 + 1, kp, stride=W_PITCH), :] = r[0:kp, LANES:2 * LANES]
        w_ref[pl.ds(FFT_N2 + n2, kp, stride=W_PITCH), :] = r[kp:2 * kp, 0:LANES]
        w_ref[pl.ds(FFT_N2 + n2 + 1, kp, stride=W_PITCH), :] = r[kp:2 * kp, LANES:2 * LANES]
        return carry

    lax.fori_loop(0, FFT_N2 // 2, body, 0, unroll=8)


def _kf_kernel(k_ref, f1_ref, g_ref, kf_ref, w_ref, xp_ref):
    scale = 1.0 / (jnp.sum(jnp.abs(k_ref[...]), 0, keepdims=True) + 1e-6)
    _pad_rows_in(k_ref, xp_ref, n_blocks=FFT_N1, zero_row_of_block=FFT_N1 // 2)
    _fft_stage1(xp_ref, w_ref, f1_ref)
    out = kf_ref.at[0]

    def body(k1, carry):
        src = pl.multiple_of(k1 * W_PITCH, 8)
        dst = pl.multiple_of(k1 * FFT_SLOT, FFT_SLOT)
        s = jnp.dot(g_ref[k1], w_ref[pl.ds(src, FFT_SLOT), :].astype(BF16), preferred_element_type=F32)
        out[pl.ds(dst, FFT_SLOT), :] = s * scale
        return carry

    lax.fori_loop(0, FFT_K1, body, 0, unroll=3)


def _filter_spectra(kfull, consts):
    L2 = kfull.shape[0]
    assert L2 == FFT_N1 * FFT_N2, "the DFT factorisation is written for 2L = 64 * 128"
    C = HY_WIDTH
    nct = C // LANES
    _, g, _, _, f1_full = consts
    return pl.pallas_call(
        _kf_kernel,
        grid=(HY_ORDER * nct,),
        in_specs=[pl.BlockSpec((L2, LANES), lambda j: (0, j)),
                  pl.BlockSpec(f1_full.shape, lambda j: (0, 0)),
                  pl.BlockSpec(g.shape, lambda j: (0, 0, 0))],
        out_specs=pl.BlockSpec((1, KF_ROWS, LANES), lambda j: (j // nct, 0, j % nct)),
        out_shape=jax.ShapeDtypeStruct((HY_ORDER, KF_ROWS, C), F32),
        scratch_shapes=[pltpu.VMEM((W_ROWS, LANES), F32), pltpu.VMEM((2 * X_ROWS, LANES), F32)],
        compiler_params=_cparams(("parallel",)),
        name="hyena_filter_fft",
    )(kfull, f1_full, g)


CONV_GROUP = 11


def _conv_kernel(z_ref, gate_ref, kf_ref, d_ref, f1_ref, g_ref, finv_ref, ginv_ref, o_ref,
                 w_ref, xp_ref, y_ref):
    zsrc = z_ref.at[0]
    kf = kf_ref.at[0]
    _pad_rows_in(zsrc, xp_ref)
    _fft_stage1(xp_ref, w_ref, f1_ref)

    def forward(k1):
        base = pl.multiple_of(k1 * W_PITCH, 8)
        kbase = pl.multiple_of(k1 * FFT_SLOT, FFT_SLOT)
        a = w_ref[pl.ds(base, FFT_SLOT), :].astype(BF16)
        s = jnp.dot(g_ref[k1], a, preferred_element_type=F32)
        sr, si = s[0:FFT_N2], s[FFT_N2:FFT_SLOT]
        kr = kf[pl.ds(kbase, FFT_N2), :]
        ki = kf[pl.ds(kbase + FFT_N2, FFT_N2), :]
        return jnp.concatenate([sr * kr - si * ki, sr * ki + si * kr], axis=0).astype(BF16)

    def inverse(k1, y):
        base = pl.multiple_of(k1 * W_PITCH, 8)
        w_ref[pl.ds(base, FFT_SLOT), :] = jnp.dot(ginv_ref[k1], y, preferred_element_type=F32)

    def step(i, do_forward, do_inverse):
        prev = [y_ref[k] for k in range(CONV_GROUP)] if do_inverse else None
        new = [forward(i * CONV_GROUP + k) for k in range(CONV_GROUP)] if do_forward else None
        if do_inverse:
            for k in range(CONV_GROUP):
                inverse((i - 1) * CONV_GROUP + k, prev[k])
        if do_forward:
            for k in range(CONV_GROUP):
                y_ref[k] = new[k]

    n_groups = FFT_K1 // CONV_GROUP
    step(0, True, False)

    def body(i, carry):
        step(i, True, True)
        return carry

    lax.fori_loop(1, n_groups, body, 0)
    step(n_groups, False, True)

    n1h = FFT_N1 // 2
    kp = FFT_K1_PAD

    def body2(i, carry):
        n2 = 2 * i
        zr = jnp.concatenate([w_ref[pl.ds(n2, kp, stride=W_PITCH), :],
                              w_ref[pl.ds(n2 + 1, kp, stride=W_PITCH), :]], axis=1)
        zi = jnp.concatenate([w_ref[pl.ds(FFT_N2 + n2, kp, stride=W_PITCH), :],
                              w_ref[pl.ds(FFT_N2 + n2 + 1, kp, stride=W_PITCH), :]], axis=1)
        zz = jnp.concatenate([zr, zi], axis=0).astype(BF16)
        x = jnp.dot(finv_ref[...], zz, preferred_element_type=F32)
        xp_ref[pl.ds(n2, n1h, stride=X_PITCH), :] = x[:, 0:LANES]
        xp_ref[pl.ds(n2 + 1, n1h, stride=X_PITCH), :] = x[:, LANES:2 * LANES]
        return carry

    lax.fori_loop(0, FFT_N2 // 2, body2, 0, unroll=8)
    d = d_ref[...]
    for n1 in range(n1h):
        rs = pl.ds(n1 * FFT_N2, FFT_N2)
        y = xp_ref[pl.ds(n1 * X_PITCH, FFT_N2), :]
        o_ref[0, rs, :] = gate_ref[0, rs, :] * (y + z_ref[0, rs, :] * d)


def _long_conv(z_arr, z_off, gate_arr, gate_off, kf_all, order, d, consts):
    B, L, _ = z_arr.shape
    assert 2 * L == FFT_N1 * FFT_N2, "the DFT factorisation is written for 2L = 64 * 128"
    C = HY_WIDTH
    nct = C // LANES
    f1, g, finv, ginv, _ = consts
    cst2 = lambda a: pl.BlockSpec(a.shape, lambda c, b: (0, 0))
    cst3 = lambda a: pl.BlockSpec(a.shape, lambda c, b: (0, 0, 0))
    return pl.pallas_call(
        _conv_kernel,
        grid=(nct, B),
        in_specs=[pl.BlockSpec((1, L, LANES), lambda c, b: (b, 0, c + z_off)),
                  pl.BlockSpec((1, L, LANES), lambda c, b: (b, 0, c + gate_off)),
                  pl.BlockSpec((1, KF_ROWS, LANES), lambda c, b: (order, 0, c)),
                  pl.BlockSpec((1, LANES), lambda c, b: (0, c)),
                  cst2(f1), cst3(g), cst2(finv), cst3(ginv)],
        out_specs=pl.BlockSpec((1, L, LANES), lambda c, b: (b, 0, c)),
        out_shape=jax.ShapeDtypeStruct((B, L, C), F32),
        scratch_shapes=[pltpu.VMEM((W_ROWS, LANES), F32), pltpu.VMEM((X_ROWS, LANES), F32),
                        pltpu.VMEM((CONV_GROUP, FFT_SLOT, LANES), BF16)],
        compiler_params=_cparams(("parallel", "parallel")),
        name=f"hyena_long_conv{order}",
    )(z_arr, gate_arr, kf_all, d.reshape(1, C), f1, g, finv, ginv)


def _mix_kernel(attn_ref, hyo_ref, h0_ref, mg_ref, w_ref, g_ref, b_ref, h1_ref, h1t_ref, *, alpha):
    def gnorm(x, goff):
        parts = []
        for gi in range(x.shape[1] // NORM_GROUP):
            xg = x[:, gi * NORM_GROUP:(gi + 1) * NORM_GROUP]
            ms = jnp.mean(xg * xg, -1, keepdims=True)
            gg = mg_ref[:, goff + gi * NORM_GROUP: goff + (gi + 1) * NORM_GROUP]
            parts.append((xg * lax.rsqrt(ms + 1e-6) * gg).astype(BF16))
        return jnp.concatenate(parts, axis=1)

    a = gnorm(attn_ref[...], 0)
    y = gnorm(hyo_ref[...], ATTN_WIDTH)
    mix = (jnp.dot(a, w_ref[0:ATTN_WIDTH, :], preferred_element_type=F32)
           + jnp.dot(y, w_ref[ATTN_WIDTH:, :], preferred_element_type=F32))
    x = alpha * h0_ref[...] + mix
    mu = jnp.mean(x, -1, keepdims=True)
    xc = x - mu
    var = jnp.mean(xc * xc, -1, keepdims=True)
    h1 = xc * lax.rsqrt(var + 1e-5) * g_ref[...] + b_ref[...]
    h1_ref[...] = h1
    h1t_ref[...] = h1.T.astype(BF16)


def _mix_out(attn, hyo, h0, mix_g, w_out_bf, ln_g, ln_b, alpha, tm=512):
    T, D = h0.shape
    kern = functools.partial(_mix_kernel, alpha=alpha)
    return pl.pallas_call(
        kern,
        grid=(T // tm,),
        in_specs=[pl.BlockSpec((tm, ATTN_WIDTH), lambda i: (i, 0)),
                  pl.BlockSpec((tm, HY_WIDTH), lambda i: (i, 0)),
                  pl.BlockSpec((tm, D), lambda i: (i, 0)),
                  pl.BlockSpec((1, D), lambda i: (0, 0)),
                  pl.BlockSpec((D, D), lambda i: (0, 0)),
                  pl.BlockSpec((1, D), lambda i: (0, 0)),
                  pl.BlockSpec((1, D), lambda i: (0, 0))],
        out_specs=[pl.BlockSpec((tm, D), lambda i: (i, 0)), pl.BlockSpec((D, tm), lambda i: (0, i))],
        out_shape=[jax.ShapeDtypeStruct((T, D), F32), jax.ShapeDtypeStruct((D, T), BF16)],
        compiler_params=_cparams(("parallel",)),
        name="mix_out_ln1",
    )(attn, hyo, h0, mix_g.reshape(1, D), w_out_bf, ln_g.reshape(1, D), ln_b.reshape(1, D))


def _oddeven_sort_pairs(n):
    pairs = []

    def merge(lo, m, r):
        step = r * 2
        if step < m:
            merge(lo, m, step)
            merge(lo + r, m, step)
            pairs.extend((i, i + r) for i in range(lo + r, lo + m - r, step))
        else:
            pairs.append((lo, lo + r))

    def sort(lo, m):
        if m > 1:
            sort(lo, m // 2)
            sort(lo + m // 2, m // 2)
            merge(lo, m, 1)

    sort(0, n)
    return pairs


def _bitonic_merge_pairs(n):
    pairs, d = [], n // 2
    while d >= 1:
        pairs.extend((i, i + d) for i in range(n) if (i & d) == 0)
        d //= 2
    return pairs


def _apply_network(xs, pairs):
    xs = list(xs)
    for i, j in pairs:
        xs[i], xs[j] = jnp.maximum(xs[i], xs[j]), jnp.minimum(xs[i], xs[j])
    return xs


def _top16_replicated(xs, live=PEER_TOPK):
    first = [p for p in _oddeven_sort_pairs(PEER_TOPK) if p[1] < live]
    xs = _apply_network(xs, first)
    merge = _bitonic_merge_pairs(PEER_TOPK)
    for shift in (4, 2, 1):
        other = [pltpu.roll(x, shift, 0) for x in xs]
        xs = [jnp.maximum(xs[i], other[PEER_TOPK - 1 - i]) for i in range(PEER_TOPK)]
        xs = _apply_network(xs, merge)
    return xs


def _pair_sum_candidates(v1, v2):
    sub = lax.broadcasted_iota(jnp.int32, v1[0].shape, 0)

    def by_sublane(vs):
        out = vs[SUBLANES - 1]
        for j in range(SUBLANES - 2, -1, -1):
            out = jnp.where(sub == j, vs[j], out)
        return out

    ninf = -jnp.inf
    v2lo, v2hi = by_sublane(v2[:SUBLANES]), by_sublane(v2[SUBLANES:])
    v1lo, v1hi = by_sublane(v1[:SUBLANES]), by_sublane(v1[SUBLANES:])
    cands = [v1[0] + v2lo, v1[0] + v2hi, v1[1] + v2lo]
    for a in (2, 3, 4):
        cands.append(jnp.where(sub < PEER_TOPK // (a + 1), v1[a] + v2lo, ninf))
    cands.append(jnp.where(sub >= 5, v1lo + v2[0], ninf))
    cands.append(v1hi + v2[0])
    cands.append(jnp.where(sub >= 5, v1lo + v2[1], ninf))
    pad = jnp.full(v1[0].shape, ninf, F32)
    return cands + [pad] * (PEER_TOPK - len(cands)), len(cands)


def _peer_score_kernel(wq_ref, h_ref, sk1_ref, sk2_ref, cnt_ref, r2_ref, e1_ref, e2_ref, q_ref, wqt_ref):
    @pl.when(pl.program_id(0) == 0)
    def _():
        chunk = 2 * LANES
        for c in range(wq_ref.shape[1] // chunk):
            wqt_ref[c * chunk:(c + 1) * chunk, :] = wq_ref[:, c * chunk:(c + 1) * chunk].T.astype(BF16)

    q_ref[...] = jnp.dot(wqt_ref[...], h_ref[...], preferred_element_type=F32)

    def head(h, carry):
        base = pl.multiple_of(h * PEER_QDIM, PEER_QDIM)
        q1 = q_ref[pl.ds(base, PEER_QHALF), :].astype(BF16)
        q2 = q_ref[pl.ds(base + PEER_QHALF, PEER_QHALF), :].astype(BF16)
        s1 = jnp.dot(sk1_ref[...], q1, preferred_element_type=F32)
        s2 = jnp.dot(sk2_ref[...], q2, preferred_element_type=F32)
        slabs = lambda s: [s[SUBLANES * i:SUBLANES * (i + 1)] for i in range(N_KEYS // SUBLANES)]
        v1 = _top16_replicated(slabs(s1))
        v2 = _top16_replicated(slabs(s2))
        cands, n_live = _pair_sum_candidates(v1, v2)
        sc = _top16_replicated(cands, live=n_live)
        zsum = jnp.zeros_like(sc[0])
        for j in range(PEER_TOPK):
            zsum = zsum + jnp.exp(sc[j] - sc[0])
        row = lambda x: x[0:1]
        th = row(sc[PEER_TOPK - 1])
        v2rows = [row(x) for x in v2]
        cnt_parts, r2_parts = [], []
        for k0 in range(0, N_KEYS, 4 * SUBLANES):
            s1c, s2c = s1[k0:k0 + 4 * SUBLANES], s2[k0:k0 + 4 * SUBLANES]
            cnt_c = jnp.zeros_like(s1c)
            r2_c = jnp.zeros_like(s2c)
            for b in range(PEER_TOPK):
                cnt_c = jnp.where(s1c + v2rows[b] >= th, float(b + 1), cnt_c)
                r2_c = jnp.where(v2rows[b] > s2c, float(b + 1), r2_c)
            cnt_parts.append(cnt_c)
            r2_parts.append(r2_c)
        cnt = jnp.concatenate(cnt_parts, axis=0)
        r2 = jnp.concatenate(r2_parts, axis=0)
        def twice(x):
            bits = pltpu.bitcast(x.astype(BF16).astype(F32), jnp.uint32)
            return bits | (bits >> 16)

        cnt_w = twice(cnt)
        e1_w = twice(0.5 * jnp.exp(s1 - row(v1[0])) / row(zsum))
        for cc in range(cnt_ref.shape[1]):
            cnt_ref[h, cc] = cnt_w[:, cc * LANES:(cc + 1) * LANES]
            e1_ref[h, cc] = e1_w[:, cc * LANES:(cc + 1) * LANES]
        r2_ref[h] = pltpu.bitcast(r2.astype(BF16), jnp.uint32)
        e2_ref[h] = pltpu.bitcast(jnp.exp(s2 - row(v2[0])).astype(BF16), jnp.uint32)
        return carry

    lax.fori_loop(0, PEER_HEADS, head, 0)


def _peer_scores(wq, h1T_bf, sk1_bf, sk2_bf, tl=256):
    D, T = h1T_bf.shape
    Q = wq.shape[1]
    big = jax.ShapeDtypeStruct((PEER_HEADS, T // LANES, N_KEYS, LANES), jnp.uint32)
    big16 = jax.ShapeDtypeStruct((PEER_HEADS, N_KEYS // 2, T), jnp.uint32)
    bspec = pl.BlockSpec((PEER_HEADS, tl // LANES, N_KEYS, LANES), lambda i: (0, i, 0, 0))
    pspec = pl.BlockSpec((PEER_HEADS, N_KEYS // 2, tl), lambda i: (0, 0, i))
    return pl.pallas_call(
        _peer_score_kernel,
        grid=(T // tl,),
        in_specs=[pl.BlockSpec((D, Q), lambda i: (0, 0), pipeline_mode=pl.Buffered(1)),
                  pl.BlockSpec((D, tl), lambda i: (0, i)),
                  pl.BlockSpec((N_KEYS, PEER_QHALF), lambda i: (0, 0)),
                  pl.BlockSpec((N_KEYS, PEER_QHALF), lambda i: (0, 0))],
        out_specs=[bspec, pspec, bspec, pspec],
        out_shape=[big, big16, big, big16],
        scratch_shapes=[pltpu.VMEM((Q, tl), F32), pltpu.VMEM((Q, D), BF16)],
        compiler_params=_cparams(("arbitrary",)),
        name="peer_scores_topk",
    )(wq, h1T_bf, sk1_bf, sk2_bf)


W_ROWS_PER_GROUP = 64


PEER_HALF = 4 * N_KEYS


def _peer_dense_kernel(hb_ref, u_ref, vt_ref, cnt_ref, r2_ref, e1_ref, e2_ref, o_ref, acc_ref,
                       a00, a01, a10, a11, w00, w01, w10, w11, *, te, tm, n_j, n_tiles):
    s = pl.program_id(0)
    sv = jnp.clip(s - 2, 0, n_tiles - 1)
    jv = sv % n_j
    jw = jnp.clip(s - 1, 0, n_tiles - 1) % n_j
    a_bufs = ((a00, a01), (a10, a11))
    w_bufs = ((w00, w01), (w10, w11))

    @pl.when(s == 0)
    def _():
        for ref in (a00, a01, a10, a11, w00, w01, w10, w11):
            ref[...] = jnp.zeros_like(ref)

    @pl.when(jv == 0)
    def _():
        acc_ref[...] = jnp.zeros_like(acc_ref)

    wide = 2 * LANES
    n_r, n_c = PEER_HALF // N_KEYS, tm // LANES
    key0 = pl.multiple_of(jw * (te // N_KEYS), SUBLANES)

    def half_body(half, a_cur, a_prv, w_cur, w_prv):
        def stage_a(q):
            ms = slice(half * PEER_HALF + (q // 2) * wide, half * PEER_HALF + (q // 2 + 1) * wide)
            ls = slice((q % 2) * wide, (q % 2 + 1) * wide)
            a_cur[(q // 2) * wide:(q // 2 + 1) * wide, ls] = jnp.dot(
                u_ref[ms, :], hb_ref[:, ls], preferred_element_type=F32)

        def stage_v(p):
            fs = slice((p // 2) * wide, (p // 2 + 1) * wide)
            ls = slice((p % 2) * wide, (p % 2 + 1) * wide)
            es = slice(half * PEER_HALF, (half + 1) * PEER_HALF)
            acc_ref[fs, ls] += jnp.dot(vt_ref[0, fs, es], w_cur[:, ls], preferred_element_type=F32)

        n_g = N_KEYS // W_ROWS_PER_GROUP
        grp = (W_ROWS_PER_GROUP, LANES)

        def stage_w(idx):
            p, g = idx // n_g, idx % n_g
            r, c = p // n_c, p % n_c
            krow = half * n_r + r
            cs = slice(c * LANES, (c + 1) * LANES)
            ks = slice(g * W_ROWS_PER_GROUP // 2, (g + 1) * W_ROWS_PER_GROUP // 2)
            gate = jnp.zeros(grp, BF16)
            pair = (W_ROWS_PER_GROUP // 2, LANES)
            for h in range(PEER_HEADS):
                cnt_row = cnt_ref[h, c, pl.ds(key0, SUBLANES), :][krow:krow + 1]
                e1_row = e1_ref[h, c, pl.ds(key0, SUBLANES), :][krow:krow + 1]
                cnt_b = pltpu.bitcast(jnp.broadcast_to(cnt_row, pair), BF16)
                e1_b = pltpu.bitcast(jnp.broadcast_to(e1_row, pair), BF16)
                sel = pltpu.bitcast(r2_ref[h, ks, cs], BF16) < cnt_b
                val = pltpu.bitcast(e2_ref[h, ks, cs], BF16) * e1_b
                gate = gate + jnp.where(sel, val, jnp.zeros(grp, BF16))
            ws = slice(r * N_KEYS + g * W_ROWS_PER_GROUP, r * N_KEYS + (g + 1) * W_ROWS_PER_GROUP)
            a = a_prv[ws, cs]
            act = a * (1.0 + lax.erf(a * (1.0 / math.sqrt(2.0))))
            w_prv[ws, cs] = gate * act.astype(BF16)

        per_v = n_r * n_c * n_g // 16
        for q in range(4):
            stage_a(q)
            for t in range(4):
                stage_v(4 * q + t)
                for gg in range(per_v):
                    stage_w((4 * q + t) * per_v + gg)

    for par in range(2):
        @pl.when(s % 2 == par)
        def _():
            for half in range(2):
                half_body(half, a_bufs[par][half], a_bufs[1 - par][half],
                          w_bufs[par][half], w_bufs[1 - par][half])

    @pl.when(jv == n_j - 1)
    def _():
        o_ref[...] = acc_ref[...]


def _peer_dense(h1T_bf, u_bf, v_bf, cnt, r2, e1, e2, tm=512, te=2 * PEER_HALF):
    D, T = h1T_bf.shape
    E = u_bf.shape[0]
    assert te == SUBLANES * N_KEYS and tm == 4 * LANES and D == 2048, \
        "the stage interleave is written for two 512-expert halves x 512 tokens"
    n_i, n_j = T // tm, E // te
    vT_bf = jnp.transpose(v_bf.reshape(n_j, te, D), (0, 2, 1))
    n_tiles = n_i * n_j
    kern = functools.partial(_peer_dense_kernel, te=te, tm=tm, n_j=n_j, n_tiles=n_tiles)
    ta = lambda s: jnp.minimum(s, n_tiles - 1)
    tw = lambda s: jnp.clip(s - 1, 0, n_tiles - 1)
    tv = lambda s: jnp.clip(s - 2, 0, n_tiles - 1)
    sspec = pl.BlockSpec((PEER_HEADS, N_KEYS // 2, tm), lambda s: (0, 0, tw(s) // n_j))
    xspec = pl.BlockSpec((PEER_HEADS, tm // LANES, N_KEYS, LANES), lambda s: (0, tw(s) // n_j, 0, 0))
    return pl.pallas_call(
        kern,
        grid=(n_tiles + 2,),
        in_specs=[pl.BlockSpec((D, tm), lambda s: (0, ta(s) // n_j)),
                  pl.BlockSpec((te, D), lambda s: (ta(s) % n_j, 0)),
                  pl.BlockSpec((1, D, te), lambda s: (tv(s) % n_j, 0, 0)),
                  xspec, sspec, xspec, sspec],
        out_specs=pl.BlockSpec((D, tm), lambda s: (0, tv(s) // n_j)),
        out_shape=jax.ShapeDtypeStruct((D, T), F32),
        scratch_shapes=([pltpu.VMEM((D, tm), F32)]
                        + [pltpu.VMEM((PEER_HALF, tm), F32)] * 4
                        + [pltpu.VMEM((PEER_HALF, tm), BF16)] * 4),
        compiler_params=_cparams(("arbitrary",)),
        name="peer_dense",
    )(h1T_bf, u_bf, vT_bf, cnt, r2, e1, e2)


def _res_ln_kernel(h_ref, ft_ref, g_ref, b_ref, o_ref, *, alpha):
    x = alpha * h_ref[...] + ft_ref[...].T
    mu = jnp.mean(x, -1, keepdims=True)
    xc = x - mu
    var = jnp.mean(xc * xc, -1, keepdims=True)
    o_ref[...] = xc * lax.rsqrt(var + 1e-5) * g_ref[...] + b_ref[...]


def _res_ln(h, fT, g, b, alpha, tm=512):
    T, D = h.shape
    row = pl.BlockSpec((tm, D), lambda i: (i, 0))
    vec = pl.BlockSpec((1, D), lambda i: (0, 0))
    return pl.pallas_call(
        functools.partial(_res_ln_kernel, alpha=alpha),
        grid=(T // tm,),
        in_specs=[row, pl.BlockSpec((D, tm), lambda i: (0, i)), vec, vec],
        out_specs=row,
        out_shape=jax.ShapeDtypeStruct((T, D), F32),
        compiler_params=_cparams(("parallel",)),
        name="res_ln2",
    )(h, fT, g.reshape(1, D), b.reshape(1, D))


def kernel(x, ln0_g, ln0_b, rel_bias, w_in, sink, conv_w, conv_b, f_w1, f_b1, f_freq1, f_w2, f_b2,
           f_freq2, f_w3, hy_bias, mix_norm_g, w_out, ln1_g, ln1_b, peer_wq, peer_subkeys, peer_u,
           peer_v, ln2_g, ln2_b):
    B, S, D = x.shape
    T = B * S
    alpha = (2.0 * DEPTH) ** 0.25
    consts = _dft_constants()

    qi = jnp.arange(BLOCK, dtype=jnp.int32)
    kj = jnp.arange(3 * BLOCK, dtype=jnp.int32)
    rel = kj[None, :] - BLOCK - qi[:, None]
    onehot = (_t5_bucket(rel)[..., None] == jnp.arange(N_BUCKETS, dtype=jnp.int32)).astype(F32)
    bias = jnp.einsum("qkb,bh->hqk", onehot, rel_bias.astype(F32), precision=lax.Precision.HIGHEST)

    h, h_bf = _ln0(x.reshape(T, D), ln0_g, ln0_b)
    for l in range(DEPTH):
        proj = _matmul(h_bf, w_in[l].astype(BF16), tm=1024, tn=1536)
        attn = _attention(proj, bias, sink[l], S)

        kfull = _filters(S, f_w1[l], f_b1[l], f_freq1[l], f_w2[l], f_b2[l], f_freq2[l], f_w3[l])
        kf = _filter_spectra(kfull, consts)
        u = _short_conv(proj.reshape(B, S, -1), conv_w[l], conv_b[l], ATTN_WIDTH + 2 * KV_WIDTH)
        nct = HY_WIDTH // LANES
        z1 = _long_conv(u, 0, u, nct, kf, 0, hy_bias[l, 0], consts)
        hyo = _long_conv(z1, 0, u, 2 * nct, kf, 1, hy_bias[l, 1], consts)

        h1, h1T_bf = _mix_out(attn, hyo.reshape(T, HY_WIDTH), h, mix_norm_g[l], w_out[l].astype(BF16),
                              ln1_g[l], ln1_b[l], alpha)
        cnt, r2, e1, e2 = _peer_scores(peer_wq[l], h1T_bf,
                                       peer_subkeys[l, 0].astype(BF16), peer_subkeys[l, 1].astype(BF16))
        ffnT = _peer_dense(h1T_bf, peer_u[l].astype(BF16), peer_v[l].astype(BF16), cnt, r2, e1, e2)
        h = _res_ln(h1, ffnT, ln2_g[l], ln2_b[l], alpha)
        if l + 1 < DEPTH:
            h_bf = h.astype(BF16)
    return h.reshape(B, S, D)
```

```python
import functools
import math

import numpy as np
import jax
import jax.numpy as jnp
from jax import lax
from jax.experimental import pallas as pl
from jax.experimental.pallas import tpu as pltpu

F32 = jnp.float32
BF16 = jnp.bfloat16

D_MODEL = 2048
HEAD_DIM = 128
N_Q_HEADS = 8
N_KV_HEADS = 2
GQA_GROUP = N_Q_HEADS // N_KV_HEADS
ATTN_WIDTH = N_Q_HEADS * HEAD_DIM
KV_WIDTH = N_KV_HEADS * HEAD_DIM
WINDOW = 128
BLOCK = 128
N_BUCKETS = 32
MAX_DISTANCE = 128
HY_WIDTH = D_MODEL - ATTN_WIDTH
HY_ORDER = 2
POS_BANDS = 16
POS_EMB = 1 + 2 * POS_BANDS
FILTER_HIDDEN = 64
FAST_DECAY_PCT = 0.3
SLOW_DECAY_PCT = 1.5
DECAY_TARGET = 1e-2
NORM_GROUP = 128
N_KEYS = 128
PEER_HEADS = 8
PEER_QDIM = 256
PEER_QHALF = PEER_QDIM // 2
PEER_TOPK = 16
NEG = -1e30
DEPTH = 1

LANES = 128
VMEM_LIMIT = 56 * 1024 * 1024

FFT_N1 = 64
FFT_N2 = 128
FFT_K1 = FFT_N1 // 2 + 1
FFT_K1_PAD = 40
FFT_SLOT = 2 * FFT_N2
KF_ROWS = FFT_K1 * FFT_SLOT
SUBLANES = 8
W_PITCH = FFT_SLOT + SUBLANES
X_PITCH = FFT_N2 + SUBLANES
W_ROWS = FFT_K1_PAD * W_PITCH
X_ROWS = (FFT_N1 // 2) * X_PITCH


def _cparams(sem, vmem=VMEM_LIMIT):
    return pltpu.CompilerParams(dimension_semantics=sem, vmem_limit_bytes=vmem)


def _ln0_kernel(x_ref, g_ref, b_ref, h_ref, hb_ref):
    x = x_ref[...]
    mu = jnp.mean(x, -1, keepdims=True)
    xc = x - mu
    var = jnp.mean(xc * xc, -1, keepdims=True)
    y = xc * lax.rsqrt(var + 1e-5) * g_ref[...] + b_ref[...]
    h_ref[...] = y
    hb_ref[...] = y.astype(BF16)


def _ln0(x2d, g, b, tm=1024):
    T, D = x2d.shape
    return pl.pallas_call(
        _ln0_kernel,
        grid=(T // tm,),
        in_specs=[pl.BlockSpec((tm, D), lambda i: (i, 0)),
                  pl.BlockSpec((1, D), lambda i: (0, 0)),
                  pl.BlockSpec((1, D), lambda i: (0, 0))],
        out_specs=[pl.BlockSpec((tm, D), lambda i: (i, 0)),
                   pl.BlockSpec((tm, D), lambda i: (i, 0))],
        out_shape=[jax.ShapeDtypeStruct((T, D), F32), jax.ShapeDtypeStruct((T, D), BF16)],
        compiler_params=_cparams(("parallel",)),
        name="ln0",
    )(x2d, g.reshape(1, D), b.reshape(1, D))


def _mm_kernel(a_ref, b_ref, o_ref):
    o_ref[...] = jnp.dot(a_ref[...], b_ref[...], preferred_element_type=F32)


def _matmul(a, b, tm, tn):
    M, K = a.shape
    N = b.shape[1]
    return pl.pallas_call(
        _mm_kernel,
        grid=(N // tn, M // tm),
        in_specs=[pl.BlockSpec((tm, K), lambda j, i: (i, 0)),
                  pl.BlockSpec((K, tn), lambda j, i: (0, j))],
        out_specs=pl.BlockSpec((tm, tn), lambda j, i: (i, j)),
        out_shape=jax.ShapeDtypeStruct((M, N), F32),
        compiler_params=_cparams(("parallel", "parallel")),
        name="in_proj",
    )(a, b)


def _t5_bucket(rel):
    nb = N_BUCKETS // 2
    ret = (rel > 0).astype(jnp.int32) * nb
    n = jnp.abs(rel)
    max_exact = nb // 2
    nf = jnp.maximum(n, 1).astype(F32)
    large = max_exact + (jnp.log(nf / max_exact) / math.log(MAX_DISTANCE / max_exact)
                         * (nb - max_exact)).astype(jnp.int32)
    large = jnp.minimum(large, nb - 1)
    return ret + jnp.where(n < max_exact, n, large)


def _attn_kernel(sink_ref, q_ref, kp_ref, kc_ref, kn_ref, vp_ref, vc_ref, vn_ref, bias_ref, o_ref,
                 *, nb, seq):
    n = pl.program_id(0) % nb
    rows = GQA_GROUP * BLOCK
    row = lax.broadcasted_iota(jnp.int32, (rows, 3 * BLOCK), 0)
    kj = lax.broadcasted_iota(jnp.int32, (rows, 3 * BLOCK), 1)
    rel = kj - BLOCK - (row & (BLOCK - 1))
    kabs = n * BLOCK + kj - BLOCK
    valid = (jnp.abs(rel) <= WINDOW) & (kabs >= 0) & (kabs < seq)
    head_of_row = lax.broadcasted_iota(jnp.int32, (rows, 1), 0) // BLOCK
    k = jnp.concatenate([kp_ref[...], kc_ref[...], kn_ref[...]], axis=0).astype(BF16)
    v = jnp.concatenate([vp_ref[...], vc_ref[...], vn_ref[...]], axis=0).astype(BF16)
    scale = 1.0 / math.sqrt(HEAD_DIM)
    for g in range(N_KV_HEADS):
        kg = k[:, g * HEAD_DIM:(g + 1) * HEAD_DIM]
        vg = v[:, g * HEAD_DIM:(g + 1) * HEAD_DIM]
        h0 = g * GQA_GROUP
        qg = jnp.concatenate([q_ref[:, (h0 + r) * HEAD_DIM:(h0 + r + 1) * HEAD_DIM]
                              for r in range(GQA_GROUP)], axis=0).astype(BF16)
        s = lax.dot_general(qg, kg, (((1,), (1,)), ((), ())), preferred_element_type=F32) * scale
        bias_g = bias_ref[h0:h0 + GQA_GROUP].reshape(rows, 3 * BLOCK)
        s = jnp.where(valid, s + bias_g, NEG)
        sk = jnp.zeros((rows, 1), F32)
        for r in range(GQA_GROUP):
            sk = jnp.where(head_of_row == r, sink_ref[h0 + r], sk)
        m = jnp.maximum(jnp.max(s, -1, keepdims=True), sk)
        p = jnp.exp(s - m)
        denom = jnp.sum(p, -1, keepdims=True) + jnp.exp(sk - m)
        p = p / denom
        o = jnp.dot(p.astype(BF16), vg, preferred_element_type=F32)
        for r in range(GQA_GROUP):
            o_ref[:, (h0 + r) * HEAD_DIM:(h0 + r + 1) * HEAD_DIM] = o[r * BLOCK:(r + 1) * BLOCK]


def _attention(proj, bias, sink, seq):
    T = proj.shape[0]
    nb = seq // BLOCK
    kcol = ATTN_WIDTH // KV_WIDTH
    vcol = kcol + 1

    def prev(i):
        return i - jnp.where(i % nb == 0, 0, 1)

    def nxt(i):
        return i + jnp.where(i % nb == nb - 1, 0, 1)

    kern = functools.partial(_attn_kernel, nb=nb, seq=seq)
    return pl.pallas_call(
        kern,
        grid=(T // BLOCK,),
        in_specs=[pl.BlockSpec(memory_space=pltpu.SMEM),
                  pl.BlockSpec((BLOCK, ATTN_WIDTH), lambda i: (i, 0)),
                  pl.BlockSpec((BLOCK, KV_WIDTH), lambda i: (prev(i), kcol)),
                  pl.BlockSpec((BLOCK, KV_WIDTH), lambda i: (i, kcol)),
                  pl.BlockSpec((BLOCK, KV_WIDTH), lambda i: (nxt(i), kcol)),
                  pl.BlockSpec((BLOCK, KV_WIDTH), lambda i: (prev(i), vcol)),
                  pl.BlockSpec((BLOCK, KV_WIDTH), lambda i: (i, vcol)),
                  pl.BlockSpec((BLOCK, KV_WIDTH), lambda i: (nxt(i), vcol)),
                  pl.BlockSpec((N_Q_HEADS, BLOCK, 3 * BLOCK), lambda i: (0, 0, 0))],
        out_specs=pl.BlockSpec((BLOCK, ATTN_WIDTH), lambda i: (i, 0)),
        out_shape=jax.ShapeDtypeStruct((T, ATTN_WIDTH), F32),
        compiler_params=_cparams(("parallel",)),
        name="window_attn",
    )(sink, proj, proj, proj, proj, proj, proj, proj, bias)


def _filter_kernel(z_ref, w1_ref, b1_ref, fr1_ref, w2_ref, b2_ref, fr2_ref, w3_ref, dl_ref, o_ref,
                   *, tt, seq):
    hi = lax.Precision.HIGHEST
    a = jnp.dot(z_ref[...], w1_ref[...], preferred_element_type=F32, precision=hi) + b1_ref[...]
    hid = jnp.sin(fr1_ref[...] * a)
    a = jnp.dot(hid, w2_ref[...], preferred_element_type=F32, precision=hi) + b2_ref[...]
    hid = jnp.sin(fr2_ref[...] * a)
    h = jnp.dot(hid.astype(BF16), w3_ref[0], preferred_element_type=F32)
    j = lax.broadcasted_iota(jnp.int32, h.shape, 0) + pl.program_id(0) * tt
    lag = jnp.where(j < seq, j, (2 * seq - j) & (seq - 1))
    tn = lag.astype(F32) / float(max(seq - 1, 1))
    o_ref[...] = h * jnp.exp(-tn * dl_ref[...])


def _filters(seq, w1, b1, fr1, w2, b2, fr2, w3, tt=512):
    assert seq & (seq - 1) == 0
    t = np.arange(seq, dtype=np.float32)
    tn = t / np.float32(max(seq - 1, 1))
    w = (np.float32(2.0 * math.pi) * t / np.float32(seq))[:, None]
    bands = np.linspace(1e-4, POS_BANDS - 1, POS_BANDS, dtype=np.float32)
    z = np.concatenate([tn[:, None], np.cos(w * bands), -np.sin(w * bands)], -1).astype(np.float32)
    zp = np.pad(z, ((0, 0), (0, FILTER_HIDDEN - POS_EMB)))
    zfull = jnp.asarray(np.concatenate([zp, zp[0:1], zp[1:][::-1]], axis=0))
    w1p = jnp.pad(w1, ((0, FILTER_HIDDEN - POS_EMB), (0, 0)))
    max_decay = math.log(DECAY_TARGET) / FAST_DECAY_PCT
    min_decay = math.log(DECAY_TARGET) / SLOW_DECAY_PCT
    deltas = jnp.abs(jnp.linspace(min_decay, max_decay, HY_WIDTH, dtype=F32))
    ncol = HY_ORDER * HY_WIDTH
    dl = jnp.tile(deltas, HY_ORDER).reshape(1, ncol)
    H = FILTER_HIDDEN
    w3s = jnp.transpose(w3.reshape(H, HY_ORDER, 2, HY_WIDTH), (2, 0, 1, 3)).reshape(2, H, ncol).astype(BF16)
    half = seq // tt
    kern = functools.partial(_filter_kernel, tt=tt, seq=seq)
    full = lambda r, c: pl.BlockSpec((r, c), lambda i: (0, 0))
    return pl.pallas_call(
        kern,
        grid=(2 * half,),
        in_specs=[pl.BlockSpec((tt, H), lambda i: (i, 0)),
                  full(H, H), full(1, H), full(1, H), full(H, H), full(1, H), full(1, H),
                  pl.BlockSpec((1, H, ncol), lambda i: (i // half, 0, 0)), full(1, ncol)],
        out_specs=pl.BlockSpec((tt, ncol), lambda i: (i, 0)),
        out_shape=jax.ShapeDtypeStruct((2 * seq, ncol), F32),
        compiler_params=_cparams(("parallel",)),
        name="hyena_filter_mlp",
    )(zfull, w1p, b1.reshape(1, H), fr1.reshape(1, H), w2, b2.reshape(1, H), fr2.reshape(1, H), w3s, dl)


def _sconv_kernel(x_ref, w_ref, b_ref, o_ref):
    x = x_ref[0]
    L = x.shape[0]
    rows = lax.broadcasted_iota(jnp.int32, x.shape, 0)
    xm = jnp.where(rows == 0, 0.0, pltpu.roll(x, 1, 0))
    xp = jnp.where(rows == L - 1, 0.0, pltpu.roll(x, L - 1, 0))
    w = w_ref[...]
    o_ref[0] = xm * w[0:1] + x * w[1:2] + xp * w[2:3] + b_ref[...]


def _short_conv(proj3, conv_w, conv_b, col0, ct=256):
    B, L, _ = proj3.shape
    C = conv_w.shape[1]
    off = col0 // ct
    return pl.pallas_call(
        _sconv_kernel,
        grid=(B, C // ct),
        in_specs=[pl.BlockSpec((1, L, ct), lambda b, c: (b, 0, c + off)),
                  pl.BlockSpec((3, ct), lambda b, c: (0, c)),
                  pl.BlockSpec((1, ct), lambda b, c: (0, c))],
        out_specs=pl.BlockSpec((1, L, ct), lambda b, c: (b, 0, c)),
        out_shape=jax.ShapeDtypeStruct((B, L, C), F32),
        compiler_params=_cparams(("parallel", "parallel")),
        name="hyena_short_conv",
    )(proj3, conv_w, conv_b.reshape(1, C))


def _dft_constants():
    n1h = FFT_N1 // 2
    k1 = np.arange(FFT_K1_PAD)[:, None].astype(np.float64)
    n1 = np.arange(FFT_N1)[None, :].astype(np.float64)
    ang = 2.0 * np.pi * k1 * n1 / FFT_N1
    live = (np.arange(FFT_K1_PAD) < FFT_K1)[:, None]
    f1_full = np.concatenate([np.where(live, np.cos(ang), 0.0), np.where(live, -np.sin(ang), 0.0)], 0)
    f1 = f1_full[:, :n1h]
    kk1 = np.arange(FFT_K1)[:, None, None].astype(np.float64)
    k2 = np.arange(FFT_N2)[None, :, None].astype(np.float64)
    n2 = np.arange(FFT_N2)[None, None, :].astype(np.float64)
    phi = 2.0 * np.pi * (n2 * k2 / FFT_N2 + n2 * kk1 / (FFT_N1 * FFT_N2))
    c, s = np.cos(phi), np.sin(phi)
    g = np.concatenate([np.concatenate([c, s], 2), np.concatenate([-s, c], 2)], 1)
    ct_, st_ = np.transpose(c, (0, 2, 1)), np.transpose(s, (0, 2, 1))
    ginv = np.concatenate([np.concatenate([ct_, -st_], 2), np.concatenate([st_, ct_], 2)], 1)
    wk = np.where((np.arange(FFT_K1_PAD) == 0) | (np.arange(FFT_K1_PAD) == FFT_N1 // 2), 1.0, 2.0)
    wk = np.where(np.arange(FFT_K1_PAD) < FFT_K1, wk, 0.0)[None, :] / (FFT_N1 * FFT_N2)
    angi = 2.0 * np.pi * np.arange(n1h)[:, None] * np.arange(FFT_K1_PAD)[None, :] / FFT_N1
    finv = np.concatenate([wk * np.cos(angi), -wk * np.sin(angi)], 1)
    as_bf = lambda a: jnp.asarray(a.astype(np.float32)).astype(BF16)
    return as_bf(f1), as_bf(g), as_bf(finv), as_bf(ginv), as_bf(f1_full)


def _pad_rows_in(src_ref, xp_ref, n_blocks=FFT_N1 // 2, zero_row_of_block=None):
    for n1 in range(n_blocks):
        blk = src_ref[pl.ds(n1 * FFT_N2, FFT_N2), :]
        if n1 == zero_row_of_block:
            rows = lax.broadcasted_iota(jnp.int32, blk.shape, 0)
            blk = jnp.where(rows == 0, 0.0, blk)
        xp_ref[pl.ds(n1 * X_PITCH, FFT_N2), :] = blk


def _fft_stage1(xp_ref, w_ref, f1_ref):
    n1h = f1_ref.shape[1]
    kp = FFT_K1_PAD

    def body(i, carry):
        n2 = 2 * i
        xa = xp_ref[pl.ds(n2, n1h, stride=X_PITCH), :]
        xb = xp_ref[pl.ds(n2 + 1, n1h, stride=X_PITCH), :]
        xs = jnp.concatenate([xa, xb], axis=1).astype(BF16)
        r = jnp.dot(f1_ref[...], xs, preferred_element_type=F32)
        w_ref[pl.ds(n2, kp, stride=W_PITCH), :] = r[0:kp, 0:LANES]
        w_ref[pl.ds(n2 + 1, kp, stride=W_PITCH), :] = r[0:kp, LANES:2 * LANES]
        w_ref[pl.ds(FFT_N2 + n2, kp, stride=W_PITCH), :] = r[kp:2 * kp, 0:LANES]
        w_ref[pl.ds(FFT_N2 + n2 + 1, kp, stride=W_PITCH), :] = r[kp:2 * kp, LANES:2 * LANES]
        return carry

    lax.fori_loop(0, FFT_N2 // 2, body, 0, unroll=8)


def _kf_kernel(k_ref, f1_ref, g_ref, kf_ref, w_ref, xp_ref):
    scale = 1.0 / (jnp.sum(jnp.abs(k_ref[...]), 0, keepdims=True) + 1e-6)
    _pad_rows_in(k_ref, xp_ref, n_blocks=FFT_N1, zero_row_of_block=FFT_N1 // 2)
    _fft_stage1(xp_ref, w_ref, f1_ref)
    out = kf_ref.at[0]

    def body(k1, carry):
        src = pl.multiple_of(k1 * W_PITCH, 8)
        dst = pl.multiple_of(k1 * FFT_SLOT, FFT_SLOT)
        s = jnp.dot(g_ref[k1], w_ref[pl.ds(src, FFT_SLOT), :].astype(BF16), preferred_element_type=F32)
        out[pl.ds(dst, FFT_SLOT), :] = s * scale
        return carry

    lax.fori_loop(0, FFT_K1, body, 0, unroll=3)


def _filter_spectra(kfull, consts):
    L2 = kfull.shape[0]
    assert L2 == FFT_N1 * FFT_N2, "the DFT factorisation is written for 2L = 64 * 128"
    C = HY_WIDTH
    nct = C // LANES
    _, g, _, _, f1_full = consts
    return pl.pallas_call(
        _kf_kernel,
        grid=(HY_ORDER * nct,),
        in_specs=[pl.BlockSpec((L2, LANES), lambda j: (0, j)),
                  pl.BlockSpec(f1_full.shape, lambda j: (0, 0)),
                  pl.BlockSpec(g.shape, lambda j: (0, 0, 0))],
        out_specs=pl.BlockSpec((1, KF_ROWS, LANES), lambda j: (j // nct, 0, j % nct)),
        out_shape=jax.ShapeDtypeStruct((HY_ORDER, KF_ROWS, C), F32),
        scratch_shapes=[pltpu.VMEM((W_ROWS, LANES), F32), pltpu.VMEM((2 * X_ROWS, LANES), F32)],
        compiler_params=_cparams(("parallel",)),
        name="hyena_filter_fft",
    )(kfull, f1_full, g)


CONV_GROUP = 11


def _conv_kernel(z_ref, gate_ref, kf_ref, d_ref, f1_ref, g_ref, finv_ref, ginv_ref, o_ref,
                 w_ref, xp_ref, y_ref):
    zsrc = z_ref.at[0]
    kf = kf_ref.at[0]
    _pad_rows_in(zsrc, xp_ref)
    _fft_stage1(xp_ref, w_ref, f1_ref)

    def forward(k1):
        base = pl.multiple_of(k1 * W_PITCH, 8)
        kbase = pl.multiple_of(k1 * FFT_SLOT, FFT_SLOT)
        a = w_ref[pl.ds(base, FFT_SLOT), :].astype(BF16)
        s = jnp.dot(g_ref[k1], a, preferred_element_type=F32)
        sr, si = s[0:FFT_N2], s[FFT_N2:FFT_SLOT]
        kr = kf[pl.ds(kbase, FFT_N2), :]
        ki = kf[pl.ds(kbase + FFT_N2, FFT_N2), :]
        return jnp.concatenate([sr * kr - si * ki, sr * ki + si * kr], axis=0).astype(BF16)

    def inverse(k1, y):
        base = pl.multiple_of(k1 * W_PITCH, 8)
        w_ref[pl.ds(base, FFT_SLOT), :] = jnp.dot(ginv_ref[k1], y, preferred_element_type=F32)

    def step(i, do_forward, do_inverse):
        prev = [y_ref[k] for k in range(CONV_GROUP)] if do_inverse else None
        new = [forward(i * CONV_GROUP + k) for k in range(CONV_GROUP)] if do_forward else None
        if do_inverse:
            for k in range(CONV_GROUP):
                inverse((i - 1) * CONV_GROUP + k, prev[k])
        if do_forward:
            for k in range(CONV_GROUP):
                y_ref[k] = new[k]

    n_groups = FFT_K1 // CONV_GROUP
    step(0, True, False)

    def body(i, carry):
        step(i, True, True)
        return carry

    lax.fori_loop(1, n_groups, body, 0)
    step(n_groups, False, True)

    n1h = FFT_N1 // 2
    kp = FFT_K1_PAD

    def body2(i, carry):
        n2 = 2 * i
        zr = jnp.concatenate([w_ref[pl.ds(n2, kp, stride=W_PITCH), :],
                              w_ref[pl.ds(n2 + 1, kp, stride=W_PITCH), :]], axis=1)
        zi = jnp.concatenate([w_ref[pl.ds(FFT_N2 + n2, kp, stride=W_PITCH), :],
                              w_ref[pl.ds(FFT_N2 + n2 + 1, kp, stride=W_PITCH), :]], axis=1)
        zz = jnp.concatenate([zr, zi], axis=0).astype(BF16)
        x = jnp.dot(finv_ref[...], zz, preferred_element_type=F32)
        xp_ref[pl.ds(n2, n1h, stride=X_PITCH), :] = x[:, 0:LANES]
        xp_ref[pl.ds(n2 + 1, n1h, stride=X_PITCH), :] = x[:, LANES:2 * LANES]
        return carry

    lax.fori_loop(0, FFT_N2 // 2, body2, 0, unroll=8)
    d = d_ref[...]
    for n1 in range(n1h):
        rs = pl.ds(n1 * FFT_N2, FFT_N2)
        y = xp_ref[pl.ds(n1 * X_PITCH, FFT_N2), :]
        o_ref[0, rs, :] = gate_ref[0, rs, :] * (y + z_ref[0, rs, :] * d)


def _long_conv(z_arr, z_off, gate_arr, gate_off, kf_all, order, d, consts):
    B, L, _ = z_arr.shape
    assert 2 * L == FFT_N1 * FFT_N2, "the DFT factorisation is written for 2L = 64 * 128"
    C = HY_WIDTH
    nct = C // LANES
    f1, g, finv, ginv, _ = consts
    cst2 = lambda a: pl.BlockSpec(a.shape, lambda c, b: (0, 0))
    cst3 = lambda a: pl.BlockSpec(a.shape, lambda c, b: (0, 0, 0))
    return pl.pallas_call(
        _conv_kernel,
        grid=(nct, B),
        in_specs=[pl.BlockSpec((1, L, LANES), lambda c, b: (b, 0, c + z_off)),
                  pl.BlockSpec((1, L, LANES), lambda c, b: (b, 0, c + gate_off)),
                  pl.BlockSpec((1, KF_ROWS, LANES), lambda c, b: (order, 0, c)),
                  pl.BlockSpec((1, LANES), lambda c, b: (0, c)),
                  cst2(f1), cst3(g), cst2(finv), cst3(ginv)],
        out_specs=pl.BlockSpec((1, L, LANES), lambda c, b: (b, 0, c)),
        out_shape=jax.ShapeDtypeStruct((B, L, C), F32),
        scratch_shapes=[pltpu.VMEM((W_ROWS, LANES), F32), pltpu.VMEM((X_ROWS, LANES), F32),
                        pltpu.VMEM((CONV_GROUP, FFT_SLOT, LANES), BF16)],
        compiler_params=_cparams(("parallel", "parallel")),
        name=f"hyena_long_conv{order}",
    )(z_arr, gate_arr, kf_all, d.reshape(1, C), f1, g, finv, ginv)


def _mix_kernel(attn_ref, hyo_ref, h0_ref, mg_ref, w_ref, g_ref, b_ref, h1_ref, h1t_ref, *, alpha):
    def gnorm(x, goff):
        parts = []
        for gi in range(x.shape[1] // NORM_GROUP):
            xg = x[:, gi * NORM_GROUP:(gi + 1) * NORM_GROUP]
            ms = jnp.mean(xg * xg, -1, keepdims=True)
            gg = mg_ref[:, goff + gi * NORM_GROUP: goff + (gi + 1) * NORM_GROUP]
            parts.append((xg * lax.rsqrt(ms + 1e-6) * gg).astype(BF16))
        return jnp.concatenate(parts, axis=1)

    a = gnorm(attn_ref[...], 0)
    y = gnorm(hyo_ref[...], ATTN_WIDTH)
    mix = (jnp.dot(a, w_ref[0:ATTN_WIDTH, :], preferred_element_type=F32)
           + jnp.dot(y, w_ref[ATTN_WIDTH:, :], preferred_element_type=F32))
    x = alpha * h0_ref[...] + mix
    mu = jnp.mean(x, -1, keepdims=True)
    xc = x - mu
    var = jnp.mean(xc * xc, -1, keepdims=True)
    h1 = xc * lax.rsqrt(var + 1e-5) * g_ref[...] + b_ref[...]
    h1_ref[...] = h1
    h1t_ref[...] = h1.T.astype(BF16)


def _mix_out(attn, hyo, h0, mix_g, w_out_bf, ln_g, ln_b, alpha, tm=512):
    T, D = h0.shape
    kern = functools.partial(_mix_kernel, alpha=alpha)
    return pl.pallas_call(
        kern,
        grid=(T // tm,),
        in_specs=[pl.BlockSpec((tm, ATTN_WIDTH), lambda i: (i, 0)),
                  pl.BlockSpec((tm, HY_WIDTH), lambda i: (i, 0)),
                  pl.BlockSpec((tm, D), lambda i: (i, 0)),
                  pl.BlockSpec((1, D), lambda i: (0, 0)),
                  pl.BlockSpec((D, D), lambda i: (0, 0)),
                  pl.BlockSpec((1, D), lambda i: (0, 0)),
                  pl.BlockSpec((1, D), lambda i: (0, 0))],
        out_specs=[pl.BlockSpec((tm, D), lambda i: (i, 0)), pl.BlockSpec((D, tm), lambda i: (0, i))],
        out_shape=[jax.ShapeDtypeStruct((T, D), F32), jax.ShapeDtypeStruct((D, T), BF16)],
        compiler_params=_cparams(("parallel",)),
        name="mix_out_ln1",
    )(attn, hyo, h0, mix_g.reshape(1, D), w_out_bf, ln_g.reshape(1, D), ln_b.reshape(1, D))


def _oddeven_sort_pairs(n):
    pairs = []

    def merge(lo, m, r):
        step = r * 2
        if step < m:
            merge(lo, m, step)
            merge(lo + r, m, step)
            pairs.extend((i, i + r) for i in range(lo + r, lo + m - r, step))
        else:
            pairs.append((lo, lo + r))

    def sort(lo, m):
        if m > 1:
            sort(lo, m // 2)
            sort(lo + m // 2, m // 2)
            merge(lo, m, 1)

    sort(0, n)
    return pairs


def _bitonic_merge_pairs(n):
    pairs, d = [], n // 2
    while d >= 1:
        pairs.extend((i, i + d) for i in range(n) if (i & d) == 0)
        d //= 2
    return pairs


def _apply_network(xs, pairs):
    xs = list(xs)
    for i, j in pairs:
        xs[i], xs[j] = jnp.maximum(xs[i], xs[j]), jnp.minimum(xs[i], xs[j])
    return xs


def _top16_replicated(xs, live=PEER_TOPK):
    first = [p for p in _oddeven_sort_pairs(PEER_TOPK) if p[1] < live]
    xs = _apply_network(xs, first)
    merge = _bitonic_merge_pairs(PEER_TOPK)
    for shift in (4, 2, 1):
        other = [pltpu.roll(x, shift, 0) for x in xs]
        xs = [jnp.maximum(xs[i], other[PEER_TOPK - 1 - i]) for i in range(PEER_TOPK)]
        xs = _apply_network(xs, merge)
    return xs


def _pair_sum_candidates(v1, v2):
    sub = lax.broadcasted_iota(jnp.int32, v1[0].shape, 0)

    def by_sublane(vs):
        out = vs[SUBLANES - 1]
        for j in range(SUBLANES - 2, -1, -1):
            out = jnp.where(sub == j, vs[j], out)
        return out

    ninf = -jnp.inf
    v2lo, v2hi = by_sublane(v2[:SUBLANES]), by_sublane(v2[SUBLANES:])
    v1lo, v1hi = by_sublane(v1[:SUBLANES]), by_sublane(v1[SUBLANES:])
    cands = [v1[0] + v2lo, v1[0] + v2hi, v1[1] + v2lo]
    for a in (2, 3, 4):
        cands.append(jnp.where(sub < PEER_TOPK // (a + 1), v1[a] + v2lo, ninf))
    cands.append(jnp.where(sub >= 5, v1lo + v2[0], ninf))
    cands.append(v1hi + v2[0])
    cands.append(jnp.where(sub >= 5, v1lo + v2[1], ninf))
    pad = jnp.full(v1[0].shape, ninf, F32)
    return cands + [pad] * (PEER_TOPK - len(cands)), len(cands)


def _peer_score_kernel(wq_ref, h_ref, sk1_ref, sk2_ref, cnt_ref, r2_ref, e1_ref, e2_ref, q_ref, wqt_ref):
    @pl.when(pl.program_id(0) == 0)
    def _():
        chunk = 2 * LANES
        for c in range(wq_ref.shape[1] // chunk):
            wqt_ref[c * chunk:(c + 1) * chunk, :] = wq_ref[:, c * chunk:(c + 1) * chunk].T.astype(BF16)

    q_ref[...] = jnp.dot(wqt_ref[...], h_ref[...], preferred_element_type=F32)

    def head(h, carry):
        base = pl.multiple_of(h * PEER_QDIM, PEER_QDIM)
        q1 = q_ref[pl.ds(base, PEER_QHALF), :].astype(BF16)
        q2 = q_ref[pl.ds(base + PEER_QHALF, PEER_QHALF), :].astype(BF16)
        s1 = jnp.dot(sk1_ref[...], q1, preferred_element_type=F32)
        s2 = jnp.dot(sk2_ref[...], q2, preferred_element_type=F32)
        slabs = lambda s: [s[SUBLANES * i:SUBLANES * (i + 1)] for i in range(N_KEYS // SUBLANES)]
        v1 = _top16_replicated(slabs(s1))
        v2 = _top16_replicated(slabs(s2))
        cands, n_live = _pair_sum_candidates(v1, v2)
        sc = _top16_replicated(cands, live=n_live)
        zsum = jnp.zeros_like(sc[0])
        for j in range(PEER_TOPK):
            zsum = zsum + jnp.exp(sc[j] - sc[0])
        row = lambda x: x[0:1]
        th = row(sc[PEER_TOPK - 1])
        v2rows = [row(x) for x in v2]

        def prefix_count(test):
            quarter_hit = [test(v2rows[4 * k + 3]) for k in range(4)]
            base = jnp.zeros_like(s1c)
            for k in range(3):
                base = jnp.where(quarter_hit[k], float(4 * (k + 1)), base)
            inner = jnp.zeros_like(s1c)
            for j in range(3):
                pivot = v2rows[j]
                for k in range(3):
                    pivot = jnp.where(quarter_hit[k], v2rows[4 * (k + 1) + j], pivot)
                inner = jnp.where(test(pivot), float(j + 1), inner)
            return jnp.where(quarter_hit[3], float(PEER_TOPK), base + inner)

        cnt_parts, r2_parts = [], []
        for k0 in range(0, N_KEYS, 4 * SUBLANES):
            s1c, s2c = s1[k0:k0 + 4 * SUBLANES], s2[k0:k0 + 4 * SUBLANES]
            cnt_parts.append(prefix_count(lambda v: s1c + v >= th))
            r2_parts.append(prefix_count(lambda v: v > s2c))
        cnt = jnp.concatenate(cnt_parts, axis=0)
        r2 = jnp.concatenate(r2_parts, axis=0)
        def twice(x):
            bits = pltpu.bitcast(x.astype(BF16).astype(F32), jnp.uint32)
            return bits | (bits >> 16)

        cnt_w = twice(cnt)
        e1_w = twice(0.5 * jnp.exp(s1 - row(v1[0])) / row(zsum))
        for cc in range(cnt_ref.shape[1]):
            cnt_ref[h, cc] = cnt_w[:, cc * LANES:(cc + 1) * LANES]
            e1_ref[h, cc] = e1_w[:, cc * LANES:(cc + 1) * LANES]
        r2_ref[h] = pltpu.bitcast(r2.astype(BF16), jnp.uint32)
        e2_ref[h] = pltpu.bitcast(jnp.exp(s2 - row(v2[0])).astype(BF16), jnp.uint32)
        return carry

    lax.fori_loop(0, PEER_HEADS, head, 0)


def _peer_scores(wq, h1T_bf, sk1_bf, sk2_bf, tl=256):
    D, T = h1T_bf.shape
    Q = wq.shape[1]
    big = jax.ShapeDtypeStruct((PEER_HEADS, T // LANES, N_KEYS, LANES), jnp.uint32)
    big16 = jax.ShapeDtypeStruct((PEER_HEADS, N_KEYS // 2, T), jnp.uint32)
    bspec = pl.BlockSpec((PEER_HEADS, tl // LANES, N_KEYS, LANES), lambda i: (0, i, 0, 0))
    pspec = pl.BlockSpec((PEER_HEADS, N_KEYS // 2, tl), lambda i: (0, 0, i))
    return pl.pallas_call(
        _peer_score_kernel,
        grid=(T // tl,),
        in_specs=[pl.BlockSpec((D, Q), lambda i: (0, 0), pipeline_mode=pl.Buffered(1)),
                  pl.BlockSpec((D, tl), lambda i: (0, i)),
                  pl.BlockSpec((N_KEYS, PEER_QHALF), lambda i: (0, 0)),
                  pl.BlockSpec((N_KEYS, PEER_QHALF), lambda i: (0, 0))],
        out_specs=[bspec, pspec, bspec, pspec],
        out_shape=[big, big16, big, big16],
        scratch_shapes=[pltpu.VMEM((Q, tl), F32), pltpu.VMEM((Q, D), BF16)],
        compiler_params=_cparams(("arbitrary",)),
        name="peer_scores_topk",
    )(wq, h1T_bf, sk1_bf, sk2_bf)


W_ROWS_PER_GROUP = 64


PEER_HALF = 4 * N_KEYS


def _peer_dense_kernel(hb_ref, u_ref, vt_ref, cnt_ref, r2_ref, e1_ref, e2_ref, o_ref, acc_ref,
                       a00, a01, a10, a11, w00, w01, w10, w11, *, te, tm, n_j, n_tiles):
    s = pl.program_id(0)
    sv = jnp.clip(s - 2, 0, n_tiles - 1)
    jv = sv % n_j
    jw = jnp.clip(s - 1, 0, n_tiles - 1) % n_j
    a_bufs = ((a00, a01), (a10, a11))
    w_bufs = ((w00, w01), (w10, w11))

    @pl.when(s == 0)
    def _():
        for ref in (a00, a01, a10, a11, w00, w01, w10, w11):
            ref[...] = jnp.zeros_like(ref)

    @pl.when(jv == 0)
    def _():
        acc_ref[...] = jnp.zeros_like(acc_ref)

    wide = 2 * LANES
    n_r, n_c = PEER_HALF // N_KEYS, tm // LANES
    key0 = pl.multiple_of(jw * (te // N_KEYS), SUBLANES)

    def half_body(half, a_cur, a_prv, w_cur, w_prv):
        def stage_a(q):
            ms = slice(half * PEER_HALF + (q // 2) * wide, half * PEER_HALF + (q // 2 + 1) * wide)
            ls = slice((q % 2) * wide, (q % 2 + 1) * wide)
            a_cur[(q // 2) * wide:(q // 2 + 1) * wide, ls] = jnp.dot(
                u_ref[ms, :], hb_ref[:, ls], preferred_element_type=F32)

        def stage_v(p):
            fs = slice((p // 2) * wide, (p // 2 + 1) * wide)
            ls = slice((p % 2) * wide, (p % 2 + 1) * wide)
            es = slice(half * PEER_HALF, (half + 1) * PEER_HALF)
            acc_ref[fs, ls] += jnp.dot(vt_ref[0, fs, es], w_cur[:, ls], preferred_element_type=F32)

        n_g = N_KEYS // W_ROWS_PER_GROUP
        grp = (W_ROWS_PER_GROUP, LANES)

        def stage_w(idx):
            p, g = idx // n_g, idx % n_g
            r, c = p // n_c, p % n_c
            krow = half * n_r + r
            cs = slice(c * LANES, (c + 1) * LANES)
            ks = slice(g * W_ROWS_PER_GROUP // 2, (g + 1) * W_ROWS_PER_GROUP // 2)
            gate = jnp.zeros(grp, BF16)
            pair = (W_ROWS_PER_GROUP // 2, LANES)
            for h in range(PEER_HEADS):
                cnt_row = cnt_ref[h, c, pl.ds(key0, SUBLANES), :][krow:krow + 1]
                e1_row = e1_ref[h, c, pl.ds(key0, SUBLANES), :][krow:krow + 1]
                cnt_b = pltpu.bitcast(jnp.broadcast_to(cnt_row, pair), BF16)
                e1_b = pltpu.bitcast(jnp.broadcast_to(e1_row, pair), BF16)
                sel = pltpu.bitcast(r2_ref[h, ks, cs], BF16) < cnt_b
                val = pltpu.bitcast(e2_ref[h, ks, cs], BF16) * e1_b
                gate = gate + jnp.where(sel, val, jnp.zeros(grp, BF16))
            ws = slice(r * N_KEYS + g * W_ROWS_PER_GROUP, r * N_KEYS + (g + 1) * W_ROWS_PER_GROUP)
            a = a_prv[ws, cs]
            act = a * (1.0 + lax.erf(a * (1.0 / math.sqrt(2.0))))
            w_prv[ws, cs] = gate * act.astype(BF16)

        per_v = n_r * n_c * n_g // 16
        for q in range(4):
            stage_a(q)
            for t in range(4):
                stage_v(4 * q + t)
                for gg in range(per_v):
                    stage_w((4 * q + t) * per_v + gg)

    for par in range(2):
        @pl.when(s % 2 == par)
        def _():
            for half in range(2):
                half_body(half, a_bufs[par][half], a_bufs[1 - par][half],
                          w_bufs[par][half], w_bufs[1 - par][half])

    @pl.when(jv == n_j - 1)
    def _():
        o_ref[...] = acc_ref[...]


def _peer_dense(h1T_bf, u_bf, v_bf, cnt, r2, e1, e2, tm=512, te=2 * PEER_HALF):
    D, T = h1T_bf.shape
    E = u_bf.shape[0]
    assert te == SUBLANES * N_KEYS and tm == 4 * LANES and D == 2048, \
        "the stage interleave is written for two 512-expert halves x 512 tokens"
    n_i, n_j = T // tm, E // te
    vT_bf = jnp.transpose(v_bf.reshape(n_j, te, D), (0, 2, 1))
    n_tiles = n_i * n_j
    kern = functools.partial(_peer_dense_kernel, te=te, tm=tm, n_j=n_j, n_tiles=n_tiles)
    ta = lambda s: jnp.minimum(s, n_tiles - 1)
    tw = lambda s: jnp.clip(s - 1, 0, n_tiles - 1)
    tv = lambda s: jnp.clip(s - 2, 0, n_tiles - 1)
    sspec = pl.BlockSpec((PEER_HEADS, N_KEYS // 2, tm), lambda s: (0, 0, tw(s) // n_j))
    xspec = pl.BlockSpec((PEER_HEADS, tm // LANES, N_KEYS, LANES), lambda s: (0, tw(s) // n_j, 0, 0))
    return pl.pallas_call(
        kern,
        grid=(n_tiles + 2,),
        in_specs=[pl.BlockSpec((D, tm), lambda s: (0, ta(s) // n_j)),
                  pl.BlockSpec((te, D), lambda s: (ta(s) % n_j, 0)),
                  pl.BlockSpec((1, D, te), lambda s: (tv(s) % n_j, 0, 0)),
                  xspec, sspec, xspec, sspec],
        out_specs=pl.BlockSpec((D, tm), lambda s: (0, tv(s) // n_j)),
        out_shape=jax.ShapeDtypeStruct((D, T), F32),
        scratch_shapes=([pltpu.VMEM((D, tm), F32)]
                        + [pltpu.VMEM((PEER_HALF, tm), F32)] * 4
                        + [pltpu.VMEM((PEER_HALF, tm), BF16)] * 4),
        compiler_params=_cparams(("arbitrary",)),
        name="peer_dense",
    )(h1T_bf, u_bf, vT_bf, cnt, r2, e1, e2)


def _res_ln_kernel(h_ref, ft_ref, g_ref, b_ref, o_ref, *, alpha):
    x = alpha * h_ref[...] + ft_ref[...].T
    mu = jnp.mean(x, -1, keepdims=True)
    xc = x - mu
    var = jnp.mean(xc * xc, -1, keepdims=True)
    o_ref[...] = xc * lax.rsqrt(var + 1e-5) * g_ref[...] + b_ref[...]


def _res_ln(h, fT, g, b, alpha, tm=512):
    T, D = h.shape
    row = pl.BlockSpec((tm, D), lambda i: (i, 0))
    vec = pl.BlockSpec((1, D), lambda i: (0, 0))
    return pl.pallas_call(
        functools.partial(_res_ln_kernel, alpha=alpha),
        grid=(T // tm,),
        in_specs=[row, pl.BlockSpec((D, tm), lambda i: (0, i)), vec, vec],
        out_specs=row,
        out_shape=jax.ShapeDtypeStruct((T, D), F32),
        compiler_params=_cparams(("parallel",)),
        name="res_ln2",
    )(h, fT, g.reshape(1, D), b.reshape(1, D))


def kernel(x, ln0_g, ln0_b, rel_bias, w_in, sink, conv_w, conv_b, f_w1, f_b1, f_freq1, f_w2, f_b2,
           f_freq2, f_w3, hy_bias, mix_norm_g, w_out, ln1_g, ln1_b, peer_wq, peer_subkeys, peer_u,
           peer_v, ln2_g, ln2_b):
    B, S, D = x.shape
    T = B * S
    alpha = (2.0 * DEPTH) ** 0.25
    consts = _dft_constants()

    qi = jnp.arange(BLOCK, dtype=jnp.int32)
    kj = jnp.arange(3 * BLOCK, dtype=jnp.int32)
    rel = kj[None, :] - BLOCK - qi[:, None]
    onehot = (_t5_bucket(rel)[..., None] == jnp.arange(N_BUCKETS, dtype=jnp.int32)).astype(F32)
    bias = jnp.einsum("qkb,bh->hqk", onehot, rel_bias.astype(F32), precision=lax.Precision.HIGHEST)

    h, h_bf = _ln0(x.reshape(T, D), ln0_g, ln0_b)
    for l in range(DEPTH):
        proj = _matmul(h_bf, w_in[l].astype(BF16), tm=1024, tn=1536)
        attn = _attention(proj, bias, sink[l], S)

        kfull = _filters(S, f_w1[l], f_b1[l], f_freq1[l], f_w2[l], f_b2[l], f_freq2[l], f_w3[l])
        kf = _filter_spectra(kfull, consts)
        u = _short_conv(proj.reshape(B, S, -1), conv_w[l], conv_b[l], ATTN_WIDTH + 2 * KV_WIDTH)
        nct = HY_WIDTH // LANES
        z1 = _long_conv(u, 0, u, nct, kf, 0, hy_bias[l, 0], consts)
        hyo = _long_conv(z1, 0, u, 2 * nct, kf, 1, hy_bias[l, 1], consts)

        h1, h1T_bf = _mix_out(attn, hyo.reshape(T, HY_WIDTH), h, mix_norm_g[l], w_out[l].astype(BF16),
                              ln1_g[l], ln1_b[l], alpha)
        cnt, r2, e1, e2 = _peer_scores(peer_wq[l], h1T_bf,
                                       peer_subkeys[l, 0].astype(BF16), peer_subkeys[l, 1].astype(BF16))
        ffnT = _peer_dense(h1T_bf, peer_u[l].astype(BF16), peer_v[l].astype(BF16), cnt, r2, e1, e2)
        h = _res_ln(h1, ffnT, ln2_g[l], ln2_b[l], alpha)
        if l + 1 < DEPTH:
            h_bf = h.astype(BF16)
    return h.reshape(B, S, D)
```

```python
import functools
import math

import numpy as np
import jax
import jax.numpy as jnp
from jax import lax
from jax.experimental import pallas as pl
from jax.experimental.pallas import tpu as pltpu

F32 = jnp.float32
BF16 = jnp.bfloat16

D_MODEL = 2048
HEAD_DIM = 128
N_Q_HEADS = 8
N_KV_HEADS = 2
GQA_GROUP = N_Q_HEADS // N_KV_HEADS
ATTN_WIDTH = N_Q_HEADS * HEAD_DIM
KV_WIDTH = N_KV_HEADS * HEAD_DIM
WINDOW = 128
BLOCK = 128
N_BUCKETS = 32
MAX_DISTANCE = 128
HY_WIDTH = D_MODEL - ATTN_WIDTH
HY_ORDER = 2
POS_BANDS = 16
POS_EMB = 1 + 2 * POS_BANDS
FILTER_HIDDEN = 64
FAST_DECAY_PCT = 0.3
SLOW_DECAY_PCT = 1.5
DECAY_TARGET = 1e-2
NORM_GROUP = 128
N_KEYS = 128
PEER_HEADS = 8
PEER_QDIM = 256
PEER_QHALF = PEER_QDIM // 2
PEER_TOPK = 16
NEG = -1e30
DEPTH = 1

LANES = 128
VMEM_LIMIT = 56 * 1024 * 1024

FFT_N1 = 64
FFT_N2 = 128
FFT_K1 = FFT_N1 // 2 + 1
FFT_K1_PAD = 40
FFT_SLOT = 2 * FFT_N2
KF_ROWS = FFT_K1 * FFT_SLOT
SUBLANES = 8
W_PITCH = FFT_SLOT + SUBLANES
X_PITCH = FFT_N2 + SUBLANES
W_ROWS = FFT_K1_PAD * W_PITCH
X_ROWS = (FFT_N1 // 2) * X_PITCH


def _cparams(sem, vmem=VMEM_LIMIT):
    return pltpu.CompilerParams(dimension_semantics=sem, vmem_limit_bytes=vmem)


def _ln0_kernel(x_ref, g_ref, b_ref, h_ref, hb_ref):
    x = x_ref[...]
    mu = jnp.mean(x, -1, keepdims=True)
    xc = x - mu
    var = jnp.mean(xc * xc, -1, keepdims=True)
    y = xc * lax.rsqrt(var + 1e-5) * g_ref[...] + b_ref[...]
    h_ref[...] = y
    hb_ref[...] = y.astype(BF16)


def _ln0(x2d, g, b, tm=1024):
    T, D = x2d.shape
    return pl.pallas_call(
        _ln0_kernel,
        grid=(T // tm,),
        in_specs=[pl.BlockSpec((tm, D), lambda i: (i, 0)),
                  pl.BlockSpec((1, D), lambda i: (0, 0)),
                  pl.BlockSpec((1, D), lambda i: (0, 0))],
        out_specs=[pl.BlockSpec((tm, D), lambda i: (i, 0)),
                   pl.BlockSpec((tm, D), lambda i: (i, 0))],
        out_shape=[jax.ShapeDtypeStruct((T, D), F32), jax.ShapeDtypeStruct((T, D), BF16)],
        compiler_params=_cparams(("parallel",)),
        name="ln0",
    )(x2d, g.reshape(1, D), b.reshape(1, D))


def _mm_kernel(a_ref, b_ref, o_ref):
    o_ref[...] = jnp.dot(a_ref[...], b_ref[...], preferred_element_type=F32)


def _matmul(a, b, tm, tn):
    M, K = a.shape
    N = b.shape[1]
    return pl.pallas_call(
        _mm_kernel,
        grid=(N // tn, M // tm),
        in_specs=[pl.BlockSpec((tm, K), lambda j, i: (i, 0)),
                  pl.BlockSpec((K, tn), lambda j, i: (0, j))],
        out_specs=pl.BlockSpec((tm, tn), lambda j, i: (i, j)),
        out_shape=jax.ShapeDtypeStruct((M, N), F32),
        compiler_params=_cparams(("parallel", "parallel")),
        name="in_proj",
    )(a, b)


def _t5_bucket(rel):
    nb = N_BUCKETS // 2
    ret = (rel > 0).astype(jnp.int32) * nb
    n = jnp.abs(rel)
    max_exact = nb // 2
    nf = jnp.maximum(n, 1).astype(F32)
    large = max_exact + (jnp.log(nf / max_exact) / math.log(MAX_DISTANCE / max_exact)
                         * (nb - max_exact)).astype(jnp.int32)
    large = jnp.minimum(large, nb - 1)
    return ret + jnp.where(n < max_exact, n, large)


def _attn_kernel(sink_ref, q_ref, kp_ref, kc_ref, kn_ref, vp_ref, vc_ref, vn_ref, bias_ref, o_ref,
                 *, nb, seq):
    n = pl.program_id(0) % nb
    rows = GQA_GROUP * BLOCK
    row = lax.broadcasted_iota(jnp.int32, (rows, 3 * BLOCK), 0)
    kj = lax.broadcasted_iota(jnp.int32, (rows, 3 * BLOCK), 1)
    rel = kj - BLOCK - (row & (BLOCK - 1))
    kabs = n * BLOCK + kj - BLOCK
    valid = (jnp.abs(rel) <= WINDOW) & (kabs >= 0) & (kabs < seq)
    head_of_row = lax.broadcasted_iota(jnp.int32, (rows, 1), 0) // BLOCK
    k = jnp.concatenate([kp_ref[...], kc_ref[...], kn_ref[...]], axis=0).astype(BF16)
    v = jnp.concatenate([vp_ref[...], vc_ref[...], vn_ref[...]], axis=0).astype(BF16)
    scale = 1.0 / math.sqrt(HEAD_DIM)
    for g in range(N_KV_HEADS):
        kg = k[:, g * HEAD_DIM:(g + 1) * HEAD_DIM]
        vg = v[:, g * HEAD_DIM:(g + 1) * HEAD_DIM]
        h0 = g * GQA_GROUP
        qg = jnp.concatenate([q_ref[:, (h0 + r) * HEAD_DIM:(h0 + r + 1) * HEAD_DIM]
                              for r in range(GQA_GROUP)], axis=0).astype(BF16)
        s = lax.dot_general(qg, kg, (((1,), (1,)), ((), ())), preferred_element_type=F32) * scale
        bias_g = bias_ref[h0:h0 + GQA_GROUP].reshape(rows, 3 * BLOCK)
        s = jnp.where(valid, s + bias_g, NEG)
        sk = jnp.zeros((rows, 1), F32)
        for r in range(GQA_GROUP):
            sk = jnp.where(head_of_row == r, sink_ref[h0 + r], sk)
        m = jnp.maximum(jnp.max(s, -1, keepdims=True), sk)
        p = jnp.exp(s - m)
        denom = jnp.sum(p, -1, keepdims=True) + jnp.exp(sk - m)
        p = p / denom
        o = jnp.dot(p.astype(BF16), vg, preferred_element_type=F32)
        for r in range(GQA_GROUP):
            o_ref[:, (h0 + r) * HEAD_DIM:(h0 + r + 1) * HEAD_DIM] = o[r * BLOCK:(r + 1) * BLOCK]


def _attention(proj, bias, sink, seq):
    T = proj.shape[0]
    nb = seq // BLOCK
    kcol = ATTN_WIDTH // KV_WIDTH
    vcol = kcol + 1

    def prev(i):
        return i - jnp.where(i % nb == 0, 0, 1)

    def nxt(i):
        return i + jnp.where(i % nb == nb - 1, 0, 1)

    kern = functools.partial(_attn_kernel, nb=nb, seq=seq)
    return pl.pallas_call(
        kern,
        grid=(T // BLOCK,),
        in_specs=[pl.BlockSpec(memory_space=pltpu.SMEM),
                  pl.BlockSpec((BLOCK, ATTN_WIDTH), lambda i: (i, 0)),
                  pl.BlockSpec((BLOCK, KV_WIDTH), lambda i: (prev(i), kcol)),
                  pl.BlockSpec((BLOCK, KV_WIDTH), lambda i: (i, kcol)),
                  pl.BlockSpec((BLOCK, KV_WIDTH), lambda i: (nxt(i), kcol)),
                  pl.BlockSpec((BLOCK, KV_WIDTH), lambda i: (prev(i), vcol)),
                  pl.BlockSpec((BLOCK, KV_WIDTH), lambda i: (i, vcol)),
                  pl.BlockSpec((BLOCK, KV_WIDTH), lambda i: (nxt(i), vcol)),
                  pl.BlockSpec((N_Q_HEADS, BLOCK, 3 * BLOCK), lambda i: (0, 0, 0))],
        out_specs=pl.BlockSpec((BLOCK, ATTN_WIDTH), lambda i: (i, 0)),
        out_shape=jax.ShapeDtypeStruct((T, ATTN_WIDTH), F32),
        compiler_params=_cparams(("parallel",)),
        name="window_attn",
    )(sink, proj, proj, proj, proj, proj, proj, proj, bias)


def _filter_kernel(z_ref, w1_ref, b1_ref, fr1_ref, w2_ref, b2_ref, fr2_ref, w3_ref, dl_ref, o_ref,
                   *, tt, seq):
    hi = lax.Precision.HIGHEST
    a = jnp.dot(z_ref[...], w1_ref[...], preferred_element_type=F32, precision=hi) + b1_ref[...]
    hid = jnp.sin(fr1_ref[...] * a)
    a = jnp.dot(hid, w2_ref[...], preferred_element_type=F32, precision=hi) + b2_ref[...]
    hid = jnp.sin(fr2_ref[...] * a)
    h = jnp.dot(hid.astype(BF16), w3_ref[0], preferred_element_type=F32)
    j = lax.broadcasted_iota(jnp.int32, h.shape, 0) + pl.program_id(0) * tt
    lag = jnp.where(j < seq, j, (2 * seq - j) & (seq - 1))
    tn = lag.astype(F32) / float(max(seq - 1, 1))
    o_ref[...] = h * jnp.exp(-tn * dl_ref[...])


def _filters(seq, w1, b1, fr1, w2, b2, fr2, w3, tt=512):
    assert seq & (seq - 1) == 0
    t = np.arange(seq, dtype=np.float32)
    tn = t / np.float32(max(seq - 1, 1))
    w = (np.float32(2.0 * math.pi) * t / np.float32(seq))[:, None]
    bands = np.linspace(1e-4, POS_BANDS - 1, POS_BANDS, dtype=np.float32)
    z = np.concatenate([tn[:, None], np.cos(w * bands), -np.sin(w * bands)], -1).astype(np.float32)
    zp = np.pad(z, ((0, 0), (0, FILTER_HIDDEN - POS_EMB)))
    zfull = jnp.asarray(np.concatenate([zp, zp[0:1], zp[1:][::-1]], axis=0))
    w1p = jnp.pad(w1, ((0, FILTER_HIDDEN - POS_EMB), (0, 0)))
    max_decay = math.log(DECAY_TARGET) / FAST_DECAY_PCT
    min_decay = math.log(DECAY_TARGET) / SLOW_DECAY_PCT
    deltas = jnp.abs(jnp.linspace(min_decay, max_decay, HY_WIDTH, dtype=F32))
    ncol = HY_ORDER * HY_WIDTH
    dl = jnp.tile(deltas, HY_ORDER).reshape(1, ncol)
    H = FILTER_HIDDEN
    w3s = jnp.transpose(w3.reshape(H, HY_ORDER, 2, HY_WIDTH), (2, 0, 1, 3)).reshape(2, H, ncol).astype(BF16)
    half = seq // tt
    kern = functools.partial(_filter_kernel, tt=tt, seq=seq)
    full = lambda r, c: pl.BlockSpec((r, c), lambda i: (0, 0))
    return pl.pallas_call(
        kern,
        grid=(2 * half,),
        in_specs=[pl.BlockSpec((tt, H), lambda i: (i, 0)),
                  full(H, H), full(1, H), full(1, H), full(H, H), full(1, H), full(1, H),
                  pl.BlockSpec((1, H, ncol), lambda i: (i // half, 0, 0)), full(1, ncol)],
        out_specs=pl.BlockSpec((tt, ncol), lambda i: (i, 0)),
        out_shape=jax.ShapeDtypeStruct((2 * seq, ncol), F32),
        compiler_params=_cparams(("parallel",)),
        name="hyena_filter_mlp",
    )(zfull, w1p, b1.reshape(1, H), fr1.reshape(1, H), w2, b2.reshape(1, H), fr2.reshape(1, H), w3s, dl)


def _sconv_kernel(x_ref, w_ref, b_ref, o_ref):
    x = x_ref[0]
    L = x.shape[0]
    rows = lax.broadcasted_iota(jnp.int32, x.shape, 0)
    xm = jnp.where(rows == 0, 0.0, pltpu.roll(x, 1, 0))
    xp = jnp.where(rows == L - 1, 0.0, pltpu.roll(x, L - 1, 0))
    w = w_ref[...]
    o_ref[0] = xm * w[0:1] + x * w[1:2] + xp * w[2:3] + b_ref[...]


def _short_conv(proj3, conv_w, conv_b, col0, ct=256):
    B, L, _ = proj3.shape
    C = conv_w.shape[1]
    off = col0 // ct
    return pl.pallas_call(
        _sconv_kernel,
        grid=(B, C // ct),
        in_specs=[pl.BlockSpec((1, L, ct), lambda b, c: (b, 0, c + off)),
                  pl.BlockSpec((3, ct), lambda b, c: (0, c)),
                  pl.BlockSpec((1, ct), lambda b, c: (0, c))],
        out_specs=pl.BlockSpec((1, L, ct), lambda b, c: (b, 0, c)),
        out_shape=jax.ShapeDtypeStruct((B, L, C), F32),
        compiler_params=_cparams(("parallel", "parallel")),
        name="hyena_short_conv",
    )(proj3, conv_w, conv_b.reshape(1, C))


def _dft_constants():
    n1h = FFT_N1 // 2
    k1 = np.arange(FFT_K1_PAD)[:, None].astype(np.float64)
    n1 = np.arange(FFT_N1)[None, :].astype(np.float64)
    ang = 2.0 * np.pi * k1 * n1 / FFT_N1
    live = (np.arange(FFT_K1_PAD) < FFT_K1)[:, None]
    f1_full = np.concatenate([np.where(live, np.cos(ang), 0.0), np.where(live, -np.sin(ang), 0.0)], 0)
    f1 = f1_full[:, :n1h]
    kk1 = np.arange(FFT_K1)[:, None, None].astype(np.float64)
    k2 = np.arange(FFT_N2)[None, :, None].astype(np.float64)
    n2 = np.arange(FFT_N2)[None, None, :].astype(np.float64)
    phi = 2.0 * np.pi * (n2 * k2 / FFT_N2 + n2 * kk1 / (FFT_N1 * FFT_N2))
    c, s = np.cos(phi), np.sin(phi)
    g = np.concatenate([np.concatenate([c, s], 2), np.concatenate([-s, c], 2)], 1)
    ct_, st_ = np.transpose(c, (0, 2, 1)), np.transpose(s, (0, 2, 1))
    ginv = np.concatenate([np.concatenate([ct_, -st_], 2), np.concatenate([st_, ct_], 2)], 1)
    wk = np.where((np.arange(FFT_K1_PAD) == 0) | (np.arange(FFT_K1_PAD) == FFT_N1 // 2), 1.0, 2.0)
    wk = np.where(np.arange(FFT_K1_PAD) < FFT_K1, wk, 0.0)[None, :] / (FFT_N1 * FFT_N2)
    angi = 2.0 * np.pi * np.arange(n1h)[:, None] * np.arange(FFT_K1_PAD)[None, :] / FFT_N1
    finv = np.concatenate([wk * np.cos(angi), -wk * np.sin(angi)], 1)
    as_bf = lambda a: jnp.asarray(a.astype(np.float32)).astype(BF16)
    return as_bf(f1), as_bf(g), as_bf(finv), as_bf(ginv), as_bf(f1_full)


def _pad_rows_in(src_ref, xp_ref, n_blocks=FFT_N1 // 2, zero_row_of_block=None):
    for n1 in range(n_blocks):
        blk = src_ref[pl.ds(n1 * FFT_N2, FFT_N2), :]
        if n1 == zero_row_of_block:
            rows = lax.broadcasted_iota(jnp.int32, blk.shape, 0)
            blk = jnp.where(rows == 0, 0.0, blk)
        xp_ref[pl.ds(n1 * X_PITCH, FFT_N2), :] = blk


def _fft_stage1(xp_ref, w_ref, f1_ref):
    n1h = f1_ref.shape[1]
    kp = FFT_K1_PAD

    def body(i, carry):
        n2 = 2 * i
        xa = xp_ref[pl.ds(n2, n1h, stride=X_PITCH), :]
        xb = xp_ref[pl.ds(n2 + 1, n1h, stride=X_PITCH), :]
        xs = jnp.concatenate([xa, xb], axis=1).astype(BF16)
        r = jnp.dot(f1_ref[...], xs, preferred_element_type=F32)
        w_ref[pl.ds(n2, kp, stride=W_PITCH), :] = r[0:kp, 0:LANES]
        w_ref[pl.ds(n2 + 1, kp, stride=W_PITCH), :] = r[0:kp, LANES:2 * LANES]
        w_ref[pl.ds(FFT_N2 + n2, kp, stride=W_PITCH), :] = r[kp:2 * kp, 0:LANES]
        w_ref[pl.ds(FFT_N2 + n2 + 1, kp, stride=W_PITCH), :] = r[kp:2 * kp, LANES:2 * LANES]
        return carry

    lax.fori_loop(0, FFT_N2 // 2, body, 0, unroll=8)


def _kf_kernel(k_ref, f1_ref, g_ref, kf_ref, w_ref, xp_ref):
    scale = 1.0 / (jnp.sum(jnp.abs(k_ref[...]), 0, keepdims=True) + 1e-6)
    _pad_rows_in(k_ref, xp_ref, n_blocks=FFT_N1, zero_row_of_block=FFT_N1 // 2)
    _fft_stage1(xp_ref, w_ref, f1_ref)
    out = kf_ref.at[0]

    def body(k1, carry):
        src = pl.multiple_of(k1 * W_PITCH, 8)
        dst = pl.multiple_of(k1 * FFT_SLOT, FFT_SLOT)
        s = jnp.dot(g_ref[k1], w_ref[pl.ds(src, FFT_SLOT), :].astype(BF16), preferred_element_type=F32)
        out[pl.ds(dst, FFT_SLOT), :] = s * scale
        return carry

    lax.fori_loop(0, FFT_K1, body, 0, unroll=3)


def _filter_spectra(kfull, consts):
    L2 = kfull.shape[0]
    assert L2 == FFT_N1 * FFT_N2, "the DFT factorisation is written for 2L = 64 * 128"
    C = HY_WIDTH
    nct = C // LANES
    _, g, _, _, f1_full = consts
    return pl.pallas_call(
        _kf_kernel,
        grid=(HY_ORDER * nct,),
        in_specs=[pl.BlockSpec((L2, LANES), lambda j: (0, j)),
                  pl.BlockSpec(f1_full.shape, lambda j: (0, 0)),
                  pl.BlockSpec(g.shape, lambda j: (0, 0, 0))],
        out_specs=pl.BlockSpec((1, KF_ROWS, LANES), lambda j: (j // nct, 0, j % nct)),
        out_shape=jax.ShapeDtypeStruct((HY_ORDER, KF_ROWS, C), F32),
        scratch_shapes=[pltpu.VMEM((W_ROWS, LANES), F32), pltpu.VMEM((2 * X_ROWS, LANES), F32)],
        compiler_params=_cparams(("parallel",)),
        name="hyena_filter_fft",
    )(kfull, f1_full, g)


CONV_GROUP = 11


def _conv_kernel(z_ref, gate_ref, kf_ref, d_ref, f1_ref, g_ref, finv_ref, ginv_ref, o_ref,
                 w_ref, xp_ref, y_ref):
    zsrc = z_ref.at[0]
    kf = kf_ref.at[0]
    _pad_rows_in(zsrc, xp_ref)
    _fft_stage1(xp_ref, w_ref, f1_ref)

    def forward(k1):
        base = pl.multiple_of(k1 * W_PITCH, 8)
        kbase = pl.multiple_of(k1 * FFT_SLOT, FFT_SLOT)
        a = w_ref[pl.ds(base, FFT_SLOT), :].astype(BF16)
        s = jnp.dot(g_ref[k1], a, preferred_element_type=F32)
        sr, si = s[0:FFT_N2], s[FFT_N2:FFT_SLOT]
        kr = kf[pl.ds(kbase, FFT_N2), :]
        ki = kf[pl.ds(kbase + FFT_N2, FFT_N2), :]
        return jnp.concatenate([sr * kr - si * ki, sr * ki + si * kr], axis=0).astype(BF16)

    def inverse(k1, y):
        base = pl.multiple_of(k1 * W_PITCH, 8)
        w_ref[pl.ds(base, FFT_SLOT), :] = jnp.dot(ginv_ref[k1], y, preferred_element_type=F32)

    def step(i, do_forward, do_inverse):
        prev = [y_ref[k] for k in range(CONV_GROUP)] if do_inverse else None
        new = [forward(i * CONV_GROUP + k) for k in range(CONV_GROUP)] if do_forward else None
        if do_inverse:
            for k in range(CONV_GROUP):
                inverse((i - 1) * CONV_GROUP + k, prev[k])
        if do_forward:
            for k in range(CONV_GROUP):
                y_ref[k] = new[k]

    n_groups = FFT_K1 // CONV_GROUP
    step(0, True, False)

    def body(i, carry):
        step(i, True, True)
        return carry

    lax.fori_loop(1, n_groups, body, 0)
    step(n_groups, False, True)

    n1h = FFT_N1 // 2
    kp = FFT_K1_PAD

    def body2(i, carry):
        n2 = 2 * i
        zr = jnp.concatenate([w_ref[pl.ds(n2, kp, stride=W_PITCH), :],
                              w_ref[pl.ds(n2 + 1, kp, stride=W_PITCH), :]], axis=1)
        zi = jnp.concatenate([w_ref[pl.ds(FFT_N2 + n2, kp, stride=W_PITCH), :],
                              w_ref[pl.ds(FFT_N2 + n2 + 1, kp, stride=W_PITCH), :]], axis=1)
        zz = jnp.concatenate([zr, zi], axis=0).astype(BF16)
        x = jnp.dot(finv_ref[...], zz, preferred_element_type=F32)
        xp_ref[pl.ds(n2, n1h, stride=X_PITCH), :] = x[:, 0:LANES]
        xp_ref[pl.ds(n2 + 1, n1h, stride=X_PITCH), :] = x[:, LANES:2 * LANES]
        return carry

    lax.fori_loop(0, FFT_N2 // 2, body2, 0, unroll=8)
    d = d_ref[...]
    for n1 in range(n1h):
        rs = pl.ds(n1 * FFT_N2, FFT_N2)
        y = xp_ref[pl.ds(n1 * X_PITCH, FFT_N2), :]
        o_ref[0, rs, :] = gate_ref[0, rs, :] * (y + z_ref[0, rs, :] * d)


def _long_conv(z_arr, z_off, gate_arr, gate_off, kf_all, order, d, consts):
    B, L, _ = z_arr.shape
    assert 2 * L == FFT_N1 * FFT_N2, "the DFT factorisation is written for 2L = 64 * 128"
    C = HY_WIDTH
    nct = C // LANES
    f1, g, finv, ginv, _ = consts
    cst2 = lambda a: pl.BlockSpec(a.shape, lambda c, b: (0, 0))
    cst3 = lambda a: pl.BlockSpec(a.shape, lambda c, b: (0, 0, 0))
    return pl.pallas_call(
        _conv_kernel,
        grid=(nct, B),
        in_specs=[pl.BlockSpec((1, L, LANES), lambda c, b: (b, 0, c + z_off)),
                  pl.BlockSpec((1, L, LANES), lambda c, b: (b, 0, c + gate_off)),
                  pl.BlockSpec((1, KF_ROWS, LANES), lambda c, b: (order, 0, c)),
                  pl.BlockSpec((1, LANES), lambda c, b: (0, c)),
                  cst2(f1), cst3(g), cst2(finv), cst3(ginv)],
        out_specs=pl.BlockSpec((1, L, LANES), lambda c, b: (b, 0, c)),
        out_shape=jax.ShapeDtypeStruct((B, L, C), F32),
        scratch_shapes=[pltpu.VMEM((W_ROWS, LANES), F32), pltpu.VMEM((X_ROWS, LANES), F32),
                        pltpu.VMEM((CONV_GROUP, FFT_SLOT, LANES), BF16)],
        compiler_params=_cparams(("parallel", "parallel")),
        name=f"hyena_long_conv{order}",
    )(z_arr, gate_arr, kf_all, d.reshape(1, C), f1, g, finv, ginv)


def _mix_kernel(attn_ref, hyo_ref, h0_ref, mg_ref, w_ref, g_ref, b_ref, h1_ref, h1t_ref, *, alpha):
    def gnorm(x, goff):
        parts = []
        for gi in range(x.shape[1] // NORM_GROUP):
            xg = x[:, gi * NORM_GROUP:(gi + 1) * NORM_GROUP]
            ms = jnp.mean(xg * xg, -1, keepdims=True)
            gg = mg_ref[:, goff + gi * NORM_GROUP: goff + (gi + 1) * NORM_GROUP]
            parts.append((xg * lax.rsqrt(ms + 1e-6) * gg).astype(BF16))
        return jnp.concatenate(parts, axis=1)

    a = gnorm(attn_ref[...], 0)
    y = gnorm(hyo_ref[...], ATTN_WIDTH)
    mix = (jnp.dot(a, w_ref[0:ATTN_WIDTH, :], preferred_element_type=F32)
           + jnp.dot(y, w_ref[ATTN_WIDTH:, :], preferred_element_type=F32))
    x = alpha * h0_ref[...] + mix
    mu = jnp.mean(x, -1, keepdims=True)
    xc = x - mu
    var = jnp.mean(xc * xc, -1, keepdims=True)
    h1 = xc * lax.rsqrt(var + 1e-5) * g_ref[...] + b_ref[...]
    h1_ref[...] = h1
    h1t_ref[...] = h1.T.astype(BF16)


def _mix_out(attn, hyo, h0, mix_g, w_out_bf, ln_g, ln_b, alpha, tm=512):
    T, D = h0.shape
    kern = functools.partial(_mix_kernel, alpha=alpha)
    return pl.pallas_call(
        kern,
        grid=(T // tm,),
        in_specs=[pl.BlockSpec((tm, ATTN_WIDTH), lambda i: (i, 0)),
                  pl.BlockSpec((tm, HY_WIDTH), lambda i: (i, 0)),
                  pl.BlockSpec((tm, D), lambda i: (i, 0)),
                  pl.BlockSpec((1, D), lambda i: (0, 0)),
                  pl.BlockSpec((D, D), lambda i: (0, 0)),
                  pl.BlockSpec((1, D), lambda i: (0, 0)),
                  pl.BlockSpec((1, D), lambda i: (0, 0))],
        out_specs=[pl.BlockSpec((tm, D), lambda i: (i, 0)), pl.BlockSpec((D, tm), lambda i: (0, i))],
        out_shape=[jax.ShapeDtypeStruct((T, D), F32), jax.ShapeDtypeStruct((D, T), BF16)],
        compiler_params=_cparams(("parallel",)),
        name="mix_out_ln1",
    )(attn, hyo, h0, mix_g.reshape(1, D), w_out_bf, ln_g.reshape(1, D), ln_b.reshape(1, D))


def _oddeven_sort_pairs(n):
    pairs = []

    def merge(lo, m, r):
        step = r * 2
        if step < m:
            merge(lo, m, step)
            merge(lo + r, m, step)
            pairs.extend((i, i + r) for i in range(lo + r, lo + m - r, step))
        else:
            pairs.append((lo, lo + r))

    def sort(lo, m):
        if m > 1:
            sort(lo, m // 2)
            sort(lo + m // 2, m // 2)
            merge(lo, m, 1)

    sort(0, n)
    return pairs


def _bitonic_merge_pairs(n):
    pairs, d = [], n // 2
    while d >= 1:
        pairs.extend((i, i + d) for i in range(n) if (i & d) == 0)
        d //= 2
    return pairs


def _apply_network(xs, pairs):
    xs = list(xs)
    for i, j in pairs:
        xs[i], xs[j] = jnp.maximum(xs[i], xs[j]), jnp.minimum(xs[i], xs[j])
    return xs


def _top16_replicated(xs, live=PEER_TOPK):
    first = [p for p in _oddeven_sort_pairs(PEER_TOPK) if p[1] < live]
    xs = _apply_network(xs, first)
    merge = _bitonic_merge_pairs(PEER_TOPK)
    for shift in (4, 2, 1):
        other = [pltpu.roll(x, shift, 0) for x in xs]
        xs = [jnp.maximum(xs[i], other[PEER_TOPK - 1 - i]) for i in range(PEER_TOPK)]
        xs = _apply_network(xs, merge)
    return xs


def _pair_sum_candidates(v1, v2):
    sub = lax.broadcasted_iota(jnp.int32, v1[0].shape, 0)

    def by_sublane(vs):
        out = vs[SUBLANES - 1]
        for j in range(SUBLANES - 2, -1, -1):
            out = jnp.where(sub == j, vs[j], out)
        return out

    ninf = -jnp.inf
    v2lo, v2hi = by_sublane(v2[:SUBLANES]), by_sublane(v2[SUBLANES:])
    v1lo, v1hi = by_sublane(v1[:SUBLANES]), by_sublane(v1[SUBLANES:])
    cands = [v1[0] + v2lo, v1[0] + v2hi, v1[1] + v2lo]
    for a in (2, 3, 4):
        cands.append(jnp.where(sub < PEER_TOPK // (a + 1), v1[a] + v2lo, ninf))
    cands.append(jnp.where(sub >= 5, v1lo + v2[0], ninf))
    cands.append(v1hi + v2[0])
    cands.append(jnp.where(sub >= 5, v1lo + v2[1], ninf))
    pad = jnp.full(v1[0].shape, ninf, F32)
    return cands + [pad] * (PEER_TOPK - len(cands)), len(cands)


def _peer_score_kernel(wq_ref, h_ref, sk1_ref, sk2_ref, cnt_ref, r2_ref, e1_ref, e2_ref, q_ref, wqt_ref):
    @pl.when(pl.program_id(0) == 0)
    def _():
        chunk = 2 * LANES
        for c in range(wq_ref.shape[1] // chunk):
            wqt_ref[c * chunk:(c + 1) * chunk, :] = wq_ref[:, c * chunk:(c + 1) * chunk].T.astype(BF16)

    q_ref[...] = jnp.dot(wqt_ref[...], h_ref[...], preferred_element_type=F32)

    def head(h, carry):
        base = pl.multiple_of(h * PEER_QDIM, PEER_QDIM)
        q1 = q_ref[pl.ds(base, PEER_QHALF), :].astype(BF16)
        q2 = q_ref[pl.ds(base + PEER_QHALF, PEER_QHALF), :].astype(BF16)
        s1 = jnp.dot(sk1_ref[...], q1, preferred_element_type=F32)
        s2 = jnp.dot(sk2_ref[...], q2, preferred_element_type=F32)
        slabs = lambda s: [s[SUBLANES * i:SUBLANES * (i + 1)] for i in range(N_KEYS // SUBLANES)]
        v1 = _top16_replicated(slabs(s1))
        v2 = _top16_replicated(slabs(s2))
        cands, n_live = _pair_sum_candidates(v1, v2)
        sc = _top16_replicated(cands, live=n_live)
        zsum = jnp.zeros_like(sc[0])
        for j in range(PEER_TOPK):
            zsum = zsum + jnp.exp(sc[j] - sc[0])
        row = lambda x: x[0:1]
        th = row(sc[PEER_TOPK - 1])
        v2rows = [row(x) for x in v2]

        def prefix_count(test):
            quarter_hit = [test(v2rows[4 * k + 3]) for k in range(4)]
            base = jnp.zeros_like(s1c)
            for k in range(3):
                base = jnp.where(quarter_hit[k], float(4 * (k + 1)), base)
            inner = jnp.zeros_like(s1c)
            for j in range(3):
                pivot = v2rows[j]
                for k in range(3):
                    pivot = jnp.where(quarter_hit[k], v2rows[4 * (k + 1) + j], pivot)
                inner = jnp.where(test(pivot), float(j + 1), inner)
            return jnp.where(quarter_hit[3], float(PEER_TOPK), base + inner)

        cnt_parts, r2_parts = [], []
        for k0 in range(0, N_KEYS, 4 * SUBLANES):
            s1c, s2c = s1[k0:k0 + 4 * SUBLANES], s2[k0:k0 + 4 * SUBLANES]
            cnt_parts.append(prefix_count(lambda v: s1c + v >= th))
            r2_parts.append(prefix_count(lambda v: v > s2c))
        cnt = jnp.concatenate(cnt_parts, axis=0)
        r2 = jnp.concatenate(r2_parts, axis=0)
        def twice(x):
            bits = pltpu.bitcast(x.astype(BF16).astype(F32), jnp.uint32)
            return bits | (bits >> 16)

        cnt_w = twice(cnt)
        e1_w = twice(0.5 * jnp.exp(s1 - row(v1[0])) / row(zsum))
        for cc in range(cnt_ref.shape[1]):
            cnt_ref[h, cc] = cnt_w[:, cc * LANES:(cc + 1) * LANES]
            e1_ref[h, cc] = e1_w[:, cc * LANES:(cc + 1) * LANES]
        r2_ref[h] = pltpu.bitcast(r2.astype(BF16), jnp.uint32)
        e2_ref[h] = pltpu.bitcast(jnp.exp(s2 - row(v2[0])).astype(BF16), jnp.uint32)
        return carry

    lax.fori_loop(0, PEER_HEADS, head, 0)


def _peer_scores(wq, h1T_bf, sk1_bf, sk2_bf, tl=256):
    D, T = h1T_bf.shape
    Q = wq.shape[1]
    big = jax.ShapeDtypeStruct((PEER_HEADS, T // LANES, N_KEYS, LANES), jnp.uint32)
    big16 = jax.ShapeDtypeStruct((PEER_HEADS, N_KEYS // 2, T), jnp.uint32)
    bspec = pl.BlockSpec((PEER_HEADS, tl // LANES, N_KEYS, LANES), lambda i: (0, i, 0, 0))
    pspec = pl.BlockSpec((PEER_HEADS, N_KEYS // 2, tl), lambda i: (0, 0, i))
    return pl.pallas_call(
        _peer_score_kernel,
        grid=(T // tl,),
        in_specs=[pl.BlockSpec((D, Q), lambda i: (0, 0), pipeline_mode=pl.Buffered(1)),
                  pl.BlockSpec((D, tl), lambda i: (0, i)),
                  pl.BlockSpec((N_KEYS, PEER_QHALF), lambda i: (0, 0)),
                  pl.BlockSpec((N_KEYS, PEER_QHALF), lambda i: (0, 0))],
        out_specs=[bspec, pspec, bspec, pspec],
        out_shape=[big, big16, big, big16],
        scratch_shapes=[pltpu.VMEM((Q, tl), F32), pltpu.VMEM((Q, D), BF16)],
        compiler_params=_cparams(("arbitrary",)),
        name="peer_scores_topk",
    )(wq, h1T_bf, sk1_bf, sk2_bf)


W_ROWS_PER_GROUP = 64


PEER_HALF = 4 * N_KEYS


def _peer_dense_kernel(hb_ref, u_ref, vt_ref, cnt_ref, r2_ref, e1_ref, e2_ref, o_ref, acc_ref,
                       a00, a01, a10, a11, w00, w01, w10, w11, *, te, tm, n_j, n_tiles):
    s = pl.program_id(0)
    sv = jnp.clip(s - 2, 0, n_tiles - 1)
    jv = sv % n_j
    jw = jnp.clip(s - 1, 0, n_tiles - 1) % n_j
    a_bufs = ((a00, a01), (a10, a11))
    w_bufs = ((w00, w01), (w10, w11))

    @pl.when(s == 0)
    def _():
        for ref in (a00, a01, a10, a11, w00, w01, w10, w11):
            ref[...] = jnp.zeros_like(ref)

    @pl.when(jv == 0)
    def _():
        acc_ref[...] = jnp.zeros_like(acc_ref)

    wide = 2 * LANES
    n_r, n_c = PEER_HALF // N_KEYS, tm // LANES
    key0 = pl.multiple_of(jw * (te // N_KEYS), SUBLANES)

    def half_body(half, a_cur, a_prv, w_cur, w_prv):
        def stage_a(q):
            ms = slice(half * PEER_HALF + (q // 2) * wide, half * PEER_HALF + (q // 2 + 1) * wide)
            ls = slice((q % 2) * wide, (q % 2 + 1) * wide)
            a_cur[(q // 2) * wide:(q // 2 + 1) * wide, ls] = jnp.dot(
                u_ref[ms, :], hb_ref[:, ls], preferred_element_type=F32)

        def stage_v(p):
            fs = slice((p // 2) * wide, (p // 2 + 1) * wide)
            ls = slice((p % 2) * wide, (p % 2 + 1) * wide)
            es = slice(half * PEER_HALF, (half + 1) * PEER_HALF)
            acc_ref[fs, ls] += jnp.dot(vt_ref[0, fs, es], w_cur[:, ls], preferred_element_type=F32)

        n_g = N_KEYS // W_ROWS_PER_GROUP
        grp = (W_ROWS_PER_GROUP, LANES)

        def stage_w(idx):
            p, g = idx // n_g, idx % n_g
            r, c = p // n_c, p % n_c
            krow = half * n_r + r
            cs = slice(c * LANES, (c + 1) * LANES)
            ks = slice(g * W_ROWS_PER_GROUP // 2, (g + 1) * W_ROWS_PER_GROUP // 2)
            gate = jnp.zeros(grp, BF16)
            pair = (W_ROWS_PER_GROUP // 2, LANES)
            for h in range(PEER_HEADS):
                cnt_row = cnt_ref[h, c, pl.ds(key0, SUBLANES), :][krow:krow + 1]
                e1_row = e1_ref[h, c, pl.ds(key0, SUBLANES), :][krow:krow + 1]
                cnt_b = pltpu.bitcast(jnp.broadcast_to(cnt_row, pair), BF16)
                e1_b = pltpu.bitcast(jnp.broadcast_to(e1_row, pair), BF16)
                sel = pltpu.bitcast(r2_ref[h, ks, cs], BF16) < cnt_b
                val = pltpu.bitcast(e2_ref[h, ks, cs], BF16) * e1_b
                gate = gate + jnp.where(sel, val, jnp.zeros(grp, BF16))
            ws = slice(r * N_KEYS + g * W_ROWS_PER_GROUP, r * N_KEYS + (g + 1) * W_ROWS_PER_GROUP)
            a = a_prv[ws, cs]
            act = a * (1.0 + lax.erf(a * (1.0 / math.sqrt(2.0))))
            w_prv[ws, cs] = gate * act.astype(BF16)

        per_v = n_r * n_c * n_g // 16
        for q in range(4):
            stage_a(q)
            for t in range(4):
                stage_v(4 * q + t)
                for gg in range(per_v):
                    stage_w((4 * q + t) * per_v + gg)

    for par in range(2):
        @pl.when(s % 2 == par)
        def _():
            for half in range(2):
                half_body(half, a_bufs[par][half], a_bufs[1 - par][half],
                          w_bufs[par][half], w_bufs[1 - par][half])

    @pl.when(jv == n_j - 1)
    def _():
        o_ref[...] = acc_ref[...]


def _peer_dense(h1T_bf, u_bf, v_bf, cnt, r2, e1, e2, tm=512, te=2 * PEER_HALF):
    D, T = h1T_bf.shape
    E = u_bf.shape[0]
    assert te == SUBLANES * N_KEYS and tm == 4 * LANES and D == 2048, \
        "the stage interleave is written for two 512-expert halves x 512 tokens"
    n_i, n_j = T // tm, E // te
    vT_bf = jnp.transpose(v_bf.reshape(n_j, te, D), (0, 2, 1))
    n_tiles = n_i * n_j
    kern = functools.partial(_peer_dense_kernel, te=te, tm=tm, n_j=n_j, n_tiles=n_tiles)
    ta = lambda s: jnp.minimum(s, n_tiles - 1)
    tw = lambda s: jnp.clip(s - 1, 0, n_tiles - 1)
    tv = lambda s: jnp.clip(s - 2, 0, n_tiles - 1)
    sspec = pl.BlockSpec((PEER_HEADS, N_KEYS // 2, tm), lambda s: (0, 0, tw(s) // n_j))
    xspec = pl.BlockSpec((PEER_HEADS, tm // LANES, N_KEYS, LANES), lambda s: (0, tw(s) // n_j, 0, 0))
    return pl.pallas_call(
        kern,
        grid=(n_tiles + 2,),
        in_specs=[pl.BlockSpec((D, tm), lambda s: (0, ta(s) // n_j)),
                  pl.BlockSpec((te, D), lambda s: (ta(s) % n_j, 0)),
                  pl.BlockSpec((1, D, te), lambda s: (tv(s) % n_j, 0, 0)),
                  xspec, sspec, xspec, sspec],
        out_specs=pl.BlockSpec((D, tm), lambda s: (0, tv(s) // n_j)),
        out_shape=jax.ShapeDtypeStruct((D, T), F32),
        scratch_shapes=([pltpu.VMEM((D, tm), F32)]
                        + [pltpu.VMEM((PEER_HALF, tm), F32)] * 4
                        + [pltpu.VMEM((PEER_HALF, tm), BF16)] * 4),
        compiler_params=_cparams(("arbitrary",)),
        name="peer_dense",
    )(h1T_bf, u_bf, vT_bf, cnt, r2, e1, e2)


def _res_ln_kernel(h_ref, ft_ref, g_ref, b_ref, o_ref, *, alpha):
    x = alpha * h_ref[...] + ft_ref[...].T
    mu = jnp.mean(x, -1, keepdims=True)
    xc = x - mu
    var = jnp.mean(xc * xc, -1, keepdims=True)
    o_ref[...] = xc * lax.rsqrt(var + 1e-5) * g_ref[...] + b_ref[...]


def _res_ln(h, fT, g, b, alpha, tm=512):
    T, D = h.shape
    row = pl.BlockSpec((tm, D), lambda i: (i, 0))
    vec = pl.BlockSpec((1, D), lambda i: (0, 0))
    return pl.pallas_call(
        functools.partial(_res_ln_kernel, alpha=alpha),
        grid=(T // tm,),
        in_specs=[row, pl.BlockSpec((D, tm), lambda i: (0, i)), vec, vec],
        out_specs=row,
        out_shape=jax.ShapeDtypeStruct((T, D), F32),
        compiler_params=_cparams(("parallel",)),
        name="res_ln2",
    )(h, fT, g.reshape(1, D), b.reshape(1, D))


def kernel(x, ln0_g, ln0_b, rel_bias, w_in, sink, conv_w, conv_b, f_w1, f_b1, f_freq1, f_w2, f_b2,
           f_freq2, f_w3, hy_bias, mix_norm_g, w_out, ln1_g, ln1_b, peer_wq, peer_subkeys, peer_u,
           peer_v, ln2_g, ln2_b):
    B, S, D = x.shape
    T = B * S
    alpha = (2.0 * DEPTH) ** 0.25
    consts = _dft_constants()

    qi = jnp.arange(BLOCK, dtype=jnp.int32)
    kj = jnp.arange(3 * BLOCK, dtype=jnp.int32)
    rel = kj[None, :] - BLOCK - qi[:, None]
    onehot = (_t5_bucket(rel)[..., None] == jnp.arange(N_BUCKETS, dtype=jnp.int32)).astype(F32)
    bias = jnp.einsum("qkb,bh->hqk", onehot, rel_bias.astype(F32), precision=lax.Precision.HIGHEST)

    h, h_bf = _ln0(x.reshape(T, D), ln0_g, ln0_b)
    for l in range(DEPTH):
        proj = _matmul(h_bf, w_in[l].astype(BF16), tm=1024, tn=2304)
        attn = _attention(proj, bias, sink[l], S)

        kfull = _filters(S, f_w1[l], f_b1[l], f_freq1[l], f_w2[l], f_b2[l], f_freq2[l], f_w3[l])
        kf = _filter_spectra(kfull, consts)
        u = _short_conv(proj.reshape(B, S, -1), conv_w[l], conv_b[l], ATTN_WIDTH + 2 * KV_WIDTH)
        nct = HY_WIDTH // LANES
        z1 = _long_conv(u, 0, u, nct, kf, 0, hy_bias[l, 0], consts)
        hyo = _long_conv(z1, 0, u, 2 * nct, kf, 1, hy_bias[l, 1], consts)

        h1, h1T_bf = _mix_out(attn, hyo.reshape(T, HY_WIDTH), h, mix_norm_g[l], w_out[l].astype(BF16),
                              ln1_g[l], ln1_b[l], alpha)
        cnt, r2, e1, e2 = _peer_scores(peer_wq[l], h1T_bf,
                                       peer_subkeys[l, 0].astype(BF16), peer_subkeys[l, 1].astype(BF16))
        ffnT = _peer_dense(h1T_bf, peer_u[l].astype(BF16), peer_v[l].astype(BF16), cnt, r2, e1, e2)
        h = _res_ln(h1, ffnT, ln2_g[l], ln2_b[l], alpha)
        if l + 1 < DEPTH:
            h_bf = h.astype(BF16)
    return h.reshape(B, S, D)
```

```python
import functools
import math

import numpy as np
import jax
import jax.numpy as jnp
from jax import lax
from jax.experimental import pallas as pl
from jax.experimental.pallas import tpu as pltpu

F32 = jnp.float32
BF16 = jnp.bfloat16

D_MODEL = 2048
HEAD_DIM = 128
N_Q_HEADS = 8
N_KV_HEADS = 2
GQA_GROUP = N_Q_HEADS // N_KV_HEADS
ATTN_WIDTH = N_Q_HEADS * HEAD_DIM
KV_WIDTH = N_KV_HEADS * HEAD_DIM
WINDOW = 128
BLOCK = 128
N_BUCKETS = 32
MAX_DISTANCE = 128
HY_WIDTH = D_MODEL - ATTN_WIDTH
HY_ORDER = 2
POS_BANDS = 16
POS_EMB = 1 + 2 * POS_BANDS
FILTER_HIDDEN = 64
FAST_DECAY_PCT = 0.3
SLOW_DECAY_PCT = 1.5
DECAY_TARGET = 1e-2
NORM_GROUP = 128
N_KEYS = 128
PEER_HEADS = 8
PEER_QDIM = 256
PEER_QHALF = PEER_QDIM // 2
PEER_TOPK = 16
NEG = -1e30
DEPTH = 1

LANES = 128
VMEM_LIMIT = 56 * 1024 * 1024

FFT_N1 = 64
FFT_N2 = 128
FFT_K1 = FFT_N1 // 2 + 1
FFT_K1_PAD = 40
FFT_SLOT = 2 * FFT_N2
KF_ROWS = FFT_K1 * FFT_SLOT
SUBLANES = 8
W_PITCH = FFT_SLOT + SUBLANES
X_PITCH = FFT_N2 + SUBLANES
W_ROWS = FFT_K1_PAD * W_PITCH
X_ROWS = (FFT_N1 // 2) * X_PITCH


def _cparams(sem, vmem=VMEM_LIMIT):
    return pltpu.CompilerParams(dimension_semantics=sem, vmem_limit_bytes=vmem)


def _ln0_kernel(x_ref, g_ref, b_ref, h_ref, hb_ref):
    x = x_ref[...]
    mu = jnp.mean(x, -1, keepdims=True)
    xc = x - mu
    var = jnp.mean(xc * xc, -1, keepdims=True)
    y = xc * lax.rsqrt(var + 1e-5) * g_ref[...] + b_ref[...]
    h_ref[...] = y
    hb_ref[...] = y.astype(BF16)


def _ln0(x2d, g, b, tm=1024):
    T, D = x2d.shape
    return pl.pallas_call(
        _ln0_kernel,
        grid=(T // tm,),
        in_specs=[pl.BlockSpec((tm, D), lambda i: (i, 0)),
                  pl.BlockSpec((1, D), lambda i: (0, 0)),
                  pl.BlockSpec((1, D), lambda i: (0, 0))],
        out_specs=[pl.BlockSpec((tm, D), lambda i: (i, 0)),
                   pl.BlockSpec((tm, D), lambda i: (i, 0))],
        out_shape=[jax.ShapeDtypeStruct((T, D), F32), jax.ShapeDtypeStruct((T, D), BF16)],
        compiler_params=_cparams(("parallel",)),
        name="ln0",
    )(x2d, g.reshape(1, D), b.reshape(1, D))


def _mm_kernel(a_ref, b_ref, o_ref):
    o_ref[...] = jnp.dot(a_ref[...], b_ref[...], preferred_element_type=F32)


def _matmul(a, b, tm, tn):
    M, K = a.shape
    N = b.shape[1]
    return pl.pallas_call(
        _mm_kernel,
        grid=(N // tn, M // tm),
        in_specs=[pl.BlockSpec((tm, K), lambda j, i: (i, 0)),
                  pl.BlockSpec((K, tn), lambda j, i: (0, j))],
        out_specs=pl.BlockSpec((tm, tn), lambda j, i: (i, j)),
        out_shape=jax.ShapeDtypeStruct((M, N), F32),
        compiler_params=_cparams(("parallel", "parallel")),
        name="in_proj",
    )(a, b)


def _t5_bucket(rel):
    nb = N_BUCKETS // 2
    ret = (rel > 0).astype(jnp.int32) * nb
    n = jnp.abs(rel)
    max_exact = nb // 2
    nf = jnp.maximum(n, 1).astype(F32)
    large = max_exact + (jnp.log(nf / max_exact) / math.log(MAX_DISTANCE / max_exact)
                         * (nb - max_exact)).astype(jnp.int32)
    large = jnp.minimum(large, nb - 1)
    return ret + jnp.where(n < max_exact, n, large)


def _attn_kernel(sink_ref, q_ref, kp_ref, kc_ref, kn_ref, vp_ref, vc_ref, vn_ref, bias_ref, o_ref,
                 *, nb, seq):
    n = pl.program_id(0) % nb
    rows = GQA_GROUP * BLOCK
    row = lax.broadcasted_iota(jnp.int32, (rows, 3 * BLOCK), 0)
    kj = lax.broadcasted_iota(jnp.int32, (rows, 3 * BLOCK), 1)
    rel = kj - BLOCK - (row & (BLOCK - 1))
    kabs = n * BLOCK + kj - BLOCK
    valid = (jnp.abs(rel) <= WINDOW) & (kabs >= 0) & (kabs < seq)
    head_of_row = lax.broadcasted_iota(jnp.int32, (rows, 1), 0) // BLOCK
    k = jnp.concatenate([kp_ref[...], kc_ref[...], kn_ref[...]], axis=0).astype(BF16)
    v = jnp.concatenate([vp_ref[...], vc_ref[...], vn_ref[...]], axis=0).astype(BF16)
    scale = 1.0 / math.sqrt(HEAD_DIM)
    for g in range(N_KV_HEADS):
        kg = k[:, g * HEAD_DIM:(g + 1) * HEAD_DIM]
        vg = v[:, g * HEAD_DIM:(g + 1) * HEAD_DIM]
        h0 = g * GQA_GROUP
        qg = jnp.concatenate([q_ref[:, (h0 + r) * HEAD_DIM:(h0 + r + 1) * HEAD_DIM]
                              for r in range(GQA_GROUP)], axis=0).astype(BF16)
        s = lax.dot_general(qg, kg, (((1,), (1,)), ((), ())), preferred_element_type=F32) * scale
        bias_g = bias_ref[h0:h0 + GQA_GROUP].reshape(rows, 3 * BLOCK)
        s = jnp.where(valid, s + bias_g, NEG)
        sk = jnp.zeros((rows, 1), F32)
        for r in range(GQA_GROUP):
            sk = jnp.where(head_of_row == r, sink_ref[h0 + r], sk)
        m = jnp.maximum(jnp.max(s, -1, keepdims=True), sk)
        p = jnp.exp(s - m)
        denom = jnp.sum(p, -1, keepdims=True) + jnp.exp(sk - m)
        p = p / denom
        o = jnp.dot(p.astype(BF16), vg, preferred_element_type=F32)
        for r in range(GQA_GROUP):
            o_ref[:, (h0 + r) * HEAD_DIM:(h0 + r + 1) * HEAD_DIM] = o[r * BLOCK:(r + 1) * BLOCK]


def _attention(proj, bias, sink, seq):
    T = proj.shape[0]
    nb = seq // BLOCK
    kcol = ATTN_WIDTH // KV_WIDTH
    vcol = kcol + 1

    def prev(i):
        return i - jnp.where(i % nb == 0, 0, 1)

    def nxt(i):
        return i + jnp.where(i % nb == nb - 1, 0, 1)

    kern = functools.partial(_attn_kernel, nb=nb, seq=seq)
    return pl.pallas_call(
        kern,
        grid=(T // BLOCK,),
        in_specs=[pl.BlockSpec(memory_space=pltpu.SMEM),
                  pl.BlockSpec((BLOCK, ATTN_WIDTH), lambda i: (i, 0)),
                  pl.BlockSpec((BLOCK, KV_WIDTH), lambda i: (prev(i), kcol)),
                  pl.BlockSpec((BLOCK, KV_WIDTH), lambda i: (i, kcol)),
                  pl.BlockSpec((BLOCK, KV_WIDTH), lambda i: (nxt(i), kcol)),
                  pl.BlockSpec((BLOCK, KV_WIDTH), lambda i: (prev(i), vcol)),
                  pl.BlockSpec((BLOCK, KV_WIDTH), lambda i: (i, vcol)),
                  pl.BlockSpec((BLOCK, KV_WIDTH), lambda i: (nxt(i), vcol)),
                  pl.BlockSpec((N_Q_HEADS, BLOCK, 3 * BLOCK), lambda i: (0, 0, 0))],
        out_specs=pl.BlockSpec((BLOCK, ATTN_WIDTH), lambda i: (i, 0)),
        out_shape=jax.ShapeDtypeStruct((T, ATTN_WIDTH), F32),
        compiler_params=_cparams(("parallel",)),
        name="window_attn",
    )(sink, proj, proj, proj, proj, proj, proj, proj, bias)


def _filter_kernel(z_ref, w1_ref, b1_ref, fr1_ref, w2_ref, b2_ref, fr2_ref, w3_ref, dl_ref, o_ref,
                   *, tt, seq):
    hi = lax.Precision.HIGHEST
    a = jnp.dot(z_ref[...], w1_ref[...], preferred_element_type=F32, precision=hi) + b1_ref[...]
    hid = jnp.sin(fr1_ref[...] * a)
    a = jnp.dot(hid, w2_ref[...], preferred_element_type=F32, precision=hi) + b2_ref[...]
    hid = jnp.sin(fr2_ref[...] * a)
    h = jnp.dot(hid.astype(BF16), w3_ref[0], preferred_element_type=F32)
    j = lax.broadcasted_iota(jnp.int32, h.shape, 0) + pl.program_id(0) * tt
    lag = jnp.where(j < seq, j, (2 * seq - j) & (seq - 1))
    tn = lag.astype(F32) / float(max(seq - 1, 1))
    o_ref[...] = h * jnp.exp(-tn * dl_ref[...])


def _filters(seq, w1, b1, fr1, w2, b2, fr2, w3, tt=512):
    assert seq & (seq - 1) == 0
    t = np.arange(seq, dtype=np.float32)
    tn = t / np.float32(max(seq - 1, 1))
    w = (np.float32(2.0 * math.pi) * t / np.float32(seq))[:, None]
    bands = np.linspace(1e-4, POS_BANDS - 1, POS_BANDS, dtype=np.float32)
    z = np.concatenate([tn[:, None], np.cos(w * bands), -np.sin(w * bands)], -1).astype(np.float32)
    zp = np.pad(z, ((0, 0), (0, FILTER_HIDDEN - POS_EMB)))
    zfull = jnp.asarray(np.concatenate([zp, zp[0:1], zp[1:][::-1]], axis=0))
    w1p = jnp.pad(w1, ((0, FILTER_HIDDEN - POS_EMB), (0, 0)))
    max_decay = math.log(DECAY_TARGET) / FAST_DECAY_PCT
    min_decay = math.log(DECAY_TARGET) / SLOW_DECAY_PCT
    deltas = jnp.abs(jnp.linspace(min_decay, max_decay, HY_WIDTH, dtype=F32))
    ncol = HY_ORDER * HY_WIDTH
    dl = jnp.tile(deltas, HY_ORDER).reshape(1, ncol)
    H = FILTER_HIDDEN
    w3s = jnp.transpose(w3.reshape(H, HY_ORDER, 2, HY_WIDTH), (2, 0, 1, 3)).reshape(2, H, ncol).astype(BF16)
    half = seq // tt
    kern = functools.partial(_filter_kernel, tt=tt, seq=seq)
    full = lambda r, c: pl.BlockSpec((r, c), lambda i: (0, 0))
    return pl.pallas_call(
        kern,
        grid=(2 * half,),
        in_specs=[pl.BlockSpec((tt, H), lambda i: (i, 0)),
                  full(H, H), full(1, H), full(1, H), full(H, H), full(1, H), full(1, H),
                  pl.BlockSpec((1, H, ncol), lambda i: (i // half, 0, 0)), full(1, ncol)],
        out_specs=pl.BlockSpec((tt, ncol), lambda i: (i, 0)),
        out_shape=jax.ShapeDtypeStruct((2 * seq, ncol), F32),
        compiler_params=_cparams(("parallel",)),
        name="hyena_filter_mlp",
    )(zfull, w1p, b1.reshape(1, H), fr1.reshape(1, H), w2, b2.reshape(1, H), fr2.reshape(1, H), w3s, dl)


def _sconv_kernel(x_ref, w_ref, b_ref, o_ref):
    x = x_ref[0]
    L = x.shape[0]
    rows = lax.broadcasted_iota(jnp.int32, x.shape, 0)
    xm = jnp.where(rows == 0, 0.0, pltpu.roll(x, 1, 0))
    xp = jnp.where(rows == L - 1, 0.0, pltpu.roll(x, L - 1, 0))
    w = w_ref[...]
    o_ref[0] = xm * w[0:1] + x * w[1:2] + xp * w[2:3] + b_ref[...]


def _short_conv(proj3, conv_w, conv_b, col0, ct=256):
    B, L, _ = proj3.shape
    C = conv_w.shape[1]
    off = col0 // ct
    return pl.pallas_call(
        _sconv_kernel,
        grid=(B, C // ct),
        in_specs=[pl.BlockSpec((1, L, ct), lambda b, c: (b, 0, c + off)),
                  pl.BlockSpec((3, ct), lambda b, c: (0, c)),
                  pl.BlockSpec((1, ct), lambda b, c: (0, c))],
        out_specs=pl.BlockSpec((1, L, ct), lambda b, c: (b, 0, c)),
        out_shape=jax.ShapeDtypeStruct((B, L, C), F32),
        compiler_params=_cparams(("parallel", "parallel")),
        name="hyena_short_conv",
    )(proj3, conv_w, conv_b.reshape(1, C))


def _dft_constants():
    n1h = FFT_N1 // 2
    k1 = np.arange(FFT_K1_PAD)[:, None].astype(np.float64)
    n1 = np.arange(FFT_N1)[None, :].astype(np.float64)
    ang = 2.0 * np.pi * k1 * n1 / FFT_N1
    live = (np.arange(FFT_K1_PAD) < FFT_K1)[:, None]
    f1_full = np.concatenate([np.where(live, np.cos(ang), 0.0), np.where(live, -np.sin(ang), 0.0)], 0)
    f1 = f1_full[:, :n1h]
    kk1 = np.arange(FFT_K1)[:, None, None].astype(np.float64)
    k2 = np.arange(FFT_N2)[None, :, None].astype(np.float64)
    n2 = np.arange(FFT_N2)[None, None, :].astype(np.float64)
    phi = 2.0 * np.pi * (n2 * k2 / FFT_N2 + n2 * kk1 / (FFT_N1 * FFT_N2))
    c, s = np.cos(phi), np.sin(phi)
    g = np.concatenate([np.concatenate([c, s], 2), np.concatenate([-s, c], 2)], 1)
    ct_, st_ = np.transpose(c, (0, 2, 1)), np.transpose(s, (0, 2, 1))
    ginv = np.concatenate([np.concatenate([ct_, -st_], 2), np.concatenate([st_, ct_], 2)], 1)
    wk = np.where((np.arange(FFT_K1_PAD) == 0) | (np.arange(FFT_K1_PAD) == FFT_N1 // 2), 1.0, 2.0)
    wk = np.where(np.arange(FFT_K1_PAD) < FFT_K1, wk, 0.0)[None, :] / (FFT_N1 * FFT_N2)
    angi = 2.0 * np.pi * np.arange(n1h)[:, None] * np.arange(FFT_K1_PAD)[None, :] / FFT_N1
    finv = np.concatenate([wk * np.cos(angi), -wk * np.sin(angi)], 1)
    as_bf = lambda a: jnp.asarray(a.astype(np.float32)).astype(BF16)
    return as_bf(f1), as_bf(g), as_bf(finv), as_bf(ginv), as_bf(f1_full)


def _pad_rows_in(src_ref, xp_ref, n_blocks=FFT_N1 // 2, zero_row_of_block=None):
    for n1 in range(n_blocks):
        blk = src_ref[pl.ds(n1 * FFT_N2, FFT_N2), :]
        if n1 == zero_row_of_block:
            rows = lax.broadcasted_iota(jnp.int32, blk.shape, 0)
            blk = jnp.where(rows == 0, 0.0, blk)
        xp_ref[pl.ds(n1 * X_PITCH, FFT_N2), :] = blk


def _fft_stage1(xp_ref, w_ref, f1_ref):
    n1h = f1_ref.shape[1]
    kp = FFT_K1_PAD

    def body(i, carry):
        n2 = 2 * i
        xa = xp_ref[pl.ds(n2, n1h, stride=X_PITCH), :]
        xb = xp_ref[pl.ds(n2 + 1, n1h, stride=X_PITCH), :]
        xs = jnp.concatenate([xa, xb], axis=1).astype(BF16)
        r = jnp.dot(f1_ref[...], xs, preferred_element_type=F32)
        w_ref[pl.ds(n2, kp, stride=W_PITCH), :] = r[0:kp, 0:LANES]
        w_ref[pl.ds(n2 + 1, kp, stride=W_PITCH), :] = r[0:kp, LANES:2 * LANES]
        w_ref[pl.ds(FFT_N2 + n2, kp, stride=W_PITCH), :] = r[kp:2 * kp, 0:LANES]
        w_ref[pl.ds(FFT_N2 + n2 + 1, kp, stride=W_PITCH), :] = r[kp:2 * kp, LANES:2 * LANES]
        return carry

    lax.fori_loop(0, FFT_N2 // 2, body, 0, unroll=8)


def _kf_kernel(k_ref, f1_ref, g_ref, kf_ref, w_ref, xp_ref):
    scale = 1.0 / (jnp.sum(jnp.abs(k_ref[...]), 0, keepdims=True) + 1e-6)
    _pad_rows_in(k_ref, xp_ref, n_blocks=FFT_N1, zero_row_of_block=FFT_N1 // 2)
    _fft_stage1(xp_ref, w_ref, f1_ref)
    out = kf_ref.at[0]

    def body(k1, carry):
        src = pl.multiple_of(k1 * W_PITCH, 8)
        dst = pl.multiple_of(k1 * FFT_SLOT, FFT_SLOT)
        s = jnp.dot(g_ref[k1], w_ref[pl.ds(src, FFT_SLOT), :].astype(BF16), preferred_element_type=F32)
        out[pl.ds(dst, FFT_SLOT), :] = s * scale
        return carry

    lax.fori_loop(0, FFT_K1, body, 0, unroll=3)


def _filter_spectra(kfull, consts):
    L2 = kfull.shape[0]
    assert L2 == FFT_N1 * FFT_N2, "the DFT factorisation is written for 2L = 64 * 128"
    C = HY_WIDTH
    nct = C // LANES
    _, g, _, _, f1_full = consts
    return pl.pallas_call(
        _kf_kernel,
        grid=(HY_ORDER * nct,),
        in_specs=[pl.BlockSpec((L2, LANES), lambda j: (0, j)),
                  pl.BlockSpec(f1_full.shape, lambda j: (0, 0)),
                  pl.BlockSpec(g.shape, lambda j: (0, 0, 0))],
        out_specs=pl.BlockSpec((1, KF_ROWS, LANES), lambda j: (j // nct, 0, j % nct)),
        out_shape=jax.ShapeDtypeStruct((HY_ORDER, KF_ROWS, C), F32),
        scratch_shapes=[pltpu.VMEM((W_ROWS, LANES), F32), pltpu.VMEM((2 * X_ROWS, LANES), F32)],
        compiler_params=_cparams(("parallel",)),
        name="hyena_filter_fft",
    )(kfull, f1_full, g)


CONV_GROUP = 11


def _sconv_value(x, w, b):
    L = x.shape[0]
    rows = lax.broadcasted_iota(jnp.int32, x.shape, 0)
    xm = jnp.where(rows == 0, 0.0, pltpu.roll(x, 1, 0))
    xn = jnp.where(rows == L - 1, 0.0, pltpu.roll(x, L - 1, 0))
    return xm * w[0:1] + x * w[1:2] + xn * w[2:3] + b


def _conv_kernel(z_ref, gate_ref, cwz_ref, cbz_ref, cwg_ref, cbg_ref, kf_ref, d_ref, f1_ref, g_ref,
                 finv_ref, ginv_ref, o_ref, w_ref, xp_ref, y_ref, zc_ref, gc_ref, *, conv_z):
    if conv_z:
        zc_ref[...] = _sconv_value(z_ref[0], cwz_ref[...], cbz_ref[...])
        zsrc = zc_ref
    else:
        zsrc = z_ref.at[0]
    gc_ref[...] = _sconv_value(gate_ref[0], cwg_ref[...], cbg_ref[...])
    kf = kf_ref.at[0]
    _pad_rows_in(zsrc, xp_ref)
    _fft_stage1(xp_ref, w_ref, f1_ref)

    def forward(k1):
        base = pl.multiple_of(k1 * W_PITCH, 8)
        kbase = pl.multiple_of(k1 * FFT_SLOT, FFT_SLOT)
        a = w_ref[pl.ds(base, FFT_SLOT), :].astype(BF16)
        s = jnp.dot(g_ref[k1], a, preferred_element_type=F32)
        sr, si = s[0:FFT_N2], s[FFT_N2:FFT_SLOT]
        kr = kf[pl.ds(kbase, FFT_N2), :]
        ki = kf[pl.ds(kbase + FFT_N2, FFT_N2), :]
        return jnp.concatenate([sr * kr - si * ki, sr * ki + si * kr], axis=0).astype(BF16)

    def inverse(k1, y):
        base = pl.multiple_of(k1 * W_PITCH, 8)
        w_ref[pl.ds(base, FFT_SLOT), :] = jnp.dot(ginv_ref[k1], y, preferred_element_type=F32)

    def step(i, do_forward, do_inverse):
        prev = [y_ref[k] for k in range(CONV_GROUP)] if do_inverse else None
        new = [forward(i * CONV_GROUP + k) for k in range(CONV_GROUP)] if do_forward else None
        if do_inverse:
            for k in range(CONV_GROUP):
                inverse((i - 1) * CONV_GROUP + k, prev[k])
        if do_forward:
            for k in range(CONV_GROUP):
                y_ref[k] = new[k]

    n_groups = FFT_K1 // CONV_GROUP
    step(0, True, False)

    def body(i, carry):
        step(i, True, True)
        return carry

    lax.fori_loop(1, n_groups, body, 0)
    step(n_groups, False, True)

    n1h = FFT_N1 // 2
    kp = FFT_K1_PAD

    def body2(i, carry):
        n2 = 2 * i
        zr = jnp.concatenate([w_ref[pl.ds(n2, kp, stride=W_PITCH), :],
                              w_ref[pl.ds(n2 + 1, kp, stride=W_PITCH), :]], axis=1)
        zi = jnp.concatenate([w_ref[pl.ds(FFT_N2 + n2, kp, stride=W_PITCH), :],
                              w_ref[pl.ds(FFT_N2 + n2 + 1, kp, stride=W_PITCH), :]], axis=1)
        zz = jnp.concatenate([zr, zi], axis=0).astype(BF16)
        x = jnp.dot(finv_ref[...], zz, preferred_element_type=F32)
        xp_ref[pl.ds(n2, n1h, stride=X_PITCH), :] = x[:, 0:LANES]
        xp_ref[pl.ds(n2 + 1, n1h, stride=X_PITCH), :] = x[:, LANES:2 * LANES]
        return carry

    lax.fori_loop(0, FFT_N2 // 2, body2, 0, unroll=8)
    d = d_ref[...]
    for n1 in range(n1h):
        rs = pl.ds(n1 * FFT_N2, FFT_N2)
        y = xp_ref[pl.ds(n1 * X_PITCH, FFT_N2), :]
        o_ref[0, rs, :] = gc_ref[rs, :] * (y + zsrc[rs, :] * d)


def _long_conv(z_arr, z_off, gate_arr, gate_off, kf_all, order, d, consts, conv_w, conv_b, zw_off, gw_off,
               conv_z):
    B, L, _ = z_arr.shape
    assert 2 * L == FFT_N1 * FFT_N2, "the DFT factorisation is written for 2L = 64 * 128"
    C = HY_WIDTH
    nct = C // LANES
    f1, g, finv, ginv, _ = consts
    cst2 = lambda a: pl.BlockSpec(a.shape, lambda c, b: (0, 0))
    cst3 = lambda a: pl.BlockSpec(a.shape, lambda c, b: (0, 0, 0))
    cb2 = conv_b.reshape(1, -1)
    return pl.pallas_call(
        functools.partial(_conv_kernel, conv_z=conv_z),
        grid=(nct, B),
        in_specs=[pl.BlockSpec((1, L, LANES), lambda c, b: (b, 0, c + z_off)),
                  pl.BlockSpec((1, L, LANES), lambda c, b: (b, 0, c + gate_off)),
                  pl.BlockSpec((3, LANES), lambda c, b: (0, c + zw_off)),
                  pl.BlockSpec((1, LANES), lambda c, b: (0, c + zw_off)),
                  pl.BlockSpec((3, LANES), lambda c, b: (0, c + gw_off)),
                  pl.BlockSpec((1, LANES), lambda c, b: (0, c + gw_off)),
                  pl.BlockSpec((1, KF_ROWS, LANES), lambda c, b: (order, 0, c)),
                  pl.BlockSpec((1, LANES), lambda c, b: (0, c)),
                  cst2(f1), cst3(g), cst2(finv), cst3(ginv)],
        out_specs=pl.BlockSpec((1, L, LANES), lambda c, b: (b, 0, c)),
        out_shape=jax.ShapeDtypeStruct((B, L, C), F32),
        scratch_shapes=[pltpu.VMEM((W_ROWS, LANES), F32), pltpu.VMEM((X_ROWS, LANES), F32),
                        pltpu.VMEM((CONV_GROUP, FFT_SLOT, LANES), BF16),
                        pltpu.VMEM((L, LANES), F32), pltpu.VMEM((L, LANES), F32)],
        compiler_params=_cparams(("parallel", "parallel")),
        name=f"hyena_long_conv{order}",
    )(z_arr, gate_arr, conv_w, cb2, conv_w, cb2, kf_all, d.reshape(1, C), f1, g, finv, ginv)


def _mix_kernel(attn_ref, hyo_ref, h0_ref, mg_ref, w_ref, g_ref, b_ref, h1_ref, h1t_ref, *, alpha):
    def gnorm(x, goff):
        parts = []
        for gi in range(x.shape[1] // NORM_GROUP):
            xg = x[:, gi * NORM_GROUP:(gi + 1) * NORM_GROUP]
            ms = jnp.mean(xg * xg, -1, keepdims=True)
            gg = mg_ref[:, goff + gi * NORM_GROUP: goff + (gi + 1) * NORM_GROUP]
            parts.append((xg * lax.rsqrt(ms + 1e-6) * gg).astype(BF16))
        return jnp.concatenate(parts, axis=1)

    a = gnorm(attn_ref[...], 0)
    y = gnorm(hyo_ref[...], ATTN_WIDTH)
    mix = (jnp.dot(a, w_ref[0:ATTN_WIDTH, :], preferred_element_type=F32)
           + jnp.dot(y, w_ref[ATTN_WIDTH:, :], preferred_element_type=F32))
    x = alpha * h0_ref[...] + mix
    mu = jnp.mean(x, -1, keepdims=True)
    xc = x - mu
    var = jnp.mean(xc * xc, -1, keepdims=True)
    h1 = xc * lax.rsqrt(var + 1e-5) * g_ref[...] + b_ref[...]
    h1_ref[...] = h1
    h1t_ref[...] = h1.T.astype(BF16)


def _mix_out(attn, hyo, h0, mix_g, w_out_bf, ln_g, ln_b, alpha, tm=512):
    T, D = h0.shape
    kern = functools.partial(_mix_kernel, alpha=alpha)
    return pl.pallas_call(
        kern,
        grid=(T // tm,),
        in_specs=[pl.BlockSpec((tm, ATTN_WIDTH), lambda i: (i, 0)),
                  pl.BlockSpec((tm, HY_WIDTH), lambda i: (i, 0)),
                  pl.BlockSpec((tm, D), lambda i: (i, 0)),
                  pl.BlockSpec((1, D), lambda i: (0, 0)),
                  pl.BlockSpec((D, D), lambda i: (0, 0)),
                  pl.BlockSpec((1, D), lambda i: (0, 0)),
                  pl.BlockSpec((1, D), lambda i: (0, 0))],
        out_specs=[pl.BlockSpec((tm, D), lambda i: (i, 0)), pl.BlockSpec((D, tm), lambda i: (0, i))],
        out_shape=[jax.ShapeDtypeStruct((T, D), F32), jax.ShapeDtypeStruct((D, T), BF16)],
        compiler_params=_cparams(("parallel",)),
        name="mix_out_ln1",
    )(attn, hyo, h0, mix_g.reshape(1, D), w_out_bf, ln_g.reshape(1, D), ln_b.reshape(1, D))


def _oddeven_sort_pairs(n):
    pairs = []

    def merge(lo, m, r):
        step = r * 2
        if step < m:
            merge(lo, m, step)
            merge(lo + r, m, step)
            pairs.extend((i, i + r) for i in range(lo + r, lo + m - r, step))
        else:
            pairs.append((lo, lo + r))

    def sort(lo, m):
        if m > 1:
            sort(lo, m // 2)
            sort(lo + m // 2, m // 2)
            merge(lo, m, 1)

    sort(0, n)
    return pairs


def _bitonic_merge_pairs(n):
    pairs, d = [], n // 2
    while d >= 1:
        pairs.extend((i, i + d) for i in range(n) if (i & d) == 0)
        d //= 2
    return pairs


def _apply_network(xs, pairs):
    xs = list(xs)
    for i, j in pairs:
        xs[i], xs[j] = jnp.maximum(xs[i], xs[j]), jnp.minimum(xs[i], xs[j])
    return xs


def _top16_replicated(xs, live=PEER_TOPK):
    first = [p for p in _oddeven_sort_pairs(PEER_TOPK) if p[1] < live]
    xs = _apply_network(xs, first)
    merge = _bitonic_merge_pairs(PEER_TOPK)
    for shift in (4, 2, 1):
        other = [pltpu.roll(x, shift, 0) for x in xs]
        xs = [jnp.maximum(xs[i], other[PEER_TOPK - 1 - i]) for i in range(PEER_TOPK)]
        xs = _apply_network(xs, merge)
    return xs


def _pair_sum_candidates(v1, v2):
    sub = lax.broadcasted_iota(jnp.int32, v1[0].shape, 0)

    def by_sublane(vs):
        out = vs[SUBLANES - 1]
        for j in range(SUBLANES - 2, -1, -1):
            out = jnp.where(sub == j, vs[j], out)
        return out

    ninf = -jnp.inf
    v2lo, v2hi = by_sublane(v2[:SUBLANES]), by_sublane(v2[SUBLANES:])
    v1lo, v1hi = by_sublane(v1[:SUBLANES]), by_sublane(v1[SUBLANES:])
    cands = [v1[0] + v2lo, v1[0] + v2hi, v1[1] + v2lo]
    for a in (2, 3, 4):
        cands.append(jnp.where(sub < PEER_TOPK // (a + 1), v1[a] + v2lo, ninf))
    cands.append(jnp.where(sub >= 5, v1lo + v2[0], ninf))
    cands.append(v1hi + v2[0])
    cands.append(jnp.where(sub >= 5, v1lo + v2[1], ninf))
    pad = jnp.full(v1[0].shape, ninf, F32)
    return cands + [pad] * (PEER_TOPK - len(cands)), len(cands)


def _peer_score_kernel(wq_ref, h_ref, sk1_ref, sk2_ref, cnt_ref, r2_ref, e1_ref, e2_ref, q_ref, wqt_ref):
    @pl.when(pl.program_id(0) == 0)
    def _():
        chunk = 2 * LANES
        for c in range(wq_ref.shape[1] // chunk):
            wqt_ref[c * chunk:(c + 1) * chunk, :] = wq_ref[:, c * chunk:(c + 1) * chunk].T.astype(BF16)

    q_ref[...] = jnp.dot(wqt_ref[...], h_ref[...], preferred_element_type=F32)

    def head(h, carry):
        base = pl.multiple_of(h * PEER_QDIM, PEER_QDIM)
        q1 = q_ref[pl.ds(base, PEER_QHALF), :].astype(BF16)
        q2 = q_ref[pl.ds(base + PEER_QHALF, PEER_QHALF), :].astype(BF16)
        s1 = jnp.dot(sk1_ref[...], q1, preferred_element_type=F32)
        s2 = jnp.dot(sk2_ref[...], q2, preferred_element_type=F32)
        slabs = lambda s: [s[SUBLANES * i:SUBLANES * (i + 1)] for i in range(N_KEYS // SUBLANES)]
        v1 = _top16_replicated(slabs(s1))
        v2 = _top16_replicated(slabs(s2))
        cands, n_live = _pair_sum_candidates(v1, v2)
        sc = _top16_replicated(cands, live=n_live)
        zsum = jnp.zeros_like(sc[0])
        for j in range(PEER_TOPK):
            zsum = zsum + jnp.exp(sc[j] - sc[0])
        row = lambda x: x[0:1]
        th = row(sc[PEER_TOPK - 1])
        v2rows = [row(x) for x in v2]

        def prefix_count(test):
            quarter_hit = [test(v2rows[4 * k + 3]) for k in range(4)]
            base = jnp.zeros_like(s1c)
            for k in range(3):
                base = jnp.where(quarter_hit[k], float(4 * (k + 1)), base)
            inner = jnp.zeros_like(s1c)
            for j in range(3):
                pivot = v2rows[j]
                for k in range(3):
                    pivot = jnp.where(quarter_hit[k], v2rows[4 * (k + 1) + j], pivot)
                inner = jnp.where(test(pivot), float(j + 1), inner)
            return jnp.where(quarter_hit[3], float(PEER_TOPK), base + inner)

        cnt_parts, r2_parts = [], []
        for k0 in range(0, N_KEYS, 4 * SUBLANES):
            s1c, s2c = s1[k0:k0 + 4 * SUBLANES], s2[k0:k0 + 4 * SUBLANES]
            cnt_parts.append(prefix_count(lambda v: s1c + v >= th))
            r2_parts.append(prefix_count(lambda v: v > s2c))
        cnt = jnp.concatenate(cnt_parts, axis=0)
        r2 = jnp.concatenate(r2_parts, axis=0)
        def twice(x):
            bits = pltpu.bitcast(x.astype(BF16).astype(F32), jnp.uint32)
            return bits | (bits >> 16)

        cnt_w = twice(cnt)
        e1_w = twice(0.5 * jnp.exp(s1 - row(v1[0])) / row(zsum))
        for cc in range(cnt_ref.shape[1]):
            cnt_ref[h, cc] = cnt_w[:, cc * LANES:(cc + 1) * LANES]
            e1_ref[h, cc] = e1_w[:, cc * LANES:(cc + 1) * LANES]
        r2_ref[h] = pltpu.bitcast(r2.astype(BF16), jnp.uint32)
        e2_ref[h] = pltpu.bitcast(jnp.exp(s2 - row(v2[0])).astype(BF16), jnp.uint32)
        return carry

    lax.fori_loop(0, PEER_HEADS, head, 0)


def _peer_scores(wq, h1T_bf, sk1_bf, sk2_bf, tl=256):
    D, T = h1T_bf.shape
    Q = wq.shape[1]
    big = jax.ShapeDtypeStruct((PEER_HEADS, T // LANES, N_KEYS, LANES), jnp.uint32)
    big16 = jax.ShapeDtypeStruct((PEER_HEADS, N_KEYS // 2, T), jnp.uint32)
    bspec = pl.BlockSpec((PEER_HEADS, tl // LANES, N_KEYS, LANES), lambda i: (0, i, 0, 0))
    pspec = pl.BlockSpec((PEER_HEADS, N_KEYS // 2, tl), lambda i: (0, 0, i))
    return pl.pallas_call(
        _peer_score_kernel,
        grid=(T // tl,),
        in_specs=[pl.BlockSpec((D, Q), lambda i: (0, 0), pipeline_mode=pl.Buffered(1)),
                  pl.BlockSpec((D, tl), lambda i: (0, i)),
                  pl.BlockSpec((N_KEYS, PEER_QHALF), lambda i: (0, 0)),
                  pl.BlockSpec((N_KEYS, PEER_QHALF), lambda i: (0, 0))],
        out_specs=[bspec, pspec, bspec, pspec],
        out_shape=[big, big16, big, big16],
        scratch_shapes=[pltpu.VMEM((Q, tl), F32), pltpu.VMEM((Q, D), BF16)],
        compiler_params=_cparams(("arbitrary",)),
        name="peer_scores_topk",
    )(wq, h1T_bf, sk1_bf, sk2_bf)


W_ROWS_PER_GROUP = 64


PEER_HALF = 4 * N_KEYS


def _peer_dense_kernel(hb_ref, u_ref, vt_ref, cnt_ref, r2_ref, e1_ref, e2_ref, o_ref, acc_ref,
                       a00, a01, a10, a11, w00, w01, w10, w11, *, te, tm, n_j, n_tiles):
    s = pl.program_id(0)
    sv = jnp.clip(s - 2, 0, n_tiles - 1)
    jv = sv % n_j
    jw = jnp.clip(s - 1, 0, n_tiles - 1) % n_j
    a_bufs = ((a00, a01), (a10, a11))
    w_bufs = ((w00, w01), (w10, w11))

    @pl.when(s == 0)
    def _():
        for ref in (a00, a01, a10, a11, w00, w01, w10, w11):
            ref[...] = jnp.zeros_like(ref)

    @pl.when(jv == 0)
    def _():
        acc_ref[...] = jnp.zeros_like(acc_ref)

    wide = 2 * LANES
    n_r, n_c = PEER_HALF // N_KEYS, tm // LANES
    key0 = pl.multiple_of(jw * (te // N_KEYS), SUBLANES)

    def half_body(half, a_cur, a_prv, w_cur, w_prv):
        def stage_a(q):
            ms = slice(half * PEER_HALF + (q // 2) * wide, half * PEER_HALF + (q // 2 + 1) * wide)
            ls = slice((q % 2) * wide, (q % 2 + 1) * wide)
            a_cur[(q // 2) * wide:(q // 2 + 1) * wide, ls] = jnp.dot(
                u_ref[ms, :], hb_ref[:, ls], preferred_element_type=F32)

        def stage_v(p):
            fs = slice((p // 2) * wide, (p // 2 + 1) * wide)
            ls = slice((p % 2) * wide, (p % 2 + 1) * wide)
            es = slice(half * PEER_HALF, (half + 1) * PEER_HALF)
            acc_ref[fs, ls] += jnp.dot(vt_ref[0, fs, es], w_cur[:, ls], preferred_element_type=F32)

        n_g = N_KEYS // W_ROWS_PER_GROUP
        grp = (W_ROWS_PER_GROUP, LANES)

        def stage_w(idx):
            p, g = idx // n_g, idx % n_g
            r, c = p // n_c, p % n_c
            krow = half * n_r + r
            cs = slice(c * LANES, (c + 1) * LANES)
            ks = slice(g * W_ROWS_PER_GROUP // 2, (g + 1) * W_ROWS_PER_GROUP // 2)
            gate = jnp.zeros(grp, BF16)
            pair = (W_ROWS_PER_GROUP // 2, LANES)
            for h in range(PEER_HEADS):
                cnt_row = cnt_ref[h, c, pl.ds(key0, SUBLANES), :][krow:krow + 1]
                e1_row = e1_ref[h, c, pl.ds(key0, SUBLANES), :][krow:krow + 1]
                cnt_b = pltpu.bitcast(jnp.broadcast_to(cnt_row, pair), BF16)
                e1_b = pltpu.bitcast(jnp.broadcast_to(e1_row, pair), BF16)
                sel = pltpu.bitcast(r2_ref[h, ks, cs], BF16) < cnt_b
                val = pltpu.bitcast(e2_ref[h, ks, cs], BF16) * e1_b
                gate = gate + jnp.where(sel, val, jnp.zeros(grp, BF16))
            ws = slice(r * N_KEYS + g * W_ROWS_PER_GROUP, r * N_KEYS + (g + 1) * W_ROWS_PER_GROUP)
            a = a_prv[ws, cs]
            act = a * (1.0 + lax.erf(a * (1.0 / math.sqrt(2.0))))
            w_prv[ws, cs] = gate * act.astype(BF16)

        per_v = n_r * n_c * n_g // 16
        for q in range(4):
            stage_a(q)
            for t in range(4):
                stage_v(4 * q + t)
                for gg in range(per_v):
                    stage_w((4 * q + t) * per_v + gg)

    for par in range(2):
        @pl.when(s % 2 == par)
        def _():
            for half in range(2):
                half_body(half, a_bufs[par][half], a_bufs[1 - par][half],
                          w_bufs[par][half], w_bufs[1 - par][half])

    @pl.when(jv == n_j - 1)
    def _():
        o_ref[...] = acc_ref[...]


def _peer_dense(h1T_bf, u_bf, v_bf, cnt, r2, e1, e2, tm=512, te=2 * PEER_HALF):
    D, T = h1T_bf.shape
    E = u_bf.shape[0]
    assert te == SUBLANES * N_KEYS and tm == 4 * LANES and D == 2048, \
        "the stage interleave is written for two 512-expert halves x 512 tokens"
    n_i, n_j = T // tm, E // te
    vT_bf = jnp.transpose(v_bf.reshape(n_j, te, D), (0, 2, 1))
    n_tiles = n_i * n_j
    kern = functools.partial(_peer_dense_kernel, te=te, tm=tm, n_j=n_j, n_tiles=n_tiles)
    ta = lambda s: jnp.minimum(s, n_tiles - 1)
    tw = lambda s: jnp.clip(s - 1, 0, n_tiles - 1)
    tv = lambda s: jnp.clip(s - 2, 0, n_tiles - 1)
    sspec = pl.BlockSpec((PEER_HEADS, N_KEYS // 2, tm), lambda s: (0, 0, tw(s) // n_j))
    xspec = pl.BlockSpec((PEER_HEADS, tm // LANES, N_KEYS, LANES), lambda s: (0, tw(s) // n_j, 0, 0))
    return pl.pallas_call(
        kern,
        grid=(n_tiles + 2,),
        in_specs=[pl.BlockSpec((D, tm), lambda s: (0, ta(s) // n_j)),
                  pl.BlockSpec((te, D), lambda s: (ta(s) % n_j, 0)),
                  pl.BlockSpec((1, D, te), lambda s: (tv(s) % n_j, 0, 0)),
                  xspec, sspec, xspec, sspec],
        out_specs=pl.BlockSpec((D, tm), lambda s: (0, tv(s) // n_j)),
        out_shape=jax.ShapeDtypeStruct((D, T), F32),
        scratch_shapes=([pltpu.VMEM((D, tm), F32)]
                        + [pltpu.VMEM((PEER_HALF, tm), F32)] * 4
                        + [pltpu.VMEM((PEER_HALF, tm), BF16)] * 4),
        compiler_params=_cparams(("arbitrary",)),
        name="peer_dense",
    )(h1T_bf, u_bf, vT_bf, cnt, r2, e1, e2)


def _res_ln_kernel(h_ref, ft_ref, g_ref, b_ref, o_ref, *, alpha):
    x = alpha * h_ref[...] + ft_ref[...].T
    mu = jnp.mean(x, -1, keepdims=True)
    xc = x - mu
    var = jnp.mean(xc * xc, -1, keepdims=True)
    o_ref[...] = xc * lax.rsqrt(var + 1e-5) * g_ref[...] + b_ref[...]


def _res_ln(h, fT, g, b, alpha, tm=512):
    T, D = h.shape
    row = pl.BlockSpec((tm, D), lambda i: (i, 0))
    vec = pl.BlockSpec((1, D), lambda i: (0, 0))
    return pl.pallas_call(
        functools.partial(_res_ln_kernel, alpha=alpha),
        grid=(T // tm,),
        in_specs=[row, pl.BlockSpec((D, tm), lambda i: (0, i)), vec, vec],
        out_specs=row,
        out_shape=jax.ShapeDtypeStruct((T, D), F32),
        compiler_params=_cparams(("parallel",)),
        name="res_ln2",
    )(h, fT, g.reshape(1, D), b.reshape(1, D))


def kernel(x, ln0_g, ln0_b, rel_bias, w_in, sink, conv_w, conv_b, f_w1, f_b1, f_freq1, f_w2, f_b2,
           f_freq2, f_w3, hy_bias, mix_norm_g, w_out, ln1_g, ln1_b, peer_wq, peer_subkeys, peer_u,
           peer_v, ln2_g, ln2_b):
    B, S, D = x.shape
    T = B * S
    alpha = (2.0 * DEPTH) ** 0.25
    consts = _dft_constants()

    qi = jnp.arange(BLOCK, dtype=jnp.int32)
    kj = jnp.arange(3 * BLOCK, dtype=jnp.int32)
    rel = kj[None, :] - BLOCK - qi[:, None]
    onehot = (_t5_bucket(rel)[..., None] == jnp.arange(N_BUCKETS, dtype=jnp.int32)).astype(F32)
    bias = jnp.einsum("qkb,bh->hqk", onehot, rel_bias.astype(F32), precision=lax.Precision.HIGHEST)

    h, h_bf = _ln0(x.reshape(T, D), ln0_g, ln0_b)
    for l in range(DEPTH):
        proj = _matmul(h_bf, w_in[l].astype(BF16), tm=1024, tn=2304)
        attn = _attention(proj, bias, sink[l], S)

        kfull = _filters(S, f_w1[l], f_b1[l], f_freq1[l], f_w2[l], f_b2[l], f_freq2[l], f_w3[l])
        kf = _filter_spectra(kfull, consts)
        proj3 = proj.reshape(B, S, -1)
        nct = HY_WIDTH // LANES
        p0 = (ATTN_WIDTH + 2 * KV_WIDTH) // LANES
        z1 = _long_conv(proj3, p0, proj3, p0 + nct, kf, 0, hy_bias[l, 0], consts,
                        conv_w[l], conv_b[l], 0, nct, True)
        hyo = _long_conv(z1, 0, proj3, p0 + 2 * nct, kf, 1, hy_bias[l, 1], consts,
                         conv_w[l], conv_b[l], 0, 2 * nct, False)

        h1, h1T_bf = _mix_out(attn, hyo.reshape(T, HY_WIDTH), h, mix_norm_g[l], w_out[l].astype(BF16),
                              ln1_g[l], ln1_b[l], alpha)
        cnt, r2, e1, e2 = _peer_scores(peer_wq[l], h1T_bf,
                                       peer_subkeys[l, 0].astype(BF16), peer_subkeys[l, 1].astype(BF16))
        ffnT = _peer_dense(h1T_bf, peer_u[l].astype(BF16), peer_v[l].astype(BF16), cnt, r2, e1, e2)
        h = _res_ln(h1, ffnT, ln2_g[l], ln2_b[l], alpha)
        if l + 1 < DEPTH:
            h_bf = h.astype(BF16)
    return h.reshape(B, S, D)
```
